```python
import math
import jax, jax.numpy as jnp
from jax import lax
import numpy as np

D_MODEL = 2048
BATCH = 2
SEQ = 4096
DEPTH = 1

CHUNK = 64
N_META = 16
D_MIX = D_MODEL
ATTN_WIDTH = D_MIX // 2
N_DIFF_HEADS = 8
DIFF_HEAD_DIM = ATTN_WIDTH // (2 * N_DIFF_HEADS)
V_HEAD_DIM = 2 * DIFF_HEAD_DIM
ROT_DIM = DIFF_HEAD_DIM // 4
ROPE_THETA = 500000.0
Q_BLOCK = 128
SSD_WIDTH = D_MIX - ATTN_WIDTH
SSD_HEAD_DIM = 64
N_SSD_HEADS = SSD_WIDTH // SSD_HEAD_DIM
N_SSD_GROUPS = 2
HEADS_PER_GROUP = N_SSD_HEADS // N_SSD_GROUPS
D_STATE = 128
CONV_WIDTH = 4
CONV_DIM = SSD_WIDTH + 2 * N_SSD_GROUPS * D_STATE
D_FF = 5632
EPS = 1e-6
IN_PROJ_DIM = 3 * ATTN_WIDTH + SSD_WIDTH + CONV_DIM + N_SSD_HEADS

kernel_name = "hymba_diffattn_ssd_macaron_block"


def _rmsnorm(x, w):
    x32 = x.astype(jnp.float32)
    y = x32 * lax.rsqrt(jnp.mean(x32 * x32, axis=-1, keepdims=True) + EPS)
    return (y * w.astype(jnp.float32)).astype(x.dtype)


def _swiglu(h, w_gate, w_up, w_down):
    return (jax.nn.silu(h @ w_gate) * (h @ w_up)) @ w_down


def _chunk_ids(n_pos):
    p = jnp.arange(n_pos, dtype=jnp.int32)
    return jnp.where(p < N_META, 0, (p - N_META) // CHUNK + 1)


def _rope_tables(n_pos):
    inv = jnp.power(ROPE_THETA, -jnp.arange(0, ROT_DIM, 2, dtype=jnp.float32) / ROT_DIM)
    ang = jnp.arange(n_pos, dtype=jnp.float32)[:, None] * inv[None, :]
    return jnp.cos(ang), jnp.sin(ang)


def _partial_rope(x, cos, sin):
    c = cos[None, :, None, None, :].astype(x.dtype)
    s = sin[None, :, None, None, :].astype(x.dtype)
    half = ROT_DIM // 2
    x1 = x[..., :half]
    x2 = x[..., half:ROT_DIM]
    return jnp.concatenate([x1 * c - x2 * s, x2 * c + x1 * s, x[..., ROT_DIM:]], axis=-1)


def _diff_attend(q_blk, q_cid, k, v, k_cid, lam):
    s = jnp.einsum('bqhsd,bkhsd->bhsqk', q_blk, k,
                   preferred_element_type=jnp.float32) * (DIFF_HEAD_DIM ** -0.5)
    mask = k_cid[None, :] <= q_cid[:, None]
    s = jnp.where(mask, s, -jnp.inf)
    p = jax.nn.softmax(s, axis=-1)
    w = p[:, :, 0] - lam * p[:, :, 1]
    return jnp.einsum('bhqk,bkhv->bqhv', w.astype(v.dtype), v)


def _diff_attention(q, k, v, cid, lam):
    bsz, n_pos = q.shape[0], q.shape[1]
    n_real = n_pos - N_META
    n_blocks = n_real // Q_BLOCK
    out_meta = _diff_attend(q[:, :N_META], cid[:N_META], k, v, cid, lam)
    qb = jnp.moveaxis(q[:, N_META:].reshape(bsz, n_blocks, Q_BLOCK, N_DIFF_HEADS, 2, DIFF_HEAD_DIM), 1, 0)
    cb = cid[N_META:].reshape(n_blocks, Q_BLOCK)
    out_real = lax.map(lambda a: _diff_attend(a[0], a[1], k, v, cid, lam), (qb, cb))
    out_real = jnp.moveaxis(out_real, 0, 1).reshape(bsz, n_real, N_DIFF_HEADS, V_HEAD_DIM)
    return jnp.concatenate([out_meta, out_real], axis=1)


def _ssd_mixer(z, xbc, dt_raw, conv_w, conv_b, dt_bias, a_log, d_skip, norm_w):
    bsz, n_pos, _ = xbc.shape
    G, E, P, N = N_SSD_GROUPS, HEADS_PER_GROUP, SSD_HEAD_DIM, D_STATE
    xbc = lax.conv_general_dilated(xbc, conv_w[:, None, :], window_strides=(1,),
                                   padding=[(CONV_WIDTH - 1, 0)],
                                   dimension_numbers=('NWC', 'WIO', 'NWC'),
                                   feature_group_count=CONV_DIM)
    xbc = jax.nn.silu(xbc + conv_b)
    x_s = xbc[..., :SSD_WIDTH]
    b_s = xbc[..., SSD_WIDTH:SSD_WIDTH + G * N]
    c_s = xbc[..., SSD_WIDTH + G * N:]
    dt = jax.nn.softplus(dt_raw.astype(jnp.float32) + dt_bias.astype(jnp.float32))
    pad = (-n_pos) % CHUNK
    def padl(t):
        return jnp.pad(t, [(0, 0), (pad, 0)] + [(0, 0)] * (t.ndim - 2))
    x_s, b_s, c_s, dt = padl(x_s), padl(b_s), padl(c_s), padl(dt)
    n_chunks = (n_pos + pad) // CHUNK
    X = x_s.reshape(bsz, n_chunks, CHUNK, G, E, P)
    Bm = b_s.reshape(bsz, n_chunks, CHUNK, G, N)
    Cm = c_s.reshape(bsz, n_chunks, CHUNK, G, N)
    dt = dt.reshape(bsz, n_chunks, CHUNK, G, E)
    A = -jnp.exp(a_log.astype(jnp.float32)).reshape(G, E)
    dA = dt * A
    Xdt = X * dt[..., None].astype(X.dtype)
    a_cs = jnp.cumsum(dA, axis=2)
    tril = jnp.tril(jnp.ones((CHUNK, CHUNK), dtype=bool))
    seg = a_cs[:, :, :, None] - a_cs[:, :, None, :]
    decay_in = jnp.exp(jnp.where(tril[None, None, :, :, None, None], seg, -jnp.inf))
    cb = jnp.einsum('bclgn,bcsgn->bclsg', Cm, Bm)
    y_diag = jnp.einsum('bclsg,bclsge,bcsgep->bclgep', cb, decay_in.astype(cb.dtype), Xdt)
    decay_to_end = jnp.exp(a_cs[:, :, -1:] - a_cs)
    chunk_states = jnp.einsum('bclgn,bclge,bclgep->bcgepn', Bm, decay_to_end.astype(Bm.dtype), Xdt)
    chunk_decay = jnp.exp(a_cs[:, :, -1])

    def step(state, inp):
        cs, dec = inp
        return state * dec[..., None, None] + cs, state

    init = jnp.zeros((bsz, G, E, P, N), jnp.float32)
    _, states_in = lax.scan(step, init, (jnp.moveaxis(chunk_states, 1, 0), jnp.moveaxis(chunk_decay, 1, 0)))
    states_in = jnp.moveaxis(states_in, 0, 1).astype(Cm.dtype)
    y_off = jnp.einsum('bclgn,bcgepn,bclge->bclgep', Cm, states_in, jnp.exp(a_cs).astype(Cm.dtype))
    y = y_diag + y_off + X * d_skip.reshape(G, E)[:, :, None].astype(X.dtype)
    y = y.reshape(bsz, n_pos + pad, SSD_WIDTH)[:, pad:].astype(z.dtype)
    return _rmsnorm(y * jax.nn.silu(z), norm_w)


def setup_inputs(seed: int = 0) -> dict:
    key = jax.random.key(seed)
    ks = jax.random.split(key, 32)
    f32 = jnp.float32

    def nrm(k, shape, scale):
        return jax.random.normal(k, shape, f32) * scale

    def gain(k, dim):
        return 1.0 + 0.01 * jax.random.normal(k, (DEPTH, dim), f32)

    dt0 = jnp.exp(jax.random.uniform(ks[20], (DEPTH, N_SSD_HEADS), f32,
                                     minval=math.log(1e-3), maxval=math.log(1e-1)))
    return {
        "x": nrm(ks[0], (BATCH, SEQ, D_MODEL), 1.0),
        "meta_tokens": nrm(ks[1], (N_META, D_MODEL), 1.0),
        "ffn1_norm": gain(ks[2], D_MODEL),
        "ffn1_w_gate": nrm(ks[3], (DEPTH, D_MODEL, D_FF), D_MODEL ** -0.5),
        "ffn1_w_up": nrm(ks[4], (DEPTH, D_MODEL, D_FF), D_MODEL ** -0.5),
        "ffn1_w_down": nrm(ks[5], (DEPTH, D_FF, D_MODEL), D_FF ** -0.5),
        "mix_norm": gain(ks[6], D_MODEL),
        "w_in": nrm(ks[7], (DEPTH, D_MODEL, IN_PROJ_DIM), D_MODEL ** -0.5),
        "q_norm": gain(ks[8], DIFF_HEAD_DIM),
        "k_norm": gain(ks[9], DIFF_HEAD_DIM),
        "lambda_q1": nrm(ks[10], (DEPTH, DIFF_HEAD_DIM), 0.1),
        "lambda_k1": nrm(ks[11], (DEPTH, DIFF_HEAD_DIM), 0.1),
        "lambda_q2": nrm(ks[12], (DEPTH, DIFF_HEAD_DIM), 0.1),
        "lambda_k2": nrm(ks[13], (DEPTH, DIFF_HEAD_DIM), 0.1),
        "attn_out_norm": gain(ks[14], V_HEAD_DIM),
        "conv_w": nrm(ks[15], (DEPTH, CONV_WIDTH, CONV_DIM), CONV_WIDTH ** -0.5),
        "conv_b": nrm(ks[16], (DEPTH, CONV_DIM), 0.01),
        "dt_bias": dt0 + jnp.log(-jnp.expm1(-dt0)),
        "a_log": jnp.log(jax.random.uniform(ks[17], (DEPTH, N_SSD_HEADS), f32, minval=1.0, maxval=16.0)),
        "d_skip": 1.0 + 0.01 * jax.random.normal(ks[18], (DEPTH, N_SSD_HEADS), f32),
        "ssd_norm": gain(ks[19], SSD_WIDTH),
        "w_out": nrm(ks[21], (DEPTH, D_MIX, D_MODEL), D_MIX ** -0.5),
        "ffn2_norm": gain(ks[22], D_MODEL),
        "ffn2_w_gate": nrm(ks[23], (DEPTH, D_MODEL, D_FF), D_MODEL ** -0.5),
        "ffn2_w_up": nrm(ks[24], (DEPTH, D_MODEL, D_FF), D_MODEL ** -0.5),
        "ffn2_w_down": nrm(ks[25], (DEPTH, D_FF, D_MODEL), D_FF ** -0.5),
    }


def reference(x, meta_tokens, ffn1_norm, ffn1_w_gate, ffn1_w_up, ffn1_w_down, mix_norm, w_in,
              q_norm, k_norm, lambda_q1, lambda_k1, lambda_q2, lambda_k2, attn_out_norm,
              conv_w, conv_b, dt_bias, a_log, d_skip, ssd_norm, w_out,
              ffn2_norm, ffn2_w_gate, ffn2_w_up, ffn2_w_down):
    bsz = x.shape[0]
    meta = jnp.broadcast_to(meta_tokens.astype(x.dtype)[None], (bsz, N_META, D_MODEL))
    h = jnp.concatenate([meta, x], axis=1)
    n_pos = h.shape[1]
    cid = _chunk_ids(n_pos)
    cos, sin = _rope_tables(n_pos)

    for l in range(DEPTH):
        h = h + 0.5 * _swiglu(_rmsnorm(h, ffn1_norm[l]), ffn1_w_gate[l], ffn1_w_up[l], ffn1_w_down[l])

        u = _rmsnorm(h, mix_norm[l]) @ w_in[l]
        o = 0
        q = u[..., o:o + ATTN_WIDTH]; o += ATTN_WIDTH
        k = u[..., o:o + ATTN_WIDTH]; o += ATTN_WIDTH
        v = u[..., o:o + ATTN_WIDTH]; o += ATTN_WIDTH
        z = u[..., o:o + SSD_WIDTH]; o += SSD_WIDTH
        xbc = u[..., o:o + CONV_DIM]; o += CONV_DIM
        dt_raw = u[..., o:o + N_SSD_HEADS]

        q = q.reshape(bsz, n_pos, N_DIFF_HEADS, 2, DIFF_HEAD_DIM)
        k = k.reshape(bsz, n_pos, N_DIFF_HEADS, 2, DIFF_HEAD_DIM)
        v = v.reshape(bsz, n_pos, N_DIFF_HEADS, V_HEAD_DIM)
        q = _partial_rope(_rmsnorm(q, q_norm[l]), cos, sin)
        k = _partial_rope(_rmsnorm(k, k_norm[l]), cos, sin)
        lam_init = 0.8 - 0.6 * math.exp(-0.3 * l)
        lam = (jnp.exp(jnp.sum(lambda_q1[l].astype(jnp.float32) * lambda_k1[l].astype(jnp.float32)))
               - jnp.exp(jnp.sum(lambda_q2[l].astype(jnp.float32) * lambda_k2[l].astype(jnp.float32)))
               + lam_init)
        attn = _diff_attention(q, k, v, cid, lam)
        attn = (_rmsnorm(attn, attn_out_norm[l]) * (1.0 - lam_init)).reshape(bsz, n_pos, ATTN_WIDTH)

        ssd = _ssd_mixer(z, xbc, dt_raw, conv_w[l], conv_b[l], dt_bias[l], a_log[l], d_skip[l], ssd_norm[l])

        h = h + jnp.concatenate([attn, ssd.astype(attn.dtype)], axis=-1) @ w_out[l]

        h = h + 0.5 * _swiglu(_rmsnorm(h, ffn2_norm[l]), ffn2_w_gate[l], ffn2_w_up[l], ffn2_w_down[l])

    return h[:, N_META:]
```

```python
import functools
import math

import jax
import jax.numpy as jnp
import numpy as np
from jax import lax
from jax.experimental import pallas as pl
from jax.experimental.pallas import tpu as pltpu

F32 = jnp.float32
BF16 = jnp.bfloat16
HIGHEST = lax.Precision.HIGHEST

EPS = 1e-6
CHUNK = 64
N_META = 16
N_DIFF_HEADS = 8
DIFF_HEAD_DIM = 64
V_HEAD_DIM = 128
ROT_DIM = 16
ROPE_THETA = 500000.0
SSD_HEAD_DIM = 64
N_SSD_HEADS = 16
N_SSD_GROUPS = 2
D_STATE = 128
CONV_WIDTH = 4
LANES = 128
VMEM_LIMIT = 56 * 1024 * 1024


def _cparams(sem):
    return pltpu.CompilerParams(dimension_semantics=sem, vmem_limit_bytes=VMEM_LIMIT)


def _silu(v):
    return v * (1.0 / (1.0 + jnp.exp(-v)))


def _rms(v, w):
    ms = jnp.mean(v * v, axis=-1, keepdims=True)
    return v * lax.rsqrt(ms + EPS) * w


def _ffn_kernel(x_ref, nw_ref, wg_ref, wu_ref, wd_ref, pnw_ref, out_ref, hn_ref, xn_sc, acc_sc):
    j = pl.program_id(1)

    @pl.when(j == 0)
    def _():
        xn_sc[...] = _rms(x_ref[...], nw_ref[...]).astype(BF16)
        acc_sc[...] = jnp.zeros_like(acc_sc)

    xn = xn_sc[...]
    g = jnp.dot(xn, wg_ref[...], preferred_element_type=F32)
    u = jnp.dot(xn, wu_ref[...], preferred_element_type=F32)
    a = (_silu(g) * u).astype(BF16)
    acc_sc[...] += jnp.dot(a, wd_ref[...], preferred_element_type=F32)

    @pl.when(j == pl.num_programs(1) - 1)
    def _():
        h = x_ref[...] + 0.5 * acc_sc[...]
        out_ref[...] = h
        hn_ref[...] = _rms(h, pnw_ref[...]).astype(BF16)


def _ffn(x2, nw, wg, wu, wd, post_nw, *, tm, tf):
    m, d = x2.shape
    f = wg.shape[1]
    return pl.pallas_call(
        _ffn_kernel,
        grid=(m // tm, f // tf),
        in_specs=[
            pl.BlockSpec((tm, d), lambda i, j: (i, 0)),
            pl.BlockSpec((1, d), lambda i, j: (0, 0)),
            pl.BlockSpec((d, tf), lambda i, j: (0, j)),
            pl.BlockSpec((d, tf), lambda i, j: (0, j)),
            pl.BlockSpec((tf, d), lambda i, j: (j, 0)),
            pl.BlockSpec((1, d), lambda i, j: (0, 0)),
        ],
        out_specs=[
            pl.BlockSpec((tm, d), lambda i, j: (i, 0)),
            pl.BlockSpec((tm, d), lambda i, j: (i, 0)),
        ],
        out_shape=[jax.ShapeDtypeStruct((m, d), F32), jax.ShapeDtypeStruct((m, d), BF16)],
        scratch_shapes=[pltpu.VMEM((tm, d), BF16), pltpu.VMEM((tm, d), F32)],
        compiler_params=_cparams(("parallel", "arbitrary")),
        name="ffn",
    )(x2, nw, wg, wu, wd, post_nw)


def _proj_kernel(x_ref, w_ref, out_ref):
    out_ref[...] = jnp.dot(x_ref[...], w_ref[...], preferred_element_type=F32).astype(out_ref.dtype)


def _proj(x2, w, *, tm, tn, out_dtype=BF16, name="proj"):
    m, d = x2.shape
    n = w.shape[1]
    return pl.pallas_call(
        _proj_kernel,
        grid=(m // tm, n // tn),
        in_specs=[pl.BlockSpec((tm, d), lambda i, j: (i, 0)), pl.BlockSpec((d, tn), lambda i, j: (0, j))],
        out_specs=pl.BlockSpec((tm, tn), lambda i, j: (i, j)),
        out_shape=jax.ShapeDtypeStruct((m, n), out_dtype),
        compiler_params=_cparams(("parallel", "arbitrary")),
        name=name,
    )(x2, w)


def _proj_qk_kernel(x_ref, w_ref, gain_ref, rope_ref, bd_ref, out_ref):
    acc = jnp.dot(x_ref[...], w_ref[...], preferred_element_type=F32)
    tn = acc.shape[1]
    cos, sin_lo, sin_hi = rope_ref[0], rope_ref[1], rope_ref[2]
    bd = bd_ref[...]
    for c in range(tn // LANES):
        y = acc[:, c * LANES:(c + 1) * LANES]
        y2 = y * y
        hi = y2.astype(BF16)
        lo = (y2 - hi.astype(F32)).astype(BF16)
        ms = jnp.dot(hi, bd, preferred_element_type=F32) + jnp.dot(lo, bd, preferred_element_type=F32)
        yn = y * lax.rsqrt(ms + EPS) * gain_ref[0][:, c * LANES:(c + 1) * LANES]
        half = ROT_DIM // 2
        r = yn * cos + pltpu.roll(yn, half, 1) * sin_hi + pltpu.roll(yn, LANES - half, 1) * sin_lo
        out_ref[:, c * LANES:(c + 1) * LANES] = r.astype(out_ref.dtype)


def _proj_qk(x2, w, gain, rope, bd, *, tm, tn):
    m, d = x2.shape
    n = w.shape[1]
    n_tab = rope.shape[1] // tm
    return pl.pallas_call(
        _proj_qk_kernel,
        grid=(m // tm, n // tn),
        in_specs=[
            pl.BlockSpec((tm, d), lambda i, j: (i, 0)),
            pl.BlockSpec((d, tn), lambda i, j: (0, j)),
            pl.BlockSpec((1, 1, tn), lambda i, j: (j, 0, 0)),
            pl.BlockSpec((3, tm, LANES), lambda i, j: (0, i % n_tab, 0)),
            pl.BlockSpec((LANES, LANES), lambda i, j: (0, 0)),
        ],
        out_specs=pl.BlockSpec((tm, tn), lambda i, j: (i, j)),
        out_shape=jax.ShapeDtypeStruct((m, n), BF16),
        compiler_params=_cparams(("parallel", "arbitrary")),
        name="proj_qk",
    )(x2, w, gain, rope, bd)


def _proj_dt_kernel(x_ref, w_ref, wt_ref, out_ref, outt_ref):
    x = x_ref[...]
    out_ref[...] = jnp.dot(x, w_ref[...], preferred_element_type=F32)
    outt_ref[...] = lax.dot_general(wt_ref[...], x, (((1,), (1,)), ((), ())), preferred_element_type=F32)


def _proj_dt(x2, w_pad, w_t, *, tm):
    m, d = x2.shape
    nh = w_t.shape[0]
    return pl.pallas_call(
        _proj_dt_kernel,
        grid=(m // tm,),
        in_specs=[
            pl.BlockSpec((tm, d), lambda i: (i, 0)),
            pl.BlockSpec((d, LANES), lambda i: (0, 0)),
            pl.BlockSpec((nh, d), lambda i: (0, 0)),
        ],
        out_specs=[pl.BlockSpec((tm, LANES), lambda i: (i, 0)), pl.BlockSpec((nh, tm), lambda i: (0, i))],
        out_shape=[jax.ShapeDtypeStruct((m, LANES), F32), jax.ShapeDtypeStruct((nh, m), F32)],
        compiler_params=_cparams(("parallel",)),
        name="proj_dt",
    )(x2, w_pad, w_t)


def _attn_kernel(qi_tab, ki_tab, q_ref, k_ref, v_ref, km_ref, vm_ref, lam_ref, gain_ref, out_ref,
                 qs_sc, m_sc, l_sc, acc_sc, *, tq, lam_init):
    s_idx = pl.program_id(2)
    qi = qi_tab[s_idx]
    ki = ki_tab[s_idx]
    nt = (((1,), (1,)), ((), ()))

    def online_update(s, v):
        m_prev = m_sc[...]
        m_new = jnp.maximum(m_prev, jnp.max(s, axis=-1, keepdims=True))
        alpha = jnp.exp(m_prev - m_new)
        p = jnp.exp(s - m_new)
        l_sc[...] = alpha * l_sc[...] + jnp.sum(p, axis=-1, keepdims=True)
        acc_sc[...] = alpha * acc_sc[...] + jnp.dot(p.astype(BF16), v, preferred_element_type=F32)
        m_sc[...] = m_new

    @pl.when(ki == 0)
    def _():
        q = q_ref[...]
        lane = lax.broadcasted_iota(jnp.int32, q.shape, 1)
        zero = jnp.zeros_like(q)
        qs_sc[0:tq, :] = jnp.where(lane < DIFF_HEAD_DIM, q, zero)
        qs_sc[tq:2 * tq, :] = jnp.where(lane >= DIFF_HEAD_DIM, q, zero)
        s = lax.dot_general(qs_sc[...], km_ref[...], nt, preferred_element_type=F32)
        col = lax.broadcasted_iota(jnp.int32, s.shape, 1)
        s = jnp.where(col < N_META, s, -jnp.inf)
        m0 = jnp.max(s, axis=-1, keepdims=True)
        p = jnp.exp(s - m0)
        m_sc[...] = m0
        l_sc[...] = jnp.sum(p, axis=-1, keepdims=True)
        acc_sc[...] = jnp.dot(p.astype(BF16), vm_ref[...], preferred_element_type=F32)

    @pl.when(ki < qi)
    def _():
        s = lax.dot_general(qs_sc[...], k_ref[...], nt, preferred_element_type=F32)
        online_update(s, v_ref[...])

    @pl.when(ki == qi)
    def _():
        s = lax.dot_general(qs_sc[...], k_ref[...], nt, preferred_element_type=F32)
        row = lax.broadcasted_iota(jnp.int32, s.shape, 0)
        col = lax.broadcasted_iota(jnp.int32, s.shape, 1)
        row_chunk = (row & (tq - 1)) // CHUNK
        s = jnp.where(col // CHUNK <= row_chunk, s, -jnp.inf)
        online_update(s, v_ref[...])
        lq1, lk1, lq2, lk2 = lam_ref[0:1, :], lam_ref[1:2, :], lam_ref[2:3, :], lam_ref[3:4, :]
        lam = (jnp.exp(jnp.sum(lq1 * lk1, axis=-1, keepdims=True))
               - jnp.exp(jnp.sum(lq2 * lk2, axis=-1, keepdims=True)) + lam_init)
        o = acc_sc[...] / l_sc[...]
        o = o[0:tq, :] - lam * o[tq:2 * tq, :]
        out_ref[...] = (_rms(o, gain_ref[...]) * (1.0 - lam_init)).astype(out_ref.dtype)


def _attention(qk, vz, kmeta, vmeta, lam_vecs, gain, *, bsz, seq, tq, lam_init):
    nq = seq // tq
    h = N_DIFF_HEADS
    qi_list, ki_list = [], []
    for a in range(nq):
        for b in range(a + 1):
            qi_list.append(a)
            ki_list.append(b)
    qi_tab = jnp.asarray(np.array(qi_list, np.int32))
    ki_tab = jnp.asarray(np.array(ki_list, np.int32))
    n_steps = len(qi_list)
    grid_spec = pltpu.PrefetchScalarGridSpec(
        num_scalar_prefetch=2,
        grid=(bsz, h, n_steps),
        in_specs=[
            pl.BlockSpec((tq, V_HEAD_DIM), lambda b, hh, s, qt, kt: (b * nq + qt[s], hh)),
            pl.BlockSpec((tq, V_HEAD_DIM), lambda b, hh, s, qt, kt: (b * nq + kt[s], h + hh)),
            pl.BlockSpec((tq, V_HEAD_DIM), lambda b, hh, s, qt, kt: (b * nq + kt[s], hh)),
            pl.BlockSpec((LANES, V_HEAD_DIM), lambda b, hh, s, qt, kt: (0, hh)),
            pl.BlockSpec((LANES, V_HEAD_DIM), lambda b, hh, s, qt, kt: (0, hh)),
            pl.BlockSpec((4, DIFF_HEAD_DIM), lambda b, hh, s, qt, kt: (0, 0)),
            pl.BlockSpec((1, V_HEAD_DIM), lambda b, hh, s, qt, kt: (0, 0)),
        ],
        out_specs=pl.BlockSpec((tq, V_HEAD_DIM), lambda b, hh, s, qt, kt: (b * nq + qt[s], hh)),
        scratch_shapes=[
            pltpu.VMEM((2 * tq, V_HEAD_DIM), BF16),
            pltpu.VMEM((2 * tq, 1), F32),
            pltpu.VMEM((2 * tq, 1), F32),
            pltpu.VMEM((2 * tq, V_HEAD_DIM), F32),
        ],
    )
    return pl.pallas_call(
        functools.partial(_attn_kernel, tq=tq, lam_init=lam_init),
        grid_spec=grid_spec,
        out_shape=jax.ShapeDtypeStruct((bsz * seq, h * V_HEAD_DIM), BF16),
        compiler_params=_cparams(("parallel", "parallel", "arbitrary")),
        name="attn",
    )(qi_tab, ki_tab, qk, qk, vz, kmeta, vmeta, lam_vecs, gain)


def _ssd_kernel(xbc_ref, z_ref, dt_ref, dtt_ref, xbcm_ref, dtm_ref, dttm_ref,
                convw_ref, convb_ref, dtb_ref, dtbt_ref, alog_ref, alogt_ref, dskip_ref, nw_ref,
                expand_ref, tril_ref, triu2_ref, out_ref, xs_sc, xm_sc, state_sc, *, t_blk):
    n_heads, p_dim, n_state, n_groups = N_SSD_HEADS, SSD_HEAD_DIM, D_STATE, N_SSD_GROUPS
    width = n_heads * p_dim
    gw = width // n_groups
    n_pairs = n_heads // 2
    nt = (((1,), (1,)), ((), ()))
    expand = expand_ref[...]
    a_row = -jnp.exp(alog_ref[...])
    a_col = -jnp.exp(alogt_ref[...])
    d_full = dskip_ref[...]
    lane = lax.broadcasted_iota(jnp.int32, (CHUNK, LANES), 1)
    row = lax.broadcasted_iota(jnp.int32, (CHUNK, LANES), 0)
    left = lane < p_dim
    causal2 = (lane & (CHUNK - 1)) <= row

    def chunk(src_ref, r0, dt_raw, dtt_raw, pad_rows, z, out_r0):
        conv = convb_ref[...]
        for k in range(CONV_WIDTH):
            conv = conv + src_ref[pl.ds(r0 - (CONV_WIDTH - 1) + k, CHUNK), :] * convw_ref[k:k + 1, :]
        xc = _silu(conv)
        x_s = xc[:, :width]
        dt = jax.nn.softplus(dt_raw + dtb_ref[...])
        dtt = jax.nn.softplus(dtt_raw + dtbt_ref[...])
        if pad_rows:
            dt = jnp.where(row >= pad_rows, dt, 0.0)
            dtt = jnp.where(lax.broadcasted_iota(jnp.int32, dtt.shape, 1) >= pad_rows, dtt, 0.0)
        a_cs = jnp.dot(tril_ref[...], dt * a_row, precision=HIGHEST, preferred_element_type=F32)
        a_cs_t2 = jnp.dot(dtt * a_col, triu2_ref[...], precision=HIGHEST, preferred_element_type=F32)
        a_full = jnp.dot(a_cs, expand, precision=HIGHEST, preferred_element_type=F32)
        dt_full = jnp.dot(dt, expand, precision=HIGHEST, preferred_element_type=F32)
        a_last = a_full[CHUNK - 1:CHUNK, :]
        xdt = x_s * dt_full
        w_state = (xdt * jnp.exp(a_last - a_full)).astype(BF16)
        xdt_b = xdt.astype(BF16)
        zero_b = jnp.zeros((CHUNK, LANES), BF16)
        y_parts = []
        for g in range(n_groups):
            b_g = xc[:, width + g * n_state: width + (g + 1) * n_state]
            c_g = xc[:, width + (n_groups + g) * n_state: width + (n_groups + g + 1) * n_state]
            b_gb = b_g.astype(BF16)
            c_gb = c_g.astype(BF16)
            st = state_sc[g]
            if out_r0 is not None:
                cb2 = lax.dot_general(c_gb, jnp.concatenate([b_gb, b_gb], axis=0), nt,
                                      preferred_element_type=F32)
                y_off = jnp.dot(c_gb, st.astype(BF16), preferred_element_type=F32)
                y_g = y_off * jnp.exp(a_full[:, g * gw:(g + 1) * gw])
                diag = []
                for pp in range(n_pairs // n_groups):
                    pr = g * (n_pairs // n_groups) + pp
                    colb = a_full[:, pr * LANES:(pr + 1) * LANES]
                    rowb = jnp.where(left, a_cs_t2[2 * pr:2 * pr + 1, :], a_cs_t2[2 * pr + 1:2 * pr + 2, :])
                    dec = jnp.exp(jnp.where(causal2, colb - rowb, -jnp.inf))
                    m_pair = (cb2 * dec).astype(BF16)
                    xp = xdt_b[:, pr * LANES:(pr + 1) * LANES]
                    rhs = jnp.concatenate([jnp.where(left, xp, zero_b), jnp.where(left, zero_b, xp)], axis=0)
                    diag.append(jnp.dot(m_pair, rhs, preferred_element_type=F32))
                y_parts.append(y_g + jnp.concatenate(diag, axis=1))
            new_st = st * jnp.exp(a_last[:, g * gw:(g + 1) * gw]) + jnp.dot(
                b_g.T.astype(BF16), w_state[:, g * gw:(g + 1) * gw], preferred_element_type=F32)
            state_sc[g] = new_st
        if out_r0 is not None:
            y = jnp.concatenate(y_parts, axis=1) + x_s * d_full
            gated = y * _silu(z)
            out_ref[pl.ds(out_r0, CHUNK), :] = _rms(gated, nw_ref[...]).astype(out_ref.dtype)

    @pl.when(pl.program_id(1) == 0)
    def _():
        state_sc[...] = jnp.zeros_like(state_sc)
        xm_sc[0:8, :] = jnp.zeros((8, xm_sc.shape[1]), F32)
        xm_sc[8:8 + CHUNK, :] = xbcm_ref[...].astype(F32)
        chunk(xm_sc, 8, dtm_ref[...], dttm_ref[...], CHUNK - N_META, None, None)
        xs_sc[0:8, :] = xm_sc[CHUNK:CHUNK + 8, :]

    xs_sc[8:8 + t_blk, :] = xbc_ref[...].astype(F32)
    for c in range(t_blk // CHUNK):
        chunk(xs_sc, 8 + c * CHUNK, dt_ref[c * CHUNK:(c + 1) * CHUNK, :], dtt_ref[:, c * CHUNK:(c + 1) * CHUNK],
              0, z_ref[c * CHUNK:(c + 1) * CHUNK, :].astype(F32), c * CHUNK)
    xs_sc[0:8, :] = xs_sc[t_blk:t_blk + 8, :]


def _ssd(xbc, vz, dt, dtt, xbc_m, dt_m, dtt_m, conv_w, conv_b, dt_bias, a_log, d_skip, norm_w,
         *, bsz, seq, t_blk):
    nb = seq // t_blk
    cdim = xbc.shape[1]
    width = N_SSD_HEADS * SSD_HEAD_DIM
    nh = N_SSD_HEADS

    def pad_row(v):
        return jnp.pad(v.reshape(1, nh), ((0, 0), (0, LANES - nh)))

    hh = np.arange(LANES)[:, None]
    ll = np.arange(width)[None, :]
    expand = jnp.asarray((ll // SSD_HEAD_DIM == hh).astype(np.float32))
    ii = np.arange(CHUNK)
    tril = jnp.asarray((ii[None, :] <= ii[:, None]).astype(np.float32))
    triu2 = jnp.asarray(np.tile((ii[:, None] <= ii[None, :]).astype(np.float32), (1, 2)))
    full = lambda shape: pl.BlockSpec(shape, lambda b, i: tuple(0 for _ in shape))
    return pl.pallas_call(
        functools.partial(_ssd_kernel, t_blk=t_blk),
        grid=(bsz, nb),
        in_specs=[
            pl.BlockSpec((t_blk, cdim), lambda b, i: (b * nb + i, 0)),
            pl.BlockSpec((t_blk, width), lambda b, i: (b * nb + i, 1)),
            pl.BlockSpec((t_blk, LANES), lambda b, i: (b * nb + i, 0)),
            pl.BlockSpec((nh, t_blk), lambda b, i: (0, b * nb + i)),
            full((CHUNK, cdim)), full((CHUNK, LANES)), full((nh, CHUNK)),
            full((CONV_WIDTH, cdim)), full((1, cdim)), full((1, LANES)), full((nh, 1)),
            full((1, LANES)), full((nh, 1)), full((1, width)), full((1, width)),
            full((LANES, width)), full((CHUNK, CHUNK)), full((CHUNK, 2 * CHUNK)),
        ],
        out_specs=pl.BlockSpec((t_blk, width), lambda b, i: (b * nb + i, 0)),
        out_shape=jax.ShapeDtypeStruct((bsz * seq, width), BF16),
        scratch_shapes=[
            pltpu.VMEM((t_blk + 8, cdim), F32),
            pltpu.VMEM((CHUNK + 8, cdim), F32),
            pltpu.VMEM((N_SSD_GROUPS, D_STATE, width // N_SSD_GROUPS), F32),
        ],
        compiler_params=_cparams(("parallel", "arbitrary")),
        name="ssd",
    )(xbc, vz, dt, dtt, xbc_m, dt_m, dtt_m, conv_w, conv_b.reshape(1, cdim), pad_row(dt_bias),
      dt_bias.reshape(nh, 1), pad_row(a_log), a_log.reshape(nh, 1), jnp.repeat(d_skip, SSD_HEAD_DIM).reshape(1, width),
      norm_w.reshape(1, width), expand, tril, triu2)


def _outproj_kernel(h_ref, a_ref, s_ref, wa_ref, ws_ref, out_ref):
    out_ref[...] = (h_ref[...]
                    + jnp.dot(a_ref[...], wa_ref[...], preferred_element_type=F32)
                    + jnp.dot(s_ref[...], ws_ref[...], preferred_element_type=F32))


def _outproj(h1, attn, ssd, wa, ws, *, tm, tn):
    m, d = h1.shape
    k = attn.shape[1]
    return pl.pallas_call(
        _outproj_kernel,
        grid=(m // tm, d // tn),
        in_specs=[
            pl.BlockSpec((tm, tn), lambda i, j: (i, j)),
            pl.BlockSpec((tm, k), lambda i, j: (i, 0)),
            pl.BlockSpec((tm, k), lambda i, j: (i, 0)),
            pl.BlockSpec((k, tn), lambda i, j: (0, j)),
            pl.BlockSpec((k, tn), lambda i, j: (0, j)),
        ],
        out_specs=pl.BlockSpec((tm, tn), lambda i, j: (i, j)),
        out_shape=jax.ShapeDtypeStruct((m, d), F32),
        compiler_params=_cparams(("parallel", "arbitrary")),
        name="outproj",
    )(h1, attn, ssd, wa, ws)


def _rope_tables(n_pos):
    inv = jnp.power(ROPE_THETA, -jnp.arange(0, ROT_DIM, 2, dtype=F32) / ROT_DIM)
    ang = jnp.arange(n_pos, dtype=F32)[:, None] * inv[None, :]
    cos, sin = jnp.cos(ang), jnp.sin(ang)
    half = ROT_DIM // 2
    r = np.arange(LANES) % DIFF_HEAD_DIM
    idx = jnp.asarray(r % half)
    lo = jnp.asarray(r < half)[None, :]
    hi = jnp.asarray((r >= half) & (r < ROT_DIM))[None, :]
    cos_t = jnp.where(lo | hi, cos[:, idx], 1.0)
    sin_lo = jnp.where(lo, -sin[:, idx], 0.0)
    sin_hi = jnp.where(hi, sin[:, idx], 0.0)
    return jnp.stack([cos_t, sin_lo, sin_hi])


def kernel(x, meta_tokens, ffn1_norm, ffn1_w_gate, ffn1_w_up, ffn1_w_down, mix_norm, w_in, q_norm, k_norm,
           lambda_q1, lambda_k1, lambda_q2, lambda_k2, attn_out_norm, conv_w, conv_b, dt_bias, a_log, d_skip,
           ssd_norm, w_out, ffn2_norm, ffn2_w_gate, ffn2_w_up, ffn2_w_down):
    bsz, seq, d = x.shape
    assert ffn1_norm.shape[0] == 1, "single-layer block"
    aw = N_DIFF_HEADS * V_HEAD_DIM
    sw = N_SSD_HEADS * SSD_HEAD_DIM
    cdim = sw + 2 * N_SSD_GROUPS * D_STATE
    lam_init = 0.8 - 0.6 * math.exp(-0.3 * 0)
    row = lambda v: v.reshape(1, -1)

    w_in0 = w_in[0]
    w_qk = w_in0[:, :2 * aw].astype(BF16)
    w_vz = w_in0[:, 2 * aw:3 * aw + sw].astype(BF16)
    w_xbc = w_in0[:, 3 * aw + sw:3 * aw + sw + cdim].astype(BF16)
    w_dt = w_in0[:, 3 * aw + sw + cdim:].astype(BF16)
    w_dt_pad = jnp.pad(w_dt, ((0, 0), (0, LANES - N_SSD_HEADS)))
    w_dt_t = w_dt.T

    n_sub = aw // DIFF_HEAD_DIM
    q_gain = jnp.tile(q_norm[0], n_sub) * (DIFF_HEAD_DIM ** -0.5)
    k_gain = jnp.tile(k_norm[0], n_sub)
    tn_qk = 512
    qk_gain = jnp.concatenate([q_gain, k_gain]).reshape(2 * aw // tn_qk, 1, tn_qk)
    rope = _rope_tables(N_META + seq)
    gi = np.arange(LANES) // DIFF_HEAD_DIM
    bd = jnp.asarray((gi[:, None] == gi[None, :]).astype(np.float32) / DIFF_HEAD_DIM, dtype=BF16)

    def ffn_weights(g, u, dn):
        return g[0].astype(BF16), u[0].astype(BF16), dn[0].astype(BF16)

    wg1, wu1, wd1 = ffn_weights(ffn1_w_gate, ffn1_w_up, ffn1_w_down)
    wg2, wu2, wd2 = ffn_weights(ffn2_w_gate, ffn2_w_up, ffn2_w_down)

    x2 = x.reshape(bsz * seq, d)
    h1, hn = _ffn(x2, row(ffn1_norm[0]), wg1, wu1, wd1, row(mix_norm[0]), tm=512, tf=512)
    _, hn_m = _ffn(meta_tokens.astype(F32), row(ffn1_norm[0]), wg1, wu1, wd1, row(mix_norm[0]), tm=N_META, tf=512)

    qk = _proj_qk(hn, w_qk, qk_gain, rope[:, N_META:], bd, tm=1024, tn=tn_qk)
    qk_m = _proj_qk(hn_m, w_qk, qk_gain, rope[:, :N_META], bd, tm=N_META, tn=tn_qk)
    vz = _proj(hn, w_vz, tm=1024, tn=512, name="proj_vz")
    vz_m = _proj(hn_m, w_vz, tm=N_META, tn=512, name="proj_vz_meta")
    xbc = _proj(hn, w_xbc, tm=1024, tn=512, name="proj_xbc")
    xbc_m = _proj(hn_m, w_xbc, tm=N_META, tn=512, name="proj_xbc_meta")
    dt, dtt = _proj_dt(hn, w_dt_pad, w_dt_t, tm=1024)
    dt_m, dtt_m = _proj_dt(hn_m, w_dt_pad, w_dt_t, tm=N_META)

    kmeta = jnp.pad(qk_m[:, aw:], ((0, LANES - N_META), (0, 0)))
    vmeta = jnp.pad(vz_m[:, :aw], ((0, LANES - N_META), (0, 0)))
    lam_vecs = jnp.stack([lambda_q1[0], lambda_k1[0], lambda_q2[0], lambda_k2[0]]).astype(F32)
    attn = _attention(qk, vz, kmeta, vmeta, lam_vecs, row(attn_out_norm[0]),
                      bsz=bsz, seq=seq, tq=512, lam_init=lam_init)

    lead = CHUNK - N_META
    ssd = _ssd(xbc, vz, dt, dtt,
               jnp.pad(xbc_m, ((lead, 0), (0, 0))), jnp.pad(dt_m, ((lead, 0), (0, 0))),
               jnp.pad(dtt_m, ((0, 0), (lead, 0))),
               conv_w[0], conv_b[0], dt_bias[0], a_log[0], d_skip[0], ssd_norm[0],
               bsz=bsz, seq=seq, t_blk=256)

    w_o = w_out[0].astype(BF16)
    h2 = _outproj(h1, attn, ssd, w_o[:aw], w_o[aw:], tm=1024, tn=512)
    out, _ = _ffn(h2, row(ffn2_norm[0]), wg2, wu2, wd2, row(ffn2_norm[0]), tm=512, tf=512)
    return out.reshape(bsz, seq, d)
```

```python
import functools
import math

import jax
import jax.numpy as jnp
import numpy as np
from jax import lax
from jax.experimental import pallas as pl
from jax.experimental.pallas import tpu as pltpu

F32 = jnp.float32
BF16 = jnp.bfloat16
HIGHEST = lax.Precision.HIGHEST

EPS = 1e-6
CHUNK = 64
N_META = 16
N_DIFF_HEADS = 8
DIFF_HEAD_DIM = 64
V_HEAD_DIM = 128
ROT_DIM = 16
ROPE_THETA = 500000.0
SSD_HEAD_DIM = 64
N_SSD_HEADS = 16
N_SSD_GROUPS = 2
D_STATE = 128
CONV_WIDTH = 4
LANES = 128
VMEM_LIMIT = 56 * 1024 * 1024


def _cparams(sem):
    return pltpu.CompilerParams(dimension_semantics=sem, vmem_limit_bytes=VMEM_LIMIT)


def _silu(v):
    return v * (1.0 / (1.0 + jnp.exp(-v)))


def _rms(v, w):
    ms = jnp.mean(v * v, axis=-1, keepdims=True)
    return v * lax.rsqrt(ms + EPS) * w


def _ffn_kernel(x_ref, nw_ref, wg_ref, wu_ref, wd_ref, pnw_ref, out_ref, hn_ref, xn_sc, acc_sc):
    j = pl.program_id(1)

    @pl.when(j == 0)
    def _():
        xn_sc[...] = _rms(x_ref[...], nw_ref[...]).astype(BF16)
        acc_sc[...] = jnp.zeros_like(acc_sc)

    xn = xn_sc[...]
    g = jnp.dot(xn, wg_ref[...], preferred_element_type=F32)
    u = jnp.dot(xn, wu_ref[...], preferred_element_type=F32)
    a = (_silu(g) * u).astype(BF16)
    acc_sc[...] += jnp.dot(a, wd_ref[...], preferred_element_type=F32)

    @pl.when(j == pl.num_programs(1) - 1)
    def _():
        h = x_ref[...] + 0.5 * acc_sc[...]
        out_ref[...] = h
        hn_ref[...] = _rms(h, pnw_ref[...]).astype(BF16)


def _ffn(x2, nw, wg, wu, wd, post_nw, *, tm, tf):
    m, d = x2.shape
    f = wg.shape[1]
    return pl.pallas_call(
        _ffn_kernel,
        grid=(m // tm, f // tf),
        in_specs=[
            pl.BlockSpec((tm, d), lambda i, j: (i, 0)),
            pl.BlockSpec((1, d), lambda i, j: (0, 0)),
            pl.BlockSpec((d, tf), lambda i, j: (0, j)),
            pl.BlockSpec((d, tf), lambda i, j: (0, j)),
            pl.BlockSpec((tf, d), lambda i, j: (j, 0)),
            pl.BlockSpec((1, d), lambda i, j: (0, 0)),
        ],
        out_specs=[
            pl.BlockSpec((tm, d), lambda i, j: (i, 0)),
            pl.BlockSpec((tm, d), lambda i, j: (i, 0)),
        ],
        out_shape=[jax.ShapeDtypeStruct((m, d), F32), jax.ShapeDtypeStruct((m, d), BF16)],
        scratch_shapes=[pltpu.VMEM((tm, d), BF16), pltpu.VMEM((tm, d), F32)],
        compiler_params=_cparams(("parallel", "arbitrary")),
        name="ffn",
    )(x2, nw, wg, wu, wd, post_nw)


def _proj_kernel(x_ref, w_ref, out_ref):
    out_ref[...] = jnp.dot(x_ref[...], w_ref[...], preferred_element_type=F32).astype(out_ref.dtype)


def _proj(x2, w, *, tm, tn, out_dtype=BF16, name="proj"):
    m, d = x2.shape
    n = w.shape[1]
    return pl.pallas_call(
        _proj_kernel,
        grid=(m // tm, n // tn),
        in_specs=[pl.BlockSpec((tm, d), lambda i, j: (i, 0)), pl.BlockSpec((d, tn), lambda i, j: (0, j))],
        out_specs=pl.BlockSpec((tm, tn), lambda i, j: (i, j)),
        out_shape=jax.ShapeDtypeStruct((m, n), out_dtype),
        compiler_params=_cparams(("parallel", "arbitrary")),
        name=name,
    )(x2, w)


def _proj_t_kernel(w_ref, x_ref, out_ref):
    out_ref[...] = lax.dot_general(w_ref[...], x_ref[...], (((1,), (1,)), ((), ())),
                                   preferred_element_type=F32).astype(out_ref.dtype)


def _proj_t(w_t, x2, *, tn, tm, name):
    n, d = w_t.shape
    m = x2.shape[0]
    return pl.pallas_call(
        _proj_t_kernel,
        grid=(m // tm, n // tn),
        in_specs=[pl.BlockSpec((tn, d), lambda i, j: (j, 0)), pl.BlockSpec((tm, d), lambda i, j: (i, 0))],
        out_specs=pl.BlockSpec((tn, tm), lambda i, j: (j, i)),
        out_shape=jax.ShapeDtypeStruct((n, m), BF16),
        compiler_params=_cparams(("parallel", "arbitrary")),
        name=name,
    )(w_t, x2)


def _proj_qk_kernel(x_ref, w_ref, gain_ref, rope_ref, bd_ref, out_ref):
    acc = jnp.dot(x_ref[...], w_ref[...], preferred_element_type=F32)
    tn = acc.shape[1]
    cos, sin_lo, sin_hi = rope_ref[0], rope_ref[1], rope_ref[2]
    bd = bd_ref[...]
    for c in range(tn // LANES):
        y = acc[:, c * LANES:(c + 1) * LANES]
        y2 = y * y
        hi = y2.astype(BF16)
        lo = (y2 - hi.astype(F32)).astype(BF16)
        ms = jnp.dot(hi, bd, preferred_element_type=F32) + jnp.dot(lo, bd, preferred_element_type=F32)
        yn = y * lax.rsqrt(ms + EPS) * gain_ref[0][:, c * LANES:(c + 1) * LANES]
        half = ROT_DIM // 2
        r = yn * cos + pltpu.roll(yn, half, 1) * sin_hi + pltpu.roll(yn, LANES - half, 1) * sin_lo
        out_ref[:, c * LANES:(c + 1) * LANES] = r.astype(out_ref.dtype)


def _proj_qk(x2, w, gain, rope, bd, *, tm, tn):
    m, d = x2.shape
    n = w.shape[1]
    n_tab = rope.shape[1] // tm
    return pl.pallas_call(
        _proj_qk_kernel,
        grid=(m // tm, n // tn),
        in_specs=[
            pl.BlockSpec((tm, d), lambda i, j: (i, 0)),
            pl.BlockSpec((d, tn), lambda i, j: (0, j)),
            pl.BlockSpec((1, 1, tn), lambda i, j: (j, 0, 0)),
            pl.BlockSpec((3, tm, LANES), lambda i, j: (0, i % n_tab, 0)),
            pl.BlockSpec((LANES, LANES), lambda i, j: (0, 0)),
        ],
        out_specs=pl.BlockSpec((tm, tn), lambda i, j: (i, j)),
        out_shape=jax.ShapeDtypeStruct((m, n), BF16),
        compiler_params=_cparams(("parallel", "arbitrary")),
        name="proj_qk",
    )(x2, w, gain, rope, bd)


def _proj_dt_kernel(x_ref, w_ref, wt_ref, out_ref, outt_ref):
    x = x_ref[...]
    out_ref[...] = jnp.dot(x, w_ref[...], preferred_element_type=F32)
    outt_ref[...] = lax.dot_general(wt_ref[...], x, (((1,), (1,)), ((), ())), preferred_element_type=F32)


def _proj_dt(x2, w_pad, w_t, *, tm):
    m, d = x2.shape
    nh = w_t.shape[0]
    return pl.pallas_call(
        _proj_dt_kernel,
        grid=(m // tm,),
        in_specs=[
            pl.BlockSpec((tm, d), lambda i: (i, 0)),
            pl.BlockSpec((d, LANES), lambda i: (0, 0)),
            pl.BlockSpec((nh, d), lambda i: (0, 0)),
        ],
        out_specs=[pl.BlockSpec((tm, LANES), lambda i: (i, 0)), pl.BlockSpec((nh, tm), lambda i: (0, i))],
        out_shape=[jax.ShapeDtypeStruct((m, LANES), F32), jax.ShapeDtypeStruct((nh, m), F32)],
        compiler_params=_cparams(("parallel",)),
        name="proj_dt",
    )(x2, w_pad, w_t)


def _attn_kernel(qi_tab, ki_tab, q_ref, k_ref, vt_ref, km_ref, vmt_ref, lam_ref, gain_ref, out_ref,
                 qs_sc, m_sc, l_sc, acc_sc, *, tq, lam_init):
    s_idx = pl.program_id(2)
    qi = qi_tab[s_idx]
    ki = ki_tab[s_idx]
    nt = (((1,), (1,)), ((), ()))

    def online_update(s_t, vt):
        m_prev = m_sc[...]
        m_new = jnp.maximum(m_prev, jnp.max(s_t, axis=0, keepdims=True))
        alpha = jnp.exp(m_prev - m_new)
        p = jnp.exp(s_t - m_new)
        l_sc[...] = alpha * l_sc[...] + jnp.sum(p, axis=0, keepdims=True)
        acc_sc[...] = alpha * acc_sc[...] + jnp.dot(vt, p.astype(BF16), preferred_element_type=F32)
        m_sc[...] = m_new

    @pl.when(ki == 0)
    def _():
        q = q_ref[...]
        lane = lax.broadcasted_iota(jnp.int32, q.shape, 1)
        zero = jnp.zeros_like(q)
        qs_sc[0:tq, :] = jnp.where(lane < DIFF_HEAD_DIM, q, zero)
        qs_sc[tq:2 * tq, :] = jnp.where(lane >= DIFF_HEAD_DIM, q, zero)
        s_t = lax.dot_general(km_ref[...], qs_sc[...], nt, preferred_element_type=F32)
        key = lax.broadcasted_iota(jnp.int32, s_t.shape, 0)
        s_t = jnp.where(key < N_META, s_t, -jnp.inf)
        m0 = jnp.max(s_t, axis=0, keepdims=True)
        p = jnp.exp(s_t - m0)
        m_sc[...] = m0
        l_sc[...] = jnp.sum(p, axis=0, keepdims=True)
        acc_sc[...] = jnp.dot(vmt_ref[...], p.astype(BF16), preferred_element_type=F32)

    @pl.when(ki < qi)
    def _():
        s_t = lax.dot_general(k_ref[...], qs_sc[...], nt, preferred_element_type=F32)
        online_update(s_t, vt_ref[...])

    @pl.when(ki == qi)
    def _():
        s_t = lax.dot_general(k_ref[...], qs_sc[...], nt, preferred_element_type=F32)
        key = lax.broadcasted_iota(jnp.int32, s_t.shape, 0)
        qry = lax.broadcasted_iota(jnp.int32, s_t.shape, 1)
        s_t = jnp.where(key // CHUNK <= (qry & (tq - 1)) // CHUNK, s_t, -jnp.inf)
        online_update(s_t, vt_ref[...])
        lq1, lk1, lq2, lk2 = lam_ref[0:1, :], lam_ref[1:2, :], lam_ref[2:3, :], lam_ref[3:4, :]
        lam = (jnp.exp(jnp.sum(lq1 * lk1, axis=-1, keepdims=True))
               - jnp.exp(jnp.sum(lq2 * lk2, axis=-1, keepdims=True)) + lam_init)
        o = acc_sc[...] / l_sc[...]
        o = o[:, 0:tq] - lam * o[:, tq:2 * tq]
        ms = jnp.mean(o * o, axis=0, keepdims=True)
        o = o * lax.rsqrt(ms + EPS) * (gain_ref[...] * (1.0 - lam_init))
        out_ref[...] = o.T.astype(out_ref.dtype)


def _attention(qk, vt, kmeta, vmeta_t, lam_vecs, gain_col, *, bsz, seq, tq, lam_init):
    nq = seq // tq
    h = N_DIFF_HEADS
    qi_list, ki_list = [], []
    for a in range(nq):
        for b in range(a + 1):
            qi_list.append(a)
            ki_list.append(b)
    qi_tab = jnp.asarray(np.array(qi_list, np.int32))
    ki_tab = jnp.asarray(np.array(ki_list, np.int32))
    n_steps = len(qi_list)
    grid_spec = pltpu.PrefetchScalarGridSpec(
        num_scalar_prefetch=2,
        grid=(bsz, h, n_steps),
        in_specs=[
            pl.BlockSpec((tq, V_HEAD_DIM), lambda b, hh, s, qt, kt: (b * nq + qt[s], hh)),
            pl.BlockSpec((tq, V_HEAD_DIM), lambda b, hh, s, qt, kt: (b * nq + kt[s], h + hh)),
            pl.BlockSpec((V_HEAD_DIM, tq), lambda b, hh, s, qt, kt: (hh, b * nq + kt[s])),
            pl.BlockSpec((LANES, V_HEAD_DIM), lambda b, hh, s, qt, kt: (0, hh)),
            pl.BlockSpec((V_HEAD_DIM, LANES), lambda b, hh, s, qt, kt: (hh, 0)),
            pl.BlockSpec((4, DIFF_HEAD_DIM), lambda b, hh, s, qt, kt: (0, 0)),
            pl.BlockSpec((V_HEAD_DIM, 1), lambda b, hh, s, qt, kt: (0, 0)),
        ],
        out_specs=pl.BlockSpec((tq, V_HEAD_DIM), lambda b, hh, s, qt, kt: (b * nq + qt[s], hh)),
        scratch_shapes=[
            pltpu.VMEM((2 * tq, V_HEAD_DIM), BF16),
            pltpu.VMEM((1, 2 * tq), F32),
            pltpu.VMEM((1, 2 * tq), F32),
            pltpu.VMEM((V_HEAD_DIM, 2 * tq), F32),
        ],
    )
    return pl.pallas_call(
        functools.partial(_attn_kernel, tq=tq, lam_init=lam_init),
        grid_spec=grid_spec,
        out_shape=jax.ShapeDtypeStruct((bsz * seq, h * V_HEAD_DIM), BF16),
        compiler_params=_cparams(("parallel", "parallel", "arbitrary")),
        name="attn",
    )(qi_tab, ki_tab, qk, qk, vt, kmeta, vmeta_t, lam_vecs, gain_col)


def _ssd_kernel(xbc_ref, z_ref, dt_ref, dtt_ref, xbcm_ref, dtm_ref, dttm_ref,
                convw_ref, convb_ref, dtb_ref, dtbt_ref, alog_ref, alogt_ref, dskip_ref, nw_ref,
                expand_ref, tril_ref, triu2_ref, out_ref, xs_sc, xm_sc, state_sc, *, t_blk):
    n_heads, p_dim, n_state, n_groups = N_SSD_HEADS, SSD_HEAD_DIM, D_STATE, N_SSD_GROUPS
    width = n_heads * p_dim
    gw = width // n_groups
    n_pairs = n_heads // 2
    nt = (((1,), (1,)), ((), ()))
    expand = expand_ref[...]
    a_row = -jnp.exp(alog_ref[...])
    a_col = -jnp.exp(alogt_ref[...])
    d_full = dskip_ref[...]
    lane = lax.broadcasted_iota(jnp.int32, (CHUNK, LANES), 1)
    row = lax.broadcasted_iota(jnp.int32, (CHUNK, LANES), 0)
    left = lane < p_dim
    causal2 = (lane & (CHUNK - 1)) <= row

    def chunk(src_ref, r0, dt_raw, dtt_raw, pad_rows, z, out_r0):
        conv = convb_ref[...]
        for k in range(CONV_WIDTH):
            conv = conv + src_ref[pl.ds(r0 - (CONV_WIDTH - 1) + k, CHUNK), :] * convw_ref[k:k + 1, :]
        xc = _silu(conv)
        x_s = xc[:, :width]
        dt = jax.nn.softplus(dt_raw + dtb_ref[...])
        dtt = jax.nn.softplus(dtt_raw + dtbt_ref[...])
        if pad_rows:
            dt = jnp.where(row >= pad_rows, dt, 0.0)
            dtt = jnp.where(lax.broadcasted_iota(jnp.int32, dtt.shape, 1) >= pad_rows, dtt, 0.0)
        a_cs = jnp.dot(tril_ref[...], dt * a_row, precision=HIGHEST, preferred_element_type=F32)
        a_cs_t2 = jnp.dot(dtt * a_col, triu2_ref[...], precision=HIGHEST, preferred_element_type=F32)
        a_full = jnp.dot(a_cs, expand, precision=HIGHEST, preferred_element_type=F32)
        dt_full = jnp.dot(dt, expand, precision=HIGHEST, preferred_element_type=F32)
        a_last = a_full[CHUNK - 1:CHUNK, :]
        xdt = x_s * dt_full
        w_state = (xdt * jnp.exp(a_last - a_full)).astype(BF16)
        xdt_b = xdt.astype(BF16)
        zero_b = jnp.zeros((CHUNK, LANES), BF16)
        y_parts = []
        for g in range(n_groups):
            b_g = xc[:, width + g * n_state: width + (g + 1) * n_state]
            c_g = xc[:, width + (n_groups + g) * n_state: width + (n_groups + g + 1) * n_state]
            b_gb = b_g.astype(BF16)
            c_gb = c_g.astype(BF16)
            st = state_sc[g]
            if out_r0 is not None:
                cb2 = lax.dot_general(c_gb, jnp.concatenate([b_gb, b_gb], axis=0), nt,
                                      preferred_element_type=F32)
                y_off = jnp.dot(c_gb, st.astype(BF16), preferred_element_type=F32)
                y_g = y_off * jnp.exp(a_full[:, g * gw:(g + 1) * gw])
                diag = []
                for pp in range(n_pairs // n_groups):
                    pr = g * (n_pairs // n_groups) + pp
                    colb = a_full[:, pr * LANES:(pr + 1) * LANES]
                    rowb = jnp.where(left, a_cs_t2[2 * pr:2 * pr + 1, :], a_cs_t2[2 * pr + 1:2 * pr + 2, :])
                    dec = jnp.exp(jnp.where(causal2, colb - rowb, -jnp.inf))
                    m_pair = (cb2 * dec).astype(BF16)
                    xp = xdt_b[:, pr * LANES:(pr + 1) * LANES]
                    rhs = jnp.concatenate([jnp.where(left, xp, zero_b), jnp.where(left, zero_b, xp)], axis=0)
                    diag.append(jnp.dot(m_pair, rhs, preferred_element_type=F32))
                y_parts.append(y_g + jnp.concatenate(diag, axis=1))
            new_st = st * jnp.exp(a_last[:, g * gw:(g + 1) * gw]) + jnp.dot(
                b_g.T.astype(BF16), w_state[:, g * gw:(g + 1) * gw], preferred_element_type=F32)
            state_sc[g] = new_st
        if out_r0 is not None:
            y = jnp.concatenate(y_parts, axis=1) + x_s * d_full
            gated = y * _silu(z)
            out_ref[pl.ds(out_r0, CHUNK), :] = _rms(gated, nw_ref[...]).astype(out_ref.dtype)

    @pl.when(pl.program_id(1) == 0)
    def _():
        state_sc[...] = jnp.zeros_like(state_sc)
        xm_sc[0:8, :] = jnp.zeros((8, xm_sc.shape[1]), F32)
        xm_sc[8:8 + CHUNK, :] = xbcm_ref[...].astype(F32)
        chunk(xm_sc, 8, dtm_ref[...], dttm_ref[...], CHUNK - N_META, None, None)
        xs_sc[0:8, :] = xm_sc[CHUNK:CHUNK + 8, :]

    xs_sc[8:8 + t_blk, :] = xbc_ref[...].astype(F32)
    for c in range(t_blk // CHUNK):
        chunk(xs_sc, 8 + c * CHUNK, dt_ref[c * CHUNK:(c + 1) * CHUNK, :], dtt_ref[:, c * CHUNK:(c + 1) * CHUNK],
              0, z_ref[c * CHUNK:(c + 1) * CHUNK, :].astype(F32), c * CHUNK)
    xs_sc[0:8, :] = xs_sc[t_blk:t_blk + 8, :]


def _ssd(xbc, z, dt, dtt, xbc_m, dt_m, dtt_m, conv_w, conv_b, dt_bias, a_log, d_skip, norm_w,
         *, bsz, seq, t_blk):
    nb = seq // t_blk
    cdim = xbc.shape[1]
    width = N_SSD_HEADS * SSD_HEAD_DIM
    nh = N_SSD_HEADS

    def pad_row(v):
        return jnp.pad(v.reshape(1, nh), ((0, 0), (0, LANES - nh)))

    hh = np.arange(LANES)[:, None]
    ll = np.arange(width)[None, :]
    expand = jnp.asarray((ll // SSD_HEAD_DIM == hh).astype(np.float32))
    ii = np.arange(CHUNK)
    tril = jnp.asarray((ii[None, :] <= ii[:, None]).astype(np.float32))
    triu2 = jnp.asarray(np.tile((ii[:, None] <= ii[None, :]).astype(np.float32), (1, 2)))
    full = lambda shape: pl.BlockSpec(shape, lambda b, i: tuple(0 for _ in shape))
    return pl.pallas_call(
        functools.partial(_ssd_kernel, t_blk=t_blk),
        grid=(bsz, nb),
        in_specs=[
            pl.BlockSpec((t_blk, cdim), lambda b, i: (b * nb + i, 0)),
            pl.BlockSpec((t_blk, width), lambda b, i: (b * nb + i, 0)),
            pl.BlockSpec((t_blk, LANES), lambda b, i: (b * nb + i, 0)),
            pl.BlockSpec((nh, t_blk), lambda b, i: (0, b * nb + i)),
            full((CHUNK, cdim)), full((CHUNK, LANES)), full((nh, CHUNK)),
            full((CONV_WIDTH, cdim)), full((1, cdim)), full((1, LANES)), full((nh, 1)),
            full((1, LANES)), full((nh, 1)), full((1, width)), full((1, width)),
            full((LANES, width)), full((CHUNK, CHUNK)), full((CHUNK, 2 * CHUNK)),
        ],
        out_specs=pl.BlockSpec((t_blk, width), lambda b, i: (b * nb + i, 0)),
        out_shape=jax.ShapeDtypeStruct((bsz * seq, width), BF16),
        scratch_shapes=[
            pltpu.VMEM((t_blk + 8, cdim), F32),
            pltpu.VMEM((CHUNK + 8, cdim), F32),
            pltpu.VMEM((N_SSD_GROUPS, D_STATE, width // N_SSD_GROUPS), F32),
        ],
        compiler_params=_cparams(("parallel", "arbitrary")),
        name="ssd",
    )(xbc, z, dt, dtt, xbc_m, dt_m, dtt_m, conv_w, conv_b.reshape(1, cdim), pad_row(dt_bias),
      dt_bias.reshape(nh, 1), pad_row(a_log), a_log.reshape(nh, 1), jnp.repeat(d_skip, SSD_HEAD_DIM).reshape(1, width),
      norm_w.reshape(1, width), expand, tril, triu2)


def _outproj_kernel(h_ref, a_ref, s_ref, wa_ref, ws_ref, out_ref):
    out_ref[...] = (h_ref[...]
                    + jnp.dot(a_ref[...], wa_ref[...], preferred_element_type=F32)
                    + jnp.dot(s_ref[...], ws_ref[...], preferred_element_type=F32))


def _outproj(h1, attn, ssd, wa, ws, *, tm, tn):
    m, d = h1.shape
    k = attn.shape[1]
    return pl.pallas_call(
        _outproj_kernel,
        grid=(m // tm, d // tn),
        in_specs=[
            pl.BlockSpec((tm, tn), lambda i, j: (i, j)),
            pl.BlockSpec((tm, k), lambda i, j: (i, 0)),
            pl.BlockSpec((tm, k), lambda i, j: (i, 0)),
            pl.BlockSpec((k, tn), lambda i, j: (0, j)),
            pl.BlockSpec((k, tn), lambda i, j: (0, j)),
        ],
        out_specs=pl.BlockSpec((tm, tn), lambda i, j: (i, j)),
        out_shape=jax.ShapeDtypeStruct((m, d), F32),
        compiler_params=_cparams(("parallel", "arbitrary")),
        name="outproj",
    )(h1, attn, ssd, wa, ws)


def _rope_tables(n_pos):
    inv = jnp.power(ROPE_THETA, -jnp.arange(0, ROT_DIM, 2, dtype=F32) / ROT_DIM)
    ang = jnp.arange(n_pos, dtype=F32)[:, None] * inv[None, :]
    cos, sin = jnp.cos(ang), jnp.sin(ang)
    half = ROT_DIM // 2
    r = np.arange(LANES) % DIFF_HEAD_DIM
    idx = jnp.asarray(r % half)
    lo = jnp.asarray(r < half)[None, :]
    hi = jnp.asarray((r >= half) & (r < ROT_DIM))[None, :]
    cos_t = jnp.where(lo | hi, cos[:, idx], 1.0)
    sin_lo = jnp.where(lo, -sin[:, idx], 0.0)
    sin_hi = jnp.where(hi, sin[:, idx], 0.0)
    return jnp.stack([cos_t, sin_lo, sin_hi])


def kernel(x, meta_tokens, ffn1_norm, ffn1_w_gate, ffn1_w_up, ffn1_w_down, mix_norm, w_in, q_norm, k_norm,
           lambda_q1, lambda_k1, lambda_q2, lambda_k2, attn_out_norm, conv_w, conv_b, dt_bias, a_log, d_skip,
           ssd_norm, w_out, ffn2_norm, ffn2_w_gate, ffn2_w_up, ffn2_w_down):
    bsz, seq, d = x.shape
    assert ffn1_norm.shape[0] == 1, "single-layer block"
    aw = N_DIFF_HEADS * V_HEAD_DIM
    sw = N_SSD_HEADS * SSD_HEAD_DIM
    cdim = sw + 2 * N_SSD_GROUPS * D_STATE
    lam_init = 0.8 - 0.6 * math.exp(-0.3 * 0)
    row = lambda v: v.reshape(1, -1)

    w_in0 = w_in[0]
    w_qk = w_in0[:, :2 * aw].astype(BF16)
    w_v_t = w_in0[:, 2 * aw:3 * aw].T.astype(BF16)
    w_z = w_in0[:, 3 * aw:3 * aw + sw].astype(BF16)
    w_xbc = w_in0[:, 3 * aw + sw:3 * aw + sw + cdim].astype(BF16)
    w_dt = w_in0[:, 3 * aw + sw + cdim:].astype(BF16)
    w_dt_pad = jnp.pad(w_dt, ((0, 0), (0, LANES - N_SSD_HEADS)))
    w_dt_t = w_dt.T

    n_sub = aw // DIFF_HEAD_DIM
    q_gain = jnp.tile(q_norm[0], n_sub) * (DIFF_HEAD_DIM ** -0.5)
    k_gain = jnp.tile(k_norm[0], n_sub)
    tn_qk = 512
    qk_gain = jnp.concatenate([q_gain, k_gain]).reshape(2 * aw // tn_qk, 1, tn_qk)
    rope = _rope_tables(N_META + seq)
    gi = np.arange(LANES) // DIFF_HEAD_DIM
    bd = jnp.asarray((gi[:, None] == gi[None, :]).astype(np.float32) / DIFF_HEAD_DIM, dtype=BF16)

    def ffn_weights(g, u, dn):
        return g[0].astype(BF16), u[0].astype(BF16), dn[0].astype(BF16)

    wg1, wu1, wd1 = ffn_weights(ffn1_w_gate, ffn1_w_up, ffn1_w_down)
    wg2, wu2, wd2 = ffn_weights(ffn2_w_gate, ffn2_w_up, ffn2_w_down)

    x2 = x.reshape(bsz * seq, d)
    h1, hn = _ffn(x2, row(ffn1_norm[0]), wg1, wu1, wd1, row(mix_norm[0]), tm=512, tf=512)
    _, hn_m = _ffn(meta_tokens.astype(F32), row(ffn1_norm[0]), wg1, wu1, wd1, row(mix_norm[0]), tm=N_META, tf=512)

    qk = _proj_qk(hn, w_qk, qk_gain, rope[:, N_META:], bd, tm=1024, tn=tn_qk)
    qk_m = _proj_qk(hn_m, w_qk, qk_gain, rope[:, :N_META], bd, tm=N_META, tn=tn_qk)
    v_t = _proj_t(w_v_t, hn, tn=512, tm=1024, name="proj_vt")
    v_t_m = _proj_t(w_v_t, hn_m, tn=512, tm=N_META, name="proj_vt_meta")
    z = _proj(hn, w_z, tm=1024, tn=512, name="proj_z")
    xbc = _proj(hn, w_xbc, tm=1024, tn=512, name="proj_xbc")
    xbc_m = _proj(hn_m, w_xbc, tm=N_META, tn=512, name="proj_xbc_meta")
    dt, dtt = _proj_dt(hn, w_dt_pad, w_dt_t, tm=1024)
    dt_m, dtt_m = _proj_dt(hn_m, w_dt_pad, w_dt_t, tm=N_META)

    kmeta = jnp.pad(qk_m[:, aw:], ((0, LANES - N_META), (0, 0)))
    vmeta_t = jnp.pad(v_t_m, ((0, 0), (0, LANES - N_META)))
    lam_vecs = jnp.stack([lambda_q1[0], lambda_k1[0], lambda_q2[0], lambda_k2[0]]).astype(F32)
    attn = _attention(qk, v_t, kmeta, vmeta_t, lam_vecs, attn_out_norm[0].reshape(V_HEAD_DIM, 1),
                      bsz=bsz, seq=seq, tq=512, lam_init=lam_init)

    lead = CHUNK - N_META
    ssd = _ssd(xbc, z, dt, dtt,
               jnp.pad(xbc_m, ((lead, 0), (0, 0))), jnp.pad(dt_m, ((lead, 0), (0, 0))),
               jnp.pad(dtt_m, ((0, 0), (lead, 0))),
               conv_w[0], conv_b[0], dt_bias[0], a_log[0], d_skip[0], ssd_norm[0],
               bsz=bsz, seq=seq, t_blk=256)

    w_o = w_out[0].astype(BF16)
    h2 = _outproj(h1, attn, ssd, w_o[:aw], w_o[aw:], tm=1024, tn=512)
    out, _ = _ffn(h2, row(ffn2_norm[0]), wg2, wu2, wd2, row(ffn2_norm[0]), tm=512, tf=512)
    return out.reshape(bsz, seq, d)
```

```python
import functools
import math

import jax
import jax.numpy as jnp
import numpy as np
from jax import lax
from jax.experimental import pallas as pl
from jax.experimental.pallas import tpu as pltpu

F32 = jnp.float32
BF16 = jnp.bfloat16
HIGHEST = lax.Precision.HIGHEST

EPS = 1e-6
CHUNK = 64
N_META = 16
N_DIFF_HEADS = 8
DIFF_HEAD_DIM = 64
V_HEAD_DIM = 128
ROT_DIM = 16
ROPE_THETA = 500000.0
SSD_HEAD_DIM = 64
N_SSD_HEADS = 16
N_SSD_GROUPS = 2
D_STATE = 128
CONV_WIDTH = 4
LANES = 128
VMEM_LIMIT = 56 * 1024 * 1024


def _cparams(sem):
    return pltpu.CompilerParams(dimension_semantics=sem, vmem_limit_bytes=VMEM_LIMIT)


def _silu(v):
    return v * (1.0 / (1.0 + jnp.exp(-v)))


def _rms(v, w):
    ms = jnp.mean(v * v, axis=-1, keepdims=True)
    return v * lax.rsqrt(ms + EPS) * w


def _ffn_kernel(x_ref, nw_ref, wg_ref, wu_ref, wd_ref, pnw_ref, out_ref, hn_ref, xn_sc, acc_sc):
    j = pl.program_id(1)

    @pl.when(j == 0)
    def _():
        xn_sc[...] = _rms(x_ref[...], nw_ref[...]).astype(BF16)
        acc_sc[...] = jnp.zeros_like(acc_sc)

    xn = xn_sc[...]
    g = jnp.dot(xn, wg_ref[...], preferred_element_type=F32)
    u = jnp.dot(xn, wu_ref[...], preferred_element_type=F32)
    a = (_silu(g) * u).astype(BF16)
    acc_sc[...] += jnp.dot(a, wd_ref[...], preferred_element_type=F32)

    @pl.when(j == pl.num_programs(1) - 1)
    def _():
        h = x_ref[...] + 0.5 * acc_sc[...]
        out_ref[...] = h
        hn_ref[...] = _rms(h, pnw_ref[...]).astype(BF16)


def _ffn(x2, nw, wg, wu, wd, post_nw, *, tm, tf):
    m, d = x2.shape
    f = wg.shape[1]
    return pl.pallas_call(
        _ffn_kernel,
        grid=(m // tm, f // tf),
        in_specs=[
            pl.BlockSpec((tm, d), lambda i, j: (i, 0)),
            pl.BlockSpec((1, d), lambda i, j: (0, 0)),
            pl.BlockSpec((d, tf), lambda i, j: (0, j)),
            pl.BlockSpec((d, tf), lambda i, j: (0, j)),
            pl.BlockSpec((tf, d), lambda i, j: (j, 0)),
            pl.BlockSpec((1, d), lambda i, j: (0, 0)),
        ],
        out_specs=[
            pl.BlockSpec((tm, d), lambda i, j: (i, 0)),
            pl.BlockSpec((tm, d), lambda i, j: (i, 0)),
        ],
        out_shape=[jax.ShapeDtypeStruct((m, d), F32), jax.ShapeDtypeStruct((m, d), BF16)],
        scratch_shapes=[pltpu.VMEM((tm, d), BF16), pltpu.VMEM((tm, d), F32)],
        compiler_params=_cparams(("parallel", "arbitrary")),
        name="ffn",
    )(x2, nw, wg, wu, wd, post_nw)


def _proj_kernel(x_ref, w_ref, out_ref):
    out_ref[...] = jnp.dot(x_ref[...], w_ref[...], preferred_element_type=F32).astype(out_ref.dtype)


def _proj(x2, w, *, tm, tn, out_dtype=BF16, name="proj"):
    m, d = x2.shape
    n = w.shape[1]
    return pl.pallas_call(
        _proj_kernel,
        grid=(m // tm, n // tn),
        in_specs=[pl.BlockSpec((tm, d), lambda i, j: (i, 0)), pl.BlockSpec((d, tn), lambda i, j: (0, j))],
        out_specs=pl.BlockSpec((tm, tn), lambda i, j: (i, j)),
        out_shape=jax.ShapeDtypeStruct((m, n), out_dtype),
        compiler_params=_cparams(("parallel", "arbitrary")),
        name=name,
    )(x2, w)


def _proj_t_kernel(w_ref, x_ref, out_ref):
    out_ref[...] = lax.dot_general(w_ref[...], x_ref[...], (((1,), (1,)), ((), ())),
                                   preferred_element_type=F32).astype(out_ref.dtype)


def _proj_t(w_t, x2, *, tn, tm, name):
    n, d = w_t.shape
    m = x2.shape[0]
    return pl.pallas_call(
        _proj_t_kernel,
        grid=(m // tm, n // tn),
        in_specs=[pl.BlockSpec((tn, d), lambda i, j: (j, 0)), pl.BlockSpec((tm, d), lambda i, j: (i, 0))],
        out_specs=pl.BlockSpec((tn, tm), lambda i, j: (j, i)),
        out_shape=jax.ShapeDtypeStruct((n, m), BF16),
        compiler_params=_cparams(("parallel", "arbitrary")),
        name=name,
    )(w_t, x2)


def _proj_qk_kernel(x_ref, w_ref, gain_ref, rope_ref, bd_ref, out_ref):
    acc = jnp.dot(x_ref[...], w_ref[...], preferred_element_type=F32)
    tn = acc.shape[1]
    cos, sin_lo, sin_hi = rope_ref[0], rope_ref[1], rope_ref[2]
    bd = bd_ref[...]
    for c in range(tn // LANES):
        y = acc[:, c * LANES:(c + 1) * LANES]
        y2 = y * y
        hi = y2.astype(BF16)
        lo = (y2 - hi.astype(F32)).astype(BF16)
        ms = jnp.dot(hi, bd, preferred_element_type=F32) + jnp.dot(lo, bd, preferred_element_type=F32)
        yn = y * lax.rsqrt(ms + EPS) * gain_ref[0][:, c * LANES:(c + 1) * LANES]
        half = ROT_DIM // 2
        r = yn * cos + pltpu.roll(yn, half, 1) * sin_hi + pltpu.roll(yn, LANES - half, 1) * sin_lo
        out_ref[:, c * LANES:(c + 1) * LANES] = r.astype(out_ref.dtype)


def _proj_qk(x2, w, gain, rope, bd, *, tm, tn):
    m, d = x2.shape
    n = w.shape[1]
    n_tab = rope.shape[1] // tm
    return pl.pallas_call(
        _proj_qk_kernel,
        grid=(m // tm, n // tn),
        in_specs=[
            pl.BlockSpec((tm, d), lambda i, j: (i, 0)),
            pl.BlockSpec((d, tn), lambda i, j: (0, j)),
            pl.BlockSpec((1, 1, tn), lambda i, j: (j, 0, 0)),
            pl.BlockSpec((3, tm, LANES), lambda i, j: (0, i % n_tab, 0)),
            pl.BlockSpec((LANES, LANES), lambda i, j: (0, 0)),
        ],
        out_specs=pl.BlockSpec((tm, tn), lambda i, j: (i, j)),
        out_shape=jax.ShapeDtypeStruct((m, n), BF16),
        compiler_params=_cparams(("parallel", "arbitrary")),
        name="proj_qk",
    )(x2, w, gain, rope, bd)


def _proj_dt_kernel(x_ref, w_ref, wt_ref, out_ref, outt_ref):
    x = x_ref[...]
    out_ref[...] = jnp.dot(x, w_ref[...], preferred_element_type=F32)
    outt_ref[...] = lax.dot_general(wt_ref[...], x, (((1,), (1,)), ((), ())), preferred_element_type=F32)


def _proj_dt(x2, w_pad, w_t, *, tm):
    m, d = x2.shape
    nh = w_t.shape[0]
    return pl.pallas_call(
        _proj_dt_kernel,
        grid=(m // tm,),
        in_specs=[
            pl.BlockSpec((tm, d), lambda i: (i, 0)),
            pl.BlockSpec((d, LANES), lambda i: (0, 0)),
            pl.BlockSpec((nh, d), lambda i: (0, 0)),
        ],
        out_specs=[pl.BlockSpec((tm, LANES), lambda i: (i, 0)), pl.BlockSpec((nh, tm), lambda i: (0, i))],
        out_shape=[jax.ShapeDtypeStruct((m, LANES), F32), jax.ShapeDtypeStruct((nh, m), F32)],
        compiler_params=_cparams(("parallel",)),
        name="proj_dt",
    )(x2, w_pad, w_t)


def _attn_kernel(qa_tab, ka_tab, qb_tab, kb_tab, q_ref, k_ref, vt_ref, km_ref, vmt_ref, lam_ref, gain_ref,
                 out_ref, qs_sc, sm_sc, s0_sc, s1_sc, mx0_sc, mx1_sc, m_sc, l_sc, acc_sc,
                 *, tq, n_steps, lam_init):
    g = pl.program_id(2)
    qa = qa_tab[g]
    ka = ka_tab[g]
    qb = qb_tab[g]
    kb = kb_tab[g]
    has_a = g < n_steps
    has_b = g >= 1
    even = (g % 2) == 0
    nt = (((1,), (1,)), ((), ()))

    def stage_a(buf, diagonal):
        s_out, mx_out = buf
        s_t = lax.dot_general(k_ref[...], qs_sc[...], nt, preferred_element_type=F32)
        if diagonal:
            key = lax.broadcasted_iota(jnp.int32, s_t.shape, 0)
            qry = lax.broadcasted_iota(jnp.int32, s_t.shape, 1)
            s_t = jnp.where(key // CHUNK <= (qry & (tq - 1)) // CHUNK, s_t, -jnp.inf)
        s_out[...] = s_t
        mx_out[...] = jnp.max(s_t, axis=0, keepdims=True)

    def stage_b(buf, last):
        s_in, mx_in = buf
        m_prev = m_sc[...]
        m_new = jnp.maximum(m_prev, mx_in[...])
        alpha = jnp.exp2(m_prev - m_new)
        p = jnp.exp2(s_in[...] - m_new).astype(BF16)
        vt_aug = jnp.concatenate([vt_ref[...], jnp.ones((16, p.shape[0]), BF16)], axis=0)
        pv = jnp.dot(vt_aug, p, preferred_element_type=F32)
        l_sc[...] = alpha * l_sc[...] + pv[V_HEAD_DIM:V_HEAD_DIM + 1, :]
        acc_sc[...] = alpha * acc_sc[...] + pv[0:V_HEAD_DIM, :]
        m_sc[...] = m_new
        if last:
            lq1, lk1, lq2, lk2 = lam_ref[0:1, :], lam_ref[1:2, :], lam_ref[2:3, :], lam_ref[3:4, :]
            lam = (jnp.exp(jnp.sum(lq1 * lk1, axis=-1, keepdims=True))
                   - jnp.exp(jnp.sum(lq2 * lk2, axis=-1, keepdims=True)) + lam_init)
            o = acc_sc[...] / l_sc[...]
            o = o[:, 0:tq] - lam * o[:, tq:2 * tq]
            ms = jnp.mean(o * o, axis=0, keepdims=True)
            o = o * lax.rsqrt(ms + EPS) * (gain_ref[...] * (1.0 - lam_init))
            out_ref[...] = o.T.astype(out_ref.dtype)

    @pl.when(has_b & (kb == 0))
    def _():
        s_t = sm_sc[...]
        m0 = jnp.max(s_t, axis=0, keepdims=True)
        p = jnp.exp2(s_t - m0)
        m_sc[...] = m0
        l_sc[...] = jnp.sum(p, axis=0, keepdims=True)
        acc_sc[...] = jnp.dot(vmt_ref[...], p.astype(BF16), preferred_element_type=F32)

    @pl.when(has_a & (ka == 0))
    def _():
        q = q_ref[...]
        lane = lax.broadcasted_iota(jnp.int32, q.shape, 1)
        zero = jnp.zeros_like(q)
        qs_sc[0:tq, :] = jnp.where(lane < DIFF_HEAD_DIM, q, zero)
        qs_sc[tq:2 * tq, :] = jnp.where(lane >= DIFF_HEAD_DIM, q, zero)
        s_t = lax.dot_general(km_ref[...], qs_sc[...], nt, preferred_element_type=F32)
        key = lax.broadcasted_iota(jnp.int32, s_t.shape, 0)
        sm_sc[...] = jnp.where(key < N_META, s_t, -jnp.inf)

    buf0, buf1 = (s0_sc, mx0_sc), (s1_sc, mx1_sc)
    a_diag = ka == qa
    b_last = kb == qb
    both = has_a & has_b
    for parity, (buf_a, buf_b) in enumerate(((buf0, buf1), (buf1, buf0))):
        par = even if parity == 0 else jnp.logical_not(even)

        @pl.when(par & both & jnp.logical_not(a_diag) & jnp.logical_not(b_last))
        def _(buf_a=buf_a, buf_b=buf_b):
            stage_a(buf_a, False)
            stage_b(buf_b, False)

        @pl.when(par & both & a_diag & jnp.logical_not(b_last))
        def _(buf_a=buf_a, buf_b=buf_b):
            stage_a(buf_a, True)
            stage_b(buf_b, False)

        @pl.when(par & both & b_last)
        def _(buf_a=buf_a, buf_b=buf_b):
            stage_a(buf_a, False)
            stage_b(buf_b, True)

        @pl.when(par & has_a & jnp.logical_not(has_b))
        def _(buf_a=buf_a):
            stage_a(buf_a, True)

        @pl.when(par & has_b & jnp.logical_not(has_a))
        def _(buf_b=buf_b):
            stage_b(buf_b, True)


def _attention(qk, vt, kmeta, vmeta_t, lam_vecs, gain_col, *, bsz, seq, tq, lam_init):
    nq = seq // tq
    h = N_DIFF_HEADS
    qi_list, ki_list = [], []
    for a in range(nq):
        for b in range(a + 1):
            qi_list.append(a)
            ki_list.append(b)
    n_steps = len(qi_list)
    ia = np.minimum(np.arange(n_steps + 1), n_steps - 1)
    ib = np.maximum(np.arange(n_steps + 1) - 1, 0)
    qi_arr, ki_arr = np.array(qi_list, np.int32), np.array(ki_list, np.int32)
    tabs = [jnp.asarray(t) for t in (qi_arr[ia], ki_arr[ia], qi_arr[ib], ki_arr[ib])]
    grid_spec = pltpu.PrefetchScalarGridSpec(
        num_scalar_prefetch=4,
        grid=(bsz, h, n_steps + 1),
        in_specs=[
            pl.BlockSpec((tq, V_HEAD_DIM), lambda b, hh, g, qa, ka, qb, kb: (b * nq + qa[g], hh)),
            pl.BlockSpec((tq, V_HEAD_DIM), lambda b, hh, g, qa, ka, qb, kb: (b * nq + ka[g], h + hh)),
            pl.BlockSpec((V_HEAD_DIM, tq), lambda b, hh, g, qa, ka, qb, kb: (hh, b * nq + kb[g])),
            pl.BlockSpec((LANES, V_HEAD_DIM), lambda b, hh, g, qa, ka, qb, kb: (0, hh)),
            pl.BlockSpec((V_HEAD_DIM, LANES), lambda b, hh, g, qa, ka, qb, kb: (hh, 0)),
            pl.BlockSpec((4, DIFF_HEAD_DIM), lambda b, hh, g, qa, ka, qb, kb: (0, 0)),
            pl.BlockSpec((V_HEAD_DIM, 1), lambda b, hh, g, qa, ka, qb, kb: (0, 0)),
        ],
        out_specs=pl.BlockSpec((tq, V_HEAD_DIM), lambda b, hh, g, qa, ka, qb, kb: (b * nq + qb[g], hh)),
        scratch_shapes=[
            pltpu.VMEM((2 * tq, V_HEAD_DIM), BF16),
            pltpu.VMEM((LANES, 2 * tq), F32),
            pltpu.VMEM((tq, 2 * tq), F32),
            pltpu.VMEM((tq, 2 * tq), F32),
            pltpu.VMEM((1, 2 * tq), F32),
            pltpu.VMEM((1, 2 * tq), F32),
            pltpu.VMEM((1, 2 * tq), F32),
            pltpu.VMEM((1, 2 * tq), F32),
            pltpu.VMEM((V_HEAD_DIM, 2 * tq), F32),
        ],
    )
    return pl.pallas_call(
        functools.partial(_attn_kernel, tq=tq, n_steps=n_steps, lam_init=lam_init),
        grid_spec=grid_spec,
        out_shape=jax.ShapeDtypeStruct((bsz * seq, h * V_HEAD_DIM), BF16),
        compiler_params=_cparams(("parallel", "parallel", "arbitrary")),
        name="attn",
    )(*tabs, qk, qk, vt, kmeta, vmeta_t, lam_vecs, gain_col)


def _ssd_kernel(xbc_ref, z_ref, dt_ref, dtt_ref, xbcm_ref, dtm_ref, dttm_ref,
                convw_ref, convb_ref, dtb_ref, dtbt_ref, alog_ref, alogt_ref, dskip_ref, nw_ref,
                expand_ref, tril_ref, triu2_ref, out_ref, xs_sc, xm_sc, state_sc, *, t_blk):
    n_heads, p_dim, n_state, n_groups = N_SSD_HEADS, SSD_HEAD_DIM, D_STATE, N_SSD_GROUPS
    width = n_heads * p_dim
    gw = width // n_groups
    n_pairs = n_heads // 2
    nt = (((1,), (1,)), ((), ()))
    expand = expand_ref[...]
    a_row = -jnp.exp(alog_ref[...])
    a_col = -jnp.exp(alogt_ref[...])
    d_full = dskip_ref[...]
    lane = lax.broadcasted_iota(jnp.int32, (CHUNK, LANES), 1)
    row = lax.broadcasted_iota(jnp.int32, (CHUNK, LANES), 0)
    left = lane < p_dim
    causal2 = (lane & (CHUNK - 1)) <= row

    def chunk(src_ref, r0, dt_raw, dtt_raw, pad_rows, z, out_r0):
        conv = convb_ref[...]
        for k in range(CONV_WIDTH):
            conv = conv + src_ref[pl.ds(r0 - (CONV_WIDTH - 1) + k, CHUNK), :] * convw_ref[k:k + 1, :]
        xc = _silu(conv)
        x_s = xc[:, :width]
        dt = jax.nn.softplus(dt_raw + dtb_ref[...])
        dtt = jax.nn.softplus(dtt_raw + dtbt_ref[...])
        if pad_rows:
            dt = jnp.where(row >= pad_rows, dt, 0.0)
            dtt = jnp.where(lax.broadcasted_iota(jnp.int32, dtt.shape, 1) >= pad_rows, dtt, 0.0)
        a_cs = jnp.dot(tril_ref[...], dt * a_row, precision=HIGHEST, preferred_element_type=F32)
        a_cs_t2 = jnp.dot(dtt * a_col, triu2_ref[...], precision=HIGHEST, preferred_element_type=F32)
        a_full = jnp.dot(a_cs, expand, precision=HIGHEST, preferred_element_type=F32)
        dt_full = jnp.dot(dt, expand, precision=HIGHEST, preferred_element_type=F32)
        a_last = a_full[CHUNK - 1:CHUNK, :]
        xdt = x_s * dt_full
        w_state = (xdt * jnp.exp(a_last - a_full)).astype(BF16)
        xdt_b = xdt.astype(BF16)
        zero_b = jnp.zeros((CHUNK, LANES), BF16)
        y_parts = []
        for g in range(n_groups):
            b_g = xc[:, width + g * n_state: width + (g + 1) * n_state]
            c_g = xc[:, width + (n_groups + g) * n_state: width + (n_groups + g + 1) * n_state]
            b_gb = b_g.astype(BF16)
            c_gb = c_g.astype(BF16)
            st = state_sc[g]
            if out_r0 is not None:
                cb2 = lax.dot_general(c_gb, jnp.concatenate([b_gb, b_gb], axis=0), nt,
                                      preferred_element_type=F32)
                y_off = jnp.dot(c_gb, st.astype(BF16), preferred_element_type=F32)
                y_g = y_off * jnp.exp(a_full[:, g * gw:(g + 1) * gw])
                diag = []
                for pp in range(n_pairs // n_groups):
                    pr = g * (n_pairs // n_groups) + pp
                    colb = a_full[:, pr * LANES:(pr + 1) * LANES]
                    rowb = jnp.where(left, a_cs_t2[2 * pr:2 * pr + 1, :], a_cs_t2[2 * pr + 1:2 * pr + 2, :])
                    dec = jnp.exp(jnp.where(causal2, colb - rowb, -jnp.inf))
                    m_pair = (cb2 * dec).astype(BF16)
                    xp = xdt_b[:, pr * LANES:(pr + 1) * LANES]
                    rhs = jnp.concatenate([jnp.where(left, xp, zero_b), jnp.where(left, zero_b, xp)], axis=0)
                    diag.append(jnp.dot(m_pair, rhs, preferred_element_type=F32))
                y_parts.append(y_g + jnp.concatenate(diag, axis=1))
            new_st = st * jnp.exp(a_last[:, g * gw:(g + 1) * gw]) + jnp.dot(
                b_g.T.astype(BF16), w_state[:, g * gw:(g + 1) * gw], preferred_element_type=F32)
            state_sc[g] = new_st
        if out_r0 is not None:
            y = jnp.concatenate(y_parts, axis=1) + x_s * d_full
            gated = y * _silu(z)
            out_ref[pl.ds(out_r0, CHUNK), :] = _rms(gated, nw_ref[...]).astype(out_ref.dtype)

    @pl.when(pl.program_id(1) == 0)
    def _():
        state_sc[...] = jnp.zeros_like(state_sc)
        xm_sc[0:8, :] = jnp.zeros((8, xm_sc.shape[1]), F32)
        xm_sc[8:8 + CHUNK, :] = xbcm_ref[...].astype(F32)
        chunk(xm_sc, 8, dtm_ref[...], dttm_ref[...], CHUNK - N_META, None, None)
        xs_sc[0:8, :] = xm_sc[CHUNK:CHUNK + 8, :]

    xs_sc[8:8 + t_blk, :] = xbc_ref[...].astype(F32)
    for c in range(t_blk // CHUNK):
        chunk(xs_sc, 8 + c * CHUNK, dt_ref[c * CHUNK:(c + 1) * CHUNK, :], dtt_ref[:, c * CHUNK:(c + 1) * CHUNK],
              0, z_ref[c * CHUNK:(c + 1) * CHUNK, :].astype(F32), c * CHUNK)
    xs_sc[0:8, :] = xs_sc[t_blk:t_blk + 8, :]


def _ssd(xbc, z, dt, dtt, xbc_m, dt_m, dtt_m, conv_w, conv_b, dt_bias, a_log, d_skip, norm_w,
         *, bsz, seq, t_blk):
    nb = seq // t_blk
    cdim = xbc.shape[1]
    width = N_SSD_HEADS * SSD_HEAD_DIM
    nh = N_SSD_HEADS

    def pad_row(v):
        return jnp.pad(v.reshape(1, nh), ((0, 0), (0, LANES - nh)))

    hh = np.arange(LANES)[:, None]
    ll = np.arange(width)[None, :]
    expand = jnp.asarray((ll // SSD_HEAD_DIM == hh).astype(np.float32))
    ii = np.arange(CHUNK)
    tril = jnp.asarray((ii[None, :] <= ii[:, None]).astype(np.float32))
    triu2 = jnp.asarray(np.tile((ii[:, None] <= ii[None, :]).astype(np.float32), (1, 2)))
    full = lambda shape: pl.BlockSpec(shape, lambda b, i: tuple(0 for _ in shape))
    return pl.pallas_call(
        functools.partial(_ssd_kernel, t_blk=t_blk),
        grid=(bsz, nb),
        in_specs=[
            pl.BlockSpec((t_blk, cdim), lambda b, i: (b * nb + i, 0)),
            pl.BlockSpec((t_blk, width), lambda b, i: (b * nb + i, 0)),
            pl.BlockSpec((t_blk, LANES), lambda b, i: (b * nb + i, 0)),
            pl.BlockSpec((nh, t_blk), lambda b, i: (0, b * nb + i)),
            full((CHUNK, cdim)), full((CHUNK, LANES)), full((nh, CHUNK)),
            full((CONV_WIDTH, cdim)), full((1, cdim)), full((1, LANES)), full((nh, 1)),
            full((1, LANES)), full((nh, 1)), full((1, width)), full((1, width)),
            full((LANES, width)), full((CHUNK, CHUNK)), full((CHUNK, 2 * CHUNK)),
        ],
        out_specs=pl.BlockSpec((t_blk, width), lambda b, i: (b * nb + i, 0)),
        out_shape=jax.ShapeDtypeStruct((bsz * seq, width), BF16),
        scratch_shapes=[
            pltpu.VMEM((t_blk + 8, cdim), F32),
            pltpu.VMEM((CHUNK + 8, cdim), F32),
            pltpu.VMEM((N_SSD_GROUPS, D_STATE, width // N_SSD_GROUPS), F32),
        ],
        compiler_params=_cparams(("parallel", "arbitrary")),
        name="ssd",
    )(xbc, z, dt, dtt, xbc_m, dt_m, dtt_m, conv_w, conv_b.reshape(1, cdim), pad_row(dt_bias),
      dt_bias.reshape(nh, 1), pad_row(a_log), a_log.reshape(nh, 1), jnp.repeat(d_skip, SSD_HEAD_DIM).reshape(1, width),
      norm_w.reshape(1, width), expand, tril, triu2)


def _outproj_kernel(h_ref, a_ref, s_ref, wa_ref, ws_ref, out_ref):
    out_ref[...] = (h_ref[...]
                    + jnp.dot(a_ref[...], wa_ref[...], preferred_element_type=F32)
                    + jnp.dot(s_ref[...], ws_ref[...], preferred_element_type=F32))


def _outproj(h1, attn, ssd, wa, ws, *, tm, tn):
    m, d = h1.shape
    k = attn.shape[1]
    return pl.pallas_call(
        _outproj_kernel,
        grid=(m // tm, d // tn),
        in_specs=[
            pl.BlockSpec((tm, tn), lambda i, j: (i, j)),
            pl.BlockSpec((tm, k), lambda i, j: (i, 0)),
            pl.BlockSpec((tm, k), lambda i, j: (i, 0)),
            pl.BlockSpec((k, tn), lambda i, j: (0, j)),
            pl.BlockSpec((k, tn), lambda i, j: (0, j)),
        ],
        out_specs=pl.BlockSpec((tm, tn), lambda i, j: (i, j)),
        out_shape=jax.ShapeDtypeStruct((m, d), F32),
        compiler_params=_cparams(("parallel", "arbitrary")),
        name="outproj",
    )(h1, attn, ssd, wa, ws)


def _rope_tables(n_pos):
    inv = jnp.power(ROPE_THETA, -jnp.arange(0, ROT_DIM, 2, dtype=F32) / ROT_DIM)
    ang = jnp.arange(n_pos, dtype=F32)[:, None] * inv[None, :]
    cos, sin = jnp.cos(ang), jnp.sin(ang)
    half = ROT_DIM // 2
    r = np.arange(LANES) % DIFF_HEAD_DIM
    idx = jnp.asarray(r % half)
    lo = jnp.asarray(r < half)[None, :]
    hi = jnp.asarray((r >= half) & (r < ROT_DIM))[None, :]
    cos_t = jnp.where(lo | hi, cos[:, idx], 1.0)
    sin_lo = jnp.where(lo, -sin[:, idx], 0.0)
    sin_hi = jnp.where(hi, sin[:, idx], 0.0)
    return jnp.stack([cos_t, sin_lo, sin_hi])


def kernel(x, meta_tokens, ffn1_norm, ffn1_w_gate, ffn1_w_up, ffn1_w_down, mix_norm, w_in, q_norm, k_norm,
           lambda_q1, lambda_k1, lambda_q2, lambda_k2, attn_out_norm, conv_w, conv_b, dt_bias, a_log, d_skip,
           ssd_norm, w_out, ffn2_norm, ffn2_w_gate, ffn2_w_up, ffn2_w_down):
    bsz, seq, d = x.shape
    assert ffn1_norm.shape[0] == 1, "single-layer block"
    aw = N_DIFF_HEADS * V_HEAD_DIM
    sw = N_SSD_HEADS * SSD_HEAD_DIM
    cdim = sw + 2 * N_SSD_GROUPS * D_STATE
    lam_init = 0.8 - 0.6 * math.exp(-0.3 * 0)
    row = lambda v: v.reshape(1, -1)

    w_in0 = w_in[0]
    w_qk = w_in0[:, :2 * aw].astype(BF16)
    w_v_t = w_in0[:, 2 * aw:3 * aw].T.astype(BF16)
    w_z = w_in0[:, 3 * aw:3 * aw + sw].astype(BF16)
    w_xbc = w_in0[:, 3 * aw + sw:3 * aw + sw + cdim].astype(BF16)
    w_dt = w_in0[:, 3 * aw + sw + cdim:].astype(BF16)
    w_dt_pad = jnp.pad(w_dt, ((0, 0), (0, LANES - N_SSD_HEADS)))
    w_dt_t = w_dt.T

    n_sub = aw // DIFF_HEAD_DIM
    q_gain = jnp.tile(q_norm[0], n_sub) * (DIFF_HEAD_DIM ** -0.5 * math.log2(math.e))
    k_gain = jnp.tile(k_norm[0], n_sub)
    tn_qk = 512
    qk_gain = jnp.concatenate([q_gain, k_gain]).reshape(2 * aw // tn_qk, 1, tn_qk)
    rope = _rope_tables(N_META + seq)
    gi = np.arange(LANES) // DIFF_HEAD_DIM
    bd = jnp.asarray((gi[:, None] == gi[None, :]).astype(np.float32) / DIFF_HEAD_DIM, dtype=BF16)

    def ffn_weights(g, u, dn):
        return g[0].astype(BF16), u[0].astype(BF16), dn[0].astype(BF16)

    wg1, wu1, wd1 = ffn_weights(ffn1_w_gate, ffn1_w_up, ffn1_w_down)
    wg2, wu2, wd2 = ffn_weights(ffn2_w_gate, ffn2_w_up, ffn2_w_down)

    x2 = x.reshape(bsz * seq, d)
    h1, hn = _ffn(x2, row(ffn1_norm[0]), wg1, wu1, wd1, row(mix_norm[0]), tm=512, tf=512)
    _, hn_m = _ffn(meta_tokens.astype(F32), row(ffn1_norm[0]), wg1, wu1, wd1, row(mix_norm[0]), tm=N_META, tf=512)

    qk = _proj_qk(hn, w_qk, qk_gain, rope[:, N_META:], bd, tm=1024, tn=tn_qk)
    qk_m = _proj_qk(hn_m, w_qk, qk_gain, rope[:, :N_META], bd, tm=N_META, tn=tn_qk)
    v_t = _proj_t(w_v_t, hn, tn=512, tm=1024, name="proj_vt")
    v_t_m = _proj_t(w_v_t, hn_m, tn=512, tm=N_META, name="proj_vt_meta")
    z = _proj(hn, w_z, tm=1024, tn=512, name="proj_z")
    xbc = _proj(hn, w_xbc, tm=1024, tn=512, name="proj_xbc")
    xbc_m = _proj(hn_m, w_xbc, tm=N_META, tn=512, name="proj_xbc_meta")
    dt, dtt = _proj_dt(hn, w_dt_pad, w_dt_t, tm=1024)
    dt_m, dtt_m = _proj_dt(hn_m, w_dt_pad, w_dt_t, tm=N_META)

    kmeta = jnp.pad(qk_m[:, aw:], ((0, LANES - N_META), (0, 0)))
    vmeta_t = jnp.pad(v_t_m, ((0, 0), (0, LANES - N_META)))
    lam_vecs = jnp.stack([lambda_q1[0], lambda_k1[0], lambda_q2[0], lambda_k2[0]]).astype(F32)
    attn = _attention(qk, v_t, kmeta, vmeta_t, lam_vecs, attn_out_norm[0].reshape(V_HEAD_DIM, 1),
                      bsz=bsz, seq=seq, tq=512, lam_init=lam_init)

    lead = CHUNK - N_META
    ssd = _ssd(xbc, z, dt, dtt,
               jnp.pad(xbc_m, ((lead, 0), (0, 0))), jnp.pad(dt_m, ((lead, 0), (0, 0))),
               jnp.pad(dtt_m, ((0, 0), (lead, 0))),
               conv_w[0], conv_b[0], dt_bias[0], a_log[0], d_skip[0], ssd_norm[0],
               bsz=bsz, seq=seq, t_blk=256)

    w_o = w_out[0].astype(BF16)
    h2 = _outproj(h1, attn, ssd, w_o[:aw], w_o[aw:], tm=1024, tn=512)
    out, _ = _ffn(h2, row(ffn2_norm[0]), wg2, wu2, wd2, row(ffn2_norm[0]), tm=512, tf=512)
    return out.reshape(bsz, seq, d)
```

```python
import functools
import math

import jax
import jax.numpy as jnp
import numpy as np
from jax import lax
from jax.experimental import pallas as pl
from jax.experimental.pallas import tpu as pltpu

F32 = jnp.float32
BF16 = jnp.bfloat16
HIGHEST = lax.Precision.HIGHEST

EPS = 1e-6
CHUNK = 64
N_META = 16
N_DIFF_HEADS = 8
DIFF_HEAD_DIM = 64
V_HEAD_DIM = 128
ROT_DIM = 16
ROPE_THETA = 500000.0
SSD_HEAD_DIM = 64
N_SSD_HEADS = 16
N_SSD_GROUPS = 2
D_STATE = 128
CONV_WIDTH = 4
LANES = 128
VMEM_LIMIT = 56 * 1024 * 1024


def _cparams(sem):
    return pltpu.CompilerParams(dimension_semantics=sem, vmem_limit_bytes=VMEM_LIMIT)


def _silu(v):
    return v * (1.0 / (1.0 + jnp.exp(-v)))


def _rms(v, w):
    ms = jnp.mean(v * v, axis=-1, keepdims=True)
    return v * lax.rsqrt(ms + EPS) * w


def _ffn_kernel(x_ref, nw_ref, wg_ref, wu_ref, wd_ref, pnw_ref, out_ref, hn_ref, xn_sc):
    j = pl.program_id(1)

    @pl.when(j == 0)
    def _():
        xn_sc[...] = _rms(x_ref[...], nw_ref[...]).astype(BF16)
        out_ref[...] = jnp.zeros_like(out_ref)

    xn = xn_sc[...]
    g = jnp.dot(xn, wg_ref[...].astype(BF16), preferred_element_type=F32)
    u = jnp.dot(xn, wu_ref[...].astype(BF16), preferred_element_type=F32)
    a = (_silu(g) * u).astype(BF16)
    out_ref[...] += jnp.dot(a, wd_ref[...].astype(BF16), preferred_element_type=F32)

    @pl.when(j == pl.num_programs(1) - 1)
    def _():
        h = x_ref[...] + 0.5 * out_ref[...]
        out_ref[...] = h
        hn_ref[...] = _rms(h, pnw_ref[...]).astype(BF16)


def _ffn(x2, nw, wg, wu, wd, post_nw, *, tm, tf):
    m, d = x2.shape
    f = wg.shape[1]
    return pl.pallas_call(
        _ffn_kernel,
        grid=(m // tm, f // tf),
        in_specs=[
            pl.BlockSpec((tm, d), lambda i, j: (i, 0), pipeline_mode=pl.Buffered(1)),
            pl.BlockSpec((1, d), lambda i, j: (0, 0)),
            pl.BlockSpec((d, tf), lambda i, j: (0, j)),
            pl.BlockSpec((d, tf), lambda i, j: (0, j)),
            pl.BlockSpec((tf, d), lambda i, j: (j, 0)),
            pl.BlockSpec((1, d), lambda i, j: (0, 0)),
        ],
        out_specs=[
            pl.BlockSpec((tm, d), lambda i, j: (i, 0)),
            pl.BlockSpec((tm, d), lambda i, j: (i, 0), pipeline_mode=pl.Buffered(1)),
        ],
        out_shape=[jax.ShapeDtypeStruct((m, d), F32), jax.ShapeDtypeStruct((m, d), BF16)],
        scratch_shapes=[pltpu.VMEM((tm, d), BF16)],
        compiler_params=_cparams(("parallel", "arbitrary")),
        name="ffn",
    )(x2, nw, wg, wu, wd, post_nw)


def _proj_kernel(x_ref, w_ref, out_ref):
    out_ref[...] = jnp.dot(x_ref[...], w_ref[...], preferred_element_type=F32).astype(out_ref.dtype)


def _proj(x2, w, *, tm, tn, out_dtype=BF16, name="proj"):
    m, d = x2.shape
    n = w.shape[1]
    return pl.pallas_call(
        _proj_kernel,
        grid=(m // tm, n // tn),
        in_specs=[pl.BlockSpec((tm, d), lambda i, j: (i, 0)), pl.BlockSpec((d, tn), lambda i, j: (0, j))],
        out_specs=pl.BlockSpec((tm, tn), lambda i, j: (i, j)),
        out_shape=jax.ShapeDtypeStruct((m, n), out_dtype),
        compiler_params=_cparams(("parallel", "arbitrary")),
        name=name,
    )(x2, w)


def _proj_t_kernel(w_ref, x_ref, out_ref):
    out_ref[...] = lax.dot_general(w_ref[...], x_ref[...], (((1,), (1,)), ((), ())),
                                   preferred_element_type=F32).astype(out_ref.dtype)


def _proj_t(w_t, x2, *, tn, tm, name):
    n, d = w_t.shape
    m = x2.shape[0]
    return pl.pallas_call(
        _proj_t_kernel,
        grid=(m // tm, n // tn),
        in_specs=[pl.BlockSpec((tn, d), lambda i, j: (j, 0)), pl.BlockSpec((tm, d), lambda i, j: (i, 0))],
        out_specs=pl.BlockSpec((tn, tm), lambda i, j: (j, i)),
        out_shape=jax.ShapeDtypeStruct((n, m), BF16),
        compiler_params=_cparams(("parallel", "arbitrary")),
        name=name,
    )(w_t, x2)


def _proj_qk_kernel(x_ref, w_ref, gain_ref, rope_ref, bd_ref, out_ref):
    acc = jnp.dot(x_ref[...], w_ref[...], preferred_element_type=F32)
    tn = acc.shape[1]
    cos, sin_lo, sin_hi = rope_ref[0], rope_ref[1], rope_ref[2]
    bd = bd_ref[...]
    for c in range(tn // LANES):
        y = acc[:, c * LANES:(c + 1) * LANES]
        y2 = y * y
        hi = y2.astype(BF16)
        lo = (y2 - hi.astype(F32)).astype(BF16)
        ms = jnp.dot(hi, bd, preferred_element_type=F32) + jnp.dot(lo, bd, preferred_element_type=F32)
        yn = y * lax.rsqrt(ms + EPS) * gain_ref[0][:, c * LANES:(c + 1) * LANES]
        half = ROT_DIM // 2
        r = yn * cos + pltpu.roll(yn, half, 1) * sin_hi + pltpu.roll(yn, LANES - half, 1) * sin_lo
        out_ref[:, c * LANES:(c + 1) * LANES] = r.astype(out_ref.dtype)


def _proj_qk(x2, w, gain, rope, bd, *, tm, tn):
    m, d = x2.shape
    n = w.shape[1]
    n_tab = rope.shape[1] // tm
    return pl.pallas_call(
        _proj_qk_kernel,
        grid=(m // tm, n // tn),
        in_specs=[
            pl.BlockSpec((tm, d), lambda i, j: (i, 0)),
            pl.BlockSpec((d, tn), lambda i, j: (0, j)),
            pl.BlockSpec((1, 1, tn), lambda i, j: (j, 0, 0)),
            pl.BlockSpec((3, tm, LANES), lambda i, j: (0, i % n_tab, 0)),
            pl.BlockSpec((LANES, LANES), lambda i, j: (0, 0)),
        ],
        out_specs=pl.BlockSpec((tm, tn), lambda i, j: (i, j)),
        out_shape=jax.ShapeDtypeStruct((m, n), BF16),
        compiler_params=_cparams(("parallel", "arbitrary")),
        name="proj_qk",
    )(x2, w, gain, rope, bd)


def _proj_dt_kernel(x_ref, w_ref, wt_ref, out_ref, outt_ref):
    x = x_ref[...]
    out_ref[...] = jnp.dot(x, w_ref[...], preferred_element_type=F32)
    outt_ref[...] = lax.dot_general(wt_ref[...], x, (((1,), (1,)), ((), ())), preferred_element_type=F32)


def _proj_dt(x2, w_pad, w_t, *, tm):
    m, d = x2.shape
    nh = w_t.shape[0]
    return pl.pallas_call(
        _proj_dt_kernel,
        grid=(m // tm,),
        in_specs=[
            pl.BlockSpec((tm, d), lambda i: (i, 0)),
            pl.BlockSpec((d, LANES), lambda i: (0, 0)),
            pl.BlockSpec((nh, d), lambda i: (0, 0)),
        ],
        out_specs=[pl.BlockSpec((tm, LANES), lambda i: (i, 0)), pl.BlockSpec((nh, tm), lambda i: (0, i))],
        out_shape=[jax.ShapeDtypeStruct((m, LANES), F32), jax.ShapeDtypeStruct((nh, m), F32)],
        compiler_params=_cparams(("parallel",)),
        name="proj_dt",
    )(x2, w_pad, w_t)


def _attn_kernel(qa_tab, ka_tab, qb_tab, kb_tab, q_ref, k_ref, vt_ref, km_ref, vmt_ref, lam_ref, gain_ref,
                 out_ref, qs_sc, sm_sc, s0_sc, s1_sc, mx0_sc, mx1_sc, m_sc, l_sc, acc_sc,
                 *, tq, n_steps, lam_init):
    g = pl.program_id(2)
    qa = qa_tab[g]
    ka = ka_tab[g]
    qb = qb_tab[g]
    kb = kb_tab[g]
    has_a = g < n_steps
    has_b = g >= 1
    even = (g % 2) == 0
    nt = (((1,), (1,)), ((), ()))

    def stage_a(buf, diagonal):
        s_out, mx_out = buf
        s_t = lax.dot_general(k_ref[...], qs_sc[...], nt, preferred_element_type=F32)
        if diagonal:
            key = lax.broadcasted_iota(jnp.int32, s_t.shape, 0)
            qry = lax.broadcasted_iota(jnp.int32, s_t.shape, 1)
            s_t = jnp.where(key // CHUNK <= (qry & (tq - 1)) // CHUNK, s_t, -jnp.inf)
        s_out[...] = s_t
        mx_out[...] = jnp.max(s_t, axis=0, keepdims=True)

    def stage_b(buf, last):
        s_in, mx_in = buf
        m_prev = m_sc[...]
        m_new = jnp.maximum(m_prev, mx_in[...])
        alpha = jnp.exp2(m_prev - m_new)
        p = jnp.exp2(s_in[...] - m_new).astype(BF16)
        vt_aug = jnp.concatenate([vt_ref[...], jnp.ones((16, p.shape[0]), BF16)], axis=0)
        pv = jnp.dot(vt_aug, p, preferred_element_type=F32)
        l_sc[...] = alpha * l_sc[...] + pv[V_HEAD_DIM:V_HEAD_DIM + 1, :]
        acc_sc[...] = alpha * acc_sc[...] + pv[0:V_HEAD_DIM, :]
        m_sc[...] = m_new
        if last:
            lq1, lk1, lq2, lk2 = lam_ref[0:1, :], lam_ref[1:2, :], lam_ref[2:3, :], lam_ref[3:4, :]
            lam = (jnp.exp(jnp.sum(lq1 * lk1, axis=-1, keepdims=True))
                   - jnp.exp(jnp.sum(lq2 * lk2, axis=-1, keepdims=True)) + lam_init)
            o = acc_sc[...] / l_sc[...]
            o = o[:, 0:tq] - lam * o[:, tq:2 * tq]
            ms = jnp.mean(o * o, axis=0, keepdims=True)
            o = o * lax.rsqrt(ms + EPS) * (gain_ref[...] * (1.0 - lam_init))
            out_ref[...] = o.T.astype(out_ref.dtype)

    @pl.when(has_b & (kb == 0))
    def _():
        s_t = sm_sc[...]
        m0 = jnp.max(s_t, axis=0, keepdims=True)
        p = jnp.exp2(s_t - m0)
        m_sc[...] = m0
        l_sc[...] = jnp.sum(p, axis=0, keepdims=True)
        acc_sc[...] = jnp.dot(vmt_ref[...], p.astype(BF16), preferred_element_type=F32)

    @pl.when(has_a & (ka == 0))
    def _():
        q = q_ref[...]
        lane = lax.broadcasted_iota(jnp.int32, q.shape, 1)
        zero = jnp.zeros_like(q)
        qs_sc[0:tq, :] = jnp.where(lane < DIFF_HEAD_DIM, q, zero)
        qs_sc[tq:2 * tq, :] = jnp.where(lane >= DIFF_HEAD_DIM, q, zero)
        s_t = lax.dot_general(km_ref[...], qs_sc[...], nt, preferred_element_type=F32)
        key = lax.broadcasted_iota(jnp.int32, s_t.shape, 0)
        sm_sc[...] = jnp.where(key < N_META, s_t, -jnp.inf)

    buf0, buf1 = (s0_sc, mx0_sc), (s1_sc, mx1_sc)
    a_diag = ka == qa
    b_last = kb == qb
    both = has_a & has_b
    for parity, (buf_a, buf_b) in enumerate(((buf0, buf1), (buf1, buf0))):
        par = even if parity == 0 else jnp.logical_not(even)

        @pl.when(par & both & jnp.logical_not(a_diag) & jnp.logical_not(b_last))
        def _(buf_a=buf_a, buf_b=buf_b):
            stage_a(buf_a, False)
            stage_b(buf_b, False)

        @pl.when(par & both & a_diag & jnp.logical_not(b_last))
        def _(buf_a=buf_a, buf_b=buf_b):
            stage_a(buf_a, True)
            stage_b(buf_b, False)

        @pl.when(par & both & b_last)
        def _(buf_a=buf_a, buf_b=buf_b):
            stage_a(buf_a, False)
            stage_b(buf_b, True)

        @pl.when(par & has_a & jnp.logical_not(has_b))
        def _(buf_a=buf_a):
            stage_a(buf_a, True)

        @pl.when(par & has_b & jnp.logical_not(has_a))
        def _(buf_b=buf_b):
            stage_b(buf_b, True)


def _attention(qk, vt, kmeta, vmeta_t, lam_vecs, gain_col, *, bsz, seq, tq, lam_init):
    nq = seq // tq
    h = N_DIFF_HEADS
    qi_list, ki_list = [], []
    for a in range(nq):
        for b in range(a + 1):
            qi_list.append(a)
            ki_list.append(b)
    n_steps = len(qi_list)
    ia = np.minimum(np.arange(n_steps + 1), n_steps - 1)
    ib = np.maximum(np.arange(n_steps + 1) - 1, 0)
    qi_arr, ki_arr = np.array(qi_list, np.int32), np.array(ki_list, np.int32)
    tabs = [jnp.asarray(t) for t in (qi_arr[ia], ki_arr[ia], qi_arr[ib], ki_arr[ib])]
    grid_spec = pltpu.PrefetchScalarGridSpec(
        num_scalar_prefetch=4,
        grid=(bsz, h, n_steps + 1),
        in_specs=[
            pl.BlockSpec((tq, V_HEAD_DIM), lambda b, hh, g, qa, ka, qb, kb: (b * nq + qa[g], hh)),
            pl.BlockSpec((tq, V_HEAD_DIM), lambda b, hh, g, qa, ka, qb, kb: (b * nq + ka[g], h + hh)),
            pl.BlockSpec((V_HEAD_DIM, tq), lambda b, hh, g, qa, ka, qb, kb: (hh, b * nq + kb[g])),
            pl.BlockSpec((LANES, V_HEAD_DIM), lambda b, hh, g, qa, ka, qb, kb: (0, hh)),
            pl.BlockSpec((V_HEAD_DIM, LANES), lambda b, hh, g, qa, ka, qb, kb: (hh, 0)),
            pl.BlockSpec((4, DIFF_HEAD_DIM), lambda b, hh, g, qa, ka, qb, kb: (0, 0)),
            pl.BlockSpec((V_HEAD_DIM, 1), lambda b, hh, g, qa, ka, qb, kb: (0, 0)),
        ],
        out_specs=pl.BlockSpec((tq, V_HEAD_DIM), lambda b, hh, g, qa, ka, qb, kb: (b * nq + qb[g], hh)),
        scratch_shapes=[
            pltpu.VMEM((2 * tq, V_HEAD_DIM), BF16),
            pltpu.VMEM((LANES, 2 * tq), F32),
            pltpu.VMEM((tq, 2 * tq), F32),
            pltpu.VMEM((tq, 2 * tq), F32),
            pltpu.VMEM((1, 2 * tq), F32),
            pltpu.VMEM((1, 2 * tq), F32),
            pltpu.VMEM((1, 2 * tq), F32),
            pltpu.VMEM((1, 2 * tq), F32),
            pltpu.VMEM((V_HEAD_DIM, 2 * tq), F32),
        ],
    )
    return pl.pallas_call(
        functools.partial(_attn_kernel, tq=tq, n_steps=n_steps, lam_init=lam_init),
        grid_spec=grid_spec,
        out_shape=jax.ShapeDtypeStruct((bsz * seq, h * V_HEAD_DIM), BF16),
        compiler_params=_cparams(("parallel", "parallel", "arbitrary")),
        name="attn",
    )(*tabs, qk, qk, vt, kmeta, vmeta_t, lam_vecs, gain_col)


def _ssd_kernel(xbc_ref, z_ref, dt_ref, dtt_ref, xbcm_ref, dtm_ref, dttm_ref,
                convw_ref, convb_ref, dtb_ref, dtbt_ref, alog_ref, alogt_ref, dskip_ref, nw_ref,
                expand_ref, tril_ref, triu2_ref, out_ref, xs_sc, xm_sc, state_sc, *, t_blk):
    n_heads, p_dim, n_state, n_groups = N_SSD_HEADS, SSD_HEAD_DIM, D_STATE, N_SSD_GROUPS
    width = n_heads * p_dim
    gw = width // n_groups
    n_pairs = n_heads // 2
    nt = (((1,), (1,)), ((), ()))
    expand = expand_ref[...]
    a_row = -jnp.exp(alog_ref[...])
    a_col = -jnp.exp(alogt_ref[...])
    d_full = dskip_ref[...]
    lane = lax.broadcasted_iota(jnp.int32, (CHUNK, LANES), 1)
    row = lax.broadcasted_iota(jnp.int32, (CHUNK, LANES), 0)
    left = lane < p_dim
    causal2 = (lane & (CHUNK - 1)) <= row

    def chunk(src_ref, r0, dt_raw, dtt_raw, pad_rows, z, out_r0):
        conv = convb_ref[...]
        for k in range(CONV_WIDTH):
            conv = conv + src_ref[pl.ds(r0 - (CONV_WIDTH - 1) + k, CHUNK), :] * convw_ref[k:k + 1, :]
        xc = _silu(conv)
        x_s = xc[:, :width]
        dt = jax.nn.softplus(dt_raw + dtb_ref[...])
        dtt = jax.nn.softplus(dtt_raw + dtbt_ref[...])
        if pad_rows:
            dt = jnp.where(row >= pad_rows, dt, 0.0)
            dtt = jnp.where(lax.broadcasted_iota(jnp.int32, dtt.shape, 1) >= pad_rows, dtt, 0.0)
        a_cs = jnp.dot(tril_ref[...], dt * a_row, precision=HIGHEST, preferred_element_type=F32)
        a_cs_t2 = jnp.dot(dtt * a_col, triu2_ref[...], precision=HIGHEST, preferred_element_type=F32)
        a_full = jnp.dot(a_cs, expand, precision=HIGHEST, preferred_element_type=F32)
        dt_full = jnp.dot(dt, expand, precision=HIGHEST, preferred_element_type=F32)
        a_last = a_full[CHUNK - 1:CHUNK, :]
        xdt = x_s * dt_full
        w_state = (xdt * jnp.exp(a_last - a_full)).astype(BF16)
        xdt_b = xdt.astype(BF16)
        zero_b = jnp.zeros((CHUNK, LANES), BF16)
        y_parts = []
        for g in range(n_groups):
            b_g = xc[:, width + g * n_state: width + (g + 1) * n_state]
            c_g = xc[:, width + (n_groups + g) * n_state: width + (n_groups + g + 1) * n_state]
            b_gb = b_g.astype(BF16)
            c_gb = c_g.astype(BF16)
            st = state_sc[g]
            if out_r0 is not None:
                cb2 = lax.dot_general(c_gb, jnp.concatenate([b_gb, b_gb], axis=0), nt,
                                      preferred_element_type=F32)
                y_off = jnp.dot(c_gb, st.astype(BF16), preferred_element_type=F32)
                y_g = y_off * jnp.exp(a_full[:, g * gw:(g + 1) * gw])
                diag = []
                for pp in range(n_pairs // n_groups):
                    pr = g * (n_pairs // n_groups) + pp
                    colb = a_full[:, pr * LANES:(pr + 1) * LANES]
                    rowb = jnp.where(left, a_cs_t2[2 * pr:2 * pr + 1, :], a_cs_t2[2 * pr + 1:2 * pr + 2, :])
                    dec = jnp.exp(jnp.where(causal2, colb - rowb, -jnp.inf))
                    m_pair = (cb2 * dec).astype(BF16)
                    xp = xdt_b[:, pr * LANES:(pr + 1) * LANES]
                    rhs = jnp.concatenate([jnp.where(left, xp, zero_b), jnp.where(left, zero_b, xp)], axis=0)
                    diag.append(jnp.dot(m_pair, rhs, preferred_element_type=F32))
                y_parts.append(y_g + jnp.concatenate(diag, axis=1))
            new_st = st * jnp.exp(a_last[:, g * gw:(g + 1) * gw]) + jnp.dot(
                b_g.T.astype(BF16), w_state[:, g * gw:(g + 1) * gw], preferred_element_type=F32)
            state_sc[g] = new_st
        if out_r0 is not None:
            y = jnp.concatenate(y_parts, axis=1) + x_s * d_full
            gated = y * _silu(z)
            out_ref[pl.ds(out_r0, CHUNK), :] = _rms(gated, nw_ref[...]).astype(out_ref.dtype)

    @pl.when(pl.program_id(1) == 0)
    def _():
        state_sc[...] = jnp.zeros_like(state_sc)
        xm_sc[0:8, :] = jnp.zeros((8, xm_sc.shape[1]), F32)
        xm_sc[8:8 + CHUNK, :] = xbcm_ref[...].astype(F32)
        chunk(xm_sc, 8, dtm_ref[...], dttm_ref[...], CHUNK - N_META, None, None)
        xs_sc[0:8, :] = xm_sc[CHUNK:CHUNK + 8, :]

    xs_sc[8:8 + t_blk, :] = xbc_ref[...].astype(F32)
    for c in range(t_blk // CHUNK):
        chunk(xs_sc, 8 + c * CHUNK, dt_ref[c * CHUNK:(c + 1) * CHUNK, :], dtt_ref[:, c * CHUNK:(c + 1) * CHUNK],
              0, z_ref[c * CHUNK:(c + 1) * CHUNK, :].astype(F32), c * CHUNK)
    xs_sc[0:8, :] = xs_sc[t_blk:t_blk + 8, :]


def _ssd(xbc, z, dt, dtt, xbc_m, dt_m, dtt_m, conv_w, conv_b, dt_bias, a_log, d_skip, norm_w,
         *, bsz, seq, t_blk):
    nb = seq // t_blk
    cdim = xbc.shape[1]
    width = N_SSD_HEADS * SSD_HEAD_DIM
    nh = N_SSD_HEADS

    def pad_row(v):
        return jnp.pad(v.reshape(1, nh), ((0, 0), (0, LANES - nh)))

    hh = np.arange(LANES)[:, None]
    ll = np.arange(width)[None, :]
    expand = jnp.asarray((ll // SSD_HEAD_DIM == hh).astype(np.float32))
    ii = np.arange(CHUNK)
    tril = jnp.asarray((ii[None, :] <= ii[:, None]).astype(np.float32))
    triu2 = jnp.asarray(np.tile((ii[:, None] <= ii[None, :]).astype(np.float32), (1, 2)))
    full = lambda shape: pl.BlockSpec(shape, lambda b, i: tuple(0 for _ in shape))
    return pl.pallas_call(
        functools.partial(_ssd_kernel, t_blk=t_blk),
        grid=(bsz, nb),
        in_specs=[
            pl.BlockSpec((t_blk, cdim), lambda b, i: (b * nb + i, 0)),
            pl.BlockSpec((t_blk, width), lambda b, i: (b * nb + i, 0)),
            pl.BlockSpec((t_blk, LANES), lambda b, i: (b * nb + i, 0)),
            pl.BlockSpec((nh, t_blk), lambda b, i: (0, b * nb + i)),
            full((CHUNK, cdim)), full((CHUNK, LANES)), full((nh, CHUNK)),
            full((CONV_WIDTH, cdim)), full((1, cdim)), full((1, LANES)), full((nh, 1)),
            full((1, LANES)), full((nh, 1)), full((1, width)), full((1, width)),
            full((LANES, width)), full((CHUNK, CHUNK)), full((CHUNK, 2 * CHUNK)),
        ],
        out_specs=pl.BlockSpec((t_blk, width), lambda b, i: (b * nb + i, 0)),
        out_shape=jax.ShapeDtypeStruct((bsz * seq, width), BF16),
        scratch_shapes=[
            pltpu.VMEM((t_blk + 8, cdim), F32),
            pltpu.VMEM((CHUNK + 8, cdim), F32),
            pltpu.VMEM((N_SSD_GROUPS, D_STATE, width // N_SSD_GROUPS), F32),
        ],
        compiler_params=_cparams(("parallel", "arbitrary")),
        name="ssd",
    )(xbc, z, dt, dtt, xbc_m, dt_m, dtt_m, conv_w, conv_b.reshape(1, cdim), pad_row(dt_bias),
      dt_bias.reshape(nh, 1), pad_row(a_log), a_log.reshape(nh, 1), jnp.repeat(d_skip, SSD_HEAD_DIM).reshape(1, width),
      norm_w.reshape(1, width), expand, tril, triu2)


def _outproj_kernel(h_ref, a_ref, s_ref, wa_ref, ws_ref, out_ref):
    out_ref[...] = (h_ref[...]
                    + jnp.dot(a_ref[...], wa_ref[...], preferred_element_type=F32)
                    + jnp.dot(s_ref[...], ws_ref[...], preferred_element_type=F32))


def _outproj(h1, attn, ssd, wa, ws, *, tm, tn):
    m, d = h1.shape
    k = attn.shape[1]
    return pl.pallas_call(
        _outproj_kernel,
        grid=(m // tm, d // tn),
        in_specs=[
            pl.BlockSpec((tm, tn), lambda i, j: (i, j)),
            pl.BlockSpec((tm, k), lambda i, j: (i, 0)),
            pl.BlockSpec((tm, k), lambda i, j: (i, 0)),
            pl.BlockSpec((k, tn), lambda i, j: (0, j)),
            pl.BlockSpec((k, tn), lambda i, j: (0, j)),
        ],
        out_specs=pl.BlockSpec((tm, tn), lambda i, j: (i, j)),
        out_shape=jax.ShapeDtypeStruct((m, d), F32),
        compiler_params=_cparams(("parallel", "arbitrary")),
        name="outproj",
    )(h1, attn, ssd, wa, ws)


def _rope_tables(n_pos):
    inv = jnp.power(ROPE_THETA, -jnp.arange(0, ROT_DIM, 2, dtype=F32) / ROT_DIM)
    ang = jnp.arange(n_pos, dtype=F32)[:, None] * inv[None, :]
    cos, sin = jnp.cos(ang), jnp.sin(ang)
    half = ROT_DIM // 2
    r = np.arange(LANES) % DIFF_HEAD_DIM
    idx = jnp.asarray(r % half)
    lo = jnp.asarray(r < half)[None, :]
    hi = jnp.asarray((r >= half) & (r < ROT_DIM))[None, :]
    cos_t = jnp.where(lo | hi, cos[:, idx], 1.0)
    sin_lo = jnp.where(lo, -sin[:, idx], 0.0)
    sin_hi = jnp.where(hi, sin[:, idx], 0.0)
    return jnp.stack([cos_t, sin_lo, sin_hi])


def kernel(x, meta_tokens, ffn1_norm, ffn1_w_gate, ffn1_w_up, ffn1_w_down, mix_norm, w_in, q_norm, k_norm,
           lambda_q1, lambda_k1, lambda_q2, lambda_k2, attn_out_norm, conv_w, conv_b, dt_bias, a_log, d_skip,
           ssd_norm, w_out, ffn2_norm, ffn2_w_gate, ffn2_w_up, ffn2_w_down):
    bsz, seq, d = x.shape
    assert ffn1_norm.shape[0] == 1, "single-layer block"
    aw = N_DIFF_HEADS * V_HEAD_DIM
    sw = N_SSD_HEADS * SSD_HEAD_DIM
    cdim = sw + 2 * N_SSD_GROUPS * D_STATE
    lam_init = 0.8 - 0.6 * math.exp(-0.3 * 0)
    row = lambda v: v.reshape(1, -1)

    w_in0 = w_in[0]
    w_qk = w_in0[:, :2 * aw].astype(BF16)
    w_v_t = w_in0[:, 2 * aw:3 * aw].T.astype(BF16)
    w_z = w_in0[:, 3 * aw:3 * aw + sw].astype(BF16)
    w_xbc = w_in0[:, 3 * aw + sw:3 * aw + sw + cdim].astype(BF16)
    w_dt = w_in0[:, 3 * aw + sw + cdim:].astype(BF16)
    w_dt_pad = jnp.pad(w_dt, ((0, 0), (0, LANES - N_SSD_HEADS)))
    w_dt_t = w_dt.T

    n_sub = aw // DIFF_HEAD_DIM
    q_gain = jnp.tile(q_norm[0], n_sub) * (DIFF_HEAD_DIM ** -0.5 * math.log2(math.e))
    k_gain = jnp.tile(k_norm[0], n_sub)
    tn_qk = 512
    qk_gain = jnp.concatenate([q_gain, k_gain]).reshape(2 * aw // tn_qk, 1, tn_qk)
    rope = _rope_tables(N_META + seq)
    gi = np.arange(LANES) // DIFF_HEAD_DIM
    bd = jnp.asarray((gi[:, None] == gi[None, :]).astype(np.float32) / DIFF_HEAD_DIM, dtype=BF16)

    wg1, wu1, wd1 = ffn1_w_gate[0], ffn1_w_up[0], ffn1_w_down[0]
    wg2, wu2, wd2 = ffn2_w_gate[0], ffn2_w_up[0], ffn2_w_down[0]

    x2 = x.reshape(bsz * seq, d)
    h1, hn = _ffn(x2, row(ffn1_norm[0]), wg1, wu1, wd1, row(mix_norm[0]), tm=1024, tf=256)
    _, hn_m = _ffn(meta_tokens.astype(F32), row(ffn1_norm[0]), wg1, wu1, wd1, row(mix_norm[0]), tm=N_META, tf=256)

    qk = _proj_qk(hn, w_qk, qk_gain, rope[:, N_META:], bd, tm=1024, tn=tn_qk)
    qk_m = _proj_qk(hn_m, w_qk, qk_gain, rope[:, :N_META], bd, tm=N_META, tn=tn_qk)
    v_t = _proj_t(w_v_t, hn, tn=512, tm=1024, name="proj_vt")
    v_t_m = _proj_t(w_v_t, hn_m, tn=512, tm=N_META, name="proj_vt_meta")
    z = _proj(hn, w_z, tm=1024, tn=512, name="proj_z")
    xbc = _proj(hn, w_xbc, tm=1024, tn=512, name="proj_xbc")
    xbc_m = _proj(hn_m, w_xbc, tm=N_META, tn=512, name="proj_xbc_meta")
    dt, dtt = _proj_dt(hn, w_dt_pad, w_dt_t, tm=1024)
    dt_m, dtt_m = _proj_dt(hn_m, w_dt_pad, w_dt_t, tm=N_META)

    kmeta = jnp.pad(qk_m[:, aw:], ((0, LANES - N_META), (0, 0)))
    vmeta_t = jnp.pad(v_t_m, ((0, 0), (0, LANES - N_META)))
    lam_vecs = jnp.stack([lambda_q1[0], lambda_k1[0], lambda_q2[0], lambda_k2[0]]).astype(F32)
    attn = _attention(qk, v_t, kmeta, vmeta_t, lam_vecs, attn_out_norm[0].reshape(V_HEAD_DIM, 1),
                      bsz=bsz, seq=seq, tq=512, lam_init=lam_init)

    lead = CHUNK - N_META
    ssd = _ssd(xbc, z, dt, dtt,
               jnp.pad(xbc_m, ((lead, 0), (0, 0))), jnp.pad(dt_m, ((lead, 0), (0, 0))),
               jnp.pad(dtt_m, ((0, 0), (lead, 0))),
               conv_w[0], conv_b[0], dt_bias[0], a_log[0], d_skip[0], ssd_norm[0],
               bsz=bsz, seq=seq, t_blk=256)

    w_o = w_out[0].astype(BF16)
    h2 = _outproj(h1, attn, ssd, w_o[:aw], w_o[aw:], tm=1024, tn=512)
    out, _ = _ffn(h2, row(ffn2_norm[0]), wg2, wu2, wd2, row(ffn2_norm[0]), tm=1024, tf=256)
    return out.reshape(bsz, seq, d)
```

```python
import functools
import math

import jax
import jax.numpy as jnp
import numpy as np
from jax import lax
from jax.experimental import pallas as pl
from jax.experimental.pallas import tpu as pltpu

F32 = jnp.float32
BF16 = jnp.bfloat16
HIGHEST = lax.Precision.HIGHEST

EPS = 1e-6
CHUNK = 64
N_META = 16
N_DIFF_HEADS = 8
DIFF_HEAD_DIM = 64
V_HEAD_DIM = 128
ROT_DIM = 16
ROPE_THETA = 500000.0
SSD_HEAD_DIM = 64
N_SSD_HEADS = 16
N_SSD_GROUPS = 2
D_STATE = 128
CONV_WIDTH = 4
LANES = 128
VMEM_LIMIT = 56 * 1024 * 1024


def _cparams(sem):
    return pltpu.CompilerParams(dimension_semantics=sem, vmem_limit_bytes=VMEM_LIMIT)


def _silu(v):
    return v * (1.0 / (1.0 + jnp.exp(-v)))


def _rms(v, w):
    ms = jnp.mean(v * v, axis=-1, keepdims=True)
    return v * lax.rsqrt(ms + EPS) * w


def _swiglu_step(xn, wg_ref, wu_ref, wd_ref):
    g = jnp.dot(xn, wg_ref[...], preferred_element_type=F32)
    u = jnp.dot(xn, wu_ref[...], preferred_element_type=F32)
    a = (_silu(g) * u).astype(BF16)
    return jnp.dot(a, wd_ref[...], preferred_element_type=F32)


def _qk_norm_rope(acc, gain, rope_ref, bd):
    cos, sin_lo, sin_hi = rope_ref[0], rope_ref[1], rope_ref[2]
    half = ROT_DIM // 2
    wide = bd.shape[0]
    out = []
    for c in range(acc.shape[1] // wide):
        y = acc[:, c * wide:(c + 1) * wide]
        y2 = y * y
        hi = y2.astype(BF16)
        lo = (y2 - hi.astype(F32)).astype(BF16)
        ms = jnp.dot(hi, bd, preferred_element_type=F32) + jnp.dot(lo, bd, preferred_element_type=F32)
        yn = y * lax.rsqrt(ms + EPS) * gain[:, c * wide:(c + 1) * wide]
        for s in range(wide // LANES):
            v = yn[:, s * LANES:(s + 1) * LANES]
            out.append(v * cos + pltpu.roll(v, half, 1) * sin_hi + pltpu.roll(v, LANES - half, 1) * sin_lo)
    return jnp.concatenate(out, axis=1)


def _stage1_kernel(x_ref, nw_ref, wg_ref, wu_ref, wd_ref, pnw_ref, wp_ref, wvt_ref, wdt_ref, wdtt_ref,
                   gain_ref, rope_ref, bd_ref,
                   h1_ref, qk_ref, z_ref, xbc_ref, vt_ref, dt_ref, dtt_ref, xn_sc, *, n_ff, n_qk, n_z, n_xbc):
    j = pl.program_id(1)
    t = j - n_ff
    nt = (((1,), (1,)), ((), ()))

    @pl.when(j == 0)
    def _():
        xn_sc[...] = _rms(x_ref[...], nw_ref[...]).astype(BF16)
        h1_ref[...] = jnp.zeros_like(h1_ref)

    @pl.when(j < n_ff)
    def _():
        h1_ref[...] += _swiglu_step(xn_sc[...], wg_ref, wu_ref, wd_ref)

    @pl.when(j == n_ff - 1)
    def _():
        h = x_ref[...] + 0.5 * h1_ref[...]
        h1_ref[...] = h
        xn_sc[...] = _rms(h, pnw_ref[...]).astype(BF16)

    def proj():
        return jnp.dot(xn_sc[...], wp_ref[...], preferred_element_type=F32)

    @pl.when((t >= 0) & (t < n_qk))
    def _():
        qk_ref[...] = _qk_norm_rope(proj(), gain_ref[0], rope_ref, bd_ref[...]).astype(qk_ref.dtype)

    @pl.when((t >= n_qk) & (t < n_qk + n_z))
    def _():
        z_ref[...] = proj().astype(z_ref.dtype)

    @pl.when((t >= n_qk + n_z) & (t < n_qk + n_z + n_xbc))
    def _():
        xbc_ref[...] = proj().astype(xbc_ref.dtype)

    @pl.when(t == n_qk + n_z + n_xbc - 1)
    def _():
        hn = xn_sc[...]
        dt_ref[...] = jnp.dot(hn, wdt_ref[...], preferred_element_type=F32)
        dtt_ref[...] = lax.dot_general(wdtt_ref[...], hn, nt, preferred_element_type=F32)

    @pl.when(t >= n_qk + n_z + n_xbc)
    def _():
        vt_ref[...] = lax.dot_general(wvt_ref[...], xn_sc[...], nt, preferred_element_type=F32).astype(vt_ref.dtype)


def _stage1(x2, nw, wg, wu, wd, post_nw, w_proj, w_v_t, w_dt_pad, w_dt_t, qk_gain, rope, bd, *, tm, tf, tn,
            n_qk, n_z, n_xbc):
    m, d = x2.shape
    n_ff = wg.shape[1] // tf
    n_p = n_qk + n_z + n_xbc
    n_v = w_v_t.shape[0] // tn
    nh = w_dt_t.shape[0]
    n_tab = rope.shape[1] // tm
    ff = lambda j: jnp.minimum(j, n_ff - 1)
    clip = lambda v, hi: jnp.clip(v, 0, hi - 1)
    outs = pl.pallas_call(
        functools.partial(_stage1_kernel, n_ff=n_ff, n_qk=n_qk, n_z=n_z, n_xbc=n_xbc),
        grid=(m // tm, n_ff + n_p + n_v),
        in_specs=[
            pl.BlockSpec((tm, d), lambda i, j: (i, 0)),
            pl.BlockSpec((1, d), lambda i, j: (0, 0)),
            pl.BlockSpec((d, tf), lambda i, j: (0, ff(j))),
            pl.BlockSpec((d, tf), lambda i, j: (0, ff(j))),
            pl.BlockSpec((tf, d), lambda i, j: (ff(j), 0)),
            pl.BlockSpec((1, d), lambda i, j: (0, 0)),
            pl.BlockSpec((d, tn), lambda i, j: (0, clip(j - n_ff, n_p))),
            pl.BlockSpec((tn, d), lambda i, j: (clip(j - n_ff - n_p, n_v), 0)),
            pl.BlockSpec((d, LANES), lambda i, j: (0, 0)),
            pl.BlockSpec((nh, d), lambda i, j: (0, 0)),
            pl.BlockSpec((1, 1, tn), lambda i, j: (clip(j - n_ff, n_qk), 0, 0)),
            pl.BlockSpec((3, tm, LANES), lambda i, j: (0, i % n_tab, 0)),
            pl.BlockSpec(bd.shape, lambda i, j: (0, 0)),
        ],
        out_specs=[
            pl.BlockSpec((tm, d), lambda i, j: (i, 0)),
            pl.BlockSpec((tm, tn), lambda i, j: (i, clip(j - n_ff, n_qk))),
            pl.BlockSpec((tm, tn), lambda i, j: (i, clip(j - n_ff - n_qk, n_z))),
            pl.BlockSpec((tm, tn), lambda i, j: (i, clip(j - n_ff - n_qk - n_z, n_xbc))),
            pl.BlockSpec((tn, tm), lambda i, j: (clip(j - n_ff - n_p, n_v), i)),
            pl.BlockSpec((tm, LANES), lambda i, j: (i, 0)),
            pl.BlockSpec((nh, tm), lambda i, j: (0, i)),
        ],
        out_shape=[
            jax.ShapeDtypeStruct((m, d), F32),
            jax.ShapeDtypeStruct((m, n_qk * tn), BF16),
            jax.ShapeDtypeStruct((m, n_z * tn), BF16),
            jax.ShapeDtypeStruct((m, n_xbc * tn), BF16),
            jax.ShapeDtypeStruct((n_v * tn, m), BF16),
            jax.ShapeDtypeStruct((m, LANES), F32),
            jax.ShapeDtypeStruct((nh, m), F32),
        ],
        scratch_shapes=[pltpu.VMEM((tm, d), BF16)],
        compiler_params=_cparams(("parallel", "arbitrary")),
        name="ffn1_proj",
    )(x2, nw, wg, wu, wd, post_nw, w_proj, w_v_t, w_dt_pad, w_dt_t, qk_gain, rope, bd)
    return outs


def _stage3_kernel(h1_ref, a_ref, s_ref, wo_ref, nw_ref, wg_ref, wu_ref, wd_ref, out_ref, h2_sc, xn_sc,
                   *, n_op, n_ff, tn):
    j = pl.program_id(1)
    ka = a_ref.shape[1]

    for c in range(n_op):
        @pl.when(j == c)
        def _(c=c):
            h2_sc[:, c * tn:(c + 1) * tn] = (
                h1_ref[...]
                + jnp.dot(a_ref[...], wo_ref[0:ka, :], preferred_element_type=F32)
                + jnp.dot(s_ref[...], wo_ref[ka:, :], preferred_element_type=F32))

    @pl.when(j == n_op)
    def _():
        xn_sc[...] = _rms(h2_sc[...], nw_ref[...]).astype(BF16)
        out_ref[...] = jnp.zeros_like(out_ref)

    @pl.when(j >= n_op)
    def _():
        out_ref[...] += _swiglu_step(xn_sc[...], wg_ref, wu_ref, wd_ref)

    @pl.when(j == n_op + n_ff - 1)
    def _():
        out_ref[...] = h2_sc[...] + 0.5 * out_ref[...]


def _stage3(h1, attn, ssd, w_o, nw, wg, wu, wd, *, tm, tf, tn):
    m, d = h1.shape
    ka = attn.shape[1]
    n_op = d // tn
    n_ff = wg.shape[1] // tf
    op = lambda j: jnp.minimum(j, n_op - 1)
    ff = lambda j: jnp.clip(j - n_op, 0, n_ff - 1)
    return pl.pallas_call(
        functools.partial(_stage3_kernel, n_op=n_op, n_ff=n_ff, tn=tn),
        grid=(m // tm, n_op + n_ff),
        in_specs=[
            pl.BlockSpec((tm, tn), lambda i, j: (i, op(j))),
            pl.BlockSpec((tm, ka), lambda i, j: (i, 0)),
            pl.BlockSpec((tm, ka), lambda i, j: (i, 0)),
            pl.BlockSpec((w_o.shape[0], tn), lambda i, j: (0, op(j))),
            pl.BlockSpec((1, d), lambda i, j: (0, 0)),
            pl.BlockSpec((d, tf), lambda i, j: (0, ff(j))),
            pl.BlockSpec((d, tf), lambda i, j: (0, ff(j))),
            pl.BlockSpec((tf, d), lambda i, j: (ff(j), 0)),
        ],
        out_specs=pl.BlockSpec((tm, d), lambda i, j: (i, 0)),
        out_shape=jax.ShapeDtypeStruct((m, d), F32),
        scratch_shapes=[pltpu.VMEM((tm, d), F32), pltpu.VMEM((tm, d), BF16)],
        compiler_params=_cparams(("parallel", "arbitrary")),
        name="outproj_ffn2",
    )(h1, attn, ssd, w_o, nw, wg, wu, wd)


def _attn_kernel(qa_tab, ka_tab, qb_tab, kb_tab, q_ref, k_ref, vt_ref, km_ref, vmt_ref, lam_ref, gain_ref,
                 out_ref, qs_sc, sm_sc, s0_sc, s1_sc, mx0_sc, mx1_sc, m_sc, l_sc, acc_sc,
                 *, tq, n_steps, lam_init):
    g = pl.program_id(2)
    qa = qa_tab[g]
    ka = ka_tab[g]
    qb = qb_tab[g]
    kb = kb_tab[g]
    has_a = g < n_steps
    has_b = g >= 1
    even = (g % 2) == 0
    nt = (((1,), (1,)), ((), ()))

    def stage_a(buf, diagonal):
        s_out, mx_out = buf
        s_t = lax.dot_general(k_ref[...], qs_sc[...], nt, preferred_element_type=F32)
        if diagonal:
            key = lax.broadcasted_iota(jnp.int32, s_t.shape, 0)
            qry = lax.broadcasted_iota(jnp.int32, s_t.shape, 1)
            s_t = jnp.where(key // CHUNK <= (qry & (tq - 1)) // CHUNK, s_t, -jnp.inf)
        s_out[...] = s_t
        mx_out[...] = jnp.max(s_t, axis=0, keepdims=True)

    def stage_b(buf, last):
        s_in, mx_in = buf
        m_prev = m_sc[...]
        m_new = jnp.maximum(m_prev, mx_in[...])
        alpha = jnp.exp2(m_prev - m_new)
        p = jnp.exp2(s_in[...] - m_new).astype(BF16)
        vt_aug = jnp.concatenate([vt_ref[...], jnp.ones((16, p.shape[0]), BF16)], axis=0)
        pv = jnp.dot(vt_aug, p, preferred_element_type=F32)
        l_sc[...] = alpha * l_sc[...] + pv[V_HEAD_DIM:V_HEAD_DIM + 1, :]
        acc_sc[...] = alpha * acc_sc[...] + pv[0:V_HEAD_DIM, :]
        m_sc[...] = m_new
        if last:
            lq1, lk1, lq2, lk2 = lam_ref[0:1, :], lam_ref[1:2, :], lam_ref[2:3, :], lam_ref[3:4, :]
            lam = (jnp.exp(jnp.sum(lq1 * lk1, axis=-1, keepdims=True))
                   - jnp.exp(jnp.sum(lq2 * lk2, axis=-1, keepdims=True)) + lam_init)
            o = acc_sc[...] / l_sc[...]
            o = o[:, 0:tq] - lam * o[:, tq:2 * tq]
            ms = jnp.mean(o * o, axis=0, keepdims=True)
            o = o * lax.rsqrt(ms + EPS) * (gain_ref[...] * (1.0 - lam_init))
            out_ref[...] = o.T.astype(out_ref.dtype)

    @pl.when(has_b & (kb == 0))
    def _():
        s_t = sm_sc[...]
        m0 = jnp.max(s_t, axis=0, keepdims=True)
        p = jnp.exp2(s_t - m0)
        m_sc[...] = m0
        l_sc[...] = jnp.sum(p, axis=0, keepdims=True)
        acc_sc[...] = jnp.dot(vmt_ref[...], p.astype(BF16), preferred_element_type=F32)

    @pl.when(has_a & (ka == 0))
    def _():
        q = q_ref[...]
        lane = lax.broadcasted_iota(jnp.int32, q.shape, 1)
        zero = jnp.zeros_like(q)
        qs_sc[0:tq, :] = jnp.where(lane < DIFF_HEAD_DIM, q, zero)
        qs_sc[tq:2 * tq, :] = jnp.where(lane >= DIFF_HEAD_DIM, q, zero)
        s_t = lax.dot_general(km_ref[...], qs_sc[...], nt, preferred_element_type=F32)
        key = lax.broadcasted_iota(jnp.int32, s_t.shape, 0)
        sm_sc[...] = jnp.where(key < N_META, s_t, -jnp.inf)

    buf0, buf1 = (s0_sc, mx0_sc), (s1_sc, mx1_sc)
    a_diag = ka == qa
    b_last = kb == qb
    both = has_a & has_b
    for parity, (buf_a, buf_b) in enumerate(((buf0, buf1), (buf1, buf0))):
        par = even if parity == 0 else jnp.logical_not(even)

        @pl.when(par & both & jnp.logical_not(a_diag) & jnp.logical_not(b_last))
        def _(buf_a=buf_a, buf_b=buf_b):
            stage_a(buf_a, False)
            stage_b(buf_b, False)

        @pl.when(par & both & a_diag & jnp.logical_not(b_last))
        def _(buf_a=buf_a, buf_b=buf_b):
            stage_a(buf_a, True)
            stage_b(buf_b, False)

        @pl.when(par & both & b_last)
        def _(buf_a=buf_a, buf_b=buf_b):
            stage_a(buf_a, False)
            stage_b(buf_b, True)

        @pl.when(par & has_a & jnp.logical_not(has_b))
        def _(buf_a=buf_a):
            stage_a(buf_a, True)

        @pl.when(par & has_b & jnp.logical_not(has_a))
        def _(buf_b=buf_b):
            stage_b(buf_b, True)


def _attention(qk, vt, kmeta, vmeta_t, lam_vecs, gain_col, *, bsz, seq, tq, lam_init):
    nq = seq // tq
    h = N_DIFF_HEADS
    qi_list, ki_list = [], []
    for a in range(nq):
        for b in range(a + 1):
            qi_list.append(a)
            ki_list.append(b)
    n_steps = len(qi_list)
    ia = np.minimum(np.arange(n_steps + 1), n_steps - 1)
    ib = np.maximum(np.arange(n_steps + 1) - 1, 0)
    qi_arr, ki_arr = np.array(qi_list, np.int32), np.array(ki_list, np.int32)
    tabs = [jnp.asarray(t) for t in (qi_arr[ia], ki_arr[ia], qi_arr[ib], ki_arr[ib])]
    grid_spec = pltpu.PrefetchScalarGridSpec(
        num_scalar_prefetch=4,
        grid=(bsz, h, n_steps + 1),
        in_specs=[
            pl.BlockSpec((tq, V_HEAD_DIM), lambda b, hh, g, qa, ka, qb, kb: (b * nq + qa[g], hh)),
            pl.BlockSpec((tq, V_HEAD_DIM), lambda b, hh, g, qa, ka, qb, kb: (b * nq + ka[g], h + hh)),
            pl.BlockSpec((V_HEAD_DIM, tq), lambda b, hh, g, qa, ka, qb, kb: (hh, b * nq + kb[g])),
            pl.BlockSpec((LANES, V_HEAD_DIM), lambda b, hh, g, qa, ka, qb, kb: (0, hh)),
            pl.BlockSpec((V_HEAD_DIM, LANES), lambda b, hh, g, qa, ka, qb, kb: (hh, 0)),
            pl.BlockSpec((4, DIFF_HEAD_DIM), lambda b, hh, g, qa, ka, qb, kb: (0, 0)),
            pl.BlockSpec((V_HEAD_DIM, 1), lambda b, hh, g, qa, ka, qb, kb: (0, 0)),
        ],
        out_specs=pl.BlockSpec((tq, V_HEAD_DIM), lambda b, hh, g, qa, ka, qb, kb: (b * nq + qb[g], hh)),
        scratch_shapes=[
            pltpu.VMEM((2 * tq, V_HEAD_DIM), BF16),
            pltpu.VMEM((LANES, 2 * tq), F32),
            pltpu.VMEM((tq, 2 * tq), F32),
            pltpu.VMEM((tq, 2 * tq), F32),
            pltpu.VMEM((1, 2 * tq), F32),
            pltpu.VMEM((1, 2 * tq), F32),
            pltpu.VMEM((1, 2 * tq), F32),
            pltpu.VMEM((1, 2 * tq), F32),
            pltpu.VMEM((V_HEAD_DIM, 2 * tq), F32),
        ],
    )
    return pl.pallas_call(
        functools.partial(_attn_kernel, tq=tq, n_steps=n_steps, lam_init=lam_init),
        grid_spec=grid_spec,
        out_shape=jax.ShapeDtypeStruct((bsz * seq, h * V_HEAD_DIM), BF16),
        compiler_params=_cparams(("parallel", "parallel", "arbitrary")),
        name="attn",
    )(*tabs, qk, qk, vt, kmeta, vmeta_t, lam_vecs, gain_col)


def _ssd_kernel(xbc_ref, z_ref, dt_ref, dtt_ref, xbcm_ref, dtm_ref, dttm_ref,
                convw_ref, convb_ref, dtb_ref, dtbt_ref, alog_ref, alogt_ref, dskip_ref, nw_ref,
                expand_ref, tril_ref, triu2_ref, out_ref, xs_sc, xm_sc, state_sc, *, t_blk):
    n_heads, p_dim, n_state, n_groups = N_SSD_HEADS, SSD_HEAD_DIM, D_STATE, N_SSD_GROUPS
    width = n_heads * p_dim
    gw = width // n_groups
    n_pairs = n_heads // 2
    nt = (((1,), (1,)), ((), ()))
    expand = expand_ref[...]
    a_row = -jnp.exp(alog_ref[...])
    a_col = -jnp.exp(alogt_ref[...])
    d_full = dskip_ref[...]
    lane = lax.broadcasted_iota(jnp.int32, (CHUNK, LANES), 1)
    row = lax.broadcasted_iota(jnp.int32, (CHUNK, LANES), 0)
    left = lane < p_dim
    causal2 = (lane & (CHUNK - 1)) <= row

    def chunk(src_ref, r0, dt_raw, dtt_raw, pad_rows, z, out_r0):
        conv = convb_ref[...]
        for k in range(CONV_WIDTH):
            conv = conv + src_ref[pl.ds(r0 - (CONV_WIDTH - 1) + k, CHUNK), :] * convw_ref[k:k + 1, :]
        xc = _silu(conv)
        x_s = xc[:, :width]
        dt = jax.nn.softplus(dt_raw + dtb_ref[...])
        dtt = jax.nn.softplus(dtt_raw + dtbt_ref[...])
        if pad_rows:
            dt = jnp.where(row >= pad_rows, dt, 0.0)
            dtt = jnp.where(lax.broadcasted_iota(jnp.int32, dtt.shape, 1) >= pad_rows, dtt, 0.0)
        a_cs = jnp.dot(tril_ref[...], dt * a_row, precision=HIGHEST, preferred_element_type=F32)
        a_cs_t2 = jnp.dot(dtt * a_col, triu2_ref[...], precision=HIGHEST, preferred_element_type=F32)
        a_full = jnp.dot(a_cs, expand, precision=HIGHEST, preferred_element_type=F32)
        dt_full = jnp.dot(dt, expand, precision=HIGHEST, preferred_element_type=F32)
        a_last = a_full[CHUNK - 1:CHUNK, :]
        xdt = x_s * dt_full
        w_state = (xdt * jnp.exp(a_last - a_full)).astype(BF16)
        xdt_b = xdt.astype(BF16)
        zero_b = jnp.zeros((CHUNK, LANES), BF16)
        y_parts = []
        for g in range(n_groups):
            b_g = xc[:, width + g * n_state: width + (g + 1) * n_state]
            c_g = xc[:, width + (n_groups + g) * n_state: width + (n_groups + g + 1) * n_state]
            b_gb = b_g.astype(BF16)
            c_gb = c_g.astype(BF16)
            st = state_sc[g]
            if out_r0 is not None:
                cb2 = lax.dot_general(c_gb, jnp.concatenate([b_gb, b_gb], axis=0), nt,
                                      preferred_element_type=F32)
                y_off = jnp.dot(c_gb, st.astype(BF16), preferred_element_type=F32)
                y_g = y_off * jnp.exp(a_full[:, g * gw:(g + 1) * gw])
                diag = []
                for pp in range(n_pairs // n_groups):
                    pr = g * (n_pairs // n_groups) + pp
                    colb = a_full[:, pr * LANES:(pr + 1) * LANES]
                    rowb = jnp.where(left, a_cs_t2[2 * pr:2 * pr + 1, :], a_cs_t2[2 * pr + 1:2 * pr + 2, :])
                    dec = jnp.exp(jnp.where(causal2, colb - rowb, -jnp.inf))
                    m_pair = (cb2 * dec).astype(BF16)
                    xp = xdt_b[:, pr * LANES:(pr + 1) * LANES]
                    rhs = jnp.concatenate([jnp.where(left, xp, zero_b), jnp.where(left, zero_b, xp)], axis=0)
                    diag.append(jnp.dot(m_pair, rhs, preferred_element_type=F32))
                y_parts.append(y_g + jnp.concatenate(diag, axis=1))
            new_st = st * jnp.exp(a_last[:, g * gw:(g + 1) * gw]) + jnp.dot(
                b_g.T.astype(BF16), w_state[:, g * gw:(g + 1) * gw], preferred_element_type=F32)
            state_sc[g] = new_st
        if out_r0 is not None:
            y = jnp.concatenate(y_parts, axis=1) + x_s * d_full
            gated = y * _silu(z)
            out_ref[pl.ds(out_r0, CHUNK), :] = _rms(gated, nw_ref[...]).astype(out_ref.dtype)

    @pl.when(pl.program_id(1) == 0)
    def _():
        state_sc[...] = jnp.zeros_like(state_sc)
        xm_sc[0:8, :] = jnp.zeros((8, xm_sc.shape[1]), F32)
        xm_sc[8:8 + CHUNK, :] = xbcm_ref[...].astype(F32)
        chunk(xm_sc, 8, dtm_ref[...], dttm_ref[...], CHUNK - N_META, None, None)
        xs_sc[0:8, :] = xm_sc[CHUNK:CHUNK + 8, :]

    xs_sc[8:8 + t_blk, :] = xbc_ref[...].astype(F32)
    for c in range(t_blk // CHUNK):
        chunk(xs_sc, 8 + c * CHUNK, dt_ref[c * CHUNK:(c + 1) * CHUNK, :], dtt_ref[:, c * CHUNK:(c + 1) * CHUNK],
              0, z_ref[c * CHUNK:(c + 1) * CHUNK, :].astype(F32), c * CHUNK)
    xs_sc[0:8, :] = xs_sc[t_blk:t_blk + 8, :]


def _ssd(xbc, z, dt, dtt, xbc_m, dt_m, dtt_m, conv_w, conv_b, dt_bias, a_log, d_skip, norm_w,
         *, bsz, seq, t_blk):
    nb = seq // t_blk
    cdim = xbc.shape[1]
    width = N_SSD_HEADS * SSD_HEAD_DIM
    nh = N_SSD_HEADS

    def pad_row(v):
        return jnp.pad(v.reshape(1, nh), ((0, 0), (0, LANES - nh)))

    hh = np.arange(LANES)[:, None]
    ll = np.arange(width)[None, :]
    expand = jnp.asarray((ll // SSD_HEAD_DIM == hh).astype(np.float32))
    ii = np.arange(CHUNK)
    tril = jnp.asarray((ii[None, :] <= ii[:, None]).astype(np.float32))
    triu2 = jnp.asarray(np.tile((ii[:, None] <= ii[None, :]).astype(np.float32), (1, 2)))
    full = lambda shape: pl.BlockSpec(shape, lambda b, i: tuple(0 for _ in shape))
    return pl.pallas_call(
        functools.partial(_ssd_kernel, t_blk=t_blk),
        grid=(bsz, nb),
        in_specs=[
            pl.BlockSpec((t_blk, cdim), lambda b, i: (b * nb + i, 0)),
            pl.BlockSpec((t_blk, width), lambda b, i: (b * nb + i, 0)),
            pl.BlockSpec((t_blk, LANES), lambda b, i: (b * nb + i, 0)),
            pl.BlockSpec((nh, t_blk), lambda b, i: (0, b * nb + i)),
            full((CHUNK, cdim)), full((CHUNK, LANES)), full((nh, CHUNK)),
            full((CONV_WIDTH, cdim)), full((1, cdim)), full((1, LANES)), full((nh, 1)),
            full((1, LANES)), full((nh, 1)), full((1, width)), full((1, width)),
            full((LANES, width)), full((CHUNK, CHUNK)), full((CHUNK, 2 * CHUNK)),
        ],
        out_specs=pl.BlockSpec((t_blk, width), lambda b, i: (b * nb + i, 0)),
        out_shape=jax.ShapeDtypeStruct((bsz * seq, width), BF16),
        scratch_shapes=[
            pltpu.VMEM((t_blk + 8, cdim), F32),
            pltpu.VMEM((CHUNK + 8, cdim), F32),
            pltpu.VMEM((N_SSD_GROUPS, D_STATE, width // N_SSD_GROUPS), F32),
        ],
        compiler_params=_cparams(("parallel", "arbitrary")),
        name="ssd",
    )(xbc, z, dt, dtt, xbc_m, dt_m, dtt_m, conv_w, conv_b.reshape(1, cdim), pad_row(dt_bias),
      dt_bias.reshape(nh, 1), pad_row(a_log), a_log.reshape(nh, 1), jnp.repeat(d_skip, SSD_HEAD_DIM).reshape(1, width),
      norm_w.reshape(1, width), expand, tril, triu2)


def _rope_tables(n_pos):
    inv = jnp.power(ROPE_THETA, -jnp.arange(0, ROT_DIM, 2, dtype=F32) / ROT_DIM)
    ang = jnp.arange(n_pos, dtype=F32)[:, None] * inv[None, :]
    cos, sin = jnp.cos(ang), jnp.sin(ang)
    half = ROT_DIM // 2
    r = np.arange(LANES) % DIFF_HEAD_DIM
    idx = jnp.asarray(r % half)
    lo = jnp.asarray(r < half)[None, :]
    hi = jnp.asarray((r >= half) & (r < ROT_DIM))[None, :]
    cos_t = jnp.where(lo | hi, cos[:, idx], 1.0)
    sin_lo = jnp.where(lo, -sin[:, idx], 0.0)
    sin_hi = jnp.where(hi, sin[:, idx], 0.0)
    return jnp.stack([cos_t, sin_lo, sin_hi])


def kernel(x, meta_tokens, ffn1_norm, ffn1_w_gate, ffn1_w_up, ffn1_w_down, mix_norm, w_in, q_norm, k_norm,
           lambda_q1, lambda_k1, lambda_q2, lambda_k2, attn_out_norm, conv_w, conv_b, dt_bias, a_log, d_skip,
           ssd_norm, w_out, ffn2_norm, ffn2_w_gate, ffn2_w_up, ffn2_w_down):
    bsz, seq, d = x.shape
    assert ffn1_norm.shape[0] == 1, "single-layer block"
    aw = N_DIFF_HEADS * V_HEAD_DIM
    sw = N_SSD_HEADS * SSD_HEAD_DIM
    cdim = sw + 2 * N_SSD_GROUPS * D_STATE
    lam_init = 0.8 - 0.6 * math.exp(-0.3 * 0)
    row = lambda v: v.reshape(1, -1)

    tn = 512
    w_in0 = w_in[0]
    w_proj = jnp.concatenate([w_in0[:, :2 * aw], w_in0[:, 3 * aw:3 * aw + sw + cdim]], axis=1).astype(BF16)
    w_v_t = w_in0[:, 2 * aw:3 * aw].T.astype(BF16)
    w_dt = w_in0[:, 3 * aw + sw + cdim:].astype(BF16)
    w_dt_pad = jnp.pad(w_dt, ((0, 0), (0, LANES - N_SSD_HEADS)))
    w_dt_t = w_dt.T

    n_sub = aw // DIFF_HEAD_DIM
    q_gain = jnp.tile(q_norm[0], n_sub) * (DIFF_HEAD_DIM ** -0.5 * math.log2(math.e))
    k_gain = jnp.tile(k_norm[0], n_sub)
    qk_gain = jnp.concatenate([q_gain, k_gain]).reshape(2 * aw // tn, 1, tn)
    rope = _rope_tables(N_META + seq)
    gi = np.arange(2 * LANES) // DIFF_HEAD_DIM
    bd = jnp.asarray((gi[:, None] == gi[None, :]).astype(np.float32) / DIFF_HEAD_DIM, dtype=BF16)

    def ffn_weights(g, u, dn):
        return g[0].astype(BF16), u[0].astype(BF16), dn[0].astype(BF16)

    wg1, wu1, wd1 = ffn_weights(ffn1_w_gate, ffn1_w_up, ffn1_w_down)
    wg2, wu2, wd2 = ffn_weights(ffn2_w_gate, ffn2_w_up, ffn2_w_down)

    x2 = x.reshape(bsz * seq, d)
    stage1 = functools.partial(_stage1, nw=row(ffn1_norm[0]), wg=wg1, wu=wu1, wd=wd1, post_nw=row(mix_norm[0]),
                               w_proj=w_proj, w_v_t=w_v_t, w_dt_pad=w_dt_pad, w_dt_t=w_dt_t, qk_gain=qk_gain,
                               bd=bd, tf=512, tn=tn, n_qk=2 * aw // tn, n_z=sw // tn, n_xbc=cdim // tn)
    h1, qk, z, xbc, v_t, dt, dtt = stage1(x2, rope=rope[:, N_META:], tm=512)
    _, qk_m, _, xbc_m, v_t_m, dt_m, dtt_m = stage1(meta_tokens.astype(F32), rope=rope[:, :N_META], tm=N_META)

    kmeta = jnp.pad(qk_m[:, aw:], ((0, LANES - N_META), (0, 0)))
    vmeta_t = jnp.pad(v_t_m, ((0, 0), (0, LANES - N_META)))
    lam_vecs = jnp.stack([lambda_q1[0], lambda_k1[0], lambda_q2[0], lambda_k2[0]]).astype(F32)
    attn = _attention(qk, v_t, kmeta, vmeta_t, lam_vecs, attn_out_norm[0].reshape(V_HEAD_DIM, 1),
                      bsz=bsz, seq=seq, tq=512, lam_init=lam_init)

    lead = CHUNK - N_META
    ssd = _ssd(xbc, z, dt, dtt,
               jnp.pad(xbc_m, ((lead, 0), (0, 0))), jnp.pad(dt_m, ((lead, 0), (0, 0))),
               jnp.pad(dtt_m, ((0, 0), (lead, 0))),
               conv_w[0], conv_b[0], dt_bias[0], a_log[0], d_skip[0], ssd_norm[0],
               bsz=bsz, seq=seq, t_blk=256)

    out = _stage3(h1, attn, ssd, w_out[0].astype(BF16), row(ffn2_norm[0]), wg2, wu2, wd2, tm=512, tf=512, tn=tn)
    return out.reshape(bsz, seq, d)
```

```python
import functools
import math

import jax
import jax.numpy as jnp
import numpy as np
from jax import lax
from jax.experimental import pallas as pl
from jax.experimental.pallas import tpu as pltpu

F32 = jnp.float32
BF16 = jnp.bfloat16
HIGHEST = lax.Precision.HIGHEST

EPS = 1e-6
CHUNK = 64
N_META = 16
N_DIFF_HEADS = 8
DIFF_HEAD_DIM = 64
V_HEAD_DIM = 128
ROT_DIM = 16
ROPE_THETA = 500000.0
SSD_HEAD_DIM = 64
N_SSD_HEADS = 16
N_SSD_GROUPS = 2
D_STATE = 128
CONV_WIDTH = 4
LANES = 128
VMEM_LIMIT = 56 * 1024 * 1024
FFN_TM = 1024
FFN_TF = 256


def _cparams(sem):
    return pltpu.CompilerParams(dimension_semantics=sem, vmem_limit_bytes=VMEM_LIMIT)


def _silu(v):
    return v * (1.0 / (1.0 + jnp.exp(-v)))


def _rms(v, w):
    ms = jnp.mean(v * v, axis=-1, keepdims=True)
    return v * lax.rsqrt(ms + EPS) * w


def _swiglu_step(xn, wg_ref, wu_ref, wd_ref):
    g = jnp.dot(xn, wg_ref[...], preferred_element_type=F32)
    u = jnp.dot(xn, wu_ref[...], preferred_element_type=F32)
    a = (_silu(g) * u).astype(BF16)
    return jnp.dot(a, wd_ref[...], preferred_element_type=F32)


def _qk_norm_rope(acc, gain, rope_ref, bd):
    cos, sin_lo, sin_hi = rope_ref[0], rope_ref[1], rope_ref[2]
    half = ROT_DIM // 2
    wide = bd.shape[0]
    out = []
    for c in range(acc.shape[1] // wide):
        y = acc[:, c * wide:(c + 1) * wide]
        y2 = y * y
        hi = y2.astype(BF16)
        lo = (y2 - hi.astype(F32)).astype(BF16)
        ms = jnp.dot(hi, bd, preferred_element_type=F32) + jnp.dot(lo, bd, preferred_element_type=F32)
        yn = y * lax.rsqrt(ms + EPS) * gain[:, c * wide:(c + 1) * wide]
        for s in range(wide // LANES):
            v = yn[:, s * LANES:(s + 1) * LANES]
            out.append(v * cos + pltpu.roll(v, half, 1) * sin_hi + pltpu.roll(v, LANES - half, 1) * sin_lo)
    return jnp.concatenate(out, axis=1)


def _ffn_kernel(x_ref, nw_ref, wg_ref, wu_ref, wd_ref, *rest, with_norm):
    if with_norm:
        pnw_ref, out_ref, hn_ref, xn_sc = rest
    else:
        out_ref, xn_sc = rest
    j = pl.program_id(1)

    @pl.when(j == 0)
    def _():
        xn_sc[...] = _rms(x_ref[...], nw_ref[...]).astype(BF16)
        out_ref[...] = jnp.zeros_like(out_ref)

    out_ref[...] += _swiglu_step(xn_sc[...], wg_ref, wu_ref, wd_ref)

    @pl.when(j == pl.num_programs(1) - 1)
    def _():
        h = x_ref[...] + 0.5 * out_ref[...]
        out_ref[...] = h
        if with_norm:
            hn_ref[...] = _rms(h, pnw_ref[...]).astype(BF16)


def _ffn(x2, nw, wg, wu, wd, post_nw=None, *, tm, tf):
    m, d = x2.shape
    f = wg.shape[1]
    with_norm = post_nw is not None
    row_spec = pl.BlockSpec((tm, d), lambda i, j: (i, 0))
    vec_spec = pl.BlockSpec((1, d), lambda i, j: (0, 0))
    in_specs = [pl.BlockSpec((tm, d), lambda i, j: (i, 0), pipeline_mode=pl.Buffered(1)), vec_spec,
                pl.BlockSpec((d, tf), lambda i, j: (0, j)),
                pl.BlockSpec((d, tf), lambda i, j: (0, j)),
                pl.BlockSpec((tf, d), lambda i, j: (j, 0))]
    args = [x2, nw, wg, wu, wd]
    out_specs, out_shape = [row_spec], [jax.ShapeDtypeStruct((m, d), F32)]
    if with_norm:
        in_specs.append(vec_spec)
        args.append(post_nw)
        out_specs.append(row_spec)
        out_shape.append(jax.ShapeDtypeStruct((m, d), BF16))
    return pl.pallas_call(
        functools.partial(_ffn_kernel, with_norm=with_norm),
        grid=(m // tm, f // tf),
        in_specs=in_specs,
        out_specs=out_specs,
        out_shape=out_shape,
        scratch_shapes=[pltpu.VMEM((tm, d), BF16)],
        compiler_params=_cparams(("parallel", "arbitrary")),
        name="ffn",
    )(*args)


def _proj_kernel(x_ref, w_ref, out_ref):
    out_ref[...] = jnp.dot(x_ref[...], w_ref[...], preferred_element_type=F32).astype(out_ref.dtype)


def _proj(x2, w, *, tm, tn, name):
    m, d = x2.shape
    n = w.shape[1]
    return pl.pallas_call(
        _proj_kernel,
        grid=(m // tm, n // tn),
        in_specs=[pl.BlockSpec((tm, d), lambda i, j: (i, 0)), pl.BlockSpec((d, tn), lambda i, j: (0, j))],
        out_specs=pl.BlockSpec((tm, tn), lambda i, j: (i, j)),
        out_shape=jax.ShapeDtypeStruct((m, n), BF16),
        compiler_params=_cparams(("parallel", "arbitrary")),
        name=name,
    )(x2, w)


def _proj_t_kernel(w_ref, x_ref, out_ref):
    out_ref[...] = lax.dot_general(w_ref[...], x_ref[...], (((1,), (1,)), ((), ())),
                                   preferred_element_type=F32).astype(out_ref.dtype)


def _proj_t(w_t, x2, *, tn, tm, name):
    n, d = w_t.shape
    m = x2.shape[0]
    return pl.pallas_call(
        _proj_t_kernel,
        grid=(m // tm, n // tn),
        in_specs=[pl.BlockSpec((tn, d), lambda i, j: (j, 0)), pl.BlockSpec((tm, d), lambda i, j: (i, 0))],
        out_specs=pl.BlockSpec((tn, tm), lambda i, j: (j, i)),
        out_shape=jax.ShapeDtypeStruct((n, m), BF16),
        compiler_params=_cparams(("parallel", "arbitrary")),
        name=name,
    )(w_t, x2)


def _proj_qk_kernel(x_ref, w_ref, gain_ref, rope_ref, bd_ref, out_ref):
    acc = jnp.dot(x_ref[...], w_ref[...], preferred_element_type=F32)
    out_ref[...] = _qk_norm_rope(acc, gain_ref[0], rope_ref, bd_ref[...]).astype(out_ref.dtype)


def _proj_qk(x2, w, gain, rope, bd, *, tm, tn):
    m, d = x2.shape
    n = w.shape[1]
    n_tab = rope.shape[1] // tm
    return pl.pallas_call(
        _proj_qk_kernel,
        grid=(m // tm, n // tn),
        in_specs=[
            pl.BlockSpec((tm, d), lambda i, j: (i, 0)),
            pl.BlockSpec((d, tn), lambda i, j: (0, j)),
            pl.BlockSpec((1, 1, tn), lambda i, j: (j, 0, 0)),
            pl.BlockSpec((3, tm, LANES), lambda i, j: (0, i % n_tab, 0)),
            pl.BlockSpec(bd.shape, lambda i, j: (0, 0)),
        ],
        out_specs=pl.BlockSpec((tm, tn), lambda i, j: (i, j)),
        out_shape=jax.ShapeDtypeStruct((m, n), BF16),
        compiler_params=_cparams(("parallel", "arbitrary")),
        name="proj_qk",
    )(x2, w, gain, rope, bd)


def _proj_dt_kernel(x_ref, w_ref, wt_ref, out_ref, outt_ref):
    x = x_ref[...]
    out_ref[...] = jnp.dot(x, w_ref[...], preferred_element_type=F32)
    outt_ref[...] = lax.dot_general(wt_ref[...], x, (((1,), (1,)), ((), ())), preferred_element_type=F32)


def _proj_dt(x2, w_pad, w_t, *, tm):
    m, d = x2.shape
    nh = w_t.shape[0]
    return pl.pallas_call(
        _proj_dt_kernel,
        grid=(m // tm,),
        in_specs=[
            pl.BlockSpec((tm, d), lambda i: (i, 0)),
            pl.BlockSpec((d, LANES), lambda i: (0, 0)),
            pl.BlockSpec((nh, d), lambda i: (0, 0)),
        ],
        out_specs=[pl.BlockSpec((tm, LANES), lambda i: (i, 0)), pl.BlockSpec((nh, tm), lambda i: (0, i))],
        out_shape=[jax.ShapeDtypeStruct((m, LANES), F32), jax.ShapeDtypeStruct((nh, m), F32)],
        compiler_params=_cparams(("parallel",)),
        name="proj_dt",
    )(x2, w_pad, w_t)


def _outproj_kernel(h_ref, a_ref, s_ref, wa_ref, ws_ref, out_ref):
    out_ref[...] = (h_ref[...]
                    + jnp.dot(a_ref[...], wa_ref[...], preferred_element_type=F32)
                    + jnp.dot(s_ref[...], ws_ref[...], preferred_element_type=F32))


def _outproj(h1, attn, ssd, wa, ws, *, tm, tn):
    m, d = h1.shape
    k = attn.shape[1]
    return pl.pallas_call(
        _outproj_kernel,
        grid=(m // tm, d // tn),
        in_specs=[
            pl.BlockSpec((tm, tn), lambda i, j: (i, j)),
            pl.BlockSpec((tm, k), lambda i, j: (i, 0)),
            pl.BlockSpec((tm, k), lambda i, j: (i, 0)),
            pl.BlockSpec((k, tn), lambda i, j: (0, j)),
            pl.BlockSpec((k, tn), lambda i, j: (0, j)),
        ],
        out_specs=pl.BlockSpec((tm, tn), lambda i, j: (i, j)),
        out_shape=jax.ShapeDtypeStruct((m, d), F32),
        compiler_params=_cparams(("parallel", "arbitrary")),
        name="outproj",
    )(h1, attn, ssd, wa, ws)


def _attn_kernel(qa_tab, ka_tab, qb_tab, kb_tab, q_ref, k_ref, vt_ref, km_ref, vmt_ref, lam_ref, gain_ref,
                 out_ref, qs_sc, sm_sc, s0_sc, s1_sc, mx0_sc, mx1_sc, m_sc, l_sc, acc_sc,
                 *, tq, n_steps, lam_init):
    g = pl.program_id(2)
    qa = qa_tab[g]
    ka = ka_tab[g]
    qb = qb_tab[g]
    kb = kb_tab[g]
    has_a = g < n_steps
    has_b = g >= 1
    even = (g % 2) == 0
    nt = (((1,), (1,)), ((), ()))

    def stage_a(buf, diagonal):
        s_out, mx_out = buf
        s_t = lax.dot_general(k_ref[...], qs_sc[...], nt, preferred_element_type=F32)
        if diagonal:
            key = lax.broadcasted_iota(jnp.int32, s_t.shape, 0)
            qry = lax.broadcasted_iota(jnp.int32, s_t.shape, 1)
            s_t = jnp.where(key // CHUNK <= (qry & (tq - 1)) // CHUNK, s_t, -jnp.inf)
        s_out[...] = s_t
        mx_out[...] = jnp.max(s_t, axis=0, keepdims=True)

    def stage_b(buf, last):
        s_in, mx_in = buf
        m_prev = m_sc[...]
        m_new = jnp.maximum(m_prev, mx_in[...])
        alpha = jnp.exp2(m_prev - m_new)
        p = jnp.exp2(s_in[...] - m_new).astype(BF16)
        vt_aug = jnp.concatenate([vt_ref[...], jnp.ones((16, p.shape[0]), BF16)], axis=0)
        pv = jnp.dot(vt_aug, p, preferred_element_type=F32)
        l_sc[...] = alpha * l_sc[...] + pv[V_HEAD_DIM:V_HEAD_DIM + 1, :]
        acc_sc[...] = alpha * acc_sc[...] + pv[0:V_HEAD_DIM, :]
        m_sc[...] = m_new
        if last:
            lq1, lk1, lq2, lk2 = lam_ref[0:1, :], lam_ref[1:2, :], lam_ref[2:3, :], lam_ref[3:4, :]
            lam = (jnp.exp(jnp.sum(lq1 * lk1, axis=-1, keepdims=True))
                   - jnp.exp(jnp.sum(lq2 * lk2, axis=-1, keepdims=True)) + lam_init)
            o = acc_sc[...] / l_sc[...]
            o = o[:, 0:tq] - lam * o[:, tq:2 * tq]
            ms = jnp.mean(o * o, axis=0, keepdims=True)
            o = o * lax.rsqrt(ms + EPS) * (gain_ref[...] * (1.0 - lam_init))
            out_ref[...] = o.T.astype(out_ref.dtype)

    @pl.when(has_b & (kb == 0))
    def _():
        s_t = sm_sc[...]
        m0 = jnp.max(s_t, axis=0, keepdims=True)
        p = jnp.exp2(s_t - m0)
        m_sc[...] = m0
        l_sc[...] = jnp.sum(p, axis=0, keepdims=True)
        acc_sc[...] = jnp.dot(vmt_ref[...], p.astype(BF16), preferred_element_type=F32)

    @pl.when(has_a & (ka == 0))
    def _():
        q = q_ref[...]
        lane = lax.broadcasted_iota(jnp.int32, q.shape, 1)
        zero = jnp.zeros_like(q)
        qs_sc[0:tq, :] = jnp.where(lane < DIFF_HEAD_DIM, q, zero)
        qs_sc[tq:2 * tq, :] = jnp.where(lane >= DIFF_HEAD_DIM, q, zero)
        s_t = lax.dot_general(km_ref[...], qs_sc[...], nt, preferred_element_type=F32)
        key = lax.broadcasted_iota(jnp.int32, s_t.shape, 0)
        sm_sc[...] = jnp.where(key < N_META, s_t, -jnp.inf)

    buf0, buf1 = (s0_sc, mx0_sc), (s1_sc, mx1_sc)
    a_diag = ka == qa
    b_last = kb == qb
    both = has_a & has_b
    for parity, (buf_a, buf_b) in enumerate(((buf0, buf1), (buf1, buf0))):
        par = even if parity == 0 else jnp.logical_not(even)

        @pl.when(par & both & jnp.logical_not(a_diag) & jnp.logical_not(b_last))
        def _(buf_a=buf_a, buf_b=buf_b):
            stage_a(buf_a, False)
            stage_b(buf_b, False)

        @pl.when(par & both & a_diag & jnp.logical_not(b_last))
        def _(buf_a=buf_a, buf_b=buf_b):
            stage_a(buf_a, True)
            stage_b(buf_b, False)

        @pl.when(par & both & b_last)
        def _(buf_a=buf_a, buf_b=buf_b):
            stage_a(buf_a, False)
            stage_b(buf_b, True)

        @pl.when(par & has_a & jnp.logical_not(has_b))
        def _(buf_a=buf_a):
            stage_a(buf_a, True)

        @pl.when(par & has_b & jnp.logical_not(has_a))
        def _(buf_b=buf_b):
            stage_b(buf_b, True)


def _attention(qk, vt, kmeta, vmeta_t, lam_vecs, gain_col, *, bsz, seq, tq, lam_init):
    nq = seq // tq
    h = N_DIFF_HEADS
    qi_list, ki_list = [], []
    for a in range(nq):
        for b in range(a + 1):
            qi_list.append(a)
            ki_list.append(b)
    n_steps = len(qi_list)
    ia = np.minimum(np.arange(n_steps + 1), n_steps - 1)
    ib = np.maximum(np.arange(n_steps + 1) - 1, 0)
    qi_arr, ki_arr = np.array(qi_list, np.int32), np.array(ki_list, np.int32)
    tabs = [jnp.asarray(t) for t in (qi_arr[ia], ki_arr[ia], qi_arr[ib], ki_arr[ib])]
    grid_spec = pltpu.PrefetchScalarGridSpec(
        num_scalar_prefetch=4,
        grid=(bsz, h, n_steps + 1),
        in_specs=[
            pl.BlockSpec((tq, V_HEAD_DIM), lambda b, hh, g, qa, ka, qb, kb: (b * nq + qa[g], hh)),
            pl.BlockSpec((tq, V_HEAD_DIM), lambda b, hh, g, qa, ka, qb, kb: (b * nq + ka[g], h + hh)),
            pl.BlockSpec((V_HEAD_DIM, tq), lambda b, hh, g, qa, ka, qb, kb: (hh, b * nq + kb[g])),
            pl.BlockSpec((LANES, V_HEAD_DIM), lambda b, hh, g, qa, ka, qb, kb: (0, hh)),
            pl.BlockSpec((V_HEAD_DIM, LANES), lambda b, hh, g, qa, ka, qb, kb: (hh, 0)),
            pl.BlockSpec((4, DIFF_HEAD_DIM), lambda b, hh, g, qa, ka, qb, kb: (0, 0)),
            pl.BlockSpec((V_HEAD_DIM, 1), lambda b, hh, g, qa, ka, qb, kb: (0, 0)),
        ],
        out_specs=pl.BlockSpec((tq, V_HEAD_DIM), lambda b, hh, g, qa, ka, qb, kb: (b * nq + qb[g], hh)),
        scratch_shapes=[
            pltpu.VMEM((2 * tq, V_HEAD_DIM), BF16),
            pltpu.VMEM((LANES, 2 * tq), F32),
            pltpu.VMEM((tq, 2 * tq), F32),
            pltpu.VMEM((tq, 2 * tq), F32),
            pltpu.VMEM((1, 2 * tq), F32),
            pltpu.VMEM((1, 2 * tq), F32),
            pltpu.VMEM((1, 2 * tq), F32),
            pltpu.VMEM((1, 2 * tq), F32),
            pltpu.VMEM((V_HEAD_DIM, 2 * tq), F32),
        ],
    )
    return pl.pallas_call(
        functools.partial(_attn_kernel, tq=tq, n_steps=n_steps, lam_init=lam_init),
        grid_spec=grid_spec,
        out_shape=jax.ShapeDtypeStruct((bsz * seq, h * V_HEAD_DIM), BF16),
        compiler_params=_cparams(("parallel", "parallel", "arbitrary")),
        name="attn",
    )(*tabs, qk, qk, vt, kmeta, vmeta_t, lam_vecs, gain_col)


def _ssd_kernel(zx_ref, dt_ref, dtt_ref, xbcm_ref, dtm_ref, dttm_ref,
                convw_ref, convb_ref, dtb_ref, dtbt_ref, alog_ref, alogt_ref, dskip_ref, nw_ref,
                expand_ref, tril_ref, triu2_ref, out_ref, xs_sc, xm_sc, state_sc, *, t_blk):
    n_heads, p_dim, n_state, n_groups = N_SSD_HEADS, SSD_HEAD_DIM, D_STATE, N_SSD_GROUPS
    width = n_heads * p_dim
    gw = width // n_groups
    n_pairs = n_heads // 2
    nt = (((1,), (1,)), ((), ()))
    expand = expand_ref[...]
    a_row = -jnp.exp(alog_ref[...])
    a_col = -jnp.exp(alogt_ref[...])
    d_full = dskip_ref[...]
    lane = lax.broadcasted_iota(jnp.int32, (CHUNK, LANES), 1)
    row = lax.broadcasted_iota(jnp.int32, (CHUNK, LANES), 0)
    left = lane < p_dim
    causal2 = (lane & (CHUNK - 1)) <= row

    def chunk(src_ref, r0, dt_raw, dtt_raw, pad_rows, z, out_r0):
        conv = convb_ref[...]
        for k in range(CONV_WIDTH):
            conv = conv + src_ref[pl.ds(r0 - (CONV_WIDTH - 1) + k, CHUNK), :] * convw_ref[k:k + 1, :]
        xc = _silu(conv)
        x_s = xc[:, :width]
        dt = jax.nn.softplus(dt_raw + dtb_ref[...])
        dtt = jax.nn.softplus(dtt_raw + dtbt_ref[...])
        if pad_rows:
            dt = jnp.where(row >= pad_rows, dt, 0.0)
            dtt = jnp.where(lax.broadcasted_iota(jnp.int32, dtt.shape, 1) >= pad_rows, dtt, 0.0)
        a_cs = jnp.dot(tril_ref[...], dt * a_row, precision=HIGHEST, preferred_element_type=F32)
        a_cs_t2 = jnp.dot(dtt * a_col, triu2_ref[...], precision=HIGHEST, preferred_element_type=F32)
        a_full = jnp.dot(a_cs, expand, precision=HIGHEST, preferred_element_type=F32)
        dt_full = jnp.dot(dt, expand, precision=HIGHEST, preferred_element_type=F32)
        a_last = a_full[CHUNK - 1:CHUNK, :]
        xdt = x_s * dt_full
        w_state = (xdt * jnp.exp(a_last - a_full)).astype(BF16)
        xdt_b = xdt.astype(BF16)
        zero_b = jnp.zeros((CHUNK, LANES), BF16)
        y_parts = []
        for g in range(n_groups):
            b_g = xc[:, width + g * n_state: width + (g + 1) * n_state]
            c_g = xc[:, width + (n_groups + g) * n_state: width + (n_groups + g + 1) * n_state]
            b_gb = b_g.astype(BF16)
            c_gb = c_g.astype(BF16)
            st = state_sc[g]
            if out_r0 is not None:
                cb2 = lax.dot_general(c_gb, jnp.concatenate([b_gb, b_gb], axis=0), nt,
                                      preferred_element_type=F32)
                y_off = jnp.dot(c_gb, st.astype(BF16), preferred_element_type=F32)
                y_g = y_off * jnp.exp(a_full[:, g * gw:(g + 1) * gw])
                diag = []
                for pp in range(n_pairs // n_groups):
                    pr = g * (n_pairs // n_groups) + pp
                    colb = a_full[:, pr * LANES:(pr + 1) * LANES]
                    rowb = jnp.where(left, a_cs_t2[2 * pr:2 * pr + 1, :], a_cs_t2[2 * pr + 1:2 * pr + 2, :])
                    dec = jnp.exp(jnp.where(causal2, colb - rowb, -jnp.inf))
                    m_pair = (cb2 * dec).astype(BF16)
                    xp = xdt_b[:, pr * LANES:(pr + 1) * LANES]
                    rhs = jnp.concatenate([jnp.where(left, xp, zero_b), jnp.where(left, zero_b, xp)], axis=0)
                    diag.append(jnp.dot(m_pair, rhs, preferred_element_type=F32))
                y_parts.append(y_g + jnp.concatenate(diag, axis=1))
            new_st = st * jnp.exp(a_last[:, g * gw:(g + 1) * gw]) + jnp.dot(
                b_g.T.astype(BF16), w_state[:, g * gw:(g + 1) * gw], preferred_element_type=F32)
            state_sc[g] = new_st
        if out_r0 is not None:
            y = jnp.concatenate(y_parts, axis=1) + x_s * d_full
            gated = y * _silu(z)
            out_ref[pl.ds(out_r0, CHUNK), :] = _rms(gated, nw_ref[...]).astype(out_ref.dtype)

    @pl.when(pl.program_id(1) == 0)
    def _():
        state_sc[...] = jnp.zeros_like(state_sc)
        xm_sc[0:8, :] = jnp.zeros((8, xm_sc.shape[1]), F32)
        xm_sc[8:8 + CHUNK, :] = xbcm_ref[...].astype(F32)
        chunk(xm_sc, 8, dtm_ref[...], dttm_ref[...], CHUNK - N_META, None, None)
        xs_sc[0:8, :] = xm_sc[CHUNK:CHUNK + 8, :]

    xs_sc[8:8 + t_blk, :] = zx_ref[:, width:].astype(F32)
    for c in range(t_blk // CHUNK):
        chunk(xs_sc, 8 + c * CHUNK, dt_ref[c * CHUNK:(c + 1) * CHUNK, :], dtt_ref[:, c * CHUNK:(c + 1) * CHUNK],
              0, zx_ref[c * CHUNK:(c + 1) * CHUNK, 0:width].astype(F32), c * CHUNK)
    xs_sc[0:8, :] = xs_sc[t_blk:t_blk + 8, :]


def _ssd(zx, dt, dtt, xbc_m, dt_m, dtt_m, conv_w, conv_b, dt_bias, a_log, d_skip, norm_w,
         *, bsz, seq, t_blk):
    nb = seq // t_blk
    width = N_SSD_HEADS * SSD_HEAD_DIM
    cdim = zx.shape[1] - width
    nh = N_SSD_HEADS

    def pad_row(v):
        return jnp.pad(v.reshape(1, nh), ((0, 0), (0, LANES - nh)))

    hh = np.arange(LANES)[:, None]
    ll = np.arange(width)[None, :]
    expand = jnp.asarray((ll // SSD_HEAD_DIM == hh).astype(np.float32))
    ii = np.arange(CHUNK)
    tril = jnp.asarray((ii[None, :] <= ii[:, None]).astype(np.float32))
    triu2 = jnp.asarray(np.tile((ii[:, None] <= ii[None, :]).astype(np.float32), (1, 2)))
    full = lambda shape: pl.BlockSpec(shape, lambda b, i: tuple(0 for _ in shape))
    return pl.pallas_call(
        functools.partial(_ssd_kernel, t_blk=t_blk),
        grid=(bsz, nb),
        in_specs=[
            pl.BlockSpec((t_blk, width + cdim), lambda b, i: (b * nb + i, 0)),
            pl.BlockSpec((t_blk, LANES), lambda b, i: (b * nb + i, 0)),
            pl.BlockSpec((nh, t_blk), lambda b, i: (0, b * nb + i)),
            full((CHUNK, cdim)), full((CHUNK, LANES)), full((nh, CHUNK)),
            full((CONV_WIDTH, cdim)), full((1, cdim)), full((1, LANES)), full((nh, 1)),
            full((1, LANES)), full((nh, 1)), full((1, width)), full((1, width)),
            full((LANES, width)), full((CHUNK, CHUNK)), full((CHUNK, 2 * CHUNK)),
        ],
        out_specs=pl.BlockSpec((t_blk, width), lambda b, i: (b * nb + i, 0)),
        out_shape=jax.ShapeDtypeStruct((bsz * seq, width), BF16),
        scratch_shapes=[
            pltpu.VMEM((t_blk + 8, cdim), F32),
            pltpu.VMEM((CHUNK + 8, cdim), F32),
            pltpu.VMEM((N_SSD_GROUPS, D_STATE, width // N_SSD_GROUPS), F32),
        ],
        compiler_params=_cparams(("parallel", "arbitrary")),
        name="ssd",
    )(zx, dt, dtt, xbc_m, dt_m, dtt_m, conv_w, conv_b.reshape(1, cdim), pad_row(dt_bias),
      dt_bias.reshape(nh, 1), pad_row(a_log), a_log.reshape(nh, 1), jnp.repeat(d_skip, SSD_HEAD_DIM).reshape(1, width),
      norm_w.reshape(1, width), expand, tril, triu2)


def _rope_tables(n_pos):
    inv = jnp.power(ROPE_THETA, -jnp.arange(0, ROT_DIM, 2, dtype=F32) / ROT_DIM)
    ang = jnp.arange(n_pos, dtype=F32)[:, None] * inv[None, :]
    cos, sin = jnp.cos(ang), jnp.sin(ang)
    half = ROT_DIM // 2
    r = np.arange(LANES) % DIFF_HEAD_DIM
    idx = jnp.asarray(r % half)
    lo = jnp.asarray(r < half)[None, :]
    hi = jnp.asarray((r >= half) & (r < ROT_DIM))[None, :]
    cos_t = jnp.where(lo | hi, cos[:, idx], 1.0)
    sin_lo = jnp.where(lo, -sin[:, idx], 0.0)
    sin_hi = jnp.where(hi, sin[:, idx], 0.0)
    return jnp.stack([cos_t, sin_lo, sin_hi])


def kernel(x, meta_tokens, ffn1_norm, ffn1_w_gate, ffn1_w_up, ffn1_w_down, mix_norm, w_in, q_norm, k_norm,
           lambda_q1, lambda_k1, lambda_q2, lambda_k2, attn_out_norm, conv_w, conv_b, dt_bias, a_log, d_skip,
           ssd_norm, w_out, ffn2_norm, ffn2_w_gate, ffn2_w_up, ffn2_w_down):
    bsz, seq, d = x.shape
    assert ffn1_norm.shape[0] == 1, "single-layer block"
    aw = N_DIFF_HEADS * V_HEAD_DIM
    sw = N_SSD_HEADS * SSD_HEAD_DIM
    cdim = sw + 2 * N_SSD_GROUPS * D_STATE
    lam_init = 0.8 - 0.6 * math.exp(-0.3 * 0)
    row = lambda v: v.reshape(1, -1)

    tn = 512
    w_in0 = w_in[0]
    w_qk = w_in0[:, :2 * aw].astype(BF16)
    w_v_t = w_in0[:, 2 * aw:3 * aw].T.astype(BF16)
    w_zx = w_in0[:, 3 * aw:3 * aw + sw + cdim].astype(BF16)
    w_dt = w_in0[:, 3 * aw + sw + cdim:].astype(BF16)
    w_dt_pad = jnp.pad(w_dt, ((0, 0), (0, LANES - N_SSD_HEADS)))
    w_dt_t = w_dt.T

    n_sub = aw // DIFF_HEAD_DIM
    q_gain = jnp.tile(q_norm[0], n_sub) * (DIFF_HEAD_DIM ** -0.5 * math.log2(math.e))
    k_gain = jnp.tile(k_norm[0], n_sub)
    qk_gain = jnp.concatenate([q_gain, k_gain]).reshape(2 * aw // tn, 1, tn)
    rope = _rope_tables(N_META + seq)
    gi = np.arange(2 * LANES) // DIFF_HEAD_DIM
    bd = jnp.asarray((gi[:, None] == gi[None, :]).astype(np.float32) / DIFF_HEAD_DIM, dtype=BF16)

    def ffn_weights(g, u, dn):
        return g[0].astype(BF16), u[0].astype(BF16), dn[0].astype(BF16)

    wg1, wu1, wd1 = ffn_weights(ffn1_w_gate, ffn1_w_up, ffn1_w_down)
    wg2, wu2, wd2 = ffn_weights(ffn2_w_gate, ffn2_w_up, ffn2_w_down)

    x2 = x.reshape(bsz * seq, d)
    h1, hn = _ffn(x2, row(ffn1_norm[0]), wg1, wu1, wd1, row(mix_norm[0]), tm=FFN_TM, tf=FFN_TF)
    _, hn_m = _ffn(meta_tokens.astype(F32), row(ffn1_norm[0]), wg1, wu1, wd1, row(mix_norm[0]),
                   tm=N_META, tf=FFN_TF)

    qk = _proj_qk(hn, w_qk, qk_gain, rope[:, N_META:], bd, tm=1024, tn=tn)
    qk_m = _proj_qk(hn_m, w_qk, qk_gain, rope[:, :N_META], bd, tm=N_META, tn=tn)
    v_t = _proj_t(w_v_t, hn, tn=tn, tm=1024, name="proj_vt")
    v_t_m = _proj_t(w_v_t, hn_m, tn=tn, tm=N_META, name="proj_vt_meta")
    zx = _proj(hn, w_zx, tm=1024, tn=tn, name="proj_zx")
    xbc_m = _proj(hn_m, w_zx, tm=N_META, tn=tn, name="proj_zx_meta")[:, sw:]
    dt, dtt = _proj_dt(hn, w_dt_pad, w_dt_t, tm=1024)
    dt_m, dtt_m = _proj_dt(hn_m, w_dt_pad, w_dt_t, tm=N_META)

    kmeta = jnp.pad(qk_m[:, aw:], ((0, LANES - N_META), (0, 0)))
    vmeta_t = jnp.pad(v_t_m, ((0, 0), (0, LANES - N_META)))
    lam_vecs = jnp.stack([lambda_q1[0], lambda_k1[0], lambda_q2[0], lambda_k2[0]]).astype(F32)
    attn = _attention(qk, v_t, kmeta, vmeta_t, lam_vecs, attn_out_norm[0].reshape(V_HEAD_DIM, 1),
                      bsz=bsz, seq=seq, tq=512, lam_init=lam_init)

    lead = CHUNK - N_META
    ssd = _ssd(zx, dt, dtt,
               jnp.pad(xbc_m, ((lead, 0), (0, 0))), jnp.pad(dt_m, ((lead, 0), (0, 0))),
               jnp.pad(dtt_m, ((0, 0), (lead, 0))),
               conv_w[0], conv_b[0], dt_bias[0], a_log[0], d_skip[0], ssd_norm[0],
               bsz=bsz, seq=seq, t_blk=256)

    w_o = w_out[0].astype(BF16)
    h2 = _outproj(h1, attn, ssd, w_o[:aw], w_o[aw:], tm=1024, tn=tn)
    (out,) = _ffn(h2, row(ffn2_norm[0]), wg2, wu2, wd2, tm=FFN_TM, tf=FFN_TF)
    return out.reshape(bsz, seq, d)
```

```python
import functools
import math

import jax
import jax.numpy as jnp
import numpy as np
from jax import lax
from jax.experimental import pallas as pl
from jax.experimental.pallas import tpu as pltpu

F32 = jnp.float32
BF16 = jnp.bfloat16

EPS = 1e-6
CHUNK = 64
N_META = 16
N_DIFF_HEADS = 8
DIFF_HEAD_DIM = 64
V_HEAD_DIM = 128
ROT_DIM = 16
ROPE_THETA = 500000.0
SSD_HEAD_DIM = 64
N_SSD_HEADS = 16
N_SSD_GROUPS = 2
D_STATE = 128
CONV_WIDTH = 4
HALO = 64
LANES = 128
VMEM_LIMIT = 56 * 1024 * 1024
FFN_TM = 512
FFN_TF = 512


def _cparams(sem):
    return pltpu.CompilerParams(dimension_semantics=sem, vmem_limit_bytes=VMEM_LIMIT)


def _silu(v):
    half = 0.5 * v
    return half + half * jnp.tanh(half)


def _split_bf16(a, pieces):
    out = []
    for _ in range(pieces - 1):
        p = a.astype(BF16)
        out.append(p)
        a = a - p.astype(F32)
    out.append(a.astype(BF16))
    return out


def _dot_f32_by_01(a, m01, pieces=3):
    return sum(jnp.dot(p, m01, preferred_element_type=F32) for p in _split_bf16(a, pieces))


def _dot_01_by_f32(m01, a, pieces=3):
    return sum(jnp.dot(m01, p, preferred_element_type=F32) for p in _split_bf16(a, pieces))


def _rms(v, w):
    ms = jnp.mean(v * v, axis=-1, keepdims=True)
    return v * lax.rsqrt(ms + EPS) * w


def _swiglu_step(xn, wg_ref, wu_ref, wd_ref):
    g = jnp.dot(xn, wg_ref[...], preferred_element_type=F32)
    u = jnp.dot(xn, wu_ref[...], preferred_element_type=F32)
    a = (_silu(g) * u).astype(BF16)
    return jnp.dot(a, wd_ref[...], preferred_element_type=F32)


def _qk_norm_rope(acc, gain, rope_ref, bd):
    cos, sin_lo, sin_hi = rope_ref[0], rope_ref[1], rope_ref[2]
    half = ROT_DIM // 2
    wide = bd.shape[0]
    out = []
    for c in range(acc.shape[1] // wide):
        y = acc[:, c * wide:(c + 1) * wide]
        ms = sum(jnp.dot(p, bd, preferred_element_type=F32) for p in _split_bf16(y * y, 2))
        yn = y * lax.rsqrt(ms + EPS) * gain[:, c * wide:(c + 1) * wide]
        for s in range(wide // LANES):
            v = yn[:, s * LANES:(s + 1) * LANES]
            out.append(v * cos + pltpu.roll(v, half, 1) * sin_hi + pltpu.roll(v, LANES - half, 1) * sin_lo)
    return jnp.concatenate(out, axis=1)


def _ffn_kernel(x_ref, nw_ref, wg_ref, wu_ref, wd_ref, *rest, with_norm):
    if with_norm:
        pnw_ref, out_ref, hn_ref, xn_sc = rest
    else:
        out_ref, xn_sc = rest
    j = pl.program_id(1)

    @pl.when(j == 0)
    def _():
        xn_sc[...] = _rms(x_ref[...], nw_ref[...]).astype(BF16)
        out_ref[...] = jnp.zeros_like(out_ref)

    out_ref[...] += _swiglu_step(xn_sc[...], wg_ref, wu_ref, wd_ref)

    @pl.when(j == pl.num_programs(1) - 1)
    def _():
        h = x_ref[...] + 0.5 * out_ref[...]
        out_ref[...] = h
        if with_norm:
            hn_ref[...] = _rms(h, pnw_ref[...]).astype(BF16)


def _ffn(x2, nw, wg, wu, wd, post_nw=None, *, tm, tf):
    m, d = x2.shape
    f = wg.shape[1]
    with_norm = post_nw is not None
    row_spec = pl.BlockSpec((tm, d), lambda i, j: (i, 0))
    vec_spec = pl.BlockSpec((1, d), lambda i, j: (0, 0))
    in_specs = [row_spec, vec_spec,
                pl.BlockSpec((d, tf), lambda i, j: (0, j)),
                pl.BlockSpec((d, tf), lambda i, j: (0, j)),
                pl.BlockSpec((tf, d), lambda i, j: (j, 0))]
    args = [x2, nw, wg, wu, wd]
    out_specs, out_shape = [row_spec], [jax.ShapeDtypeStruct((m, d), F32)]
    if with_norm:
        in_specs.append(vec_spec)
        args.append(post_nw)
        out_specs.append(row_spec)
        out_shape.append(jax.ShapeDtypeStruct((m, d), BF16))
    return pl.pallas_call(
        functools.partial(_ffn_kernel, with_norm=with_norm),
        grid=(m // tm, f // tf),
        in_specs=in_specs,
        out_specs=out_specs,
        out_shape=out_shape,
        scratch_shapes=[pltpu.VMEM((tm, d), BF16)],
        compiler_params=_cparams(("parallel", "arbitrary")),
        name="ffn",
    )(*args)


def _proj_kernel(x_ref, w_ref, out_ref):
    out_ref[...] = jnp.dot(x_ref[...], w_ref[...], preferred_element_type=F32).astype(out_ref.dtype)


def _proj(x2, w, *, tm, tn, name):
    m, d = x2.shape
    n = w.shape[1]
    return pl.pallas_call(
        _proj_kernel,
        grid=(m // tm, n // tn),
        in_specs=[pl.BlockSpec((tm, d), lambda i, j: (i, 0)), pl.BlockSpec((d, tn), lambda i, j: (0, j))],
        out_specs=pl.BlockSpec((tm, tn), lambda i, j: (i, j)),
        out_shape=jax.ShapeDtypeStruct((m, n), BF16),
        compiler_params=_cparams(("parallel", "arbitrary")),
        name=name,
    )(x2, w)


def _proj_t_kernel(w_ref, x_ref, out_ref):
    out_ref[...] = lax.dot_general(w_ref[...], x_ref[...], (((1,), (1,)), ((), ())),
                                   preferred_element_type=F32).astype(out_ref.dtype)


def _proj_t(w_t, x2, *, tn, tm, name):
    n, d = w_t.shape
    m = x2.shape[0]
    return pl.pallas_call(
        _proj_t_kernel,
        grid=(m // tm, n // tn),
        in_specs=[pl.BlockSpec((tn, d), lambda i, j: (j, 0)), pl.BlockSpec((tm, d), lambda i, j: (i, 0))],
        out_specs=pl.BlockSpec((tn, tm), lambda i, j: (j, i)),
        out_shape=jax.ShapeDtypeStruct((n, m), BF16),
        compiler_params=_cparams(("parallel", "arbitrary")),
        name=name,
    )(w_t, x2)


def _proj_qk_kernel(x_ref, w_ref, gain_ref, rope_ref, bd_ref, out_ref):
    acc = jnp.dot(x_ref[...], w_ref[...], preferred_element_type=F32)
    out_ref[...] = _qk_norm_rope(acc, gain_ref[0], rope_ref, bd_ref[...]).astype(out_ref.dtype)


def _proj_qk(x2, w, gain, rope, bd, *, tm, tn):
    m, d = x2.shape
    n = w.shape[1]
    n_tab = rope.shape[1] // tm
    return pl.pallas_call(
        _proj_qk_kernel,
        grid=(m // tm, n // tn),
        in_specs=[
            pl.BlockSpec((tm, d), lambda i, j: (i, 0)),
            pl.BlockSpec((d, tn), lambda i, j: (0, j)),
            pl.BlockSpec((1, 1, tn), lambda i, j: (j, 0, 0)),
            pl.BlockSpec((3, tm, LANES), lambda i, j: (0, i % n_tab, 0)),
            pl.BlockSpec(bd.shape, lambda i, j: (0, 0)),
        ],
        out_specs=pl.BlockSpec((tm, tn), lambda i, j: (i, j)),
        out_shape=jax.ShapeDtypeStruct((m, n), BF16),
        compiler_params=_cparams(("parallel", "arbitrary")),
        name="proj_qk",
    )(x2, w, gain, rope, bd)


def _proj_dt_kernel(x_ref, w_ref, wt_ref, out_ref, outt_ref):
    x = x_ref[...]
    out_ref[...] = jnp.dot(x, w_ref[...], preferred_element_type=F32)
    outt_ref[...] = lax.dot_general(wt_ref[...], x, (((1,), (1,)), ((), ())), preferred_element_type=F32)


def _proj_dt(x2, w_pad, w_t, *, tm):
    m, d = x2.shape
    nh = w_t.shape[0]
    return pl.pallas_call(
        _proj_dt_kernel,
        grid=(m // tm,),
        in_specs=[
            pl.BlockSpec((tm, d), lambda i: (i, 0)),
            pl.BlockSpec((d, LANES), lambda i: (0, 0)),
            pl.BlockSpec((nh, d), lambda i: (0, 0)),
        ],
        out_specs=[pl.BlockSpec((tm, LANES), lambda i: (i, 0)), pl.BlockSpec((nh, tm), lambda i: (0, i))],
        out_shape=[jax.ShapeDtypeStruct((m, LANES), F32), jax.ShapeDtypeStruct((nh, m), F32)],
        compiler_params=_cparams(("parallel",)),
        name="proj_dt",
    )(x2, w_pad, w_t)


def _outproj_kernel(h_ref, a_ref, s_ref, wa_ref, ws_ref, out_ref):
    out_ref[...] = (h_ref[...]
                    + jnp.dot(a_ref[...], wa_ref[...], preferred_element_type=F32)
                    + jnp.dot(s_ref[...], ws_ref[...], preferred_element_type=F32))


def _outproj(h1, attn, ssd, wa, ws, *, tm, tn):
    m, d = h1.shape
    k = attn.shape[1]
    return pl.pallas_call(
        _outproj_kernel,
        grid=(m // tm, d // tn),
        in_specs=[
            pl.BlockSpec((tm, tn), lambda i, j: (i, j)),
            pl.BlockSpec((tm, k), lambda i, j: (i, 0)),
            pl.BlockSpec((tm, k), lambda i, j: (i, 0)),
            pl.BlockSpec((k, tn), lambda i, j: (0, j)),
            pl.BlockSpec((k, tn), lambda i, j: (0, j)),
        ],
        out_specs=pl.BlockSpec((tm, tn), lambda i, j: (i, j)),
        out_shape=jax.ShapeDtypeStruct((m, d), F32),
        compiler_params=_cparams(("parallel", "arbitrary")),
        name="outproj",
    )(h1, attn, ssd, wa, ws)


def _attn_kernel(qa_tab, ka_tab, qb_tab, kb_tab, q_ref, k_ref, vt_ref, km_ref, vmt_ref, lam_ref, gain_ref,
                 out_ref, qs_sc, sm_sc, s0_sc, s1_sc, mx0_sc, mx1_sc, m_sc, l_sc, acc_sc,
                 *, tq, n_steps, lam_init):
    g = pl.program_id(2)
    qa = qa_tab[g]
    ka = ka_tab[g]
    qb = qb_tab[g]
    kb = kb_tab[g]
    has_a = g < n_steps
    has_b = g >= 1
    even = (g % 2) == 0
    nt = (((1,), (1,)), ((), ()))

    def stage_a(buf, diagonal):
        s_out, mx_out = buf
        s_t = lax.dot_general(k_ref[...], qs_sc[...], nt, preferred_element_type=F32)
        if diagonal:
            key = lax.broadcasted_iota(jnp.int32, s_t.shape, 0)
            qry = lax.broadcasted_iota(jnp.int32, s_t.shape, 1)
            s_t = jnp.where(key // CHUNK <= (qry & (tq - 1)) // CHUNK, s_t, -jnp.inf)
        s_out[...] = s_t
        mx_out[...] = jnp.max(s_t, axis=0, keepdims=True)

    def stage_b(buf, last):
        s_in, mx_in = buf
        m_prev = m_sc[...]
        m_new = jnp.maximum(m_prev, mx_in[...])
        alpha = jnp.exp2(m_prev - m_new)
        p = jnp.exp2(s_in[...] - m_new).astype(BF16)
        vt_aug = jnp.concatenate([vt_ref[...], jnp.ones((16, p.shape[0]), BF16)], axis=0)
        pv = jnp.dot(vt_aug, p, preferred_element_type=F32)
        l_sc[...] = alpha * l_sc[...] + pv[V_HEAD_DIM:V_HEAD_DIM + 1, :]
        acc_sc[...] = alpha * acc_sc[...] + pv[0:V_HEAD_DIM, :]
        m_sc[...] = m_new
        if last:
            lq1, lk1, lq2, lk2 = lam_ref[0:1, :], lam_ref[1:2, :], lam_ref[2:3, :], lam_ref[3:4, :]
            lam = (jnp.exp(jnp.sum(lq1 * lk1, axis=-1, keepdims=True))
                   - jnp.exp(jnp.sum(lq2 * lk2, axis=-1, keepdims=True)) + lam_init)
            o = acc_sc[...] / l_sc[...]
            o = o[:, 0:tq] - lam * o[:, tq:2 * tq]
            ms = jnp.mean(o * o, axis=0, keepdims=True)
            o = o * lax.rsqrt(ms + EPS) * (gain_ref[...] * (1.0 - lam_init))
            out_ref[...] = o.T.astype(out_ref.dtype)

    @pl.when(has_b & (kb == 0))
    def _():
        s_t = sm_sc[...]
        m0 = jnp.max(s_t, axis=0, keepdims=True)
        p = jnp.exp2(s_t - m0)
        m_sc[...] = m0
        l_sc[...] = jnp.sum(p, axis=0, keepdims=True)
        acc_sc[...] = jnp.dot(vmt_ref[...], p.astype(BF16), preferred_element_type=F32)

    @pl.when(has_a & (ka == 0))
    def _():
        q = q_ref[...]
        lane = lax.broadcasted_iota(jnp.int32, q.shape, 1)
        zero = jnp.zeros_like(q)
        qs_sc[0:tq, :] = jnp.where(lane < DIFF_HEAD_DIM, q, zero)
        qs_sc[tq:2 * tq, :] = jnp.where(lane >= DIFF_HEAD_DIM, q, zero)
        s_t = lax.dot_general(km_ref[...], qs_sc[...], nt, preferred_element_type=F32)
        key = lax.broadcasted_iota(jnp.int32, s_t.shape, 0)
        sm_sc[...] = jnp.where(key < N_META, s_t, -jnp.inf)

    buf0, buf1 = (s0_sc, mx0_sc), (s1_sc, mx1_sc)
    a_diag = ka == qa
    b_last = kb == qb
    both = has_a & has_b
    for parity, (buf_a, buf_b) in enumerate(((buf0, buf1), (buf1, buf0))):
        par = even if parity == 0 else jnp.logical_not(even)

        @pl.when(par & both & jnp.logical_not(a_diag) & jnp.logical_not(b_last))
        def _(buf_a=buf_a, buf_b=buf_b):
            stage_a(buf_a, False)
            stage_b(buf_b, False)

        @pl.when(par & both & a_diag & jnp.logical_not(b_last))
        def _(buf_a=buf_a, buf_b=buf_b):
            stage_a(buf_a, True)
            stage_b(buf_b, False)

        @pl.when(par & both & b_last)
        def _(buf_a=buf_a, buf_b=buf_b):
            stage_a(buf_a, False)
            stage_b(buf_b, True)

        @pl.when(par & has_a & jnp.logical_not(has_b))
        def _(buf_a=buf_a):
            stage_a(buf_a, True)

        @pl.when(par & has_b & jnp.logical_not(has_a))
        def _(buf_b=buf_b):
            stage_b(buf_b, True)


def _attention(qk, vt, kmeta, vmeta_t, lam_vecs, gain_col, *, bsz, seq, tq, lam_init):
    nq = seq // tq
    h = N_DIFF_HEADS
    qi_list, ki_list = [], []
    for a in range(nq):
        for b in range(a + 1):
            qi_list.append(a)
            ki_list.append(b)
    n_steps = len(qi_list)
    ia = np.minimum(np.arange(n_steps + 1), n_steps - 1)
    ib = np.maximum(np.arange(n_steps + 1) - 1, 0)
    qi_arr, ki_arr = np.array(qi_list, np.int32), np.array(ki_list, np.int32)
    tabs = [jnp.asarray(t) for t in (qi_arr[ia], ki_arr[ia], qi_arr[ib], ki_arr[ib])]
    grid_spec = pltpu.PrefetchScalarGridSpec(
        num_scalar_prefetch=4,
        grid=(bsz, h, n_steps + 1),
        in_specs=[
            pl.BlockSpec((tq, V_HEAD_DIM), lambda b, hh, g, qa, ka, qb, kb: (b * nq + qa[g], hh)),
            pl.BlockSpec((tq, V_HEAD_DIM), lambda b, hh, g, qa, ka, qb, kb: (b * nq + ka[g], h + hh)),
            pl.BlockSpec((V_HEAD_DIM, tq), lambda b, hh, g, qa, ka, qb, kb: (hh, b * nq + kb[g])),
            pl.BlockSpec((LANES, V_HEAD_DIM), lambda b, hh, g, qa, ka, qb, kb: (0, hh)),
            pl.BlockSpec((V_HEAD_DIM, LANES), lambda b, hh, g, qa, ka, qb, kb: (hh, 0)),
            pl.BlockSpec((4, DIFF_HEAD_DIM), lambda b, hh, g, qa, ka, qb, kb: (0, 0)),
            pl.BlockSpec((V_HEAD_DIM, 1), lambda b, hh, g, qa, ka, qb, kb: (0, 0)),
        ],
        out_specs=pl.BlockSpec((tq, V_HEAD_DIM), lambda b, hh, g, qa, ka, qb, kb: (b * nq + qb[g], hh)),
        scratch_shapes=[
            pltpu.VMEM((2 * tq, V_HEAD_DIM), BF16),
            pltpu.VMEM((LANES, 2 * tq), F32),
            pltpu.VMEM((tq, 2 * tq), F32),
            pltpu.VMEM((tq, 2 * tq), F32),
            pltpu.VMEM((1, 2 * tq), F32),
            pltpu.VMEM((1, 2 * tq), F32),
            pltpu.VMEM((1, 2 * tq), F32),
            pltpu.VMEM((1, 2 * tq), F32),
            pltpu.VMEM((V_HEAD_DIM, 2 * tq), F32),
        ],
    )
    return pl.pallas_call(
        functools.partial(_attn_kernel, tq=tq, n_steps=n_steps, lam_init=lam_init),
        grid_spec=grid_spec,
        out_shape=jax.ShapeDtypeStruct((bsz * seq, h * V_HEAD_DIM), BF16),
        compiler_params=_cparams(("parallel", "parallel", "arbitrary")),
        name="attn",
    )(*tabs, qk, qk, vt, kmeta, vmeta_t, lam_vecs, gain_col)


def _ssd_kernel(zx_ref, dt_ref, dtt_ref, xbcm_ref, dtm_ref, dttm_ref,
                convw_ref, convb_ref, dtb_ref, dtbt_ref, alog_ref, alogt_ref, dskip_ref, nw_ref,
                expand_ref, tril_ref, triu2_ref, shift_ref, out_ref, xs_sc, xm_sc, state_sc, *, t_blk):
    n_heads, p_dim, n_state, n_groups = N_SSD_HEADS, SSD_HEAD_DIM, D_STATE, N_SSD_GROUPS
    width = n_heads * p_dim
    gw = width // n_groups
    n_pairs = n_heads // 2
    nt = (((1,), (1,)), ((), ()))
    expand = expand_ref[...]
    a_row = -jnp.exp(alog_ref[...])
    a_col = -jnp.exp(alogt_ref[...])
    d_full = dskip_ref[...]
    lane = lax.broadcasted_iota(jnp.int32, (CHUNK, LANES), 1)
    row = lax.broadcasted_iota(jnp.int32, (CHUNK, LANES), 0)
    left = lane < p_dim
    causal2 = (lane & (CHUNK - 1)) <= row

    def chunk(src_ref, r0, dt_raw, dtt_raw, pad_rows, z, out_r0):
        win = src_ref[r0 - HALO:r0 + CHUNK, :].astype(BF16)
        conv = convb_ref[...] + win[HALO:, :].astype(F32) * convw_ref[CONV_WIDTH - 1:CONV_WIDTH, :]
        for k in range(CONV_WIDTH - 1):
            conv = conv + jnp.dot(shift_ref[k], win, preferred_element_type=F32) * convw_ref[k:k + 1, :]
        xc = _silu(conv)
        x_s = xc[:, :width]
        dt = jax.nn.softplus(dt_raw + dtb_ref[...])
        dtt = jax.nn.softplus(dtt_raw + dtbt_ref[...])
        if pad_rows:
            dt = jnp.where(row >= pad_rows, dt, 0.0)
            dtt = jnp.where(lax.broadcasted_iota(jnp.int32, dtt.shape, 1) >= pad_rows, dtt, 0.0)
        a_cs = _dot_01_by_f32(tril_ref[...], dt * a_row)
        a_cs_t2 = _dot_f32_by_01(dtt * a_col, triu2_ref[...])
        a_full = _dot_f32_by_01(a_cs, expand)
        dt_full = _dot_f32_by_01(dt, expand)
        a_last = a_full[CHUNK - 1:CHUNK, :]
        xdt = x_s * dt_full
        w_state = (xdt * jnp.exp(a_last - a_full)).astype(BF16)
        xdt_b = xdt.astype(BF16)
        zero_b = jnp.zeros((CHUNK, LANES), BF16)
        y_parts = []
        for g in range(n_groups):
            b_g = xc[:, width + g * n_state: width + (g + 1) * n_state]
            c_g = xc[:, width + (n_groups + g) * n_state: width + (n_groups + g + 1) * n_state]
            b_gb = b_g.astype(BF16)
            c_gb = c_g.astype(BF16)
            st = state_sc[g]
            if out_r0 is not None:
                cb2 = lax.dot_general(c_gb, jnp.concatenate([b_gb, b_gb], axis=0), nt,
                                      preferred_element_type=F32)
                y_off = jnp.dot(c_gb, st.astype(BF16), preferred_element_type=F32)
                y_g = y_off * jnp.exp(a_full[:, g * gw:(g + 1) * gw])
                diag = []
                for pp in range(n_pairs // n_groups):
                    pr = g * (n_pairs // n_groups) + pp
                    colb = a_full[:, pr * LANES:(pr + 1) * LANES]
                    rowb = jnp.where(left, a_cs_t2[2 * pr:2 * pr + 1, :], a_cs_t2[2 * pr + 1:2 * pr + 2, :])
                    dec = jnp.exp(jnp.where(causal2, colb - rowb, -jnp.inf))
                    m_pair = (cb2 * dec).astype(BF16)
                    xp = xdt_b[:, pr * LANES:(pr + 1) * LANES]
                    rhs = jnp.concatenate([jnp.where(left, xp, zero_b), jnp.where(left, zero_b, xp)], axis=0)
                    diag.append(jnp.dot(m_pair, rhs, preferred_element_type=F32))
                y_parts.append(y_g + jnp.concatenate(diag, axis=1))
            new_st = st * jnp.exp(a_last[:, g * gw:(g + 1) * gw]) + jnp.dot(
                b_g.T.astype(BF16), w_state[:, g * gw:(g + 1) * gw], preferred_element_type=F32)
            state_sc[g] = new_st
        if out_r0 is not None:
            y = jnp.concatenate(y_parts, axis=1) + x_s * d_full
            gated = y * _silu(z)
            out_ref[pl.ds(out_r0, CHUNK), :] = _rms(gated, nw_ref[...]).astype(out_ref.dtype)

    @pl.when(pl.program_id(1) == 0)
    def _():
        state_sc[...] = jnp.zeros_like(state_sc)
        xm_sc[0:HALO, :] = jnp.zeros((HALO, xm_sc.shape[1]), F32)
        xm_sc[HALO:HALO + CHUNK, :] = xbcm_ref[...].astype(F32)
        chunk(xm_sc, HALO, dtm_ref[...], dttm_ref[...], CHUNK - N_META, None, None)
        xs_sc[0:HALO, :] = xm_sc[CHUNK:CHUNK + HALO, :]

    xs_sc[HALO:HALO + t_blk, :] = zx_ref[:, width:].astype(F32)
    for c in range(t_blk // CHUNK):
        chunk(xs_sc, HALO + c * CHUNK, dt_ref[c * CHUNK:(c + 1) * CHUNK, :],
              dtt_ref[:, c * CHUNK:(c + 1) * CHUNK], 0,
              zx_ref[c * CHUNK:(c + 1) * CHUNK, 0:width].astype(F32), c * CHUNK)
    xs_sc[0:HALO, :] = xs_sc[t_blk:t_blk + HALO, :]


def _ssd(zx, dt, dtt, xbc_m, dt_m, dtt_m, conv_w, conv_b, dt_bias, a_log, d_skip, norm_w,
         *, bsz, seq, t_blk):
    nb = seq // t_blk
    width = N_SSD_HEADS * SSD_HEAD_DIM
    cdim = zx.shape[1] - width
    nh = N_SSD_HEADS

    def pad_row(v):
        return jnp.pad(v.reshape(1, nh), ((0, 0), (0, LANES - nh)))

    hh = np.arange(LANES)[:, None]
    ll = np.arange(width)[None, :]
    one_hot = lambda m: jnp.asarray(m.astype(np.float32), dtype=BF16)
    expand = one_hot(ll // SSD_HEAD_DIM == hh)
    ii = np.arange(CHUNK)
    tril = one_hot(ii[None, :] <= ii[:, None])
    triu2 = one_hot(np.tile(ii[:, None] <= ii[None, :], (1, 2)))
    jj = np.arange(HALO + CHUNK)
    shift = one_hot(np.stack([jj[None, :] == ii[:, None] + HALO - (CONV_WIDTH - 1) + k
                              for k in range(CONV_WIDTH - 1)]))
    full = lambda shape: pl.BlockSpec(shape, lambda b, i: tuple(0 for _ in shape))
    return pl.pallas_call(
        functools.partial(_ssd_kernel, t_blk=t_blk),
        grid=(bsz, nb),
        in_specs=[
            pl.BlockSpec((t_blk, width + cdim), lambda b, i: (b * nb + i, 0)),
            pl.BlockSpec((t_blk, LANES), lambda b, i: (b * nb + i, 0)),
            pl.BlockSpec((nh, t_blk), lambda b, i: (0, b * nb + i)),
            full((CHUNK, cdim)), full((CHUNK, LANES)), full((nh, CHUNK)),
            full((CONV_WIDTH, cdim)), full((1, cdim)), full((1, LANES)), full((nh, 1)),
            full((1, LANES)), full((nh, 1)), full((1, width)), full((1, width)),
            full((LANES, width)), full((CHUNK, CHUNK)), full((CHUNK, 2 * CHUNK)), full(shift.shape),
        ],
        out_specs=pl.BlockSpec((t_blk, width), lambda b, i: (b * nb + i, 0)),
        out_shape=jax.ShapeDtypeStruct((bsz * seq, width), BF16),
        scratch_shapes=[
            pltpu.VMEM((t_blk + HALO, cdim), F32),
            pltpu.VMEM((CHUNK + HALO, cdim), F32),
            pltpu.VMEM((N_SSD_GROUPS, D_STATE, width // N_SSD_GROUPS), F32),
        ],
        compiler_params=_cparams(("parallel", "arbitrary")),
        name="ssd",
    )(zx, dt, dtt, xbc_m, dt_m, dtt_m, conv_w, conv_b.reshape(1, cdim), pad_row(dt_bias),
      dt_bias.reshape(nh, 1), pad_row(a_log), a_log.reshape(nh, 1), jnp.repeat(d_skip, SSD_HEAD_DIM).reshape(1, width),
      norm_w.reshape(1, width), expand, tril, triu2, shift)


def _rope_tables(n_pos):
    inv = jnp.power(ROPE_THETA, -jnp.arange(0, ROT_DIM, 2, dtype=F32) / ROT_DIM)
    ang = jnp.arange(n_pos, dtype=F32)[:, None] * inv[None, :]
    cos, sin = jnp.cos(ang), jnp.sin(ang)
    half = ROT_DIM // 2
    r = np.arange(LANES) % DIFF_HEAD_DIM
    idx = jnp.asarray(r % half)
    lo = jnp.asarray(r < half)[None, :]
    hi = jnp.asarray((r >= half) & (r < ROT_DIM))[None, :]
    cos_t = jnp.where(lo | hi, cos[:, idx], 1.0)
    sin_lo = jnp.where(lo, -sin[:, idx], 0.0)
    sin_hi = jnp.where(hi, sin[:, idx], 0.0)
    return jnp.stack([cos_t, sin_lo, sin_hi])


def kernel(x, meta_tokens, ffn1_norm, ffn1_w_gate, ffn1_w_up, ffn1_w_down, mix_norm, w_in, q_norm, k_norm,
           lambda_q1, lambda_k1, lambda_q2, lambda_k2, attn_out_norm, conv_w, conv_b, dt_bias, a_log, d_skip,
           ssd_norm, w_out, ffn2_norm, ffn2_w_gate, ffn2_w_up, ffn2_w_down):
    bsz, seq, d = x.shape
    assert ffn1_norm.shape[0] == 1, "single-layer block"
    aw = N_DIFF_HEADS * V_HEAD_DIM
    sw = N_SSD_HEADS * SSD_HEAD_DIM
    cdim = sw + 2 * N_SSD_GROUPS * D_STATE
    lam_init = 0.8 - 0.6 * math.exp(-0.3 * 0)
    row = lambda v: v.reshape(1, -1)

    tn = 512
    w_in0 = w_in[0]
    w_qk = w_in0[:, :2 * aw].astype(BF16)
    w_v_t = w_in0[:, 2 * aw:3 * aw].T.astype(BF16)
    w_zx = w_in0[:, 3 * aw:3 * aw + sw + cdim].astype(BF16)
    w_dt = w_in0[:, 3 * aw + sw + cdim:].astype(BF16)
    w_dt_pad = jnp.pad(w_dt, ((0, 0), (0, LANES - N_SSD_HEADS)))
    w_dt_t = w_dt.T

    n_sub = aw // DIFF_HEAD_DIM
    q_gain = jnp.tile(q_norm[0], n_sub) * (DIFF_HEAD_DIM ** -0.5 * math.log2(math.e))
    k_gain = jnp.tile(k_norm[0], n_sub)
    qk_gain = jnp.concatenate([q_gain, k_gain]).reshape(2 * aw // tn, 1, tn)
    rope = _rope_tables(N_META + seq)
    gi = np.arange(2 * LANES) // DIFF_HEAD_DIM
    bd = jnp.asarray((gi[:, None] == gi[None, :]).astype(np.float32) / DIFF_HEAD_DIM, dtype=BF16)

    def ffn_weights(g, u, dn):
        return g[0].astype(BF16), u[0].astype(BF16), dn[0].astype(BF16)

    wg1, wu1, wd1 = ffn_weights(ffn1_w_gate, ffn1_w_up, ffn1_w_down)
    wg2, wu2, wd2 = ffn_weights(ffn2_w_gate, ffn2_w_up, ffn2_w_down)

    x2 = x.reshape(bsz * seq, d)
    h1, hn = _ffn(x2, row(ffn1_norm[0]), wg1, wu1, wd1, row(mix_norm[0]), tm=FFN_TM, tf=FFN_TF)
    _, hn_m = _ffn(meta_tokens.astype(F32), row(ffn1_norm[0]), wg1, wu1, wd1, row(mix_norm[0]),
                   tm=N_META, tf=FFN_TF)

    qk = _proj_qk(hn, w_qk, qk_gain, rope[:, N_META:], bd, tm=1024, tn=tn)
    qk_m = _proj_qk(hn_m, w_qk, qk_gain, rope[:, :N_META], bd, tm=N_META, tn=tn)
    v_t = _proj_t(w_v_t, hn, tn=tn, tm=1024, name="proj_vt")
    v_t_m = _proj_t(w_v_t, hn_m, tn=tn, tm=N_META, name="proj_vt_meta")
    zx = _proj(hn, w_zx, tm=1024, tn=tn, name="proj_zx")
    xbc_m = _proj(hn_m, w_zx, tm=N_META, tn=tn, name="proj_zx_meta")[:, sw:]
    dt, dtt = _proj_dt(hn, w_dt_pad, w_dt_t, tm=1024)
    dt_m, dtt_m = _proj_dt(hn_m, w_dt_pad, w_dt_t, tm=N_META)

    kmeta = jnp.pad(qk_m[:, aw:], ((0, LANES - N_META), (0, 0)))
    vmeta_t = jnp.pad(v_t_m, ((0, 0), (0, LANES - N_META)))
    lam_vecs = jnp.stack([lambda_q1[0], lambda_k1[0], lambda_q2[0], lambda_k2[0]]).astype(F32)
    attn = _attention(qk, v_t, kmeta, vmeta_t, lam_vecs, attn_out_norm[0].reshape(V_HEAD_DIM, 1),
                      bsz=bsz, seq=seq, tq=512, lam_init=lam_init)

    lead = CHUNK - N_META
    ssd = _ssd(zx, dt, dtt,
               jnp.pad(xbc_m, ((lead, 0), (0, 0))), jnp.pad(dt_m, ((lead, 0), (0, 0))),
               jnp.pad(dtt_m, ((0, 0), (lead, 0))),
               conv_w[0], conv_b[0], dt_bias[0], a_log[0], d_skip[0], ssd_norm[0],
               bsz=bsz, seq=seq, t_blk=256)

    w_o = w_out[0].astype(BF16)
    h2 = _outproj(h1, attn, ssd, w_o[:aw], w_o[aw:], tm=1024, tn=tn)
    (out,) = _ffn(h2, row(ffn2_norm[0]), wg2, wu2, wd2, tm=FFN_TM, tf=FFN_TF)
    return out.reshape(bsz, seq, d)
```

```python
import functools
import math

import jax
import jax.numpy as jnp
import numpy as np
from jax import lax
from jax.experimental import pallas as pl
from jax.experimental.pallas import tpu as pltpu

F32 = jnp.float32
BF16 = jnp.bfloat16

EPS = 1e-6
CHUNK = 64
N_META = 16
N_DIFF_HEADS = 8
DIFF_HEAD_DIM = 64
V_HEAD_DIM = 128
ROT_DIM = 16
ROPE_THETA = 500000.0
SSD_HEAD_DIM = 64
N_SSD_HEADS = 16
N_SSD_GROUPS = 2
D_STATE = 128
CONV_WIDTH = 4
HALO = 64
LANES = 128
VMEM_LIMIT = 56 * 1024 * 1024
FFN_TM = 512
FFN_TF = 512


def _cparams(sem):
    return pltpu.CompilerParams(dimension_semantics=sem, vmem_limit_bytes=VMEM_LIMIT)


def _silu(v):
    half = 0.5 * v
    return half + half * jnp.tanh(half)


def _split_bf16(a, pieces):
    out = []
    for _ in range(pieces - 1):
        p = a.astype(BF16)
        out.append(p)
        a = a - p.astype(F32)
    out.append(a.astype(BF16))
    return out


def _dot_f32_by_01(a, m01, pieces=3):
    return sum(jnp.dot(p, m01, preferred_element_type=F32) for p in _split_bf16(a, pieces))


def _dot_01_by_f32(m01, a, pieces=3):
    return sum(jnp.dot(m01, p, preferred_element_type=F32) for p in _split_bf16(a, pieces))


def _rms(v, w):
    ms = jnp.mean(v * v, axis=-1, keepdims=True)
    return v * lax.rsqrt(ms + EPS) * w


def _swiglu_step(xn, wg_ref, wu_ref, wd_ref):
    g = jnp.dot(xn, wg_ref[...], preferred_element_type=F32)
    u = jnp.dot(xn, wu_ref[...], preferred_element_type=F32)
    a = (_silu(g) * u).astype(BF16)
    return jnp.dot(a, wd_ref[...], preferred_element_type=F32)


def _qk_norm_rope(acc, gain, rope_ref, bd):
    cos, sin_lo, sin_hi = rope_ref[0], rope_ref[1], rope_ref[2]
    half = ROT_DIM // 2
    wide = bd.shape[0]
    out = []
    for c in range(acc.shape[1] // wide):
        y = acc[:, c * wide:(c + 1) * wide]
        ms = sum(jnp.dot(p, bd, preferred_element_type=F32) for p in _split_bf16(y * y, 2))
        yn = y * lax.rsqrt(ms + EPS) * gain[:, c * wide:(c + 1) * wide]
        for s in range(wide // LANES):
            v = yn[:, s * LANES:(s + 1) * LANES]
            out.append(v * cos + pltpu.roll(v, half, 1) * sin_hi + pltpu.roll(v, LANES - half, 1) * sin_lo)
    return jnp.concatenate(out, axis=1)


def _ffn_kernel(x_ref, nw_ref, wg_ref, wu_ref, wd_ref, *rest, with_norm):
    if with_norm:
        pnw_ref, out_ref, hn_ref, xn_sc = rest
    else:
        out_ref, xn_sc = rest
    j = pl.program_id(1)

    @pl.when(j == 0)
    def _():
        xn_sc[...] = _rms(x_ref[...], nw_ref[...]).astype(BF16)
        out_ref[...] = jnp.zeros_like(out_ref)

    out_ref[...] += _swiglu_step(xn_sc[...], wg_ref, wu_ref, wd_ref)

    @pl.when(j == pl.num_programs(1) - 1)
    def _():
        h = x_ref[...] + 0.5 * out_ref[...]
        out_ref[...] = h
        if with_norm:
            hn_ref[...] = _rms(h, pnw_ref[...]).astype(BF16)


def _ffn(x2, nw, wg, wu, wd, post_nw=None, *, tm, tf):
    m, d = x2.shape
    f = wg.shape[1]
    with_norm = post_nw is not None
    row_spec = pl.BlockSpec((tm, d), lambda i, j: (i, 0))
    vec_spec = pl.BlockSpec((1, d), lambda i, j: (0, 0))
    in_specs = [row_spec, vec_spec,
                pl.BlockSpec((d, tf), lambda i, j: (0, j)),
                pl.BlockSpec((d, tf), lambda i, j: (0, j)),
                pl.BlockSpec((tf, d), lambda i, j: (j, 0))]
    args = [x2, nw, wg, wu, wd]
    out_specs, out_shape = [row_spec], [jax.ShapeDtypeStruct((m, d), F32)]
    if with_norm:
        in_specs.append(vec_spec)
        args.append(post_nw)
        out_specs.append(row_spec)
        out_shape.append(jax.ShapeDtypeStruct((m, d), BF16))
    return pl.pallas_call(
        functools.partial(_ffn_kernel, with_norm=with_norm),
        grid=(m // tm, f // tf),
        in_specs=in_specs,
        out_specs=out_specs,
        out_shape=out_shape,
        scratch_shapes=[pltpu.VMEM((tm, d), BF16)],
        compiler_params=_cparams(("parallel", "arbitrary")),
        name="ffn",
    )(*args)


def _proj_kernel(x_ref, w_ref, out_ref):
    out_ref[...] = jnp.dot(x_ref[...], w_ref[...], preferred_element_type=F32).astype(out_ref.dtype)


def _proj(x2, w, *, tm, tn, name):
    m, d = x2.shape
    n = w.shape[1]
    return pl.pallas_call(
        _proj_kernel,
        grid=(m // tm, n // tn),
        in_specs=[pl.BlockSpec((tm, d), lambda i, j: (i, 0)), pl.BlockSpec((d, tn), lambda i, j: (0, j))],
        out_specs=pl.BlockSpec((tm, tn), lambda i, j: (i, j)),
        out_shape=jax.ShapeDtypeStruct((m, n), BF16),
        compiler_params=_cparams(("parallel", "arbitrary")),
        name=name,
    )(x2, w)


def _proj_t_kernel(w_ref, x_ref, out_ref):
    out_ref[...] = lax.dot_general(w_ref[...], x_ref[...], (((1,), (1,)), ((), ())),
                                   preferred_element_type=F32).astype(out_ref.dtype)


def _proj_t(w_t, x2, *, tn, tm, name):
    n, d = w_t.shape
    m = x2.shape[0]
    return pl.pallas_call(
        _proj_t_kernel,
        grid=(m // tm, n // tn),
        in_specs=[pl.BlockSpec((tn, d), lambda i, j: (j, 0)), pl.BlockSpec((tm, d), lambda i, j: (i, 0))],
        out_specs=pl.BlockSpec((tn, tm), lambda i, j: (j, i)),
        out_shape=jax.ShapeDtypeStruct((n, m), BF16),
        compiler_params=_cparams(("parallel", "arbitrary")),
        name=name,
    )(w_t, x2)


def _proj_qk_kernel(x_ref, w_ref, gain_ref, rope_ref, bd_ref, out_ref):
    acc = jnp.dot(x_ref[...], w_ref[...], preferred_element_type=F32)
    out_ref[...] = _qk_norm_rope(acc, gain_ref[0], rope_ref, bd_ref[...]).astype(out_ref.dtype)


def _proj_qk(x2, w, gain, rope, bd, *, tm, tn):
    m, d = x2.shape
    n = w.shape[1]
    n_tab = rope.shape[1] // tm
    return pl.pallas_call(
        _proj_qk_kernel,
        grid=(m // tm, n // tn),
        in_specs=[
            pl.BlockSpec((tm, d), lambda i, j: (i, 0)),
            pl.BlockSpec((d, tn), lambda i, j: (0, j)),
            pl.BlockSpec((1, 1, tn), lambda i, j: (j, 0, 0)),
            pl.BlockSpec((3, tm, LANES), lambda i, j: (0, i % n_tab, 0)),
            pl.BlockSpec(bd.shape, lambda i, j: (0, 0)),
        ],
        out_specs=pl.BlockSpec((tm, tn), lambda i, j: (i, j)),
        out_shape=jax.ShapeDtypeStruct((m, n), BF16),
        compiler_params=_cparams(("parallel", "arbitrary")),
        name="proj_qk",
    )(x2, w, gain, rope, bd)


def _proj_dt_kernel(x_ref, w_ref, wt_ref, out_ref, outt_ref):
    x = x_ref[...]
    out_ref[...] = jnp.dot(x, w_ref[...], preferred_element_type=F32)
    outt_ref[...] = lax.dot_general(wt_ref[...], x, (((1,), (1,)), ((), ())), preferred_element_type=F32)


def _proj_dt(x2, w_pad, w_t, *, tm):
    m, d = x2.shape
    nh = w_t.shape[0]
    return pl.pallas_call(
        _proj_dt_kernel,
        grid=(m // tm,),
        in_specs=[
            pl.BlockSpec((tm, d), lambda i: (i, 0)),
            pl.BlockSpec((d, LANES), lambda i: (0, 0)),
            pl.BlockSpec((nh, d), lambda i: (0, 0)),
        ],
        out_specs=[pl.BlockSpec((tm, LANES), lambda i: (i, 0)), pl.BlockSpec((nh, tm), lambda i: (0, i))],
        out_shape=[jax.ShapeDtypeStruct((m, LANES), F32), jax.ShapeDtypeStruct((nh, m), F32)],
        compiler_params=_cparams(("parallel",)),
        name="proj_dt",
    )(x2, w_pad, w_t)


def _outproj_kernel(h_ref, a_ref, s_ref, wa_ref, ws_ref, out_ref):
    out_ref[...] = (h_ref[...]
                    + jnp.dot(a_ref[...], wa_ref[...], preferred_element_type=F32)
                    + jnp.dot(s_ref[...], ws_ref[...], preferred_element_type=F32))


def _outproj(h1, attn, ssd, wa, ws, *, tm, tn):
    m, d = h1.shape
    k = attn.shape[1]
    return pl.pallas_call(
        _outproj_kernel,
        grid=(m // tm, d // tn),
        in_specs=[
            pl.BlockSpec((tm, tn), lambda i, j: (i, j)),
            pl.BlockSpec((tm, k), lambda i, j: (i, 0)),
            pl.BlockSpec((tm, k), lambda i, j: (i, 0)),
            pl.BlockSpec((k, tn), lambda i, j: (0, j)),
            pl.BlockSpec((k, tn), lambda i, j: (0, j)),
        ],
        out_specs=pl.BlockSpec((tm, tn), lambda i, j: (i, j)),
        out_shape=jax.ShapeDtypeStruct((m, d), F32),
        compiler_params=_cparams(("parallel", "arbitrary")),
        name="outproj",
    )(h1, attn, ssd, wa, ws)


def _attn_kernel(qa_tab, ka_tab, qb_tab, kb_tab, q_ref, k_ref, vt_ref, km_ref, vmt_ref, lam_ref, gain_ref,
                 out_ref, qs_sc, sm_sc, s0_sc, s1_sc, mx0_sc, mx1_sc, m_sc, l_sc, acc_sc,
                 *, tq, n_steps, lam_init):
    g = pl.program_id(2)
    qa = qa_tab[g]
    ka = ka_tab[g]
    qb = qb_tab[g]
    kb = kb_tab[g]
    has_a = g < n_steps
    has_b = g >= 1
    even = (g % 2) == 0
    nt = (((1,), (1,)), ((), ()))

    def stage_a(buf, diagonal):
        s_out, mx_out = buf
        s_t = lax.dot_general(k_ref[...], qs_sc[...], nt, preferred_element_type=F32)
        if diagonal:
            key = lax.broadcasted_iota(jnp.int32, s_t.shape, 0)
            qry = lax.broadcasted_iota(jnp.int32, s_t.shape, 1)
            s_t = jnp.where(key // CHUNK <= (qry & (tq - 1)) // CHUNK, s_t, -jnp.inf)
        s_out[...] = s_t
        mx_out[...] = jnp.max(s_t, axis=0, keepdims=True)

    def stage_b(buf, last):
        s_in, mx_in = buf
        m_prev = m_sc[...]
        m_new = jnp.maximum(m_prev, mx_in[...])
        alpha = jnp.exp2(m_prev - m_new)
        p = jnp.exp2(s_in[...] - m_new).astype(BF16)
        vt_aug = jnp.concatenate([vt_ref[...], jnp.ones((16, p.shape[0]), BF16)], axis=0)
        pv = jnp.dot(vt_aug, p, preferred_element_type=F32)
        l_sc[...] = alpha * l_sc[...] + pv[V_HEAD_DIM:V_HEAD_DIM + 1, :]
        acc_sc[...] = alpha * acc_sc[...] + pv[0:V_HEAD_DIM, :]
        m_sc[...] = m_new
        if last:
            lq1, lk1, lq2, lk2 = lam_ref[0:1, :], lam_ref[1:2, :], lam_ref[2:3, :], lam_ref[3:4, :]
            lam = (jnp.exp(jnp.sum(lq1 * lk1, axis=-1, keepdims=True))
                   - jnp.exp(jnp.sum(lq2 * lk2, axis=-1, keepdims=True)) + lam_init)
            o = acc_sc[...] / l_sc[...]
            o = o[:, 0:tq] - lam * o[:, tq:2 * tq]
            ms = jnp.mean(o * o, axis=0, keepdims=True)
            o = o * lax.rsqrt(ms + EPS) * (gain_ref[...] * (1.0 - lam_init))
            out_ref[...] = o.T.astype(out_ref.dtype)

    @pl.when(has_b & (kb == 0))
    def _():
        s_t = sm_sc[...]
        m0 = jnp.max(s_t, axis=0, keepdims=True)
        p = jnp.exp2(s_t - m0)
        m_sc[...] = m0
        l_sc[...] = jnp.sum(p, axis=0, keepdims=True)
        acc_sc[...] = jnp.dot(vmt_ref[...], p.astype(BF16), preferred_element_type=F32)

    @pl.when(has_a & (ka == 0))
    def _():
        q = q_ref[...]
        lane = lax.broadcasted_iota(jnp.int32, q.shape, 1)
        zero = jnp.zeros_like(q)
        qs_sc[0:tq, :] = jnp.where(lane < DIFF_HEAD_DIM, q, zero)
        qs_sc[tq:2 * tq, :] = jnp.where(lane >= DIFF_HEAD_DIM, q, zero)
        s_t = lax.dot_general(km_ref[...], qs_sc[...], nt, preferred_element_type=F32)
        key = lax.broadcasted_iota(jnp.int32, s_t.shape, 0)
        sm_sc[...] = jnp.where(key < N_META, s_t, -jnp.inf)

    buf0, buf1 = (s0_sc, mx0_sc), (s1_sc, mx1_sc)
    a_diag = ka == qa
    b_last = kb == qb
    both = has_a & has_b
    for parity, (buf_a, buf_b) in enumerate(((buf0, buf1), (buf1, buf0))):
        par = even if parity == 0 else jnp.logical_not(even)

        @pl.when(par & both & jnp.logical_not(a_diag) & jnp.logical_not(b_last))
        def _(buf_a=buf_a, buf_b=buf_b):
            stage_a(buf_a, False)
            stage_b(buf_b, False)

        @pl.when(par & both & a_diag & jnp.logical_not(b_last))
        def _(buf_a=buf_a, buf_b=buf_b):
            stage_a(buf_a, True)
            stage_b(buf_b, False)

        @pl.when(par & both & b_last)
        def _(buf_a=buf_a, buf_b=buf_b):
            stage_a(buf_a, False)
            stage_b(buf_b, True)

        @pl.when(par & has_a & jnp.logical_not(has_b))
        def _(buf_a=buf_a):
            stage_a(buf_a, True)

        @pl.when(par & has_b & jnp.logical_not(has_a))
        def _(buf_b=buf_b):
            stage_b(buf_b, True)


def _attention(qk, vt, kmeta, vmeta_t, lam_vecs, gain_col, *, bsz, seq, tq, lam_init):
    nq = seq // tq
    h = N_DIFF_HEADS
    qi_list, ki_list = [], []
    for a in range(nq):
        for b in range(a + 1):
            qi_list.append(a)
            ki_list.append(b)
    n_steps = len(qi_list)
    ia = np.minimum(np.arange(n_steps + 1), n_steps - 1)
    ib = np.maximum(np.arange(n_steps + 1) - 1, 0)
    qi_arr, ki_arr = np.array(qi_list, np.int32), np.array(ki_list, np.int32)
    tabs = [jnp.asarray(t) for t in (qi_arr[ia], ki_arr[ia], qi_arr[ib], ki_arr[ib])]
    grid_spec = pltpu.PrefetchScalarGridSpec(
        num_scalar_prefetch=4,
        grid=(bsz, h, n_steps + 1),
        in_specs=[
            pl.BlockSpec((tq, V_HEAD_DIM), lambda b, hh, g, qa, ka, qb, kb: (b * nq + qa[g], hh)),
            pl.BlockSpec((tq, V_HEAD_DIM), lambda b, hh, g, qa, ka, qb, kb: (b * nq + ka[g], h + hh)),
            pl.BlockSpec((V_HEAD_DIM, tq), lambda b, hh, g, qa, ka, qb, kb: (hh, b * nq + kb[g])),
            pl.BlockSpec((LANES, V_HEAD_DIM), lambda b, hh, g, qa, ka, qb, kb: (0, hh)),
            pl.BlockSpec((V_HEAD_DIM, LANES), lambda b, hh, g, qa, ka, qb, kb: (hh, 0)),
            pl.BlockSpec((4, DIFF_HEAD_DIM), lambda b, hh, g, qa, ka, qb, kb: (0, 0)),
            pl.BlockSpec((V_HEAD_DIM, 1), lambda b, hh, g, qa, ka, qb, kb: (0, 0)),
        ],
        out_specs=pl.BlockSpec((tq, V_HEAD_DIM), lambda b, hh, g, qa, ka, qb, kb: (b * nq + qb[g], hh)),
        scratch_shapes=[
            pltpu.VMEM((2 * tq, V_HEAD_DIM), BF16),
            pltpu.VMEM((LANES, 2 * tq), F32),
            pltpu.VMEM((tq, 2 * tq), F32),
            pltpu.VMEM((tq, 2 * tq), F32),
            pltpu.VMEM((1, 2 * tq), F32),
            pltpu.VMEM((1, 2 * tq), F32),
            pltpu.VMEM((1, 2 * tq), F32),
            pltpu.VMEM((1, 2 * tq), F32),
            pltpu.VMEM((V_HEAD_DIM, 2 * tq), F32),
        ],
    )
    return pl.pallas_call(
        functools.partial(_attn_kernel, tq=tq, n_steps=n_steps, lam_init=lam_init),
        grid_spec=grid_spec,
        out_shape=jax.ShapeDtypeStruct((bsz * seq, h * V_HEAD_DIM), BF16),
        compiler_params=_cparams(("parallel", "parallel", "arbitrary")),
        name="attn",
    )(*tabs, qk, qk, vt, kmeta, vmeta_t, lam_vecs, gain_col)


def _ssd_kernel(*refs, t_blk, bsz):
    dtt_refs = refs[:bsz]
    (zx_ref, dt_ref, xbcm_ref, dtm_ref, dttm_ref,
     convw_ref, convb_ref, dtb_ref, dtbt_ref, alog_ref, alogt_ref, dskip_ref, nw_ref,
     expand_ref, tril_ref, triu2_ref, shift_ref, out_ref, xs_sc, xm_sc, state_sc) = refs[bsz:]
    n_heads, p_dim, n_state, n_groups = N_SSD_HEADS, SSD_HEAD_DIM, D_STATE, N_SSD_GROUPS
    width = n_heads * p_dim
    gw = width // n_groups
    n_pairs = n_heads // 2
    nt = (((1,), (1,)), ((), ()))
    expand = expand_ref[...]
    a_row = -jnp.exp(alog_ref[...])
    a_col = -jnp.exp(alogt_ref[...])
    d_full = dskip_ref[...]
    lane = lax.broadcasted_iota(jnp.int32, (CHUNK, LANES), 1)
    row = lax.broadcasted_iota(jnp.int32, (CHUNK, LANES), 0)
    left = lane < p_dim
    causal2 = (lane & (CHUNK - 1)) <= row

    def chunk(win_f32, st_ref, dt_raw, dtt_raw, pad_rows, z, out_r0, b=0):
        win = win_f32.astype(BF16)
        conv = convb_ref[...] + win[HALO:, :].astype(F32) * convw_ref[CONV_WIDTH - 1:CONV_WIDTH, :]
        for k in range(CONV_WIDTH - 1):
            conv = conv + jnp.dot(shift_ref[k], win, preferred_element_type=F32) * convw_ref[k:k + 1, :]
        xc = _silu(conv)
        x_s = xc[:, :width]
        dt = jax.nn.softplus(dt_raw + dtb_ref[...])
        dtt = jax.nn.softplus(dtt_raw + dtbt_ref[...])
        if pad_rows:
            dt = jnp.where(row >= pad_rows, dt, 0.0)
            dtt = jnp.where(lax.broadcasted_iota(jnp.int32, dtt.shape, 1) >= pad_rows, dtt, 0.0)
        a_cs = _dot_01_by_f32(tril_ref[...], dt * a_row)
        a_cs_t2 = _dot_f32_by_01(dtt * a_col, triu2_ref[...])
        a_full = _dot_f32_by_01(a_cs, expand)
        dt_full = _dot_f32_by_01(dt, expand)
        a_last = a_full[CHUNK - 1:CHUNK, :]
        xdt = x_s * dt_full
        w_state = (xdt * jnp.exp(a_last - a_full)).astype(BF16)
        xdt_b = xdt.astype(BF16)
        zero_b = jnp.zeros((CHUNK, LANES), BF16)
        y_parts = []
        for g in range(n_groups):
            b_g = xc[:, width + g * n_state: width + (g + 1) * n_state]
            c_g = xc[:, width + (n_groups + g) * n_state: width + (n_groups + g + 1) * n_state]
            b_gb = b_g.astype(BF16)
            c_gb = c_g.astype(BF16)
            st = st_ref[g]
            if out_r0 is not None:
                cb2 = lax.dot_general(c_gb, jnp.concatenate([b_gb, b_gb], axis=0), nt,
                                      preferred_element_type=F32)
                y_off = jnp.dot(c_gb, st.astype(BF16), preferred_element_type=F32)
                y_g = y_off * jnp.exp(a_full[:, g * gw:(g + 1) * gw])
                diag = []
                for pp in range(n_pairs // n_groups):
                    pr = g * (n_pairs // n_groups) + pp
                    colb = a_full[:, pr * LANES:(pr + 1) * LANES]
                    rowb = jnp.where(left, a_cs_t2[2 * pr:2 * pr + 1, :], a_cs_t2[2 * pr + 1:2 * pr + 2, :])
                    dec = jnp.exp(jnp.where(causal2, colb - rowb, -jnp.inf))
                    m_pair = (cb2 * dec).astype(BF16)
                    xp = xdt_b[:, pr * LANES:(pr + 1) * LANES]
                    rhs = jnp.concatenate([jnp.where(left, xp, zero_b), jnp.where(left, zero_b, xp)], axis=0)
                    diag.append(jnp.dot(m_pair, rhs, preferred_element_type=F32))
                y_parts.append(y_g + jnp.concatenate(diag, axis=1))
            new_st = st * jnp.exp(a_last[:, g * gw:(g + 1) * gw]) + jnp.dot(
                b_g.T.astype(BF16), w_state[:, g * gw:(g + 1) * gw], preferred_element_type=F32)
            st_ref[g] = new_st
        if out_r0 is not None:
            y = jnp.concatenate(y_parts, axis=1) + x_s * d_full
            gated = y * _silu(z)
            out_ref[b, out_r0:out_r0 + CHUNK, :] = _rms(gated, nw_ref[...]).astype(out_ref.dtype)

    @pl.when(pl.program_id(0) == 0)
    def _():
        state_sc[...] = jnp.zeros_like(state_sc)
        xm_sc[0:HALO, :] = jnp.zeros((HALO, xm_sc.shape[1]), F32)
        xm_sc[HALO:HALO + CHUNK, :] = xbcm_ref[...].astype(F32)
        chunk(xm_sc[...], state_sc.at[0], dtm_ref[...], dttm_ref[...], CHUNK - N_META, None, None)
        for b in range(bsz):
            if b:
                state_sc[b] = state_sc[0]
            xs_sc[b, 0:HALO, :] = xm_sc[CHUNK:CHUNK + HALO, :]

    for b in range(bsz):
        xs_sc[b, HALO:HALO + t_blk, :] = zx_ref[b, :, width:].astype(F32)
    for c in range(t_blk // CHUNK):
        rows = slice(c * CHUNK, (c + 1) * CHUNK)
        for b in range(bsz):
            chunk(xs_sc[b, c * CHUNK:(c + 1) * CHUNK + HALO, :], state_sc.at[b], dt_ref[b, rows, :],
                  dtt_refs[b][:, rows], 0, zx_ref[b, rows, 0:width].astype(F32), c * CHUNK, b)
    for b in range(bsz):
        xs_sc[b, 0:HALO, :] = xs_sc[b, t_blk:t_blk + HALO, :]


def _ssd(zx, dt, dtt, xbc_m, dt_m, dtt_m, conv_w, conv_b, dt_bias, a_log, d_skip, norm_w,
         *, bsz, seq, t_blk):
    nb = seq // t_blk
    width = N_SSD_HEADS * SSD_HEAD_DIM
    cdim = zx.shape[1] - width
    nh = N_SSD_HEADS

    def pad_row(v):
        return jnp.pad(v.reshape(1, nh), ((0, 0), (0, LANES - nh)))

    hh = np.arange(LANES)[:, None]
    ll = np.arange(width)[None, :]
    one_hot = lambda m: jnp.asarray(m.astype(np.float32), dtype=BF16)
    expand = one_hot(ll // SSD_HEAD_DIM == hh)
    ii = np.arange(CHUNK)
    tril = one_hot(ii[None, :] <= ii[:, None])
    triu2 = one_hot(np.tile(ii[:, None] <= ii[None, :], (1, 2)))
    jj = np.arange(HALO + CHUNK)
    shift = one_hot(np.stack([jj[None, :] == ii[:, None] + HALO - (CONV_WIDTH - 1) + k
                              for k in range(CONV_WIDTH - 1)]))
    full = lambda shape: pl.BlockSpec(shape, lambda i: tuple(0 for _ in shape))
    out = pl.pallas_call(
        functools.partial(_ssd_kernel, t_blk=t_blk, bsz=bsz),
        grid=(nb,),
        in_specs=[pl.BlockSpec((nh, t_blk), functools.partial(lambda i, b: (0, b * nb + i), b=b))
                  for b in range(bsz)] + [
            pl.BlockSpec((bsz, t_blk, width + cdim), lambda i: (0, i, 0)),
            pl.BlockSpec((bsz, t_blk, LANES), lambda i: (0, i, 0)),
            full((CHUNK, cdim)), full((CHUNK, LANES)), full((nh, CHUNK)),
            full((CONV_WIDTH, cdim)), full((1, cdim)), full((1, LANES)), full((nh, 1)),
            full((1, LANES)), full((nh, 1)), full((1, width)), full((1, width)),
            full((LANES, width)), full((CHUNK, CHUNK)), full((CHUNK, 2 * CHUNK)), full(shift.shape),
        ],
        out_specs=pl.BlockSpec((bsz, t_blk, width), lambda i: (0, i, 0)),
        out_shape=jax.ShapeDtypeStruct((bsz, seq, width), BF16),
        scratch_shapes=[
            pltpu.VMEM((bsz, t_blk + HALO, cdim), F32),
            pltpu.VMEM((CHUNK + HALO, cdim), F32),
            pltpu.VMEM((bsz, N_SSD_GROUPS, D_STATE, width // N_SSD_GROUPS), F32),
        ],
        compiler_params=_cparams(("arbitrary",)),
        name="ssd",
    )(*([dtt] * bsz), zx.reshape(bsz, seq, width + cdim), dt.reshape(bsz, seq, LANES), xbc_m, dt_m, dtt_m,
      conv_w, conv_b.reshape(1, cdim), pad_row(dt_bias), dt_bias.reshape(nh, 1), pad_row(a_log),
      a_log.reshape(nh, 1), jnp.repeat(d_skip, SSD_HEAD_DIM).reshape(1, width),
      norm_w.reshape(1, width), expand, tril, triu2, shift)
    return out.reshape(bsz * seq, width)


def _rope_tables(n_pos):
    inv = jnp.power(ROPE_THETA, -jnp.arange(0, ROT_DIM, 2, dtype=F32) / ROT_DIM)
    ang = jnp.arange(n_pos, dtype=F32)[:, None] * inv[None, :]
    cos, sin = jnp.cos(ang), jnp.sin(ang)
    half = ROT_DIM // 2
    r = np.arange(LANES) % DIFF_HEAD_DIM
    idx = jnp.asarray(r % half)
    lo = jnp.asarray(r < half)[None, :]
    hi = jnp.asarray((r >= half) & (r < ROT_DIM))[None, :]
    cos_t = jnp.where(lo | hi, cos[:, idx], 1.0)
    sin_lo = jnp.where(lo, -sin[:, idx], 0.0)
    sin_hi = jnp.where(hi, sin[:, idx], 0.0)
    return jnp.stack([cos_t, sin_lo, sin_hi])


def kernel(x, meta_tokens, ffn1_norm, ffn1_w_gate, ffn1_w_up, ffn1_w_down, mix_norm, w_in, q_norm, k_norm,
           lambda_q1, lambda_k1, lambda_q2, lambda_k2, attn_out_norm, conv_w, conv_b, dt_bias, a_log, d_skip,
           ssd_norm, w_out, ffn2_norm, ffn2_w_gate, ffn2_w_up, ffn2_w_down):
    bsz, seq, d = x.shape
    assert ffn1_norm.shape[0] == 1, "single-layer block"
    aw = N_DIFF_HEADS * V_HEAD_DIM
    sw = N_SSD_HEADS * SSD_HEAD_DIM
    cdim = sw + 2 * N_SSD_GROUPS * D_STATE
    lam_init = 0.8 - 0.6 * math.exp(-0.3 * 0)
    row = lambda v: v.reshape(1, -1)

    tn = 512
    w_in0 = w_in[0]
    w_qk = w_in0[:, :2 * aw].astype(BF16)
    w_v_t = w_in0[:, 2 * aw:3 * aw].T.astype(BF16)
    w_zx = w_in0[:, 3 * aw:3 * aw + sw + cdim].astype(BF16)
    w_dt = w_in0[:, 3 * aw + sw + cdim:].astype(BF16)
    w_dt_pad = jnp.pad(w_dt, ((0, 0), (0, LANES - N_SSD_HEADS)))
    w_dt_t = w_dt.T

    n_sub = aw // DIFF_HEAD_DIM
    q_gain = jnp.tile(q_norm[0], n_sub) * (DIFF_HEAD_DIM ** -0.5 * math.log2(math.e))
    k_gain = jnp.tile(k_norm[0], n_sub)
    qk_gain = jnp.concatenate([q_gain, k_gain]).reshape(2 * aw // tn, 1, tn)
    rope = _rope_tables(N_META + seq)
    gi = np.arange(2 * LANES) // DIFF_HEAD_DIM
    bd = jnp.asarray((gi[:, None] == gi[None, :]).astype(np.float32) / DIFF_HEAD_DIM, dtype=BF16)

    def ffn_weights(g, u, dn):
        return g[0].astype(BF16), u[0].astype(BF16), dn[0].astype(BF16)

    wg1, wu1, wd1 = ffn_weights(ffn1_w_gate, ffn1_w_up, ffn1_w_down)
    wg2, wu2, wd2 = ffn_weights(ffn2_w_gate, ffn2_w_up, ffn2_w_down)

    x2 = x.reshape(bsz * seq, d)
    h1, hn = _ffn(x2, row(ffn1_norm[0]), wg1, wu1, wd1, row(mix_norm[0]), tm=FFN_TM, tf=FFN_TF)
    _, hn_m = _ffn(meta_tokens.astype(F32), row(ffn1_norm[0]), wg1, wu1, wd1, row(mix_norm[0]),
                   tm=N_META, tf=FFN_TF)

    qk = _proj_qk(hn, w_qk, qk_gain, rope[:, N_META:], bd, tm=1024, tn=tn)
    qk_m = _proj_qk(hn_m, w_qk, qk_gain, rope[:, :N_META], bd, tm=N_META, tn=tn)
    v_t = _proj_t(w_v_t, hn, tn=tn, tm=1024, name="proj_vt")
    v_t_m = _proj_t(w_v_t, hn_m, tn=tn, tm=N_META, name="proj_vt_meta")
    zx = _proj(hn, w_zx, tm=1024, tn=tn, name="proj_zx")
    xbc_m = _proj(hn_m, w_zx, tm=N_META, tn=tn, name="proj_zx_meta")[:, sw:]
    dt, dtt = _proj_dt(hn, w_dt_pad, w_dt_t, tm=1024)
    dt_m, dtt_m = _proj_dt(hn_m, w_dt_pad, w_dt_t, tm=N_META)

    kmeta = jnp.pad(qk_m[:, aw:], ((0, LANES - N_META), (0, 0)))
    vmeta_t = jnp.pad(v_t_m, ((0, 0), (0, LANES - N_META)))
    lam_vecs = jnp.stack([lambda_q1[0], lambda_k1[0], lambda_q2[0], lambda_k2[0]]).astype(F32)
    attn = _attention(qk, v_t, kmeta, vmeta_t, lam_vecs, attn_out_norm[0].reshape(V_HEAD_DIM, 1),
                      bsz=bsz, seq=seq, tq=1024, lam_init=lam_init)

    lead = CHUNK - N_META
    ssd = _ssd(zx, dt, dtt,
               jnp.pad(xbc_m, ((lead, 0), (0, 0))), jnp.pad(dt_m, ((lead, 0), (0, 0))),
               jnp.pad(dtt_m, ((0, 0), (lead, 0))),
               conv_w[0], conv_b[0], dt_bias[0], a_log[0], d_skip[0], ssd_norm[0],
               bsz=bsz, seq=seq, t_blk=256)

    w_o = w_out[0].astype(BF16)
    h2 = _outproj(h1, attn, ssd, w_o[:aw], w_o[aw:], tm=512, tn=d)
    (out,) = _ffn(h2, row(ffn2_norm[0]), wg2, wu2, wd2, tm=FFN_TM, tf=FFN_TF)
    return out.reshape(bsz, seq, d)
```

```python
import functools
import math

import jax
import jax.numpy as jnp
import numpy as np
from jax import lax
from jax.experimental import pallas as pl
from jax.experimental.pallas import tpu as pltpu

F32 = jnp.float32
BF16 = jnp.bfloat16

EPS = 1e-6
CHUNK = 64
N_META = 16
N_DIFF_HEADS = 8
DIFF_HEAD_DIM = 64
V_HEAD_DIM = 128
ROT_DIM = 16
ROPE_THETA = 500000.0
SSD_HEAD_DIM = 64
N_SSD_HEADS = 16
N_SSD_GROUPS = 2
D_STATE = 128
CONV_WIDTH = 4
HALO = 64
LANES = 128
VMEM_LIMIT = 56 * 1024 * 1024
FFN_TM = 512
FFN_TF = 512


def _cparams(sem):
    return pltpu.CompilerParams(dimension_semantics=sem, vmem_limit_bytes=VMEM_LIMIT)


def _silu(v):
    half = 0.5 * v
    return half + half * jnp.tanh(half)


def _split_bf16(a, pieces):
    out = []
    for _ in range(pieces - 1):
        p = a.astype(BF16)
        out.append(p)
        a = a - p.astype(F32)
    out.append(a.astype(BF16))
    return out


def _dot_f32_by_01(a, m01, pieces=3):
    return sum(jnp.dot(p, m01, preferred_element_type=F32) for p in _split_bf16(a, pieces))


def _dot_01_by_f32(m01, a, pieces=3):
    return sum(jnp.dot(m01, p, preferred_element_type=F32) for p in _split_bf16(a, pieces))


def _rms(v, w):
    ms = jnp.mean(v * v, axis=-1, keepdims=True)
    return v * lax.rsqrt(ms + EPS) * w


def _swiglu_step(xn, wg_ref, wu_ref, wd_ref):
    g = jnp.dot(xn, wg_ref[...], preferred_element_type=F32)
    u = jnp.dot(xn, wu_ref[...], preferred_element_type=F32)
    a = (_silu(g) * u).astype(BF16)
    return jnp.dot(a, wd_ref[...], preferred_element_type=F32)


def _qk_norm_rope(acc, gain, rope_ref, bd):
    cos, sin_lo, sin_hi = rope_ref[0], rope_ref[1], rope_ref[2]
    half = ROT_DIM // 2
    wide = bd.shape[0]
    out = []
    for c in range(acc.shape[1] // wide):
        y = acc[:, c * wide:(c + 1) * wide]
        ms = sum(jnp.dot(p, bd, preferred_element_type=F32) for p in _split_bf16(y * y, 2))
        yn = y * lax.rsqrt(ms + EPS) * gain[:, c * wide:(c + 1) * wide]
        for s in range(wide // LANES):
            v = yn[:, s * LANES:(s + 1) * LANES]
            out.append(v * cos + pltpu.roll(v, half, 1) * sin_hi + pltpu.roll(v, LANES - half, 1) * sin_lo)
    return jnp.concatenate(out, axis=1)


def _ffn_kernel(x_ref, nw_ref, wg_ref, wu_ref, wd_ref, *rest, with_norm):
    if with_norm:
        pnw_ref, out_ref, hn_ref, xn_sc = rest
    else:
        out_ref, xn_sc = rest
    j = pl.program_id(1)

    @pl.when(j == 0)
    def _():
        xn_sc[...] = _rms(x_ref[...], nw_ref[...]).astype(BF16)
        out_ref[...] = jnp.zeros_like(out_ref)

    out_ref[...] += _swiglu_step(xn_sc[...], wg_ref, wu_ref, wd_ref)

    @pl.when(j == pl.num_programs(1) - 1)
    def _():
        h = x_ref[...] + 0.5 * out_ref[...]
        out_ref[...] = h
        if with_norm:
            hn_ref[...] = _rms(h, pnw_ref[...]).astype(BF16)


def _ffn(x2, nw, wg, wu, wd, post_nw=None, *, tm, tf):
    m, d = x2.shape
    f = wg.shape[1]
    with_norm = post_nw is not None
    row_spec = pl.BlockSpec((tm, d), lambda i, j: (i, 0))
    vec_spec = pl.BlockSpec((1, d), lambda i, j: (0, 0))
    in_specs = [row_spec, vec_spec,
                pl.BlockSpec((d, tf), lambda i, j: (0, j)),
                pl.BlockSpec((d, tf), lambda i, j: (0, j)),
                pl.BlockSpec((tf, d), lambda i, j: (j, 0))]
    args = [x2, nw, wg, wu, wd]
    out_specs, out_shape = [row_spec], [jax.ShapeDtypeStruct((m, d), F32)]
    if with_norm:
        in_specs.append(vec_spec)
        args.append(post_nw)
        out_specs.append(row_spec)
        out_shape.append(jax.ShapeDtypeStruct((m, d), BF16))
    return pl.pallas_call(
        functools.partial(_ffn_kernel, with_norm=with_norm),
        grid=(m // tm, f // tf),
        in_specs=in_specs,
        out_specs=out_specs,
        out_shape=out_shape,
        scratch_shapes=[pltpu.VMEM((tm, d), BF16)],
        compiler_params=_cparams(("parallel", "arbitrary")),
        name="ffn",
    )(*args)


def _proj_kernel(x_ref, w_ref, out_ref):
    out_ref[...] = jnp.dot(x_ref[...], w_ref[...], preferred_element_type=F32).astype(out_ref.dtype)


def _proj(x2, w, *, tm, tn, name):
    m, d = x2.shape
    n = w.shape[1]
    return pl.pallas_call(
        _proj_kernel,
        grid=(m // tm, n // tn),
        in_specs=[pl.BlockSpec((tm, d), lambda i, j: (i, 0)), pl.BlockSpec((d, tn), lambda i, j: (0, j))],
        out_specs=pl.BlockSpec((tm, tn), lambda i, j: (i, j)),
        out_shape=jax.ShapeDtypeStruct((m, n), BF16),
        compiler_params=_cparams(("parallel", "arbitrary")),
        name=name,
    )(x2, w)


def _proj_t_kernel(w_ref, x_ref, out_ref):
    out_ref[...] = lax.dot_general(w_ref[...], x_ref[...], (((1,), (1,)), ((), ())),
                                   preferred_element_type=F32).astype(out_ref.dtype)


def _proj_t(w_t, x2, *, tn, tm, name):
    n, d = w_t.shape
    m = x2.shape[0]
    return pl.pallas_call(
        _proj_t_kernel,
        grid=(m // tm, n // tn),
        in_specs=[pl.BlockSpec((tn, d), lambda i, j: (j, 0)), pl.BlockSpec((tm, d), lambda i, j: (i, 0))],
        out_specs=pl.BlockSpec((tn, tm), lambda i, j: (j, i)),
        out_shape=jax.ShapeDtypeStruct((n, m), BF16),
        compiler_params=_cparams(("parallel", "arbitrary")),
        name=name,
    )(w_t, x2)


def _proj_qk_kernel(x_ref, w_ref, gain_ref, rope_ref, bd_ref, out_ref):
    acc = jnp.dot(x_ref[...], w_ref[...], preferred_element_type=F32)
    out_ref[...] = _qk_norm_rope(acc, gain_ref[0], rope_ref, bd_ref[...]).astype(out_ref.dtype)


def _proj_qk(x2, w, gain, rope, bd, *, tm, tn):
    m, d = x2.shape
    n = w.shape[1]
    n_tab = rope.shape[1] // tm
    return pl.pallas_call(
        _proj_qk_kernel,
        grid=(m // tm, n // tn),
        in_specs=[
            pl.BlockSpec((tm, d), lambda i, j: (i, 0)),
            pl.BlockSpec((d, tn), lambda i, j: (0, j)),
            pl.BlockSpec((1, 1, tn), lambda i, j: (j, 0, 0)),
            pl.BlockSpec((3, tm, LANES), lambda i, j: (0, i % n_tab, 0)),
            pl.BlockSpec(bd.shape, lambda i, j: (0, 0)),
        ],
        out_specs=pl.BlockSpec((tm, tn), lambda i, j: (i, j)),
        out_shape=jax.ShapeDtypeStruct((m, n), BF16),
        compiler_params=_cparams(("parallel", "arbitrary")),
        name="proj_qk",
    )(x2, w, gain, rope, bd)


def _proj_dt_kernel(x_ref, w_ref, wt_ref, out_ref, outt_ref):
    x = x_ref[...]
    out_ref[...] = jnp.dot(x, w_ref[...], preferred_element_type=F32)
    outt_ref[...] = lax.dot_general(wt_ref[...], x, (((1,), (1,)), ((), ())), preferred_element_type=F32)


def _proj_dt(x2, w_pad, w_t, *, tm):
    m, d = x2.shape
    nh = w_t.shape[0]
    return pl.pallas_call(
        _proj_dt_kernel,
        grid=(m // tm,),
        in_specs=[
            pl.BlockSpec((tm, d), lambda i: (i, 0)),
            pl.BlockSpec((d, LANES), lambda i: (0, 0)),
            pl.BlockSpec((nh, d), lambda i: (0, 0)),
        ],
        out_specs=[pl.BlockSpec((tm, LANES), lambda i: (i, 0)), pl.BlockSpec((nh, tm), lambda i: (0, i))],
        out_shape=[jax.ShapeDtypeStruct((m, LANES), F32), jax.ShapeDtypeStruct((nh, m), F32)],
        compiler_params=_cparams(("parallel",)),
        name="proj_dt",
    )(x2, w_pad, w_t)


def _outproj_kernel(h_ref, a_ref, s_ref, wa_ref, ws_ref, out_ref):
    out_ref[...] = (h_ref[...]
                    + jnp.dot(a_ref[...], wa_ref[...], preferred_element_type=F32)
                    + jnp.dot(s_ref[...], ws_ref[...], preferred_element_type=F32))


def _outproj(h1, attn, ssd, wa, ws, *, tm, tn):
    m, d = h1.shape
    k = attn.shape[1]
    return pl.pallas_call(
        _outproj_kernel,
        grid=(m // tm, d // tn),
        in_specs=[
            pl.BlockSpec((tm, tn), lambda i, j: (i, j)),
            pl.BlockSpec((tm, k), lambda i, j: (i, 0)),
            pl.BlockSpec((tm, k), lambda i, j: (i, 0)),
            pl.BlockSpec((k, tn), lambda i, j: (0, j)),
            pl.BlockSpec((k, tn), lambda i, j: (0, j)),
        ],
        out_specs=pl.BlockSpec((tm, tn), lambda i, j: (i, j)),
        out_shape=jax.ShapeDtypeStruct((m, d), F32),
        compiler_params=_cparams(("parallel", "arbitrary")),
        name="outproj",
    )(h1, attn, ssd, wa, ws)


def _attn_kernel(qa_tab, ka_tab, qb_tab, kb_tab, q_ref, k_ref, vt_ref, km_ref, vmt_ref, lam_ref, gain_ref,
                 out_ref, qs_sc, sm_sc, s0_sc, s1_sc, mx0_sc, mx1_sc, m_sc, l_sc, acc_sc,
                 *, tq, dg, n_steps, lam_init):
    g = pl.program_id(2)
    qa = qa_tab[g]
    ka = ka_tab[g]
    qb = qb_tab[g]
    kb = kb_tab[g]
    has_a = g < n_steps
    has_b = g >= 1
    even = (g % 2) == 0
    nt = (((1,), (1,)), ((), ()))

    full_groups = [(slice(0, 2 * tq), tq)]
    diag_groups = [(slice(sub * tq + j * dg, sub * tq + (j + 1) * dg), (j + 1) * dg)
                   for sub in range(2) for j in range(tq // dg)]

    def stage_a(buf, diagonal):
        s_out, mx_out = buf
        for cols, nk in (diag_groups if diagonal else full_groups):
            s_t = lax.dot_general(k_ref[0:nk, :], qs_sc[cols, :], nt, preferred_element_type=F32)
            if diagonal:
                key = lax.broadcasted_iota(jnp.int32, s_t.shape, 0)
                qry = lax.broadcasted_iota(jnp.int32, s_t.shape, 1) + (nk - dg)
                s_t = jnp.where(key // CHUNK <= qry // CHUNK, s_t, -jnp.inf)
            s_out[0:nk, cols] = s_t
            mx_out[:, cols] = jnp.max(s_t, axis=0, keepdims=True)

    def stage_b(buf, last):
        s_in, mx_in = buf
        vt_aug = jnp.concatenate([vt_ref[...], jnp.ones((16, vt_ref.shape[1]), BF16)], axis=0)
        for cols, nk in (diag_groups if last else full_groups):
            m_prev = m_sc[:, cols]
            m_new = jnp.maximum(m_prev, mx_in[:, cols])
            alpha = jnp.exp2(m_prev - m_new)
            p = jnp.exp2(s_in[0:nk, cols] - m_new).astype(BF16)
            pv = jnp.dot(vt_aug[:, 0:nk], p, preferred_element_type=F32)
            l_sc[:, cols] = alpha * l_sc[:, cols] + pv[V_HEAD_DIM:V_HEAD_DIM + 1, :]
            acc_sc[:, cols] = alpha * acc_sc[:, cols] + pv[0:V_HEAD_DIM, :]
            m_sc[:, cols] = m_new
        if last:
            lq1, lk1, lq2, lk2 = lam_ref[0:1, :], lam_ref[1:2, :], lam_ref[2:3, :], lam_ref[3:4, :]
            lam = (jnp.exp(jnp.sum(lq1 * lk1, axis=-1, keepdims=True))
                   - jnp.exp(jnp.sum(lq2 * lk2, axis=-1, keepdims=True)) + lam_init)
            o = acc_sc[...] / l_sc[...]
            o = o[:, 0:tq] - lam * o[:, tq:2 * tq]
            ms = jnp.mean(o * o, axis=0, keepdims=True)
            o = o * lax.rsqrt(ms + EPS) * (gain_ref[...] * (1.0 - lam_init))
            out_ref[...] = o.T.astype(out_ref.dtype)

    @pl.when(has_b & (kb == 0))
    def _():
        s_t = sm_sc[...]
        m0 = jnp.max(s_t, axis=0, keepdims=True)
        p = jnp.exp2(s_t - m0)
        m_sc[...] = m0
        l_sc[...] = jnp.sum(p, axis=0, keepdims=True)
        acc_sc[...] = jnp.dot(vmt_ref[...], p.astype(BF16), preferred_element_type=F32)

    @pl.when(has_a & (ka == 0))
    def _():
        q = q_ref[...]
        lane = lax.broadcasted_iota(jnp.int32, q.shape, 1)
        zero = jnp.zeros_like(q)
        qs_sc[0:tq, :] = jnp.where(lane < DIFF_HEAD_DIM, q, zero)
        qs_sc[tq:2 * tq, :] = jnp.where(lane >= DIFF_HEAD_DIM, q, zero)
        s_t = lax.dot_general(km_ref[...], qs_sc[...], nt, preferred_element_type=F32)
        key = lax.broadcasted_iota(jnp.int32, s_t.shape, 0)
        sm_sc[...] = jnp.where(key < N_META, s_t, -jnp.inf)

    buf0, buf1 = (s0_sc, mx0_sc), (s1_sc, mx1_sc)
    a_diag = ka == qa
    b_last = kb == qb
    both = has_a & has_b
    for parity, (buf_a, buf_b) in enumerate(((buf0, buf1), (buf1, buf0))):
        par = even if parity == 0 else jnp.logical_not(even)

        @pl.when(par & both & jnp.logical_not(a_diag) & jnp.logical_not(b_last))
        def _(buf_a=buf_a, buf_b=buf_b):
            stage_a(buf_a, False)
            stage_b(buf_b, False)

        @pl.when(par & both & a_diag & jnp.logical_not(b_last))
        def _(buf_a=buf_a, buf_b=buf_b):
            stage_a(buf_a, True)
            stage_b(buf_b, False)

        @pl.when(par & both & b_last)
        def _(buf_a=buf_a, buf_b=buf_b):
            stage_a(buf_a, False)
            stage_b(buf_b, True)

        @pl.when(par & has_a & jnp.logical_not(has_b))
        def _(buf_a=buf_a):
            stage_a(buf_a, True)

        @pl.when(par & has_b & jnp.logical_not(has_a))
        def _(buf_b=buf_b):
            stage_b(buf_b, True)


def _attention(qk, vt, kmeta, vmeta_t, lam_vecs, gain_col, *, bsz, seq, tq, dg, lam_init):
    nq = seq // tq
    h = N_DIFF_HEADS
    qi_list, ki_list = [], []
    for a in range(nq):
        for b in range(a + 1):
            qi_list.append(a)
            ki_list.append(b)
    n_steps = len(qi_list)
    ia = np.minimum(np.arange(n_steps + 1), n_steps - 1)
    ib = np.maximum(np.arange(n_steps + 1) - 1, 0)
    qi_arr, ki_arr = np.array(qi_list, np.int32), np.array(ki_list, np.int32)
    tabs = [jnp.asarray(t) for t in (qi_arr[ia], ki_arr[ia], qi_arr[ib], ki_arr[ib])]
    grid_spec = pltpu.PrefetchScalarGridSpec(
        num_scalar_prefetch=4,
        grid=(bsz, h, n_steps + 1),
        in_specs=[
            pl.BlockSpec((tq, V_HEAD_DIM), lambda b, hh, g, qa, ka, qb, kb: (b * nq + qa[g], hh)),
            pl.BlockSpec((tq, V_HEAD_DIM), lambda b, hh, g, qa, ka, qb, kb: (b * nq + ka[g], h + hh)),
            pl.BlockSpec((V_HEAD_DIM, tq), lambda b, hh, g, qa, ka, qb, kb: (hh, b * nq + kb[g])),
            pl.BlockSpec((LANES, V_HEAD_DIM), lambda b, hh, g, qa, ka, qb, kb: (0, hh)),
            pl.BlockSpec((V_HEAD_DIM, LANES), lambda b, hh, g, qa, ka, qb, kb: (hh, 0)),
            pl.BlockSpec((4, DIFF_HEAD_DIM), lambda b, hh, g, qa, ka, qb, kb: (0, 0)),
            pl.BlockSpec((V_HEAD_DIM, 1), lambda b, hh, g, qa, ka, qb, kb: (0, 0)),
        ],
        out_specs=pl.BlockSpec((tq, V_HEAD_DIM), lambda b, hh, g, qa, ka, qb, kb: (b * nq + qb[g], hh)),
        scratch_shapes=[
            pltpu.VMEM((2 * tq, V_HEAD_DIM), BF16),
            pltpu.VMEM((LANES, 2 * tq), F32),
            pltpu.VMEM((tq, 2 * tq), F32),
            pltpu.VMEM((tq, 2 * tq), F32),
            pltpu.VMEM((1, 2 * tq), F32),
            pltpu.VMEM((1, 2 * tq), F32),
            pltpu.VMEM((1, 2 * tq), F32),
            pltpu.VMEM((1, 2 * tq), F32),
            pltpu.VMEM((V_HEAD_DIM, 2 * tq), F32),
        ],
    )
    return pl.pallas_call(
        functools.partial(_attn_kernel, tq=tq, dg=dg, n_steps=n_steps, lam_init=lam_init),
        grid_spec=grid_spec,
        out_shape=jax.ShapeDtypeStruct((bsz * seq, h * V_HEAD_DIM), BF16),
        compiler_params=_cparams(("parallel", "parallel", "arbitrary")),
        name="attn",
    )(*tabs, qk, qk, vt, kmeta, vmeta_t, lam_vecs, gain_col)


def _ssd_kernel(*refs, t_blk, bsz):
    dtt_refs = refs[:bsz]
    (zx_ref, dt_ref, xbcm_ref, dtm_ref, dttm_ref,
     convw_ref, convb_ref, dtb_ref, dtbt_ref, alog_ref, alogt_ref, dskip_ref, nw_ref,
     expand_ref, tril_ref, triu2_ref, shift_ref, out_ref, xs_sc, xm_sc, state_sc) = refs[bsz:]
    n_heads, p_dim, n_state, n_groups = N_SSD_HEADS, SSD_HEAD_DIM, D_STATE, N_SSD_GROUPS
    width = n_heads * p_dim
    gw = width // n_groups
    n_pairs = n_heads // 2
    nt = (((1,), (1,)), ((), ()))
    expand = expand_ref[...]
    a_row = -jnp.exp(alog_ref[...])
    a_col = -jnp.exp(alogt_ref[...])
    d_full = dskip_ref[...]
    lane = lax.broadcasted_iota(jnp.int32, (CHUNK, LANES), 1)
    row = lax.broadcasted_iota(jnp.int32, (CHUNK, LANES), 0)
    left = lane < p_dim
    causal2 = (lane & (CHUNK - 1)) <= row

    def chunk(win_f32, st_ref, dt_raw, dtt_raw, pad_rows, z, out_r0, b=0):
        win = win_f32.astype(BF16)
        conv = convb_ref[...] + win[HALO:, :].astype(F32) * convw_ref[CONV_WIDTH - 1:CONV_WIDTH, :]
        for k in range(CONV_WIDTH - 1):
            conv = conv + jnp.dot(shift_ref[k], win, preferred_element_type=F32) * convw_ref[k:k + 1, :]
        xc = _silu(conv)
        x_s = xc[:, :width]
        dt = jax.nn.softplus(dt_raw + dtb_ref[...])
        dtt = jax.nn.softplus(dtt_raw + dtbt_ref[...])
        if pad_rows:
            dt = jnp.where(row >= pad_rows, dt, 0.0)
            dtt = jnp.where(lax.broadcasted_iota(jnp.int32, dtt.shape, 1) >= pad_rows, dtt, 0.0)
        a_cs = _dot_01_by_f32(tril_ref[...], dt * a_row)
        a_cs_t2 = _dot_f32_by_01(dtt * a_col, triu2_ref[...])
        a_full = _dot_f32_by_01(a_cs, expand)
        dt_full = _dot_f32_by_01(dt, expand)
        a_last = a_full[CHUNK - 1:CHUNK, :]
        xdt = x_s * dt_full
        w_state = (xdt * jnp.exp(a_last - a_full)).astype(BF16)
        xdt_b = xdt.astype(BF16)
        zero_b = jnp.zeros((CHUNK, LANES), BF16)
        y_parts = []
        for g in range(n_groups):
            b_g = xc[:, width + g * n_state: width + (g + 1) * n_state]
            c_g = xc[:, width + (n_groups + g) * n_state: width + (n_groups + g + 1) * n_state]
            b_gb = b_g.astype(BF16)
            c_gb = c_g.astype(BF16)
            st = st_ref[g]
            if out_r0 is not None:
                cb2 = lax.dot_general(c_gb, jnp.concatenate([b_gb, b_gb], axis=0), nt,
                                      preferred_element_type=F32)
                y_off = jnp.dot(c_gb, st.astype(BF16), preferred_element_type=F32)
                y_g = y_off * jnp.exp(a_full[:, g * gw:(g + 1) * gw])
                diag = []
                for pp in range(n_pairs // n_groups):
                    pr = g * (n_pairs // n_groups) + pp
                    colb = a_full[:, pr * LANES:(pr + 1) * LANES]
                    rowb = jnp.where(left, a_cs_t2[2 * pr:2 * pr + 1, :], a_cs_t2[2 * pr + 1:2 * pr + 2, :])
                    dec = jnp.exp(jnp.where(causal2, colb - rowb, -jnp.inf))
                    m_pair = (cb2 * dec).astype(BF16)
                    xp = xdt_b[:, pr * LANES:(pr + 1) * LANES]
                    rhs = jnp.concatenate([jnp.where(left, xp, zero_b), jnp.where(left, zero_b, xp)], axis=0)
                    diag.append(jnp.dot(m_pair, rhs, preferred_element_type=F32))
                y_parts.append(y_g + jnp.concatenate(diag, axis=1))
            new_st = st * jnp.exp(a_last[:, g * gw:(g + 1) * gw]) + jnp.dot(
                b_g.T.astype(BF16), w_state[:, g * gw:(g + 1) * gw], preferred_element_type=F32)
            st_ref[g] = new_st
        if out_r0 is not None:
            y = jnp.concatenate(y_parts, axis=1) + x_s * d_full
            gated = y * _silu(z)
            out_ref[b, out_r0:out_r0 + CHUNK, :] = _rms(gated, nw_ref[...]).astype(out_ref.dtype)

    @pl.when(pl.program_id(0) == 0)
    def _():
        state_sc[...] = jnp.zeros_like(state_sc)
        xm_sc[0:HALO, :] = jnp.zeros((HALO, xm_sc.shape[1]), F32)
        xm_sc[HALO:HALO + CHUNK, :] = xbcm_ref[...].astype(F32)
        chunk(xm_sc[...], state_sc.at[0], dtm_ref[...], dttm_ref[...], CHUNK - N_META, None, None)
        for b in range(bsz):
            if b:
                state_sc[b] = state_sc[0]
            xs_sc[b, 0:HALO, :] = xm_sc[CHUNK:CHUNK + HALO, :]

    for b in range(bsz):
        xs_sc[b, HALO:HALO + t_blk, :] = zx_ref[b, :, width:].astype(F32)
    for c in range(t_blk // CHUNK):
        rows = slice(c * CHUNK, (c + 1) * CHUNK)
        for b in range(bsz):
            chunk(xs_sc[b, c * CHUNK:(c + 1) * CHUNK + HALO, :], state_sc.at[b], dt_ref[b, rows, :],
                  dtt_refs[b][:, rows], 0, zx_ref[b, rows, 0:width].astype(F32), c * CHUNK, b)
    for b in range(bsz):
        xs_sc[b, 0:HALO, :] = xs_sc[b, t_blk:t_blk + HALO, :]


def _ssd(zx, dt, dtt, xbc_m, dt_m, dtt_m, conv_w, conv_b, dt_bias, a_log, d_skip, norm_w,
         *, bsz, seq, t_blk):
    nb = seq // t_blk
    width = N_SSD_HEADS * SSD_HEAD_DIM
    cdim = zx.shape[1] - width
    nh = N_SSD_HEADS

    def pad_row(v):
        return jnp.pad(v.reshape(1, nh), ((0, 0), (0, LANES - nh)))

    hh = np.arange(LANES)[:, None]
    ll = np.arange(width)[None, :]
    one_hot = lambda m: jnp.asarray(m.astype(np.float32), dtype=BF16)
    expand = one_hot(ll // SSD_HEAD_DIM == hh)
    ii = np.arange(CHUNK)
    tril = one_hot(ii[None, :] <= ii[:, None])
    triu2 = one_hot(np.tile(ii[:, None] <= ii[None, :], (1, 2)))
    jj = np.arange(HALO + CHUNK)
    shift = one_hot(np.stack([jj[None, :] == ii[:, None] + HALO - (CONV_WIDTH - 1) + k
                              for k in range(CONV_WIDTH - 1)]))
    full = lambda shape: pl.BlockSpec(shape, lambda i: tuple(0 for _ in shape))
    out = pl.pallas_call(
        functools.partial(_ssd_kernel, t_blk=t_blk, bsz=bsz),
        grid=(nb,),
        in_specs=[pl.BlockSpec((nh, t_blk), functools.partial(lambda i, b: (0, b * nb + i), b=b))
                  for b in range(bsz)] + [
            pl.BlockSpec((bsz, t_blk, width + cdim), lambda i: (0, i, 0)),
            pl.BlockSpec((bsz, t_blk, LANES), lambda i: (0, i, 0)),
            full((CHUNK, cdim)), full((CHUNK, LANES)), full((nh, CHUNK)),
            full((CONV_WIDTH, cdim)), full((1, cdim)), full((1, LANES)), full((nh, 1)),
            full((1, LANES)), full((nh, 1)), full((1, width)), full((1, width)),
            full((LANES, width)), full((CHUNK, CHUNK)), full((CHUNK, 2 * CHUNK)), full(shift.shape),
        ],
        out_specs=pl.BlockSpec((bsz, t_blk, width), lambda i: (0, i, 0)),
        out_shape=jax.ShapeDtypeStruct((bsz, seq, width), BF16),
        scratch_shapes=[
            pltpu.VMEM((bsz, t_blk + HALO, cdim), F32),
            pltpu.VMEM((CHUNK + HALO, cdim), F32),
            pltpu.VMEM((bsz, N_SSD_GROUPS, D_STATE, width // N_SSD_GROUPS), F32),
        ],
        compiler_params=_cparams(("arbitrary",)),
        name="ssd",
    )(*([dtt] * bsz), zx.reshape(bsz, seq, width + cdim), dt.reshape(bsz, seq, LANES), xbc_m, dt_m, dtt_m,
      conv_w, conv_b.reshape(1, cdim), pad_row(dt_bias), dt_bias.reshape(nh, 1), pad_row(a_log),
      a_log.reshape(nh, 1), jnp.repeat(d_skip, SSD_HEAD_DIM).reshape(1, width),
      norm_w.reshape(1, width), expand, tril, triu2, shift)
    return out.reshape(bsz * seq, width)


def _rope_tables(n_pos):
    inv = jnp.power(ROPE_THETA, -jnp.arange(0, ROT_DIM, 2, dtype=F32) / ROT_DIM)
    ang = jnp.arange(n_pos, dtype=F32)[:, None] * inv[None, :]
    cos, sin = jnp.cos(ang), jnp.sin(ang)
    half = ROT_DIM // 2
    r = np.arange(LANES) % DIFF_HEAD_DIM
    idx = jnp.asarray(r % half)
    lo = jnp.asarray(r < half)[None, :]
    hi = jnp.asarray((r >= half) & (r < ROT_DIM))[None, :]
    cos_t = jnp.where(lo | hi, cos[:, idx], 1.0)
    sin_lo = jnp.where(lo, -sin[:, idx], 0.0)
    sin_hi = jnp.where(hi, sin[:, idx], 0.0)
    return jnp.stack([cos_t, sin_lo, sin_hi])


def kernel(x, meta_tokens, ffn1_norm, ffn1_w_gate, ffn1_w_up, ffn1_w_down, mix_norm, w_in, q_norm, k_norm,
           lambda_q1, lambda_k1, lambda_q2, lambda_k2, attn_out_norm, conv_w, conv_b, dt_bias, a_log, d_skip,
           ssd_norm, w_out, ffn2_norm, ffn2_w_gate, ffn2_w_up, ffn2_w_down):
    bsz, seq, d = x.shape
    assert ffn1_norm.shape[0] == 1, "single-layer block"
    aw = N_DIFF_HEADS * V_HEAD_DIM
    sw = N_SSD_HEADS * SSD_HEAD_DIM
    cdim = sw + 2 * N_SSD_GROUPS * D_STATE
    lam_init = 0.8 - 0.6 * math.exp(-0.3 * 0)
    row = lambda v: v.reshape(1, -1)

    tn = 512
    w_in0 = w_in[0]
    w_qk = w_in0[:, :2 * aw].astype(BF16)
    w_v_t = w_in0[:, 2 * aw:3 * aw].T.astype(BF16)
    w_zx = w_in0[:, 3 * aw:3 * aw + sw + cdim].astype(BF16)
    w_dt = w_in0[:, 3 * aw + sw + cdim:].astype(BF16)
    w_dt_pad = jnp.pad(w_dt, ((0, 0), (0, LANES - N_SSD_HEADS)))
    w_dt_t = w_dt.T

    n_sub = aw // DIFF_HEAD_DIM
    q_gain = jnp.tile(q_norm[0], n_sub) * (DIFF_HEAD_DIM ** -0.5 * math.log2(math.e))
    k_gain = jnp.tile(k_norm[0], n_sub)
    qk_gain = jnp.concatenate([q_gain, k_gain]).reshape(2 * aw // tn, 1, tn)
    rope = _rope_tables(N_META + seq)
    gi = np.arange(2 * LANES) // DIFF_HEAD_DIM
    bd = jnp.asarray((gi[:, None] == gi[None, :]).astype(np.float32) / DIFF_HEAD_DIM, dtype=BF16)

    def ffn_weights(g, u, dn):
        return g[0].astype(BF16), u[0].astype(BF16), dn[0].astype(BF16)

    wg1, wu1, wd1 = ffn_weights(ffn1_w_gate, ffn1_w_up, ffn1_w_down)
    wg2, wu2, wd2 = ffn_weights(ffn2_w_gate, ffn2_w_up, ffn2_w_down)

    x2 = x.reshape(bsz * seq, d)
    h1, hn = _ffn(x2, row(ffn1_norm[0]), wg1, wu1, wd1, row(mix_norm[0]), tm=FFN_TM, tf=FFN_TF)
    _, hn_m = _ffn(meta_tokens.astype(F32), row(ffn1_norm[0]), wg1, wu1, wd1, row(mix_norm[0]),
                   tm=N_META, tf=FFN_TF)

    qk = _proj_qk(hn, w_qk, qk_gain, rope[:, N_META:], bd, tm=1024, tn=tn)
    qk_m = _proj_qk(hn_m, w_qk, qk_gain, rope[:, :N_META], bd, tm=N_META, tn=tn)
    v_t = _proj_t(w_v_t, hn, tn=tn, tm=1024, name="proj_vt")
    v_t_m = _proj_t(w_v_t, hn_m, tn=tn, tm=N_META, name="proj_vt_meta")
    zx = _proj(hn, w_zx, tm=1024, tn=tn, name="proj_zx")
    xbc_m = _proj(hn_m, w_zx, tm=N_META, tn=tn, name="proj_zx_meta")[:, sw:]
    dt, dtt = _proj_dt(hn, w_dt_pad, w_dt_t, tm=1024)
    dt_m, dtt_m = _proj_dt(hn_m, w_dt_pad, w_dt_t, tm=N_META)

    kmeta = jnp.pad(qk_m[:, aw:], ((0, LANES - N_META), (0, 0)))
    vmeta_t = jnp.pad(v_t_m, ((0, 0), (0, LANES - N_META)))
    lam_vecs = jnp.stack([lambda_q1[0], lambda_k1[0], lambda_q2[0], lambda_k2[0]]).astype(F32)
    attn = _attention(qk, v_t, kmeta, vmeta_t, lam_vecs, attn_out_norm[0].reshape(V_HEAD_DIM, 1),
                      bsz=bsz, seq=seq, tq=1024, dg=256, lam_init=lam_init)

    lead = CHUNK - N_META
    ssd = _ssd(zx, dt, dtt,
               jnp.pad(xbc_m, ((lead, 0), (0, 0))), jnp.pad(dt_m, ((lead, 0), (0, 0))),
               jnp.pad(dtt_m, ((0, 0), (lead, 0))),
               conv_w[0], conv_b[0], dt_bias[0], a_log[0], d_skip[0], ssd_norm[0],
               bsz=bsz, seq=seq, t_blk=256)

    w_o = w_out[0].astype(BF16)
    h2 = _outproj(h1, attn, ssd, w_o[:aw], w_o[aw:], tm=512, tn=d)
    (out,) = _ffn(h2, row(ffn2_norm[0]), wg2, wu2, wd2, tm=FFN_TM, tf=FFN_TF)
    return out.reshape(bsz, seq, d)
```

```python
import functools
import math

import jax
import jax.numpy as jnp
import numpy as np
from jax import lax
from jax.experimental import pallas as pl
from jax.experimental.pallas import tpu as pltpu

F32 = jnp.float32
BF16 = jnp.bfloat16

EPS = 1e-6
CHUNK = 64
N_META = 16
N_DIFF_HEADS = 8
DIFF_HEAD_DIM = 64
V_HEAD_DIM = 128
ROT_DIM = 16
ROPE_THETA = 500000.0
SSD_HEAD_DIM = 64
N_SSD_HEADS = 16
N_SSD_GROUPS = 2
D_STATE = 128
CONV_WIDTH = 4
HALO = 64
LANES = 128
VMEM_LIMIT = 56 * 1024 * 1024
FFN_TM = 512
FFN_TF = 512


def _cparams(sem):
    return pltpu.CompilerParams(dimension_semantics=sem, vmem_limit_bytes=VMEM_LIMIT)


def _silu(v):
    half = 0.5 * v
    return half + half * jnp.tanh(half)


def _split_bf16(a, pieces):
    out = []
    for _ in range(pieces - 1):
        p = a.astype(BF16)
        out.append(p)
        a = a - p.astype(F32)
    out.append(a.astype(BF16))
    return out


def _dot_f32_by_01(a, m01, pieces=3):
    return sum(jnp.dot(p, m01, preferred_element_type=F32) for p in _split_bf16(a, pieces))


def _dot_01_by_f32(m01, a, pieces=3):
    return sum(jnp.dot(m01, p, preferred_element_type=F32) for p in _split_bf16(a, pieces))


def _rms(v, w):
    ms = jnp.mean(v * v, axis=-1, keepdims=True)
    return v * lax.rsqrt(ms + EPS) * w


def _swiglu_step(xn, wg_ref, wu_ref, wd_ref):
    g = jnp.dot(xn, wg_ref[...], preferred_element_type=F32)
    u = jnp.dot(xn, wu_ref[...], preferred_element_type=F32)
    a = (_silu(g) * u).astype(BF16)
    return jnp.dot(a, wd_ref[...], preferred_element_type=F32)


def _qk_norm_rope(acc, gain, rope_ref, bd):
    cos, sin_lo, sin_hi = rope_ref[0], rope_ref[1], rope_ref[2]
    half = ROT_DIM // 2
    wide = bd.shape[0]
    out = []
    for c in range(acc.shape[1] // wide):
        y = acc[:, c * wide:(c + 1) * wide]
        ms = sum(jnp.dot(p, bd, preferred_element_type=F32) for p in _split_bf16(y * y, 2))
        yn = y * lax.rsqrt(ms + EPS) * gain[:, c * wide:(c + 1) * wide]
        for s in range(wide // LANES):
            v = yn[:, s * LANES:(s + 1) * LANES]
            out.append(v * cos + pltpu.roll(v, half, 1) * sin_hi + pltpu.roll(v, LANES - half, 1) * sin_lo)
    return jnp.concatenate(out, axis=1)


def _ffn_kernel(x_ref, nw_ref, wg_ref, wu_ref, wd_ref, *rest, with_norm):
    if with_norm:
        pnw_ref, out_ref, hn_ref, xn_sc = rest
    else:
        out_ref, xn_sc = rest
    j = pl.program_id(1)

    @pl.when(j == 0)
    def _():
        xn_sc[...] = _rms(x_ref[...], nw_ref[...]).astype(BF16)
        out_ref[...] = jnp.zeros_like(out_ref)

    out_ref[...] += _swiglu_step(xn_sc[...], wg_ref, wu_ref, wd_ref)

    @pl.when(j == pl.num_programs(1) - 1)
    def _():
        h = x_ref[...] + 0.5 * out_ref[...]
        out_ref[...] = h
        if with_norm:
            hn_ref[...] = _rms(h, pnw_ref[...]).astype(BF16)


def _ffn(x2, nw, wg, wu, wd, post_nw=None, *, tm, tf):
    m, d = x2.shape
    f = wg.shape[1]
    with_norm = post_nw is not None
    row_spec = pl.BlockSpec((tm, d), lambda i, j: (i, 0))
    vec_spec = pl.BlockSpec((1, d), lambda i, j: (0, 0))
    in_specs = [row_spec, vec_spec,
                pl.BlockSpec((d, tf), lambda i, j: (0, j)),
                pl.BlockSpec((d, tf), lambda i, j: (0, j)),
                pl.BlockSpec((tf, d), lambda i, j: (j, 0))]
    args = [x2, nw, wg, wu, wd]
    out_specs, out_shape = [row_spec], [jax.ShapeDtypeStruct((m, d), F32)]
    if with_norm:
        in_specs.append(vec_spec)
        args.append(post_nw)
        out_specs.append(row_spec)
        out_shape.append(jax.ShapeDtypeStruct((m, d), BF16))
    return pl.pallas_call(
        functools.partial(_ffn_kernel, with_norm=with_norm),
        grid=(m // tm, f // tf),
        in_specs=in_specs,
        out_specs=out_specs,
        out_shape=out_shape,
        scratch_shapes=[pltpu.VMEM((tm, d), BF16)],
        compiler_params=_cparams(("parallel", "arbitrary")),
        name="ffn",
    )(*args)


def _proj_kernel(x_ref, w_ref, out_ref):
    out_ref[...] = jnp.dot(x_ref[...], w_ref[...], preferred_element_type=F32).astype(out_ref.dtype)


def _proj(x2, w, *, tm, tn, name):
    m, d = x2.shape
    n = w.shape[1]
    return pl.pallas_call(
        _proj_kernel,
        grid=(m // tm, n // tn),
        in_specs=[pl.BlockSpec((tm, d), lambda i, j: (i, 0)), pl.BlockSpec((d, tn), lambda i, j: (0, j))],
        out_specs=pl.BlockSpec((tm, tn), lambda i, j: (i, j)),
        out_shape=jax.ShapeDtypeStruct((m, n), BF16),
        compiler_params=_cparams(("parallel", "arbitrary")),
        name=name,
    )(x2, w)


def _proj_t_kernel(w_ref, x_ref, out_ref):
    out_ref[...] = lax.dot_general(w_ref[...], x_ref[...], (((1,), (1,)), ((), ())),
                                   preferred_element_type=F32).astype(out_ref.dtype)


def _proj_t(w_t, x2, *, tn, tm, name):
    n, d = w_t.shape
    m = x2.shape[0]
    return pl.pallas_call(
        _proj_t_kernel,
        grid=(m // tm, n // tn),
        in_specs=[pl.BlockSpec((tn, d), lambda i, j: (j, 0)), pl.BlockSpec((tm, d), lambda i, j: (i, 0))],
        out_specs=pl.BlockSpec((tn, tm), lambda i, j: (j, i)),
        out_shape=jax.ShapeDtypeStruct((n, m), BF16),
        compiler_params=_cparams(("parallel", "arbitrary")),
        name=name,
    )(w_t, x2)


def _proj_qk_kernel(x_ref, w_ref, gain_ref, rope_ref, bd_ref, out_ref):
    acc = jnp.dot(x_ref[...], w_ref[...], preferred_element_type=F32)
    out_ref[...] = _qk_norm_rope(acc, gain_ref[0], rope_ref, bd_ref[...]).astype(out_ref.dtype)


def _proj_qk(x2, w, gain, rope, bd, *, tm, tn):
    m, d = x2.shape
    n = w.shape[1]
    n_tab = rope.shape[1] // tm
    return pl.pallas_call(
        _proj_qk_kernel,
        grid=(m // tm, n // tn),
        in_specs=[
            pl.BlockSpec((tm, d), lambda i, j: (i, 0)),
            pl.BlockSpec((d, tn), lambda i, j: (0, j)),
            pl.BlockSpec((1, 1, tn), lambda i, j: (j, 0, 0)),
            pl.BlockSpec((3, tm, LANES), lambda i, j: (0, i % n_tab, 0)),
            pl.BlockSpec(bd.shape, lambda i, j: (0, 0)),
        ],
        out_specs=pl.BlockSpec((tm, tn), lambda i, j: (i, j)),
        out_shape=jax.ShapeDtypeStruct((m, n), BF16),
        compiler_params=_cparams(("parallel", "arbitrary")),
        name="proj_qk",
    )(x2, w, gain, rope, bd)


def _proj_dt_kernel(x_ref, w_ref, wt_ref, out_ref, outt_ref):
    x = x_ref[...]
    out_ref[...] = jnp.dot(x, w_ref[...], preferred_element_type=F32)
    outt_ref[...] = lax.dot_general(wt_ref[...], x, (((1,), (1,)), ((), ())), preferred_element_type=F32)


def _proj_dt(x2, w_pad, w_t, *, tm):
    m, d = x2.shape
    nh = w_t.shape[0]
    return pl.pallas_call(
        _proj_dt_kernel,
        grid=(m // tm,),
        in_specs=[
            pl.BlockSpec((tm, d), lambda i: (i, 0)),
            pl.BlockSpec((d, LANES), lambda i: (0, 0)),
            pl.BlockSpec((nh, d), lambda i: (0, 0)),
        ],
        out_specs=[pl.BlockSpec((tm, LANES), lambda i: (i, 0)), pl.BlockSpec((nh, tm), lambda i: (0, i))],
        out_shape=[jax.ShapeDtypeStruct((m, LANES), F32), jax.ShapeDtypeStruct((nh, m), F32)],
        compiler_params=_cparams(("parallel",)),
        name="proj_dt",
    )(x2, w_pad, w_t)


def _outproj_kernel(h_ref, a_ref, s_ref, wa_ref, ws_ref, out_ref):
    out_ref[...] = (h_ref[...]
                    + jnp.dot(a_ref[...], wa_ref[...], preferred_element_type=F32)
                    + jnp.dot(s_ref[...], ws_ref[...], preferred_element_type=F32))


def _outproj(h1, attn, ssd, w_o, *, tm, tn):
    m, d = h1.shape
    k = attn.shape[1]
    return pl.pallas_call(
        _outproj_kernel,
        grid=(m // tm, d // tn),
        in_specs=[
            pl.BlockSpec((tm, tn), lambda i, j: (i, j)),
            pl.BlockSpec((tm, k), lambda i, j: (i, 0)),
            pl.BlockSpec((tm, k), lambda i, j: (i, 0)),
            pl.BlockSpec((k, tn), lambda i, j: (0, j)),
            pl.BlockSpec((k, tn), lambda i, j: (1, j)),
        ],
        out_specs=pl.BlockSpec((tm, tn), lambda i, j: (i, j)),
        out_shape=jax.ShapeDtypeStruct((m, d), F32),
        compiler_params=_cparams(("parallel", "arbitrary")),
        name="outproj",
    )(h1, attn, ssd, w_o, w_o)


def _attn_kernel(qa_tab, ka_tab, qb_tab, kb_tab, q_ref, k_ref, vt_ref, km_ref, vmt_ref, lam_ref, gain_ref,
                 out_ref, qs_sc, sm_sc, s0_sc, s1_sc, mx0_sc, mx1_sc, m_sc, l_sc, acc_sc,
                 *, tq, dg, n_steps, lam_init):
    g = pl.program_id(2)
    qa = qa_tab[g]
    ka = ka_tab[g]
    qb = qb_tab[g]
    kb = kb_tab[g]
    has_a = g < n_steps
    has_b = g >= 1
    even = (g % 2) == 0
    nt = (((1,), (1,)), ((), ()))

    full_groups = [(slice(0, 2 * tq), tq)]
    diag_groups = [(slice(sub * tq + j * dg, sub * tq + (j + 1) * dg), (j + 1) * dg)
                   for sub in range(2) for j in range(tq // dg)]

    def stage_a(buf, diagonal):
        s_out, mx_out = buf
        for cols, nk in (diag_groups if diagonal else full_groups):
            s_t = lax.dot_general(k_ref[0:nk, :], qs_sc[cols, :], nt, preferred_element_type=F32)
            if diagonal:
                key = lax.broadcasted_iota(jnp.int32, s_t.shape, 0)
                qry = lax.broadcasted_iota(jnp.int32, s_t.shape, 1) + (nk - dg)
                s_t = jnp.where(key // CHUNK <= qry // CHUNK, s_t, -jnp.inf)
            s_out[0:nk, cols] = s_t
            mx_out[:, cols] = jnp.max(s_t, axis=0, keepdims=True)

    def stage_b(buf, last):
        s_in, mx_in = buf
        vt_aug = jnp.concatenate([vt_ref[...], jnp.ones((16, vt_ref.shape[1]), BF16)], axis=0)
        for cols, nk in (diag_groups if last else full_groups):
            m_prev = m_sc[:, cols]
            m_new = jnp.maximum(m_prev, mx_in[:, cols])
            alpha = jnp.exp2(m_prev - m_new)
            p = jnp.exp2(s_in[0:nk, cols] - m_new).astype(BF16)
            pv = jnp.dot(vt_aug[:, 0:nk], p, preferred_element_type=F32)
            l_sc[:, cols] = alpha * l_sc[:, cols] + pv[V_HEAD_DIM:V_HEAD_DIM + 1, :]
            acc_sc[:, cols] = alpha * acc_sc[:, cols] + pv[0:V_HEAD_DIM, :]
            m_sc[:, cols] = m_new
        if last:
            lq1, lk1, lq2, lk2 = lam_ref[0:1, :], lam_ref[1:2, :], lam_ref[2:3, :], lam_ref[3:4, :]
            lam = (jnp.exp(jnp.sum(lq1 * lk1, axis=-1, keepdims=True))
                   - jnp.exp(jnp.sum(lq2 * lk2, axis=-1, keepdims=True)) + lam_init)
            o = acc_sc[...] / l_sc[...]
            o = o[:, 0:tq] - lam * o[:, tq:2 * tq]
            ms = jnp.mean(o * o, axis=0, keepdims=True)
            o = o * lax.rsqrt(ms + EPS) * (gain_ref[...] * (1.0 - lam_init))
            out_ref[...] = o.T.astype(out_ref.dtype)

    @pl.when(has_b & (kb == 0))
    def _():
        s_t = sm_sc[...]
        m0 = jnp.max(s_t, axis=0, keepdims=True)
        p = jnp.exp2(s_t - m0)
        m_sc[...] = m0
        l_sc[...] = jnp.sum(p, axis=0, keepdims=True)
        acc_sc[...] = jnp.dot(vmt_ref[...], p.astype(BF16), preferred_element_type=F32)

    @pl.when(has_a & (ka == 0))
    def _():
        q = q_ref[...]
        lane = lax.broadcasted_iota(jnp.int32, q.shape, 1)
        zero = jnp.zeros_like(q)
        qs_sc[0:tq, :] = jnp.where(lane < DIFF_HEAD_DIM, q, zero)
        qs_sc[tq:2 * tq, :] = jnp.where(lane >= DIFF_HEAD_DIM, q, zero)
        s_t = lax.dot_general(km_ref[...], qs_sc[...], nt, preferred_element_type=F32)
        key = lax.broadcasted_iota(jnp.int32, s_t.shape, 0)
        sm_sc[...] = jnp.where(key < N_META, s_t, -jnp.inf)

    buf0, buf1 = (s0_sc, mx0_sc), (s1_sc, mx1_sc)
    a_diag = ka == qa
    b_last = kb == qb
    both = has_a & has_b
    for parity, (buf_a, buf_b) in enumerate(((buf0, buf1), (buf1, buf0))):
        par = even if parity == 0 else jnp.logical_not(even)

        @pl.when(par & both & jnp.logical_not(a_diag) & jnp.logical_not(b_last))
        def _(buf_a=buf_a, buf_b=buf_b):
            stage_a(buf_a, False)
            stage_b(buf_b, False)

        @pl.when(par & both & a_diag & jnp.logical_not(b_last))
        def _(buf_a=buf_a, buf_b=buf_b):
            stage_a(buf_a, True)
            stage_b(buf_b, False)

        @pl.when(par & both & b_last)
        def _(buf_a=buf_a, buf_b=buf_b):
            stage_a(buf_a, False)
            stage_b(buf_b, True)

        @pl.when(par & has_a & jnp.logical_not(has_b))
        def _(buf_a=buf_a):
            stage_a(buf_a, True)

        @pl.when(par & has_b & jnp.logical_not(has_a))
        def _(buf_b=buf_b):
            stage_b(buf_b, True)


def _attention(qk, vt, kmeta, vmeta_t, lam_vecs, gain_col, *, bsz, seq, tq, dg, lam_init):
    nq = seq // tq
    h = N_DIFF_HEADS
    qi_list, ki_list = [], []
    for a in range(nq):
        for b in range(a + 1):
            qi_list.append(a)
            ki_list.append(b)
    n_steps = len(qi_list)
    ia = np.minimum(np.arange(n_steps + 1), n_steps - 1)
    ib = np.maximum(np.arange(n_steps + 1) - 1, 0)
    qi_arr, ki_arr = np.array(qi_list, np.int32), np.array(ki_list, np.int32)
    tabs = [jnp.asarray(t) for t in (qi_arr[ia], ki_arr[ia], qi_arr[ib], ki_arr[ib])]
    grid_spec = pltpu.PrefetchScalarGridSpec(
        num_scalar_prefetch=4,
        grid=(bsz, h, n_steps + 1),
        in_specs=[
            pl.BlockSpec((tq, V_HEAD_DIM), lambda b, hh, g, qa, ka, qb, kb: (b * nq + qa[g], hh)),
            pl.BlockSpec((tq, V_HEAD_DIM), lambda b, hh, g, qa, ka, qb, kb: (b * nq + ka[g], h + hh)),
            pl.BlockSpec((V_HEAD_DIM, tq), lambda b, hh, g, qa, ka, qb, kb: (hh, b * nq + kb[g])),
            pl.BlockSpec((LANES, V_HEAD_DIM), lambda b, hh, g, qa, ka, qb, kb: (0, hh)),
            pl.BlockSpec((V_HEAD_DIM, LANES), lambda b, hh, g, qa, ka, qb, kb: (hh, 0)),
            pl.BlockSpec((4, DIFF_HEAD_DIM), lambda b, hh, g, qa, ka, qb, kb: (0, 0)),
            pl.BlockSpec((V_HEAD_DIM, 1), lambda b, hh, g, qa, ka, qb, kb: (0, 0)),
        ],
        out_specs=pl.BlockSpec((tq, V_HEAD_DIM), lambda b, hh, g, qa, ka, qb, kb: (b * nq + qb[g], hh)),
        scratch_shapes=[
            pltpu.VMEM((2 * tq, V_HEAD_DIM), BF16),
            pltpu.VMEM((LANES, 2 * tq), F32),
            pltpu.VMEM((tq, 2 * tq), F32),
            pltpu.VMEM((tq, 2 * tq), F32),
            pltpu.VMEM((1, 2 * tq), F32),
            pltpu.VMEM((1, 2 * tq), F32),
            pltpu.VMEM((1, 2 * tq), F32),
            pltpu.VMEM((1, 2 * tq), F32),
            pltpu.VMEM((V_HEAD_DIM, 2 * tq), F32),
        ],
    )
    return pl.pallas_call(
        functools.partial(_attn_kernel, tq=tq, dg=dg, n_steps=n_steps, lam_init=lam_init),
        grid_spec=grid_spec,
        out_shape=jax.ShapeDtypeStruct((bsz * seq, h * V_HEAD_DIM), BF16),
        compiler_params=_cparams(("parallel", "parallel", "arbitrary")),
        name="attn",
    )(*tabs, qk, qk, vt, kmeta, vmeta_t, lam_vecs, gain_col)


def _ssd_kernel(*refs, t_blk, bsz, n_cast):
    dtt_refs = refs[:bsz]
    cast_in = refs[bsz:bsz + n_cast]
    (zx_ref, dt_ref, xbcm_ref, dtm_ref, dttm_ref,
     convw_ref, convb_ref, dtb_ref, dtbt_ref, alog_ref, alogt_ref, dskip_ref, nw_ref,
     expand_ref, tril_ref, triu2_ref, shift_ref, out_ref) = refs[bsz + n_cast:bsz + n_cast + 18]
    cast_out = refs[bsz + n_cast + 18:bsz + 2 * n_cast + 18]
    xs_sc, xm_sc, state_sc = refs[bsz + 2 * n_cast + 18:]
    for src, dst in zip(cast_in, cast_out):
        dst[...] = src[...].astype(BF16)
    n_heads, p_dim, n_state, n_groups = N_SSD_HEADS, SSD_HEAD_DIM, D_STATE, N_SSD_GROUPS
    width = n_heads * p_dim
    gw = width // n_groups
    n_pairs = n_heads // 2
    nt = (((1,), (1,)), ((), ()))
    expand = expand_ref[...]
    a_row = -jnp.exp(alog_ref[...])
    a_col = -jnp.exp(alogt_ref[...])
    d_full = dskip_ref[...]
    lane = lax.broadcasted_iota(jnp.int32, (CHUNK, LANES), 1)
    row = lax.broadcasted_iota(jnp.int32, (CHUNK, LANES), 0)
    left = lane < p_dim
    causal2 = (lane & (CHUNK - 1)) <= row

    def chunk(win_f32, st_ref, dt_raw, dtt_raw, pad_rows, z, out_r0, b=0):
        win = win_f32.astype(BF16)
        conv = convb_ref[...] + win[HALO:, :].astype(F32) * convw_ref[CONV_WIDTH - 1:CONV_WIDTH, :]
        for k in range(CONV_WIDTH - 1):
            conv = conv + jnp.dot(shift_ref[k], win, preferred_element_type=F32) * convw_ref[k:k + 1, :]
        xc = _silu(conv)
        x_s = xc[:, :width]
        dt = jax.nn.softplus(dt_raw + dtb_ref[...])
        dtt = jax.nn.softplus(dtt_raw + dtbt_ref[...])
        if pad_rows:
            dt = jnp.where(row >= pad_rows, dt, 0.0)
            dtt = jnp.where(lax.broadcasted_iota(jnp.int32, dtt.shape, 1) >= pad_rows, dtt, 0.0)
        a_cs = _dot_01_by_f32(tril_ref[...], dt * a_row)
        a_cs_t2 = _dot_f32_by_01(dtt * a_col, triu2_ref[...])
        a_full = _dot_f32_by_01(a_cs, expand)
        dt_full = _dot_f32_by_01(dt, expand)
        a_last = a_full[CHUNK - 1:CHUNK, :]
        xdt = x_s * dt_full
        w_state = (xdt * jnp.exp(a_last - a_full)).astype(BF16)
        xdt_b = xdt.astype(BF16)
        zero_b = jnp.zeros((CHUNK, LANES), BF16)
        y_parts = []
        for g in range(n_groups):
            b_g = xc[:, width + g * n_state: width + (g + 1) * n_state]
            c_g = xc[:, width + (n_groups + g) * n_state: width + (n_groups + g + 1) * n_state]
            b_gb = b_g.astype(BF16)
            c_gb = c_g.astype(BF16)
            st = st_ref[g]
            if out_r0 is not None:
                cb2 = lax.dot_general(c_gb, jnp.concatenate([b_gb, b_gb], axis=0), nt,
                                      preferred_element_type=F32)
                y_off = jnp.dot(c_gb, st.astype(BF16), preferred_element_type=F32)
                y_g = y_off * jnp.exp(a_full[:, g * gw:(g + 1) * gw])
                diag = []
                for pp in range(n_pairs // n_groups):
                    pr = g * (n_pairs // n_groups) + pp
                    colb = a_full[:, pr * LANES:(pr + 1) * LANES]
                    rowb = jnp.where(left, a_cs_t2[2 * pr:2 * pr + 1, :], a_cs_t2[2 * pr + 1:2 * pr + 2, :])
                    dec = jnp.exp(jnp.where(causal2, colb - rowb, -jnp.inf))
                    m_pair = (cb2 * dec).astype(BF16)
                    xp = xdt_b[:, pr * LANES:(pr + 1) * LANES]
                    rhs = jnp.concatenate([jnp.where(left, xp, zero_b), jnp.where(left, zero_b, xp)], axis=0)
                    diag.append(jnp.dot(m_pair, rhs, preferred_element_type=F32))
                y_parts.append(y_g + jnp.concatenate(diag, axis=1))
            new_st = st * jnp.exp(a_last[:, g * gw:(g + 1) * gw]) + jnp.dot(
                b_g.T.astype(BF16), w_state[:, g * gw:(g + 1) * gw], preferred_element_type=F32)
            st_ref[g] = new_st
        if out_r0 is not None:
            y = jnp.concatenate(y_parts, axis=1) + x_s * d_full
            gated = y * _silu(z)
            out_ref[b, out_r0:out_r0 + CHUNK, :] = _rms(gated, nw_ref[...]).astype(out_ref.dtype)

    @pl.when(pl.program_id(0) == 0)
    def _():
        state_sc[...] = jnp.zeros_like(state_sc)
        xm_sc[0:HALO, :] = jnp.zeros((HALO, xm_sc.shape[1]), F32)
        xm_sc[HALO:HALO + CHUNK, :] = xbcm_ref[...].astype(F32)
        chunk(xm_sc[...], state_sc.at[0], dtm_ref[...], dttm_ref[...], CHUNK - N_META, None, None)
        for b in range(bsz):
            if b:
                state_sc[b] = state_sc[0]
            xs_sc[b, 0:HALO, :] = xm_sc[CHUNK:CHUNK + HALO, :]

    for b in range(bsz):
        xs_sc[b, HALO:HALO + t_blk, :] = zx_ref[b, :, width:].astype(F32)
    for c in range(t_blk // CHUNK):
        rows = slice(c * CHUNK, (c + 1) * CHUNK)
        for b in range(bsz):
            chunk(xs_sc[b, c * CHUNK:(c + 1) * CHUNK + HALO, :], state_sc.at[b], dt_ref[b, rows, :],
                  dtt_refs[b][:, rows], 0, zx_ref[b, rows, 0:width].astype(F32), c * CHUNK, b)
    for b in range(bsz):
        xs_sc[b, 0:HALO, :] = xs_sc[b, t_blk:t_blk + HALO, :]


def _ssd(zx, dt, dtt, xbc_m, dt_m, dtt_m, conv_w, conv_b, dt_bias, a_log, d_skip, norm_w, cast_weights,
         *, bsz, seq, t_blk):
    nb = seq // t_blk
    slab = lambda w: pl.BlockSpec((w.shape[0] // nb, w.shape[1]), lambda i: (i, 0))
    width = N_SSD_HEADS * SSD_HEAD_DIM
    cdim = zx.shape[1] - width
    nh = N_SSD_HEADS

    def pad_row(v):
        return jnp.pad(v.reshape(1, nh), ((0, 0), (0, LANES - nh)))

    hh = np.arange(LANES)[:, None]
    ll = np.arange(width)[None, :]
    one_hot = lambda m: jnp.asarray(m.astype(np.float32), dtype=BF16)
    expand = one_hot(ll // SSD_HEAD_DIM == hh)
    ii = np.arange(CHUNK)
    tril = one_hot(ii[None, :] <= ii[:, None])
    triu2 = one_hot(np.tile(ii[:, None] <= ii[None, :], (1, 2)))
    jj = np.arange(HALO + CHUNK)
    shift = one_hot(np.stack([jj[None, :] == ii[:, None] + HALO - (CONV_WIDTH - 1) + k
                              for k in range(CONV_WIDTH - 1)]))
    full = lambda shape: pl.BlockSpec(shape, lambda i: tuple(0 for _ in shape))
    out, *cast = pl.pallas_call(
        functools.partial(_ssd_kernel, t_blk=t_blk, bsz=bsz, n_cast=len(cast_weights)),
        grid=(nb,),
        in_specs=[pl.BlockSpec((nh, t_blk), functools.partial(lambda i, b: (0, b * nb + i), b=b))
                  for b in range(bsz)] + [slab(w) for w in cast_weights] + [
            pl.BlockSpec((bsz, t_blk, width + cdim), lambda i: (0, i, 0)),
            pl.BlockSpec((bsz, t_blk, LANES), lambda i: (0, i, 0)),
            full((CHUNK, cdim)), full((CHUNK, LANES)), full((nh, CHUNK)),
            full((CONV_WIDTH, cdim)), full((1, cdim)), full((1, LANES)), full((nh, 1)),
            full((1, LANES)), full((nh, 1)), full((1, width)), full((1, width)),
            full((LANES, width)), full((CHUNK, CHUNK)), full((CHUNK, 2 * CHUNK)), full(shift.shape),
        ],
        out_specs=[pl.BlockSpec((bsz, t_blk, width), lambda i: (0, i, 0))] + [slab(w) for w in cast_weights],
        out_shape=[jax.ShapeDtypeStruct((bsz, seq, width), BF16)]
        + [jax.ShapeDtypeStruct(w.shape, BF16) for w in cast_weights],
        scratch_shapes=[
            pltpu.VMEM((bsz, t_blk + HALO, cdim), F32),
            pltpu.VMEM((CHUNK + HALO, cdim), F32),
            pltpu.VMEM((bsz, N_SSD_GROUPS, D_STATE, width // N_SSD_GROUPS), F32),
        ],
        compiler_params=_cparams(("arbitrary",)),
        name="ssd",
    )(*([dtt] * bsz), *cast_weights, zx.reshape(bsz, seq, width + cdim), dt.reshape(bsz, seq, LANES),
      xbc_m, dt_m, dtt_m,
      conv_w, conv_b.reshape(1, cdim), pad_row(dt_bias), dt_bias.reshape(nh, 1), pad_row(a_log),
      a_log.reshape(nh, 1), jnp.repeat(d_skip, SSD_HEAD_DIM).reshape(1, width),
      norm_w.reshape(1, width), expand, tril, triu2, shift)
    return out.reshape(bsz * seq, width), cast


def _rope_tables(n_pos):
    inv = jnp.power(ROPE_THETA, -jnp.arange(0, ROT_DIM, 2, dtype=F32) / ROT_DIM)
    ang = jnp.arange(n_pos, dtype=F32)[:, None] * inv[None, :]
    cos, sin = jnp.cos(ang), jnp.sin(ang)
    half = ROT_DIM // 2
    r = np.arange(LANES) % DIFF_HEAD_DIM
    idx = jnp.asarray(r % half)
    lo = jnp.asarray(r < half)[None, :]
    hi = jnp.asarray((r >= half) & (r < ROT_DIM))[None, :]
    cos_t = jnp.where(lo | hi, cos[:, idx], 1.0)
    sin_lo = jnp.where(lo, -sin[:, idx], 0.0)
    sin_hi = jnp.where(hi, sin[:, idx], 0.0)
    return jnp.stack([cos_t, sin_lo, sin_hi])


def kernel(x, meta_tokens, ffn1_norm, ffn1_w_gate, ffn1_w_up, ffn1_w_down, mix_norm, w_in, q_norm, k_norm,
           lambda_q1, lambda_k1, lambda_q2, lambda_k2, attn_out_norm, conv_w, conv_b, dt_bias, a_log, d_skip,
           ssd_norm, w_out, ffn2_norm, ffn2_w_gate, ffn2_w_up, ffn2_w_down):
    bsz, seq, d = x.shape
    assert ffn1_norm.shape[0] == 1, "single-layer block"
    aw = N_DIFF_HEADS * V_HEAD_DIM
    sw = N_SSD_HEADS * SSD_HEAD_DIM
    cdim = sw + 2 * N_SSD_GROUPS * D_STATE
    lam_init = 0.8 - 0.6 * math.exp(-0.3 * 0)
    row = lambda v: v.reshape(1, -1)

    tn = 512
    w_in0 = w_in[0]
    w_qk = w_in0[:, :2 * aw].astype(BF16)
    w_v_t = w_in0[:, 2 * aw:3 * aw].T.astype(BF16)
    w_zx = w_in0[:, 3 * aw:3 * aw + sw + cdim].astype(BF16)
    w_dt = w_in0[:, 3 * aw + sw + cdim:].astype(BF16)
    w_dt_pad = jnp.pad(w_dt, ((0, 0), (0, LANES - N_SSD_HEADS)))
    w_dt_t = w_dt.T

    n_sub = aw // DIFF_HEAD_DIM
    q_gain = jnp.tile(q_norm[0], n_sub) * (DIFF_HEAD_DIM ** -0.5 * math.log2(math.e))
    k_gain = jnp.tile(k_norm[0], n_sub)
    qk_gain = jnp.concatenate([q_gain, k_gain]).reshape(2 * aw // tn, 1, tn)
    rope = _rope_tables(N_META + seq)
    gi = np.arange(2 * LANES) // DIFF_HEAD_DIM
    bd = jnp.asarray((gi[:, None] == gi[None, :]).astype(np.float32) / DIFF_HEAD_DIM, dtype=BF16)

    wg1, wu1, wd1 = ffn1_w_gate[0].astype(BF16), ffn1_w_up[0].astype(BF16), ffn1_w_down[0].astype(BF16)

    x2 = x.reshape(bsz * seq, d)
    h1, hn = _ffn(x2, row(ffn1_norm[0]), wg1, wu1, wd1, row(mix_norm[0]), tm=FFN_TM, tf=FFN_TF)
    _, hn_m = _ffn(meta_tokens.astype(F32), row(ffn1_norm[0]), wg1, wu1, wd1, row(mix_norm[0]),
                   tm=N_META, tf=FFN_TF)

    qk = _proj_qk(hn, w_qk, qk_gain, rope[:, N_META:], bd, tm=1024, tn=tn)
    qk_m = _proj_qk(hn_m, w_qk, qk_gain, rope[:, :N_META], bd, tm=N_META, tn=tn)
    v_t = _proj_t(w_v_t, hn, tn=tn, tm=1024, name="proj_vt")
    v_t_m = _proj_t(w_v_t, hn_m, tn=tn, tm=N_META, name="proj_vt_meta")
    zx = _proj(hn, w_zx, tm=1024, tn=tn, name="proj_zx")
    xbc_m = _proj(hn_m, w_zx, tm=N_META, tn=tn, name="proj_zx_meta")[:, sw:]
    dt, dtt = _proj_dt(hn, w_dt_pad, w_dt_t, tm=1024)
    dt_m, dtt_m = _proj_dt(hn_m, w_dt_pad, w_dt_t, tm=N_META)

    kmeta = jnp.pad(qk_m[:, aw:], ((0, LANES - N_META), (0, 0)))
    vmeta_t = jnp.pad(v_t_m, ((0, 0), (0, LANES - N_META)))
    lam_vecs = jnp.stack([lambda_q1[0], lambda_k1[0], lambda_q2[0], lambda_k2[0]]).astype(F32)
    attn = _attention(qk, v_t, kmeta, vmeta_t, lam_vecs, attn_out_norm[0].reshape(V_HEAD_DIM, 1),
                      bsz=bsz, seq=seq, tq=1024, dg=256, lam_init=lam_init)

    lead = CHUNK - N_META
    ssd, (wg2, wu2, wd2, w_o) = _ssd(
        zx, dt, dtt,
        jnp.pad(xbc_m, ((lead, 0), (0, 0))), jnp.pad(dt_m, ((lead, 0), (0, 0))), jnp.pad(dtt_m, ((0, 0), (lead, 0))),
        conv_w[0], conv_b[0], dt_bias[0], a_log[0], d_skip[0], ssd_norm[0],
        (ffn2_w_gate[0], ffn2_w_up[0], ffn2_w_down[0], w_out[0]),
        bsz=bsz, seq=seq, t_blk=256)

    h2 = _outproj(h1, attn, ssd, w_o, tm=512, tn=d)
    (out,) = _ffn(h2, row(ffn2_norm[0]), wg2, wu2, wd2, tm=FFN_TM, tf=FFN_TF)
    return out.reshape(bsz, seq, d)
```

```python
import functools
import math

import jax
import jax.numpy as jnp
import numpy as np
from jax import lax
from jax.experimental import pallas as pl
from jax.experimental.pallas import tpu as pltpu

F32 = jnp.float32
BF16 = jnp.bfloat16

EPS = 1e-6
CHUNK = 64
N_META = 16
N_DIFF_HEADS = 8
DIFF_HEAD_DIM = 64
V_HEAD_DIM = 128
ROT_DIM = 16
ROPE_THETA = 500000.0
SSD_HEAD_DIM = 64
N_SSD_HEADS = 16
N_SSD_GROUPS = 2
D_STATE = 128
CONV_WIDTH = 4
HALO = 64
LANES = 128
VMEM_LIMIT = 56 * 1024 * 1024
FFN_TM = 512
FFN_TF = 512


def _cparams(sem):
    return pltpu.CompilerParams(dimension_semantics=sem, vmem_limit_bytes=VMEM_LIMIT)


def _silu(v):
    half = 0.5 * v
    return half + half * jnp.tanh(half)


def _split_bf16(a, pieces):
    out = []
    for _ in range(pieces - 1):
        p = a.astype(BF16)
        out.append(p)
        a = a - p.astype(F32)
    out.append(a.astype(BF16))
    return out


def _dot_f32_by_01(a, m01, pieces=3):
    return sum(jnp.dot(p, m01, preferred_element_type=F32) for p in _split_bf16(a, pieces))


def _dot_01_by_f32(m01, a, pieces=3):
    return sum(jnp.dot(m01, p, preferred_element_type=F32) for p in _split_bf16(a, pieces))


def _rms(v, w):
    ms = jnp.mean(v * v, axis=-1, keepdims=True)
    return v * lax.rsqrt(ms + EPS) * w


def _swiglu_step(xn, wg_ref, wu_ref, wd_ref):
    g = jnp.dot(xn, wg_ref[...], preferred_element_type=F32)
    u = jnp.dot(xn, wu_ref[...], preferred_element_type=F32)
    a = (_silu(g) * u).astype(BF16)
    return jnp.dot(a, wd_ref[...], preferred_element_type=F32)


def _qk_norm_rope(acc, gain, rope_ref, bd):
    cos, sin_lo, sin_hi = rope_ref[0], rope_ref[1], rope_ref[2]
    half = ROT_DIM // 2
    wide = bd.shape[0]
    out = []
    for c in range(acc.shape[1] // wide):
        y = acc[:, c * wide:(c + 1) * wide]
        ms = sum(jnp.dot(p, bd, preferred_element_type=F32) for p in _split_bf16(y * y, 2))
        yn = y * lax.rsqrt(ms + EPS) * gain[:, c * wide:(c + 1) * wide]
        for s in range(wide // LANES):
            v = yn[:, s * LANES:(s + 1) * LANES]
            out.append(v * cos + pltpu.roll(v, half, 1) * sin_hi + pltpu.roll(v, LANES - half, 1) * sin_lo)
    return jnp.concatenate(out, axis=1)


def _ffn_kernel(x_ref, nw_ref, wg_ref, wu_ref, wd_ref, *rest, with_norm):
    if with_norm:
        pnw_ref, out_ref, hn_ref, xn_sc = rest
    else:
        out_ref, xn_sc = rest
    j = pl.program_id(1)

    @pl.when(j == 0)
    def _():
        xn_sc[...] = _rms(x_ref[...], nw_ref[...]).astype(BF16)
        out_ref[...] = jnp.zeros_like(out_ref)

    out_ref[...] += _swiglu_step(xn_sc[...], wg_ref, wu_ref, wd_ref)

    @pl.when(j == pl.num_programs(1) - 1)
    def _():
        h = x_ref[...] + 0.5 * out_ref[...]
        out_ref[...] = h
        if with_norm:
            hn_ref[...] = _rms(h, pnw_ref[...]).astype(BF16)


def _ffn(x2, nw, wg, wu, wd, post_nw=None, *, tm, tf):
    m, d = x2.shape
    f = wg.shape[1]
    with_norm = post_nw is not None
    row_spec = pl.BlockSpec((tm, d), lambda i, j: (i, 0))
    vec_spec = pl.BlockSpec((1, d), lambda i, j: (0, 0))
    in_specs = [row_spec, vec_spec,
                pl.BlockSpec((d, tf), lambda i, j: (0, j)),
                pl.BlockSpec((d, tf), lambda i, j: (0, j)),
                pl.BlockSpec((tf, d), lambda i, j: (j, 0))]
    args = [x2, nw, wg, wu, wd]
    out_specs, out_shape = [row_spec], [jax.ShapeDtypeStruct((m, d), F32)]
    if with_norm:
        in_specs.append(vec_spec)
        args.append(post_nw)
        out_specs.append(row_spec)
        out_shape.append(jax.ShapeDtypeStruct((m, d), BF16))
    return pl.pallas_call(
        functools.partial(_ffn_kernel, with_norm=with_norm),
        grid=(m // tm, f // tf),
        in_specs=in_specs,
        out_specs=out_specs,
        out_shape=out_shape,
        scratch_shapes=[pltpu.VMEM((tm, d), BF16)],
        compiler_params=_cparams(("parallel", "arbitrary")),
        name="ffn",
    )(*args)


def _ffn_meta_kernel(x_ref, nw_ref, wg_ref, wu_ref, wd_ref, win_ref, pnw_ref,
                     hn_ref, wgb_ref, wub_ref, wdb_ref, winb_ref, xn_sc, acc_sc):
    j = pl.program_id(0)

    @pl.when(j == 0)
    def _():
        xn_sc[...] = _rms(x_ref[...], nw_ref[...]).astype(BF16)
        acc_sc[...] = jnp.zeros_like(acc_sc)

    wgb_ref[...] = wg_ref[...].astype(BF16)
    wub_ref[...] = wu_ref[...].astype(BF16)
    wdb_ref[...] = wd_ref[...].astype(BF16)
    winb_ref[...] = win_ref[...].astype(BF16)
    acc_sc[...] += _swiglu_step(xn_sc[...], wgb_ref, wub_ref, wdb_ref)

    @pl.when(j == pl.num_programs(0) - 1)
    def _():
        hn_ref[...] = _rms(x_ref[...] + 0.5 * acc_sc[...], pnw_ref[...]).astype(BF16)


def _ffn_meta(xm, nw, wg, wu, wd, w_in, post_nw, *, tf):
    m, d = xm.shape
    n_ff = wg.shape[1] // tf
    vec_spec = pl.BlockSpec((1, d), lambda j: (0, 0))
    row_spec = pl.BlockSpec((m, d), lambda j: (0, 0))
    col_tile = pl.BlockSpec((d, tf), lambda j: (0, j))
    row_tile = pl.BlockSpec((tf, d), lambda j: (j, 0))
    return pl.pallas_call(
        _ffn_meta_kernel,
        grid=(n_ff,),
        in_specs=[row_spec, vec_spec, col_tile, col_tile, row_tile, col_tile, vec_spec],
        out_specs=[row_spec, col_tile, col_tile, row_tile, col_tile],
        out_shape=[jax.ShapeDtypeStruct((m, d), BF16), jax.ShapeDtypeStruct(wg.shape, BF16),
                   jax.ShapeDtypeStruct(wu.shape, BF16), jax.ShapeDtypeStruct(wd.shape, BF16),
                   jax.ShapeDtypeStruct((d, n_ff * tf), BF16)],
        scratch_shapes=[pltpu.VMEM((m, d), BF16), pltpu.VMEM((m, d), F32)],
        compiler_params=_cparams(("arbitrary",)),
        name="ffn_meta_cast",
    )(xm, nw, wg, wu, wd, w_in, post_nw)


def _proj_kernel(x_ref, w_ref, out_ref, *, transpose):
    if transpose:
        acc = lax.dot_general(w_ref[...], x_ref[...], (((1,), (1,)), ((), ())), preferred_element_type=F32)
    else:
        acc = jnp.dot(x_ref[...], w_ref[...], preferred_element_type=F32)
    out_ref[...] = acc.astype(out_ref.dtype)


def _proj(x2, w, *, tm, tn, col0, n, name, transpose=False):
    m, d = x2.shape
    c0 = col0 // tn
    if transpose:
        w_spec = pl.BlockSpec((tn, d), lambda i, j: (c0 + j, 0))
        out_spec, out_shape = pl.BlockSpec((tn, tm), lambda i, j: (j, i)), (n, m)
    else:
        w_spec = pl.BlockSpec((d, tn), lambda i, j: (0, c0 + j))
        out_spec, out_shape = pl.BlockSpec((tm, tn), lambda i, j: (i, j)), (m, n)
    return pl.pallas_call(
        functools.partial(_proj_kernel, transpose=transpose),
        grid=(m // tm, n // tn),
        in_specs=[pl.BlockSpec((tm, d), lambda i, j: (i, 0)), w_spec],
        out_specs=out_spec,
        out_shape=jax.ShapeDtypeStruct(out_shape, BF16),
        compiler_params=_cparams(("parallel", "arbitrary")),
        name=name,
    )(x2, w)


def _proj_qk_kernel(x_ref, w_ref, gain_ref, rope_ref, bd_ref, out_ref):
    acc = jnp.dot(x_ref[...], w_ref[...], preferred_element_type=F32)
    out_ref[...] = _qk_norm_rope(acc, gain_ref[0], rope_ref, bd_ref[...]).astype(out_ref.dtype)


def _proj_qk(x2, w, gain, rope, bd, *, tm, tn, n):
    m, d = x2.shape
    n_tab = rope.shape[1] // tm
    return pl.pallas_call(
        _proj_qk_kernel,
        grid=(m // tm, n // tn),
        in_specs=[
            pl.BlockSpec((tm, d), lambda i, j: (i, 0)),
            pl.BlockSpec((d, tn), lambda i, j: (0, j)),
            pl.BlockSpec((1, 1, tn), lambda i, j: (j, 0, 0)),
            pl.BlockSpec((3, tm, LANES), lambda i, j: (0, i % n_tab, 0)),
            pl.BlockSpec(bd.shape, lambda i, j: (0, 0)),
        ],
        out_specs=pl.BlockSpec((tm, tn), lambda i, j: (i, j)),
        out_shape=jax.ShapeDtypeStruct((m, n), BF16),
        compiler_params=_cparams(("parallel", "arbitrary")),
        name="proj_qk",
    )(x2, w, gain, rope, bd)


def _proj_dt_kernel(x_ref, w_ref, wt_ref, out_ref, outt_ref):
    x = x_ref[...]
    out_ref[...] = jnp.dot(x, w_ref[...], preferred_element_type=F32)
    outt_ref[...] = lax.dot_general(wt_ref[...], x, (((1,), (1,)), ((), ())), preferred_element_type=F32)


def _proj_dt(x2, w_pad, w_t, *, tm):
    m, d = x2.shape
    nh = w_t.shape[0]
    return pl.pallas_call(
        _proj_dt_kernel,
        grid=(m // tm,),
        in_specs=[
            pl.BlockSpec((tm, d), lambda i: (i, 0)),
            pl.BlockSpec((d, LANES), lambda i: (0, 0)),
            pl.BlockSpec((nh, d), lambda i: (0, 0)),
        ],
        out_specs=[pl.BlockSpec((tm, LANES), lambda i: (i, 0)), pl.BlockSpec((nh, tm), lambda i: (0, i))],
        out_shape=[jax.ShapeDtypeStruct((m, LANES), F32), jax.ShapeDtypeStruct((nh, m), F32)],
        compiler_params=_cparams(("parallel",)),
        name="proj_dt",
    )(x2, w_pad, w_t)


def _outproj_kernel(h_ref, a_ref, s_ref, wa_ref, ws_ref, out_ref):
    out_ref[...] = (h_ref[...]
                    + jnp.dot(a_ref[...], wa_ref[...], preferred_element_type=F32)
                    + jnp.dot(s_ref[...], ws_ref[...], preferred_element_type=F32))


def _outproj(h1, attn, ssd, w_o, *, tm, tn):
    m, d = h1.shape
    k = attn.shape[1]
    return pl.pallas_call(
        _outproj_kernel,
        grid=(m // tm, d // tn),
        in_specs=[
            pl.BlockSpec((tm, tn), lambda i, j: (i, j)),
            pl.BlockSpec((tm, k), lambda i, j: (i, 0)),
            pl.BlockSpec((tm, k), lambda i, j: (i, 0)),
            pl.BlockSpec((k, tn), lambda i, j: (0, j)),
            pl.BlockSpec((k, tn), lambda i, j: (1, j)),
        ],
        out_specs=pl.BlockSpec((tm, tn), lambda i, j: (i, j)),
        out_shape=jax.ShapeDtypeStruct((m, d), F32),
        compiler_params=_cparams(("parallel", "arbitrary")),
        name="outproj",
    )(h1, attn, ssd, w_o, w_o)


def _attn_kernel(qa_tab, ka_tab, qb_tab, kb_tab, q_ref, k_ref, vt_ref, km_ref, vmt_ref, lam_ref, gain_ref,
                 out_ref, qs_sc, sm_sc, s0_sc, s1_sc, mx0_sc, mx1_sc, m_sc, l_sc, acc_sc,
                 *, tq, dg, n_steps, lam_init):
    g = pl.program_id(2)
    qa = qa_tab[g]
    ka = ka_tab[g]
    qb = qb_tab[g]
    kb = kb_tab[g]
    has_a = g < n_steps
    has_b = g >= 1
    even = (g % 2) == 0
    nt = (((1,), (1,)), ((), ()))

    full_groups = [(slice(0, 2 * tq), tq)]
    diag_groups = [(slice(sub * tq + j * dg, sub * tq + (j + 1) * dg), (j + 1) * dg)
                   for sub in range(2) for j in range(tq // dg)]

    def stage_a(buf, diagonal):
        s_out, mx_out = buf
        for cols, nk in (diag_groups if diagonal else full_groups):
            s_t = lax.dot_general(k_ref[0:nk, :], qs_sc[cols, :], nt, preferred_element_type=F32)
            if diagonal:
                key = lax.broadcasted_iota(jnp.int32, s_t.shape, 0)
                qry = lax.broadcasted_iota(jnp.int32, s_t.shape, 1) + (nk - dg)
                s_t = jnp.where(key // CHUNK <= qry // CHUNK, s_t, -jnp.inf)
            s_out[0:nk, cols] = s_t
            mx_out[:, cols] = jnp.max(s_t, axis=0, keepdims=True)

    def stage_b(buf, last):
        s_in, mx_in = buf
        vt_aug = jnp.concatenate([vt_ref[...], jnp.ones((16, vt_ref.shape[1]), BF16)], axis=0)
        for cols, nk in (diag_groups if last else full_groups):
            m_prev = m_sc[:, cols]
            m_new = jnp.maximum(m_prev, mx_in[:, cols])
            alpha = jnp.exp2(m_prev - m_new)
            p = jnp.exp2(s_in[0:nk, cols] - m_new).astype(BF16)
            pv = jnp.dot(vt_aug[:, 0:nk], p, preferred_element_type=F32)
            l_sc[:, cols] = alpha * l_sc[:, cols] + pv[V_HEAD_DIM:V_HEAD_DIM + 1, :]
            acc_sc[:, cols] = alpha * acc_sc[:, cols] + pv[0:V_HEAD_DIM, :]
            m_sc[:, cols] = m_new
        if last:
            lq1, lk1, lq2, lk2 = lam_ref[0:1, :], lam_ref[1:2, :], lam_ref[2:3, :], lam_ref[3:4, :]
            lam = (jnp.exp(jnp.sum(lq1 * lk1, axis=-1, keepdims=True))
                   - jnp.exp(jnp.sum(lq2 * lk2, axis=-1, keepdims=True)) + lam_init)
            o = acc_sc[...] / l_sc[...]
            o = o[:, 0:tq] - lam * o[:, tq:2 * tq]
            ms = jnp.mean(o * o, axis=0, keepdims=True)
            o = o * lax.rsqrt(ms + EPS) * (gain_ref[...] * (1.0 - lam_init))
            out_ref[...] = o.T.astype(out_ref.dtype)

    @pl.when(has_b & (kb == 0))
    def _():
        s_t = sm_sc[...]
        m0 = jnp.max(s_t, axis=0, keepdims=True)
        p = jnp.exp2(s_t - m0)
        m_sc[...] = m0
        l_sc[...] = jnp.sum(p, axis=0, keepdims=True)
        acc_sc[...] = jnp.dot(vmt_ref[...], p.astype(BF16), preferred_element_type=F32)

    @pl.when(has_a & (ka == 0))
    def _():
        q = q_ref[...]
        lane = lax.broadcasted_iota(jnp.int32, q.shape, 1)
        zero = jnp.zeros_like(q)
        qs_sc[0:tq, :] = jnp.where(lane < DIFF_HEAD_DIM, q, zero)
        qs_sc[tq:2 * tq, :] = jnp.where(lane >= DIFF_HEAD_DIM, q, zero)
        s_t = lax.dot_general(km_ref[...], qs_sc[...], nt, preferred_element_type=F32)
        key = lax.broadcasted_iota(jnp.int32, s_t.shape, 0)
        sm_sc[...] = jnp.where(key < N_META, s_t, -jnp.inf)

    buf0, buf1 = (s0_sc, mx0_sc), (s1_sc, mx1_sc)
    a_diag = ka == qa
    b_last = kb == qb
    both = has_a & has_b
    for parity, (buf_a, buf_b) in enumerate(((buf0, buf1), (buf1, buf0))):
        par = even if parity == 0 else jnp.logical_not(even)

        @pl.when(par & both & jnp.logical_not(a_diag) & jnp.logical_not(b_last))
        def _(buf_a=buf_a, buf_b=buf_b):
            stage_a(buf_a, False)
            stage_b(buf_b, False)

        @pl.when(par & both & a_diag & jnp.logical_not(b_last))
        def _(buf_a=buf_a, buf_b=buf_b):
            stage_a(buf_a, True)
            stage_b(buf_b, False)

        @pl.when(par & both & b_last)
        def _(buf_a=buf_a, buf_b=buf_b):
            stage_a(buf_a, False)
            stage_b(buf_b, True)

        @pl.when(par & has_a & jnp.logical_not(has_b))
        def _(buf_a=buf_a):
            stage_a(buf_a, True)

        @pl.when(par & has_b & jnp.logical_not(has_a))
        def _(buf_b=buf_b):
            stage_b(buf_b, True)


def _attention(qk, vt, kmeta, vmeta_t, lam_vecs, gain_col, *, bsz, seq, tq, dg, lam_init):
    nq = seq // tq
    h = N_DIFF_HEADS
    qi_list, ki_list = [], []
    for a in range(nq):
        for b in range(a + 1):
            qi_list.append(a)
            ki_list.append(b)
    n_steps = len(qi_list)
    ia = np.minimum(np.arange(n_steps + 1), n_steps - 1)
    ib = np.maximum(np.arange(n_steps + 1) - 1, 0)
    qi_arr, ki_arr = np.array(qi_list, np.int32), np.array(ki_list, np.int32)
    tabs = [jnp.asarray(t) for t in (qi_arr[ia], ki_arr[ia], qi_arr[ib], ki_arr[ib])]
    grid_spec = pltpu.PrefetchScalarGridSpec(
        num_scalar_prefetch=4,
        grid=(bsz, h, n_steps + 1),
        in_specs=[
            pl.BlockSpec((tq, V_HEAD_DIM), lambda b, hh, g, qa, ka, qb, kb: (b * nq + qa[g], hh)),
            pl.BlockSpec((tq, V_HEAD_DIM), lambda b, hh, g, qa, ka, qb, kb: (b * nq + ka[g], h + hh)),
            pl.BlockSpec((V_HEAD_DIM, tq), lambda b, hh, g, qa, ka, qb, kb: (hh, b * nq + kb[g])),
            pl.BlockSpec((LANES, V_HEAD_DIM), lambda b, hh, g, qa, ka, qb, kb: (0, hh)),
            pl.BlockSpec((V_HEAD_DIM, LANES), lambda b, hh, g, qa, ka, qb, kb: (hh, 0)),
            pl.BlockSpec((4, DIFF_HEAD_DIM), lambda b, hh, g, qa, ka, qb, kb: (0, 0)),
            pl.BlockSpec((V_HEAD_DIM, 1), lambda b, hh, g, qa, ka, qb, kb: (0, 0)),
        ],
        out_specs=pl.BlockSpec((tq, V_HEAD_DIM), lambda b, hh, g, qa, ka, qb, kb: (b * nq + qb[g], hh)),
        scratch_shapes=[
            pltpu.VMEM((2 * tq, V_HEAD_DIM), BF16),
            pltpu.VMEM((LANES, 2 * tq), F32),
            pltpu.VMEM((tq, 2 * tq), F32),
            pltpu.VMEM((tq, 2 * tq), F32),
            pltpu.VMEM((1, 2 * tq), F32),
            pltpu.VMEM((1, 2 * tq), F32),
            pltpu.VMEM((1, 2 * tq), F32),
            pltpu.VMEM((1, 2 * tq), F32),
            pltpu.VMEM((V_HEAD_DIM, 2 * tq), F32),
        ],
    )
    return pl.pallas_call(
        functools.partial(_attn_kernel, tq=tq, dg=dg, n_steps=n_steps, lam_init=lam_init),
        grid_spec=grid_spec,
        out_shape=jax.ShapeDtypeStruct((bsz * seq, h * V_HEAD_DIM), BF16),
        compiler_params=_cparams(("parallel", "parallel", "arbitrary")),
        name="attn",
    )(*tabs, qk, qk, vt, kmeta, vmeta_t, lam_vecs, gain_col)


def _ssd_kernel(*refs, t_blk, bsz, n_cast):
    dtt_refs = refs[:bsz]
    cast_in = refs[bsz:bsz + n_cast]
    (zx_ref, dt_ref, xbcm_ref, dtm_ref, dttm_ref,
     convw_ref, convb_ref, dtb_ref, dtbt_ref, alog_ref, alogt_ref, dskip_ref, nw_ref,
     expand_ref, tril_ref, triu2_ref, shift_ref, out_ref) = refs[bsz + n_cast:bsz + n_cast + 18]
    cast_out = refs[bsz + n_cast + 18:bsz + 2 * n_cast + 18]
    xs_sc, xm_sc, state_sc = refs[bsz + 2 * n_cast + 18:]
    for src, dst in zip(cast_in, cast_out):
        dst[...] = src[...].astype(BF16)
    n_heads, p_dim, n_state, n_groups = N_SSD_HEADS, SSD_HEAD_DIM, D_STATE, N_SSD_GROUPS
    width = n_heads * p_dim
    gw = width // n_groups
    n_pairs = n_heads // 2
    nt = (((1,), (1,)), ((), ()))
    expand = expand_ref[...]
    a_row = -jnp.exp(alog_ref[...])
    a_col = -jnp.exp(alogt_ref[...])
    d_full = dskip_ref[...]
    lane = lax.broadcasted_iota(jnp.int32, (CHUNK, LANES), 1)
    row = lax.broadcasted_iota(jnp.int32, (CHUNK, LANES), 0)
    left = lane < p_dim
    causal2 = (lane & (CHUNK - 1)) <= row

    def chunk(win_f32, st_ref, dt_raw, dtt_raw, pad_rows, z, out_r0, b=0):
        win = win_f32.astype(BF16)
        conv = convb_ref[...] + win[HALO:, :].astype(F32) * convw_ref[CONV_WIDTH - 1:CONV_WIDTH, :]
        for k in range(CONV_WIDTH - 1):
            conv = conv + jnp.dot(shift_ref[k], win, preferred_element_type=F32) * convw_ref[k:k + 1, :]
        xc = _silu(conv)
        x_s = xc[:, :width]
        dt = jax.nn.softplus(dt_raw + dtb_ref[...])
        dtt = jax.nn.softplus(dtt_raw + dtbt_ref[...])
        if pad_rows:
            dt = jnp.where(row >= pad_rows, dt, 0.0)
            dtt = jnp.where(lax.broadcasted_iota(jnp.int32, dtt.shape, 1) >= pad_rows, dtt, 0.0)
        a_cs = _dot_01_by_f32(tril_ref[...], dt * a_row)
        a_cs_t2 = _dot_f32_by_01(dtt * a_col, triu2_ref[...])
        a_full = _dot_f32_by_01(a_cs, expand)
        dt_full = _dot_f32_by_01(dt, expand)
        a_last = a_full[CHUNK - 1:CHUNK, :]
        xdt = x_s * dt_full
        w_state = (xdt * jnp.exp(a_last - a_full)).astype(BF16)
        xdt_b = xdt.astype(BF16)
        zero_b = jnp.zeros((CHUNK, LANES), BF16)
        y_parts = []
        for g in range(n_groups):
            b_g = xc[:, width + g * n_state: width + (g + 1) * n_state]
            c_g = xc[:, width + (n_groups + g) * n_state: width + (n_groups + g + 1) * n_state]
            b_gb = b_g.astype(BF16)
            c_gb = c_g.astype(BF16)
            st = st_ref[g]
            if out_r0 is not None:
                cb2 = lax.dot_general(c_gb, jnp.concatenate([b_gb, b_gb], axis=0), nt,
                                      preferred_element_type=F32)
                y_off = jnp.dot(c_gb, st.astype(BF16), preferred_element_type=F32)
                y_g = y_off * jnp.exp(a_full[:, g * gw:(g + 1) * gw])
                diag = []
                for pp in range(n_pairs // n_groups):
                    pr = g * (n_pairs // n_groups) + pp
                    colb = a_full[:, pr * LANES:(pr + 1) * LANES]
                    rowb = jnp.where(left, a_cs_t2[2 * pr:2 * pr + 1, :], a_cs_t2[2 * pr + 1:2 * pr + 2, :])
                    dec = jnp.exp(jnp.where(causal2, colb - rowb, -jnp.inf))
                    m_pair = (cb2 * dec).astype(BF16)
                    xp = xdt_b[:, pr * LANES:(pr + 1) * LANES]
                    rhs = jnp.concatenate([jnp.where(left, xp, zero_b), jnp.where(left, zero_b, xp)], axis=0)
                    diag.append(jnp.dot(m_pair, rhs, preferred_element_type=F32))
                y_parts.append(y_g + jnp.concatenate(diag, axis=1))
            new_st = st * jnp.exp(a_last[:, g * gw:(g + 1) * gw]) + jnp.dot(
                b_g.T.astype(BF16), w_state[:, g * gw:(g + 1) * gw], preferred_element_type=F32)
            st_ref[g] = new_st
        if out_r0 is not None:
            y = jnp.concatenate(y_parts, axis=1) + x_s * d_full
            gated = y * _silu(z)
            out_ref[b, out_r0:out_r0 + CHUNK, :] = _rms(gated, nw_ref[...]).astype(out_ref.dtype)

    @pl.when(pl.program_id(0) == 0)
    def _():
        state_sc[...] = jnp.zeros_like(state_sc)
        xm_sc[0:HALO, :] = jnp.zeros((HALO, xm_sc.shape[1]), F32)
        xm_sc[HALO:HALO + CHUNK, :] = xbcm_ref[...].astype(F32)
        chunk(xm_sc[...], state_sc.at[0], dtm_ref[...], dttm_ref[...], CHUNK - N_META, None, None)
        for b in range(bsz):
            if b:
                state_sc[b] = state_sc[0]
            xs_sc[b, 0:HALO, :] = xm_sc[CHUNK:CHUNK + HALO, :]

    for b in range(bsz):
        xs_sc[b, HALO:HALO + t_blk, :] = zx_ref[b, :, width:].astype(F32)
    for c in range(t_blk // CHUNK):
        rows = slice(c * CHUNK, (c + 1) * CHUNK)
        for b in range(bsz):
            chunk(xs_sc[b, c * CHUNK:(c + 1) * CHUNK + HALO, :], state_sc.at[b], dt_ref[b, rows, :],
                  dtt_refs[b][:, rows], 0, zx_ref[b, rows, 0:width].astype(F32), c * CHUNK, b)
    for b in range(bsz):
        xs_sc[b, 0:HALO, :] = xs_sc[b, t_blk:t_blk + HALO, :]


def _ssd(zx, dt, dtt, xbc_m, dt_m, dtt_m, conv_w, conv_b, dt_bias, a_log, d_skip, norm_w, cast_weights,
         *, bsz, seq, t_blk):
    nb = seq // t_blk
    slab = lambda w: pl.BlockSpec((w.shape[0] // nb, w.shape[1]), lambda i: (i, 0))
    width = N_SSD_HEADS * SSD_HEAD_DIM
    cdim = zx.shape[1] - width
    nh = N_SSD_HEADS

    def pad_row(v):
        return jnp.pad(v.reshape(1, nh), ((0, 0), (0, LANES - nh)))

    hh = np.arange(LANES)[:, None]
    ll = np.arange(width)[None, :]
    one_hot = lambda m: jnp.asarray(m.astype(np.float32), dtype=BF16)
    expand = one_hot(ll // SSD_HEAD_DIM == hh)
    ii = np.arange(CHUNK)
    tril = one_hot(ii[None, :] <= ii[:, None])
    triu2 = one_hot(np.tile(ii[:, None] <= ii[None, :], (1, 2)))
    jj = np.arange(HALO + CHUNK)
    shift = one_hot(np.stack([jj[None, :] == ii[:, None] + HALO - (CONV_WIDTH - 1) + k
                              for k in range(CONV_WIDTH - 1)]))
    full = lambda shape: pl.BlockSpec(shape, lambda i: tuple(0 for _ in shape))
    out, *cast = pl.pallas_call(
        functools.partial(_ssd_kernel, t_blk=t_blk, bsz=bsz, n_cast=len(cast_weights)),
        grid=(nb,),
        in_specs=[pl.BlockSpec((nh, t_blk), functools.partial(lambda i, b: (0, b * nb + i), b=b))
                  for b in range(bsz)] + [slab(w) for w in cast_weights] + [
            pl.BlockSpec((bsz, t_blk, width + cdim), lambda i: (0, i, 0)),
            pl.BlockSpec((bsz, t_blk, LANES), lambda i: (0, i, 0)),
            full((CHUNK, cdim)), full((CHUNK, LANES)), full((nh, CHUNK)),
            full((CONV_WIDTH, cdim)), full((1, cdim)), full((1, LANES)), full((nh, 1)),
            full((1, LANES)), full((nh, 1)), full((1, width)), full((1, width)),
            full((LANES, width)), full((CHUNK, CHUNK)), full((CHUNK, 2 * CHUNK)), full(shift.shape),
        ],
        out_specs=[pl.BlockSpec((bsz, t_blk, width), lambda i: (0, i, 0))] + [slab(w) for w in cast_weights],
        out_shape=[jax.ShapeDtypeStruct((bsz, seq, width), BF16)]
        + [jax.ShapeDtypeStruct(w.shape, BF16) for w in cast_weights],
        scratch_shapes=[
            pltpu.VMEM((bsz, t_blk + HALO, cdim), F32),
            pltpu.VMEM((CHUNK + HALO, cdim), F32),
            pltpu.VMEM((bsz, N_SSD_GROUPS, D_STATE, width // N_SSD_GROUPS), F32),
        ],
        compiler_params=_cparams(("arbitrary",)),
        name="ssd",
    )(*([dtt] * bsz), *cast_weights, zx.reshape(bsz, seq, width + cdim), dt.reshape(bsz, seq, LANES),
      xbc_m, dt_m, dtt_m,
      conv_w, conv_b.reshape(1, cdim), pad_row(dt_bias), dt_bias.reshape(nh, 1), pad_row(a_log),
      a_log.reshape(nh, 1), jnp.repeat(d_skip, SSD_HEAD_DIM).reshape(1, width),
      norm_w.reshape(1, width), expand, tril, triu2, shift)
    return out.reshape(bsz * seq, width), cast


def _rope_tables(n_pos):
    inv = jnp.power(ROPE_THETA, -jnp.arange(0, ROT_DIM, 2, dtype=F32) / ROT_DIM)
    ang = jnp.arange(n_pos, dtype=F32)[:, None] * inv[None, :]
    cos, sin = jnp.cos(ang), jnp.sin(ang)
    half = ROT_DIM // 2
    r = np.arange(LANES) % DIFF_HEAD_DIM
    idx = jnp.asarray(r % half)
    lo = jnp.asarray(r < half)[None, :]
    hi = jnp.asarray((r >= half) & (r < ROT_DIM))[None, :]
    cos_t = jnp.where(lo | hi, cos[:, idx], 1.0)
    sin_lo = jnp.where(lo, -sin[:, idx], 0.0)
    sin_hi = jnp.where(hi, sin[:, idx], 0.0)
    return jnp.stack([cos_t, sin_lo, sin_hi])


def kernel(x, meta_tokens, ffn1_norm, ffn1_w_gate, ffn1_w_up, ffn1_w_down, mix_norm, w_in, q_norm, k_norm,
           lambda_q1, lambda_k1, lambda_q2, lambda_k2, attn_out_norm, conv_w, conv_b, dt_bias, a_log, d_skip,
           ssd_norm, w_out, ffn2_norm, ffn2_w_gate, ffn2_w_up, ffn2_w_down):
    bsz, seq, d = x.shape
    assert ffn1_norm.shape[0] == 1, "single-layer block"
    aw = N_DIFF_HEADS * V_HEAD_DIM
    sw = N_SSD_HEADS * SSD_HEAD_DIM
    cdim = sw + 2 * N_SSD_GROUPS * D_STATE
    lam_init = 0.8 - 0.6 * math.exp(-0.3 * 0)
    row = lambda v: v.reshape(1, -1)

    tn = 512
    w_in0 = w_in[0]
    w_dt = w_in0[:, 3 * aw + sw + cdim:].astype(BF16)
    w_dt_pad = jnp.pad(w_dt, ((0, 0), (0, LANES - N_SSD_HEADS)))
    w_dt_t = w_dt.T

    n_sub = aw // DIFF_HEAD_DIM
    q_gain = jnp.tile(q_norm[0], n_sub) * (DIFF_HEAD_DIM ** -0.5 * math.log2(math.e))
    k_gain = jnp.tile(k_norm[0], n_sub)
    qk_gain = jnp.concatenate([q_gain, k_gain]).reshape(2 * aw // tn, 1, tn)
    rope = _rope_tables(N_META + seq)
    gi = np.arange(2 * LANES) // DIFF_HEAD_DIM
    bd = jnp.asarray((gi[:, None] == gi[None, :]).astype(np.float32) / DIFF_HEAD_DIM, dtype=BF16)

    assert FFN_TF == tn and (3 * aw + sw + cdim) % tn == 0
    hn_m, wg1, wu1, wd1, w_in_b = _ffn_meta(meta_tokens.astype(F32), row(ffn1_norm[0]), ffn1_w_gate[0],
                                            ffn1_w_up[0], ffn1_w_down[0], w_in0, row(mix_norm[0]), tf=FFN_TF)
    x2 = x.reshape(bsz * seq, d)
    h1, hn = _ffn(x2, row(ffn1_norm[0]), wg1, wu1, wd1, row(mix_norm[0]), tm=FFN_TM, tf=FFN_TF)

    qk = _proj_qk(hn, w_in_b, qk_gain, rope[:, N_META:], bd, tm=1024, tn=tn, n=2 * aw)
    qk_m = _proj_qk(hn_m, w_in_b, qk_gain, rope[:, :N_META], bd, tm=N_META, tn=tn, n=2 * aw)
    v_t = _proj(hn, w_in_b[:, 2 * aw:3 * aw].T, tm=1024, tn=tn, col0=0, n=aw, name="proj_vt", transpose=True)
    vzx_m = _proj(hn_m, w_in_b, tm=N_META, tn=tn, col0=2 * aw, n=aw + sw + cdim, name="proj_vzx_meta")
    zx = _proj(hn, w_in_b, tm=1024, tn=tn, col0=3 * aw, n=sw + cdim, name="proj_zx")
    v_t_m, xbc_m = vzx_m[:, :aw].T, vzx_m[:, aw + sw:]
    dt, dtt = _proj_dt(hn, w_dt_pad, w_dt_t, tm=1024)
    dt_m, dtt_m = _proj_dt(hn_m, w_dt_pad, w_dt_t, tm=N_META)

    kmeta = jnp.pad(qk_m[:, aw:], ((0, LANES - N_META), (0, 0)))
    vmeta_t = jnp.pad(v_t_m, ((0, 0), (0, LANES - N_META)))
    lam_vecs = jnp.stack([lambda_q1[0], lambda_k1[0], lambda_q2[0], lambda_k2[0]]).astype(F32)
    attn = _attention(qk, v_t, kmeta, vmeta_t, lam_vecs, attn_out_norm[0].reshape(V_HEAD_DIM, 1),
                      bsz=bsz, seq=seq, tq=1024, dg=256, lam_init=lam_init)

    lead = CHUNK - N_META
    ssd, (wg2, wu2, wd2, w_o) = _ssd(
        zx, dt, dtt,
        jnp.pad(xbc_m, ((lead, 0), (0, 0))), jnp.pad(dt_m, ((lead, 0), (0, 0))), jnp.pad(dtt_m, ((0, 0), (lead, 0))),
        conv_w[0], conv_b[0], dt_bias[0], a_log[0], d_skip[0], ssd_norm[0],
        (ffn2_w_gate[0], ffn2_w_up[0], ffn2_w_down[0], w_out[0]),
        bsz=bsz, seq=seq, t_blk=256)

    h2 = _outproj(h1, attn, ssd, w_o, tm=512, tn=d)
    (out,) = _ffn(h2, row(ffn2_norm[0]), wg2, wu2, wd2, tm=FFN_TM, tf=FFN_TF)
    return out.reshape(bsz, seq, d)
```

```python
import functools
import math

import jax
import jax.numpy as jnp
import numpy as np
from jax import lax
from jax.experimental import pallas as pl
from jax.experimental.pallas import tpu as pltpu

F32 = jnp.float32
BF16 = jnp.bfloat16

EPS = 1e-6
CHUNK = 64
N_META = 16
N_DIFF_HEADS = 8
DIFF_HEAD_DIM = 64
V_HEAD_DIM = 128
ROT_DIM = 16
ROPE_THETA = 500000.0
SSD_HEAD_DIM = 64
N_SSD_HEADS = 16
N_SSD_GROUPS = 2
D_STATE = 128
CONV_WIDTH = 4
HALO = 64
LANES = 128
VMEM_LIMIT = 56 * 1024 * 1024
FFN_TM = 512
FFN_TF = 512


def _cparams(sem):
    return pltpu.CompilerParams(dimension_semantics=sem, vmem_limit_bytes=VMEM_LIMIT)


def _silu(v):
    half = 0.5 * v
    return half + half * jnp.tanh(half)


def _split_bf16(a, pieces):
    out = []
    for _ in range(pieces - 1):
        p = a.astype(BF16)
        out.append(p)
        a = a - p.astype(F32)
    out.append(a.astype(BF16))
    return out


def _dot_f32_by_01(a, m01, pieces=3):
    return sum(jnp.dot(p, m01, preferred_element_type=F32) for p in _split_bf16(a, pieces))


def _dot_01_by_f32(m01, a, pieces=3):
    return sum(jnp.dot(m01, p, preferred_element_type=F32) for p in _split_bf16(a, pieces))


def _rms(v, w):
    ms = jnp.mean(v * v, axis=-1, keepdims=True)
    return v * lax.rsqrt(ms + EPS) * w


def _swiglu_step(xn, wg_ref, wu_ref, wd_ref):
    g = jnp.dot(xn, wg_ref[...], preferred_element_type=F32)
    u = jnp.dot(xn, wu_ref[...], preferred_element_type=F32)
    a = (_silu(g) * u).astype(BF16)
    return jnp.dot(a, wd_ref[...], preferred_element_type=F32)


def _qk_norm_rope(acc, gain, rope_ref, bd):
    cos, sin_lo, sin_hi = rope_ref[0], rope_ref[1], rope_ref[2]
    half = ROT_DIM // 2
    wide = bd.shape[0]
    out = []
    for c in range(acc.shape[1] // wide):
        y = acc[:, c * wide:(c + 1) * wide]
        ms = sum(jnp.dot(p, bd, preferred_element_type=F32) for p in _split_bf16(y * y, 2))
        yn = y * lax.rsqrt(ms + EPS) * gain[:, c * wide:(c + 1) * wide]
        for s in range(wide // LANES):
            v = yn[:, s * LANES:(s + 1) * LANES]
            out.append(v * cos + pltpu.roll(v, half, 1) * sin_hi + pltpu.roll(v, LANES - half, 1) * sin_lo)
    return jnp.concatenate(out, axis=1)


def _ffn_kernel(x_ref, nw_ref, wg_ref, wu_ref, wd_ref, *rest, with_norm):
    if with_norm:
        pnw_ref, out_ref, hn_ref, xn_sc = rest
    else:
        out_ref, xn_sc = rest
    j = pl.program_id(1)

    @pl.when(j == 0)
    def _():
        xn_sc[...] = _rms(x_ref[...], nw_ref[...]).astype(BF16)
        out_ref[...] = jnp.zeros_like(out_ref)

    out_ref[...] += _swiglu_step(xn_sc[...], wg_ref, wu_ref, wd_ref)

    @pl.when(j == pl.num_programs(1) - 1)
    def _():
        h = x_ref[...] + 0.5 * out_ref[...]
        out_ref[...] = h
        if with_norm:
            hn_ref[...] = _rms(h, pnw_ref[...]).astype(BF16)


def _ffn(x2, nw, wg, wu, wd, post_nw=None, *, tm, tf):
    m, d = x2.shape
    f = wg.shape[1]
    with_norm = post_nw is not None
    row_spec = pl.BlockSpec((tm, d), lambda i, j: (i, 0))
    vec_spec = pl.BlockSpec((1, d), lambda i, j: (0, 0))
    in_specs = [row_spec, vec_spec,
                pl.BlockSpec((d, tf), lambda i, j: (0, j)),
                pl.BlockSpec((d, tf), lambda i, j: (0, j)),
                pl.BlockSpec((tf, d), lambda i, j: (j, 0))]
    args = [x2, nw, wg, wu, wd]
    out_specs, out_shape = [row_spec], [jax.ShapeDtypeStruct((m, d), F32)]
    if with_norm:
        in_specs.append(vec_spec)
        args.append(post_nw)
        out_specs.append(row_spec)
        out_shape.append(jax.ShapeDtypeStruct((m, d), BF16))
    return pl.pallas_call(
        functools.partial(_ffn_kernel, with_norm=with_norm),
        grid=(m // tm, f // tf),
        in_specs=in_specs,
        out_specs=out_specs,
        out_shape=out_shape,
        scratch_shapes=[pltpu.VMEM((tm, d), BF16)],
        compiler_params=_cparams(("parallel", "arbitrary")),
        name="ffn",
    )(*args)


def _ffn_meta_kernel(x_ref, nw_ref, wg_ref, wu_ref, wd_ref, win_ref, pnw_ref,
                     hn_ref, wgb_ref, wub_ref, wdb_ref, winb_ref, xn_sc, acc_sc):
    j = pl.program_id(0)

    @pl.when(j == 0)
    def _():
        xn_sc[...] = _rms(x_ref[...], nw_ref[...]).astype(BF16)
        acc_sc[...] = jnp.zeros_like(acc_sc)

    wgb_ref[...] = wg_ref[...].astype(BF16)
    wub_ref[...] = wu_ref[...].astype(BF16)
    wdb_ref[...] = wd_ref[...].astype(BF16)
    winb_ref[...] = win_ref[...].astype(BF16)
    acc_sc[...] += _swiglu_step(xn_sc[...], wgb_ref, wub_ref, wdb_ref)

    @pl.when(j == pl.num_programs(0) - 1)
    def _():
        hn_ref[...] = _rms(x_ref[...] + 0.5 * acc_sc[...], pnw_ref[...]).astype(BF16)


def _ffn_meta(xm, nw, wg, wu, wd, w_in, post_nw, *, tf):
    m, d = xm.shape
    n_ff = wg.shape[1] // tf
    vec_spec = pl.BlockSpec((1, d), lambda j: (0, 0))
    row_spec = pl.BlockSpec((m, d), lambda j: (0, 0))
    col_tile = pl.BlockSpec((d, tf), lambda j: (0, j))
    row_tile = pl.BlockSpec((tf, d), lambda j: (j, 0))
    return pl.pallas_call(
        _ffn_meta_kernel,
        grid=(n_ff,),
        in_specs=[row_spec, vec_spec, col_tile, col_tile, row_tile,
                  pl.BlockSpec((None, d, tf), lambda j: (0, 0, j)), vec_spec],
        out_specs=[row_spec, col_tile, col_tile, row_tile, col_tile],
        out_shape=[jax.ShapeDtypeStruct((m, d), BF16), jax.ShapeDtypeStruct(wg.shape, BF16),
                   jax.ShapeDtypeStruct(wu.shape, BF16), jax.ShapeDtypeStruct(wd.shape, BF16),
                   jax.ShapeDtypeStruct((d, n_ff * tf), BF16)],
        scratch_shapes=[pltpu.VMEM((m, d), BF16), pltpu.VMEM((m, d), F32)],
        compiler_params=_cparams(("arbitrary",)),
        name="ffn_meta_cast",
    )(xm, nw, wg, wu, wd, w_in, post_nw)


def _proj_kernel(x_ref, w_ref, out_ref, *, transpose):
    if transpose:
        acc = lax.dot_general(w_ref[...], x_ref[...], (((1,), (1,)), ((), ())), preferred_element_type=F32)
    else:
        acc = jnp.dot(x_ref[...], w_ref[...], preferred_element_type=F32)
    out_ref[...] = acc.astype(out_ref.dtype)


def _proj(x2, w, *, tm, tn, col0, n, name, transpose=False):
    m, d = x2.shape
    c0 = col0 // tn
    if transpose:
        w_spec = pl.BlockSpec((tn, d), lambda i, j: (c0 + j, 0))
        out_spec, out_shape = pl.BlockSpec((tn, tm), lambda i, j: (j, i)), (n, m)
    else:
        w_spec = pl.BlockSpec((d, tn), lambda i, j: (0, c0 + j))
        out_spec, out_shape = pl.BlockSpec((tm, tn), lambda i, j: (i, j)), (m, n)
    return pl.pallas_call(
        functools.partial(_proj_kernel, transpose=transpose),
        grid=(m // tm, n // tn),
        in_specs=[pl.BlockSpec((tm, d), lambda i, j: (i, 0)), w_spec],
        out_specs=out_spec,
        out_shape=jax.ShapeDtypeStruct(out_shape, BF16),
        compiler_params=_cparams(("parallel", "arbitrary")),
        name=name,
    )(x2, w)


def _proj_qk_kernel(x_ref, w_ref, gain_ref, rope_ref, bd_ref, out_ref):
    acc = jnp.dot(x_ref[...], w_ref[...], preferred_element_type=F32)
    out_ref[...] = _qk_norm_rope(acc, gain_ref[0], rope_ref, bd_ref[...]).astype(out_ref.dtype)


def _proj_qk(x2, w, gain, rope, bd, *, tm, tn, n):
    m, d = x2.shape
    n_tab = rope.shape[1] // tm
    return pl.pallas_call(
        _proj_qk_kernel,
        grid=(m // tm, n // tn),
        in_specs=[
            pl.BlockSpec((tm, d), lambda i, j: (i, 0)),
            pl.BlockSpec((d, tn), lambda i, j: (0, j)),
            pl.BlockSpec((1, 1, tn), lambda i, j: (j, 0, 0)),
            pl.BlockSpec((3, tm, LANES), lambda i, j: (0, i % n_tab, 0)),
            pl.BlockSpec(bd.shape, lambda i, j: (0, 0)),
        ],
        out_specs=pl.BlockSpec((tm, tn), lambda i, j: (i, j)),
        out_shape=jax.ShapeDtypeStruct((m, n), BF16),
        compiler_params=_cparams(("parallel", "arbitrary")),
        name="proj_qk",
    )(x2, w, gain, rope, bd)


def _proj_dt_kernel(x_ref, w_ref, wt_ref, out_ref, outt_ref):
    x = x_ref[...]
    out_ref[...] = jnp.dot(x, w_ref[...], preferred_element_type=F32)
    outt_ref[...] = lax.dot_general(wt_ref[...], x, (((1,), (1,)), ((), ())), preferred_element_type=F32)


def _proj_dt(x2, w_pad, w_t, *, tm):
    m, d = x2.shape
    nh = w_t.shape[0]
    return pl.pallas_call(
        _proj_dt_kernel,
        grid=(m // tm,),
        in_specs=[
            pl.BlockSpec((tm, d), lambda i: (i, 0)),
            pl.BlockSpec((d, LANES), lambda i: (0, 0)),
            pl.BlockSpec((nh, d), lambda i: (0, 0)),
        ],
        out_specs=[pl.BlockSpec((tm, LANES), lambda i: (i, 0)), pl.BlockSpec((nh, tm), lambda i: (0, i))],
        out_shape=[jax.ShapeDtypeStruct((m, LANES), F32), jax.ShapeDtypeStruct((nh, m), F32)],
        compiler_params=_cparams(("parallel",)),
        name="proj_dt",
    )(x2, w_pad, w_t)


def _outproj_kernel(h_ref, a_ref, s_ref, wa_ref, ws_ref, out_ref):
    out_ref[...] = (h_ref[...]
                    + jnp.dot(a_ref[...], wa_ref[...], preferred_element_type=F32)
                    + jnp.dot(s_ref[...], ws_ref[...], preferred_element_type=F32))


def _outproj(h1, attn, ssd, w_o, *, tm, tn):
    m, d = h1.shape
    k = attn.shape[1]
    return pl.pallas_call(
        _outproj_kernel,
        grid=(m // tm, d // tn),
        in_specs=[
            pl.BlockSpec((tm, tn), lambda i, j: (i, j)),
            pl.BlockSpec((tm, k), lambda i, j: (i, 0)),
            pl.BlockSpec((tm, k), lambda i, j: (i, 0)),
            pl.BlockSpec((k, tn), lambda i, j: (0, j)),
            pl.BlockSpec((k, tn), lambda i, j: (1, j)),
        ],
        out_specs=pl.BlockSpec((tm, tn), lambda i, j: (i, j)),
        out_shape=jax.ShapeDtypeStruct((m, d), F32),
        compiler_params=_cparams(("parallel", "arbitrary")),
        name="outproj",
    )(h1, attn, ssd, w_o, w_o)


def _attn_kernel(qa_tab, ka_tab, qb_tab, kb_tab, q_ref, k_ref, vt_ref, km_ref, vmt_ref, lam_ref, gain_ref,
                 out_ref, qs_sc, sm_sc, s0_sc, s1_sc, mx0_sc, mx1_sc, m_sc, l_sc, acc_sc,
                 *, tq, dg, n_steps, lam_init):
    g = pl.program_id(2)
    qa = qa_tab[g]
    ka = ka_tab[g]
    qb = qb_tab[g]
    kb = kb_tab[g]
    has_a = g < n_steps
    has_b = g >= 1
    even = (g % 2) == 0
    nt = (((1,), (1,)), ((), ()))

    full_groups = [(slice(0, 2 * tq), tq)]
    diag_groups = [(slice(sub * tq + j * dg, sub * tq + (j + 1) * dg), (j + 1) * dg)
                   for sub in range(2) for j in range(tq // dg)]

    def stage_a(buf, diagonal):
        s_out, mx_out = buf
        for cols, nk in (diag_groups if diagonal else full_groups):
            s_t = lax.dot_general(k_ref[0:nk, :], qs_sc[cols, :], nt, preferred_element_type=F32)
            if diagonal:
                key = lax.broadcasted_iota(jnp.int32, s_t.shape, 0)
                qry = lax.broadcasted_iota(jnp.int32, s_t.shape, 1) + (nk - dg)
                s_t = jnp.where(key // CHUNK <= qry // CHUNK, s_t, -jnp.inf)
            s_out[0:nk, cols] = s_t
            mx_out[:, cols] = jnp.max(s_t, axis=0, keepdims=True)

    def stage_b(buf, last):
        s_in, mx_in = buf
        vt_aug = jnp.concatenate([vt_ref[...], jnp.ones((16, vt_ref.shape[1]), BF16)], axis=0)
        for cols, nk in (diag_groups if last else full_groups):
            m_prev = m_sc[:, cols]
            m_new = jnp.maximum(m_prev, mx_in[:, cols])
            alpha = jnp.exp2(m_prev - m_new)
            p = jnp.exp2(s_in[0:nk, cols] - m_new).astype(BF16)
            pv = jnp.dot(vt_aug[:, 0:nk], p, preferred_element_type=F32)
            l_sc[:, cols] = alpha * l_sc[:, cols] + pv[V_HEAD_DIM:V_HEAD_DIM + 1, :]
            acc_sc[:, cols] = alpha * acc_sc[:, cols] + pv[0:V_HEAD_DIM, :]
            m_sc[:, cols] = m_new
        if last:
            lq1, lk1, lq2, lk2 = lam_ref[0:1, :], lam_ref[1:2, :], lam_ref[2:3, :], lam_ref[3:4, :]
            lam = (jnp.exp(jnp.sum(lq1 * lk1, axis=-1, keepdims=True))
                   - jnp.exp(jnp.sum(lq2 * lk2, axis=-1, keepdims=True)) + lam_init)
            o = acc_sc[...] / l_sc[...]
            o = o[:, 0:tq] - lam * o[:, tq:2 * tq]
            ms = jnp.mean(o * o, axis=0, keepdims=True)
            o = o * lax.rsqrt(ms + EPS) * (gain_ref[...] * (1.0 - lam_init))
            out_ref[...] = o.T.astype(out_ref.dtype)

    @pl.when(has_b & (kb == 0))
    def _():
        s_t = sm_sc[...]
        m0 = jnp.max(s_t, axis=0, keepdims=True)
        p = jnp.exp2(s_t - m0)
        m_sc[...] = m0
        l_sc[...] = jnp.sum(p, axis=0, keepdims=True)
        acc_sc[...] = jnp.dot(vmt_ref[...], p.astype(BF16), preferred_element_type=F32)

    @pl.when(has_a & (ka == 0))
    def _():
        q = q_ref[...]
        lane = lax.broadcasted_iota(jnp.int32, q.shape, 1)
        zero = jnp.zeros_like(q)
        qs_sc[0:tq, :] = jnp.where(lane < DIFF_HEAD_DIM, q, zero)
        qs_sc[tq:2 * tq, :] = jnp.where(lane >= DIFF_HEAD_DIM, q, zero)
        s_t = lax.dot_general(km_ref[...], qs_sc[...], nt, preferred_element_type=F32)
        key = lax.broadcasted_iota(jnp.int32, s_t.shape, 0)
        sm_sc[...] = jnp.where(key < N_META, s_t, -jnp.inf)

    buf0, buf1 = (s0_sc, mx0_sc), (s1_sc, mx1_sc)
    a_diag = ka == qa
    b_last = kb == qb
    both = has_a & has_b
    for parity, (buf_a, buf_b) in enumerate(((buf0, buf1), (buf1, buf0))):
        par = even if parity == 0 else jnp.logical_not(even)

        @pl.when(par & both & jnp.logical_not(a_diag) & jnp.logical_not(b_last))
        def _(buf_a=buf_a, buf_b=buf_b):
            stage_a(buf_a, False)
            stage_b(buf_b, False)

        @pl.when(par & both & a_diag & jnp.logical_not(b_last))
        def _(buf_a=buf_a, buf_b=buf_b):
            stage_a(buf_a, True)
            stage_b(buf_b, False)

        @pl.when(par & both & b_last)
        def _(buf_a=buf_a, buf_b=buf_b):
            stage_a(buf_a, False)
            stage_b(buf_b, True)

        @pl.when(par & has_a & jnp.logical_not(has_b))
        def _(buf_a=buf_a):
            stage_a(buf_a, True)

        @pl.when(par & has_b & jnp.logical_not(has_a))
        def _(buf_b=buf_b):
            stage_b(buf_b, True)


def _attention(qk, vt, kmeta, vmeta_t, lam_vecs, gain_col, *, bsz, seq, tq, dg, lam_init):
    nq = seq // tq
    h = N_DIFF_HEADS
    qi_list, ki_list = [], []
    for a in range(nq):
        for b in range(a + 1):
            qi_list.append(a)
            ki_list.append(b)
    n_steps = len(qi_list)
    ia = np.minimum(np.arange(n_steps + 1), n_steps - 1)
    ib = np.maximum(np.arange(n_steps + 1) - 1, 0)
    qi_arr, ki_arr = np.array(qi_list, np.int32), np.array(ki_list, np.int32)
    tabs = [jnp.asarray(t) for t in (qi_arr[ia], ki_arr[ia], qi_arr[ib], ki_arr[ib])]
    grid_spec = pltpu.PrefetchScalarGridSpec(
        num_scalar_prefetch=4,
        grid=(bsz, h, n_steps + 1),
        in_specs=[
            pl.BlockSpec((tq, V_HEAD_DIM), lambda b, hh, g, qa, ka, qb, kb: (b * nq + qa[g], hh)),
            pl.BlockSpec((tq, V_HEAD_DIM), lambda b, hh, g, qa, ka, qb, kb: (b * nq + ka[g], h + hh)),
            pl.BlockSpec((V_HEAD_DIM, tq), lambda b, hh, g, qa, ka, qb, kb: (hh, b * nq + kb[g])),
            pl.BlockSpec((LANES, V_HEAD_DIM), lambda b, hh, g, qa, ka, qb, kb: (0, hh)),
            pl.BlockSpec((V_HEAD_DIM, LANES), lambda b, hh, g, qa, ka, qb, kb: (hh, 0)),
            pl.BlockSpec((4, DIFF_HEAD_DIM), lambda b, hh, g, qa, ka, qb, kb: (0, 0)),
            pl.BlockSpec((V_HEAD_DIM, 1), lambda b, hh, g, qa, ka, qb, kb: (0, 0)),
        ],
        out_specs=pl.BlockSpec((tq, V_HEAD_DIM), lambda b, hh, g, qa, ka, qb, kb: (b * nq + qb[g], hh)),
        scratch_shapes=[
            pltpu.VMEM((2 * tq, V_HEAD_DIM), BF16),
            pltpu.VMEM((LANES, 2 * tq), F32),
            pltpu.VMEM((tq, 2 * tq), F32),
            pltpu.VMEM((tq, 2 * tq), F32),
            pltpu.VMEM((1, 2 * tq), F32),
            pltpu.VMEM((1, 2 * tq), F32),
            pltpu.VMEM((1, 2 * tq), F32),
            pltpu.VMEM((1, 2 * tq), F32),
            pltpu.VMEM((V_HEAD_DIM, 2 * tq), F32),
        ],
    )
    return pl.pallas_call(
        functools.partial(_attn_kernel, tq=tq, dg=dg, n_steps=n_steps, lam_init=lam_init),
        grid_spec=grid_spec,
        out_shape=jax.ShapeDtypeStruct((bsz * seq, h * V_HEAD_DIM), BF16),
        compiler_params=_cparams(("parallel", "parallel", "arbitrary")),
        name="attn",
    )(*tabs, qk, qk, vt, kmeta, vmeta_t, lam_vecs, gain_col)


def _ssd_kernel(*refs, t_blk, bsz, n_cast):
    dtt_refs = refs[:bsz]
    cast_in = refs[bsz:bsz + n_cast]
    (zx_ref, dt_ref, xbcm_ref, dtm_ref, dttm_ref,
     convw_ref, convb_ref, dtb_ref, dtbt_ref, alog_ref, alogt_ref, dskip_ref, nw_ref,
     expand_ref, tril_ref, triu2_ref, shift_ref, out_ref) = refs[bsz + n_cast:bsz + n_cast + 18]
    cast_out = refs[bsz + n_cast + 18:bsz + 2 * n_cast + 18]
    xs_sc, xm_sc, state_sc = refs[bsz + 2 * n_cast + 18:]
    for src, dst in zip(cast_in, cast_out):
        dst[...] = src[...].astype(BF16)
    n_heads, p_dim, n_state, n_groups = N_SSD_HEADS, SSD_HEAD_DIM, D_STATE, N_SSD_GROUPS
    width = n_heads * p_dim
    gw = width // n_groups
    n_pairs = n_heads // 2
    nt = (((1,), (1,)), ((), ()))
    expand = expand_ref[...]
    a_row = -jnp.exp(alog_ref[...])
    a_col = -jnp.exp(alogt_ref[...])
    d_full = dskip_ref[...]
    lane = lax.broadcasted_iota(jnp.int32, (CHUNK, LANES), 1)
    row = lax.broadcasted_iota(jnp.int32, (CHUNK, LANES), 0)
    left = lane < p_dim
    causal2 = (lane & (CHUNK - 1)) <= row

    def chunk(win_f32, st_ref, dt_raw, dtt_raw, pad_rows, z, out_r0, b=0):
        win = win_f32.astype(BF16)
        conv = convb_ref[...] + win[HALO:, :].astype(F32) * convw_ref[CONV_WIDTH - 1:CONV_WIDTH, :]
        for k in range(CONV_WIDTH - 1):
            conv = conv + jnp.dot(shift_ref[k], win, preferred_element_type=F32) * convw_ref[k:k + 1, :]
        xc = _silu(conv)
        x_s = xc[:, :width]
        dt = jax.nn.softplus(dt_raw + dtb_ref[...])
        dtt = jax.nn.softplus(dtt_raw + dtbt_ref[...])
        if pad_rows:
            dt = jnp.where(row >= pad_rows, dt, 0.0)
            dtt = jnp.where(lax.broadcasted_iota(jnp.int32, dtt.shape, 1) >= pad_rows, dtt, 0.0)
        a_cs = _dot_01_by_f32(tril_ref[...], dt * a_row)
        a_cs_t2 = _dot_f32_by_01(dtt * a_col, triu2_ref[...])
        a_full = _dot_f32_by_01(a_cs, expand)
        dt_full = _dot_f32_by_01(dt, expand)
        a_last = a_full[CHUNK - 1:CHUNK, :]
        xdt = x_s * dt_full
        w_state = (xdt * jnp.exp(a_last - a_full)).astype(BF16)
        xdt_b = xdt.astype(BF16)
        zero_b = jnp.zeros((CHUNK, LANES), BF16)
        y_parts = []
        for g in range(n_groups):
            b_g = xc[:, width + g * n_state: width + (g + 1) * n_state]
            c_g = xc[:, width + (n_groups + g) * n_state: width + (n_groups + g + 1) * n_state]
            b_gb = b_g.astype(BF16)
            c_gb = c_g.astype(BF16)
            st = st_ref[g]
            if out_r0 is not None:
                cb2 = lax.dot_general(c_gb, jnp.concatenate([b_gb, b_gb], axis=0), nt,
                                      preferred_element_type=F32)
                y_off = jnp.dot(c_gb, st.astype(BF16), preferred_element_type=F32)
                y_g = y_off * jnp.exp(a_full[:, g * gw:(g + 1) * gw])
                diag = []
                for pp in range(n_pairs // n_groups):
                    pr = g * (n_pairs // n_groups) + pp
                    colb = a_full[:, pr * LANES:(pr + 1) * LANES]
                    rowb = jnp.where(left, a_cs_t2[2 * pr:2 * pr + 1, :], a_cs_t2[2 * pr + 1:2 * pr + 2, :])
                    dec = jnp.exp(jnp.where(causal2, colb - rowb, -jnp.inf))
                    m_pair = (cb2 * dec).astype(BF16)
                    xp = xdt_b[:, pr * LANES:(pr + 1) * LANES]
                    rhs = jnp.concatenate([jnp.where(left, xp, zero_b), jnp.where(left, zero_b, xp)], axis=0)
                    diag.append(jnp.dot(m_pair, rhs, preferred_element_type=F32))
                y_parts.append(y_g + jnp.concatenate(diag, axis=1))
            new_st = st * jnp.exp(a_last[:, g * gw:(g + 1) * gw]) + jnp.dot(
                b_g.T.astype(BF16), w_state[:, g * gw:(g + 1) * gw], preferred_element_type=F32)
            st_ref[g] = new_st
        if out_r0 is not None:
            y = jnp.concatenate(y_parts, axis=1) + x_s * d_full
            gated = y * _silu(z)
            out_ref[b, out_r0:out_r0 + CHUNK, :] = _rms(gated, nw_ref[...]).astype(out_ref.dtype)

    @pl.when(pl.program_id(0) == 0)
    def _():
        state_sc[...] = jnp.zeros_like(state_sc)
        xm_sc[0:HALO, :] = jnp.zeros((HALO, xm_sc.shape[1]), F32)
        xm_sc[HALO:HALO + CHUNK, :] = xbcm_ref[...].astype(F32)
        chunk(xm_sc[...], state_sc.at[0], dtm_ref[...], dttm_ref[...], CHUNK - N_META, None, None)
        for b in range(bsz):
            if b:
                state_sc[b] = state_sc[0]
            xs_sc[b, 0:HALO, :] = xm_sc[CHUNK:CHUNK + HALO, :]

    for b in range(bsz):
        xs_sc[b, HALO:HALO + t_blk, :] = zx_ref[b, :, width:].astype(F32)
    for c in range(t_blk // CHUNK):
        rows = slice(c * CHUNK, (c + 1) * CHUNK)
        for b in range(bsz):
            chunk(xs_sc[b, c * CHUNK:(c + 1) * CHUNK + HALO, :], state_sc.at[b], dt_ref[b, rows, :],
                  dtt_refs[b][:, rows], 0, zx_ref[b, rows, 0:width].astype(F32), c * CHUNK, b)
    for b in range(bsz):
        xs_sc[b, 0:HALO, :] = xs_sc[b, t_blk:t_blk + HALO, :]


def _ssd(zx, dt, dtt, xbc_m, dt_m, dtt_m, conv_w, conv_b, dt_bias, a_log, d_skip, norm_w, cast_weights,
         *, bsz, seq, t_blk):
    nb = seq // t_blk
    slab = lambda w: pl.BlockSpec((w.shape[0] // nb, w.shape[1]), lambda i: (i, 0))
    width = N_SSD_HEADS * SSD_HEAD_DIM
    cdim = zx.shape[1] - width
    nh = N_SSD_HEADS

    def pad_row(v):
        return jnp.pad(v.reshape(1, nh), ((0, 0), (0, LANES - nh)))

    hh = np.arange(LANES)[:, None]
    ll = np.arange(width)[None, :]
    one_hot = lambda m: jnp.asarray(m.astype(np.float32), dtype=BF16)
    expand = one_hot(ll // SSD_HEAD_DIM == hh)
    ii = np.arange(CHUNK)
    tril = one_hot(ii[None, :] <= ii[:, None])
    triu2 = one_hot(np.tile(ii[:, None] <= ii[None, :], (1, 2)))
    jj = np.arange(HALO + CHUNK)
    shift = one_hot(np.stack([jj[None, :] == ii[:, None] + HALO - (CONV_WIDTH - 1) + k
                              for k in range(CONV_WIDTH - 1)]))
    full = lambda shape: pl.BlockSpec(shape, lambda i: tuple(0 for _ in shape))
    out, *cast = pl.pallas_call(
        functools.partial(_ssd_kernel, t_blk=t_blk, bsz=bsz, n_cast=len(cast_weights)),
        grid=(nb,),
        in_specs=[pl.BlockSpec((nh, t_blk), functools.partial(lambda i, b: (0, b * nb + i), b=b))
                  for b in range(bsz)] + [slab(w) for w in cast_weights] + [
            pl.BlockSpec((bsz, t_blk, width + cdim), lambda i: (0, i, 0)),
            pl.BlockSpec((bsz, t_blk, LANES), lambda i: (0, i, 0)),
            full((CHUNK, cdim)), full((CHUNK, LANES)), full((nh, CHUNK)),
            full((CONV_WIDTH, cdim)), full((1, cdim)), full((1, LANES)), full((nh, 1)),
            full((1, LANES)), full((nh, 1)), full((1, width)), full((1, width)),
            full((LANES, width)), full((CHUNK, CHUNK)), full((CHUNK, 2 * CHUNK)), full(shift.shape),
        ],
        out_specs=[pl.BlockSpec((bsz, t_blk, width), lambda i: (0, i, 0))] + [slab(w) for w in cast_weights],
        out_shape=[jax.ShapeDtypeStruct((bsz, seq, width), BF16)]
        + [jax.ShapeDtypeStruct(w.shape, BF16) for w in cast_weights],
        scratch_shapes=[
            pltpu.VMEM((bsz, t_blk + HALO, cdim), F32),
            pltpu.VMEM((CHUNK + HALO, cdim), F32),
            pltpu.VMEM((bsz, N_SSD_GROUPS, D_STATE, width // N_SSD_GROUPS), F32),
        ],
        compiler_params=_cparams(("arbitrary",)),
        name="ssd",
    )(*([dtt] * bsz), *cast_weights, zx.reshape(bsz, seq, width + cdim), dt.reshape(bsz, seq, LANES),
      xbc_m, dt_m, dtt_m,
      conv_w, conv_b.reshape(1, cdim), pad_row(dt_bias), dt_bias.reshape(nh, 1), pad_row(a_log),
      a_log.reshape(nh, 1), jnp.repeat(d_skip, SSD_HEAD_DIM).reshape(1, width),
      norm_w.reshape(1, width), expand, tril, triu2, shift)
    return out.reshape(bsz * seq, width), cast


def _rope_tables(n_pos):
    inv = jnp.power(ROPE_THETA, -jnp.arange(0, ROT_DIM, 2, dtype=F32) / ROT_DIM)
    ang = jnp.arange(n_pos, dtype=F32)[:, None] * inv[None, :]
    cos, sin = jnp.cos(ang), jnp.sin(ang)
    half = ROT_DIM // 2
    r = np.arange(LANES) % DIFF_HEAD_DIM
    idx = jnp.asarray(r % half)
    lo = jnp.asarray(r < half)[None, :]
    hi = jnp.asarray((r >= half) & (r < ROT_DIM))[None, :]
    cos_t = jnp.where(lo | hi, cos[:, idx], 1.0)
    sin_lo = jnp.where(lo, -sin[:, idx], 0.0)
    sin_hi = jnp.where(hi, sin[:, idx], 0.0)
    return jnp.stack([cos_t, sin_lo, sin_hi])


def kernel(x, meta_tokens, ffn1_norm, ffn1_w_gate, ffn1_w_up, ffn1_w_down, mix_norm, w_in, q_norm, k_norm,
           lambda_q1, lambda_k1, lambda_q2, lambda_k2, attn_out_norm, conv_w, conv_b, dt_bias, a_log, d_skip,
           ssd_norm, w_out, ffn2_norm, ffn2_w_gate, ffn2_w_up, ffn2_w_down):
    bsz, seq, d = x.shape
    assert ffn1_norm.shape[0] == 1, "single-layer block"
    aw = N_DIFF_HEADS * V_HEAD_DIM
    sw = N_SSD_HEADS * SSD_HEAD_DIM
    cdim = sw + 2 * N_SSD_GROUPS * D_STATE
    lam_init = 0.8 - 0.6 * math.exp(-0.3 * 0)
    row = lambda v: v.reshape(1, -1)

    tn = 512
    w_in0 = w_in[0]
    w_dt = w_in0[:, 3 * aw + sw + cdim:].astype(BF16)
    w_dt_pad = jnp.pad(w_dt, ((0, 0), (0, LANES - N_SSD_HEADS)))
    w_dt_t = w_dt.T

    n_sub = aw // DIFF_HEAD_DIM
    q_gain = jnp.tile(q_norm[0], n_sub) * (DIFF_HEAD_DIM ** -0.5 * math.log2(math.e))
    k_gain = jnp.tile(k_norm[0], n_sub)
    qk_gain = jnp.concatenate([q_gain, k_gain]).reshape(2 * aw // tn, 1, tn)
    rope = _rope_tables(N_META + seq)
    gi = np.arange(2 * LANES) // DIFF_HEAD_DIM
    bd = jnp.asarray((gi[:, None] == gi[None, :]).astype(np.float32) / DIFF_HEAD_DIM, dtype=BF16)

    assert FFN_TF == tn and (3 * aw + sw + cdim) % tn == 0
    hn_m, wg1, wu1, wd1, w_in_b = _ffn_meta(meta_tokens.astype(F32), row(ffn1_norm[0]), ffn1_w_gate[0],
                                            ffn1_w_up[0], ffn1_w_down[0], w_in, row(mix_norm[0]), tf=FFN_TF)
    x2 = x.reshape(bsz * seq, d)
    h1, hn = _ffn(x2, row(ffn1_norm[0]), wg1, wu1, wd1, row(mix_norm[0]), tm=FFN_TM, tf=FFN_TF)

    qk = _proj_qk(hn, w_in_b, qk_gain, rope[:, N_META:], bd, tm=1024, tn=tn, n=2 * aw)
    qk_m = _proj_qk(hn_m, w_in_b, qk_gain, rope[:, :N_META], bd, tm=N_META, tn=tn, n=2 * aw)
    v_t = _proj(hn, w_in_b[:, 2 * aw:3 * aw].T, tm=1024, tn=tn, col0=0, n=aw, name="proj_vt", transpose=True)
    vzx_m = _proj(hn_m, w_in_b, tm=N_META, tn=tn, col0=2 * aw, n=aw + sw + cdim, name="proj_vzx_meta")
    zx = _proj(hn, w_in_b, tm=1024, tn=tn, col0=3 * aw, n=sw + cdim, name="proj_zx")
    v_t_m, xbc_m = vzx_m[:, :aw].T, vzx_m[:, aw + sw:]
    dt, dtt = _proj_dt(hn, w_dt_pad, w_dt_t, tm=1024)
    dt_m, dtt_m = _proj_dt(hn_m, w_dt_pad, w_dt_t, tm=N_META)

    kmeta = jnp.pad(qk_m[:, aw:], ((0, LANES - N_META), (0, 0)))
    vmeta_t = jnp.pad(v_t_m, ((0, 0), (0, LANES - N_META)))
    lam_vecs = jnp.stack([lambda_q1[0], lambda_k1[0], lambda_q2[0], lambda_k2[0]]).astype(F32)
    attn = _attention(qk, v_t, kmeta, vmeta_t, lam_vecs, attn_out_norm[0].reshape(V_HEAD_DIM, 1),
                      bsz=bsz, seq=seq, tq=1024, dg=256, lam_init=lam_init)

    lead = CHUNK - N_META
    ssd, (wg2, wu2, wd2, w_o) = _ssd(
        zx, dt, dtt,
        jnp.pad(xbc_m, ((lead, 0), (0, 0))), jnp.pad(dt_m, ((lead, 0), (0, 0))), jnp.pad(dtt_m, ((0, 0), (lead, 0))),
        conv_w[0], conv_b[0], dt_bias[0], a_log[0], d_skip[0], ssd_norm[0],
        (ffn2_w_gate[0], ffn2_w_up[0], ffn2_w_down[0], w_out[0]),
        bsz=bsz, seq=seq, t_blk=256)

    h2 = _outproj(h1, attn, ssd, w_o, tm=512, tn=d)
    (out,) = _ffn(h2, row(ffn2_norm[0]), wg2, wu2, wd2, tm=FFN_TM, tf=FFN_TF)
    return out.reshape(bsz, seq, d)
```

```python
import functools
import math

import jax
import jax.numpy as jnp
import numpy as np
from jax import lax
from jax.experimental import pallas as pl
from jax.experimental.pallas import tpu as pltpu

F32 = jnp.float32
BF16 = jnp.bfloat16

EPS = 1e-6
CHUNK = 64
N_META = 16
N_DIFF_HEADS = 8
DIFF_HEAD_DIM = 64
V_HEAD_DIM = 128
ROT_DIM = 16
ROPE_THETA = 500000.0
SSD_HEAD_DIM = 64
N_SSD_HEADS = 16
N_SSD_GROUPS = 2
D_STATE = 128
CONV_WIDTH = 4
HALO = 64
LANES = 128
VMEM_LIMIT = 56 * 1024 * 1024
FFN_TM = 512
FFN_TF = 512


def _cparams(sem):
    return pltpu.CompilerParams(dimension_semantics=sem, vmem_limit_bytes=VMEM_LIMIT)


def _silu(v):
    half = 0.5 * v
    return half + half * jnp.tanh(half)


def _split_bf16(a, pieces):
    out = []
    for _ in range(pieces - 1):
        p = a.astype(BF16)
        out.append(p)
        a = a - p.astype(F32)
    out.append(a.astype(BF16))
    return out


def _dot_f32_by_01(a, m01, pieces=3):
    return sum(jnp.dot(p, m01, preferred_element_type=F32) for p in _split_bf16(a, pieces))


def _dot_01_by_f32(m01, a, pieces=3):
    return sum(jnp.dot(m01, p, preferred_element_type=F32) for p in _split_bf16(a, pieces))


def _rms(v, w):
    ms = jnp.mean(v * v, axis=-1, keepdims=True)
    return v * lax.rsqrt(ms + EPS) * w


def _swiglu_step(xn, wg_ref, wu_ref, wd_ref):
    g = jnp.dot(xn, wg_ref[...], preferred_element_type=F32)
    u = jnp.dot(xn, wu_ref[...], preferred_element_type=F32)
    a = (_silu(g) * u).astype(BF16)
    return jnp.dot(a, wd_ref[...], preferred_element_type=F32)


def _qk_norm_rope(acc, gain, rope_ref, bd):
    cos, sin_lo, sin_hi = rope_ref[0], rope_ref[1], rope_ref[2]
    half = ROT_DIM // 2
    wide = bd.shape[0]
    out = []
    for c in range(acc.shape[1] // wide):
        y = acc[:, c * wide:(c + 1) * wide]
        ms = sum(jnp.dot(p, bd, preferred_element_type=F32) for p in _split_bf16(y * y, 2))
        yn = y * lax.rsqrt(ms + EPS) * gain[:, c * wide:(c + 1) * wide]
        for s in range(wide // LANES):
            v = yn[:, s * LANES:(s + 1) * LANES]
            out.append(v * cos + pltpu.roll(v, half, 1) * sin_hi + pltpu.roll(v, LANES - half, 1) * sin_lo)
    return jnp.concatenate(out, axis=1)


def _ffn_kernel(x_ref, nw_ref, wg_ref, wu_ref, wd_ref, *rest, with_norm):
    if with_norm:
        pnw_ref, out_ref, hn_ref, xn_sc = rest
    else:
        out_ref, xn_sc = rest
    j = pl.program_id(1)

    @pl.when(j == 0)
    def _():
        xn_sc[...] = _rms(x_ref[...], nw_ref[...]).astype(BF16)
        out_ref[...] = jnp.zeros_like(out_ref)

    out_ref[...] += _swiglu_step(xn_sc[...], wg_ref, wu_ref, wd_ref)

    @pl.when(j == pl.num_programs(1) - 1)
    def _():
        h = x_ref[...] + 0.5 * out_ref[...]
        out_ref[...] = h
        if with_norm:
            hn_ref[...] = _rms(h, pnw_ref[...]).astype(BF16)


def _ffn(x2, nw, wg, wu, wd, post_nw=None, *, tm, tf):
    m, d = x2.shape
    f = wg.shape[1]
    with_norm = post_nw is not None
    row_spec = pl.BlockSpec((tm, d), lambda i, j: (i, 0))
    vec_spec = pl.BlockSpec((1, d), lambda i, j: (0, 0))
    in_specs = [row_spec, vec_spec,
                pl.BlockSpec((d, tf), lambda i, j: (0, j)),
                pl.BlockSpec((d, tf), lambda i, j: (0, j)),
                pl.BlockSpec((tf, d), lambda i, j: (j, 0))]
    args = [x2, nw, wg, wu, wd]
    out_specs, out_shape = [row_spec], [jax.ShapeDtypeStruct((m, d), F32)]
    if with_norm:
        in_specs.append(vec_spec)
        args.append(post_nw)
        out_specs.append(row_spec)
        out_shape.append(jax.ShapeDtypeStruct((m, d), BF16))
    return pl.pallas_call(
        functools.partial(_ffn_kernel, with_norm=with_norm),
        grid=(m // tm, f // tf),
        in_specs=in_specs,
        out_specs=out_specs,
        out_shape=out_shape,
        scratch_shapes=[pltpu.VMEM((tm, d), BF16)],
        compiler_params=_cparams(("parallel", "arbitrary")),
        name="ffn",
    )(*args)


def _ffn_meta_kernel(x_ref, nw_ref, wg_ref, wu_ref, wd_ref, pnw_ref,
                     hn_ref, wgb_ref, wub_ref, wdb_ref, xn_sc, acc_sc):
    j = pl.program_id(0)

    @pl.when(j == 0)
    def _():
        xn_sc[...] = _rms(x_ref[...], nw_ref[...]).astype(BF16)
        acc_sc[...] = jnp.zeros_like(acc_sc)

    wgb_ref[...] = wg_ref[...].astype(BF16)
    wub_ref[...] = wu_ref[...].astype(BF16)
    wdb_ref[...] = wd_ref[...].astype(BF16)
    acc_sc[...] += _swiglu_step(xn_sc[...], wgb_ref, wub_ref, wdb_ref)

    @pl.when(j == pl.num_programs(0) - 1)
    def _():
        hn_ref[...] = _rms(x_ref[...] + 0.5 * acc_sc[...], pnw_ref[...]).astype(BF16)


def _ffn_meta(xm, nw, wg, wu, wd, post_nw, *, tf):
    m, d = xm.shape
    n_ff = wg.shape[1] // tf
    vec_spec = pl.BlockSpec((1, d), lambda j: (0, 0))
    row_spec = pl.BlockSpec((m, d), lambda j: (0, 0))
    col_tile = pl.BlockSpec((d, tf), lambda j: (0, j))
    row_tile = pl.BlockSpec((tf, d), lambda j: (j, 0))
    return pl.pallas_call(
        _ffn_meta_kernel,
        grid=(n_ff,),
        in_specs=[row_spec, vec_spec, col_tile, col_tile, row_tile, vec_spec],
        out_specs=[row_spec, col_tile, col_tile, row_tile],
        out_shape=[jax.ShapeDtypeStruct((m, d), BF16), jax.ShapeDtypeStruct(wg.shape, BF16),
                   jax.ShapeDtypeStruct(wu.shape, BF16), jax.ShapeDtypeStruct(wd.shape, BF16)],
        scratch_shapes=[pltpu.VMEM((m, d), BF16), pltpu.VMEM((m, d), F32)],
        compiler_params=_cparams(("arbitrary",)),
        name="ffn_meta_cast",
    )(xm, nw, wg, wu, wd, post_nw)


def _proj_kernel(x_ref, w_ref, out_ref, *, transpose):
    if transpose:
        acc = lax.dot_general(w_ref[...], x_ref[...], (((1,), (1,)), ((), ())), preferred_element_type=F32)
    else:
        acc = jnp.dot(x_ref[...], w_ref[...], preferred_element_type=F32)
    out_ref[...] = acc.astype(out_ref.dtype)


def _proj(x2, w, *, tm, tn, col0, n, name, transpose=False):
    m, d = x2.shape
    c0 = col0 // tn
    if transpose:
        w_spec = pl.BlockSpec((tn, d), lambda i, j: (c0 + j, 0))
        out_spec, out_shape = pl.BlockSpec((tn, tm), lambda i, j: (j, i)), (n, m)
    else:
        w_spec = pl.BlockSpec((d, tn), lambda i, j: (0, c0 + j))
        out_spec, out_shape = pl.BlockSpec((tm, tn), lambda i, j: (i, j)), (m, n)
    return pl.pallas_call(
        functools.partial(_proj_kernel, transpose=transpose),
        grid=(m // tm, n // tn),
        in_specs=[pl.BlockSpec((tm, d), lambda i, j: (i, 0)), w_spec],
        out_specs=out_spec,
        out_shape=jax.ShapeDtypeStruct(out_shape, BF16),
        compiler_params=_cparams(("parallel", "arbitrary")),
        name=name,
    )(x2, w)


def _proj_qk_kernel(x_ref, w_ref, gain_ref, rope_ref, bd_ref, out_ref):
    acc = jnp.dot(x_ref[...], w_ref[...], preferred_element_type=F32)
    out_ref[...] = _qk_norm_rope(acc, gain_ref[0], rope_ref, bd_ref[...]).astype(out_ref.dtype)


def _proj_qk(x2, w, gain, rope, bd, *, tm, tn, n):
    m, d = x2.shape
    n_tab = rope.shape[1] // tm
    return pl.pallas_call(
        _proj_qk_kernel,
        grid=(m // tm, n // tn),
        in_specs=[
            pl.BlockSpec((tm, d), lambda i, j: (i, 0)),
            pl.BlockSpec((d, tn), lambda i, j: (0, j)),
            pl.BlockSpec((1, 1, tn), lambda i, j: (j, 0, 0)),
            pl.BlockSpec((3, tm, LANES), lambda i, j: (0, i % n_tab, 0)),
            pl.BlockSpec(bd.shape, lambda i, j: (0, 0)),
        ],
        out_specs=pl.BlockSpec((tm, tn), lambda i, j: (i, j)),
        out_shape=jax.ShapeDtypeStruct((m, n), BF16),
        compiler_params=_cparams(("parallel", "arbitrary")),
        name="proj_qk",
    )(x2, w, gain, rope, bd)


def _proj_dt_kernel(x_ref, w_ref, wt_ref, out_ref, outt_ref):
    x = x_ref[...]
    out_ref[...] = jnp.dot(x, w_ref[...], preferred_element_type=F32)
    outt_ref[...] = lax.dot_general(wt_ref[...], x, (((1,), (1,)), ((), ())), preferred_element_type=F32)


def _proj_dt(x2, w_pad, w_t, *, tm):
    m, d = x2.shape
    nh = w_t.shape[0]
    return pl.pallas_call(
        _proj_dt_kernel,
        grid=(m // tm,),
        in_specs=[
            pl.BlockSpec((tm, d), lambda i: (i, 0)),
            pl.BlockSpec((d, LANES), lambda i: (0, 0)),
            pl.BlockSpec((nh, d), lambda i: (0, 0)),
        ],
        out_specs=[pl.BlockSpec((tm, LANES), lambda i: (i, 0)), pl.BlockSpec((nh, tm), lambda i: (0, i))],
        out_shape=[jax.ShapeDtypeStruct((m, LANES), F32), jax.ShapeDtypeStruct((nh, m), F32)],
        compiler_params=_cparams(("parallel",)),
        name="proj_dt",
    )(x2, w_pad, w_t)


def _outproj_kernel(h_ref, a_ref, s_ref, wa_ref, ws_ref, out_ref):
    out_ref[...] = (h_ref[...]
                    + jnp.dot(a_ref[...], wa_ref[...], preferred_element_type=F32)
                    + jnp.dot(s_ref[...], ws_ref[...], preferred_element_type=F32))


def _outproj(h1, attn, ssd, w_o, *, tm, tn):
    m, d = h1.shape
    k = attn.shape[1]
    return pl.pallas_call(
        _outproj_kernel,
        grid=(m // tm, d // tn),
        in_specs=[
            pl.BlockSpec((tm, tn), lambda i, j: (i, j)),
            pl.BlockSpec((tm, k), lambda i, j: (i, 0)),
            pl.BlockSpec((tm, k), lambda i, j: (i, 0)),
            pl.BlockSpec((k, tn), lambda i, j: (0, j)),
            pl.BlockSpec((k, tn), lambda i, j: (1, j)),
        ],
        out_specs=pl.BlockSpec((tm, tn), lambda i, j: (i, j)),
        out_shape=jax.ShapeDtypeStruct((m, d), F32),
        compiler_params=_cparams(("parallel", "arbitrary")),
        name="outproj",
    )(h1, attn, ssd, w_o, w_o)


def _attn_kernel(qa_tab, ka_tab, qb_tab, kb_tab, q_ref, k_ref, vt_ref, km_ref, vmt_ref, lam_ref, gain_ref,
                 out_ref, qs_sc, sm_sc, s0_sc, s1_sc, mx0_sc, mx1_sc, m_sc, l_sc, acc_sc,
                 *, tq, dg, n_steps, lam_init):
    g = pl.program_id(2)
    qa = qa_tab[g]
    ka = ka_tab[g]
    qb = qb_tab[g]
    kb = kb_tab[g]
    has_a = g < n_steps
    has_b = g >= 1
    even = (g % 2) == 0
    nt = (((1,), (1,)), ((), ()))

    full_groups = [(slice(0, 2 * tq), tq)]
    diag_groups = [(slice(sub * tq + j * dg, sub * tq + (j + 1) * dg), (j + 1) * dg)
                   for sub in range(2) for j in range(tq // dg)]

    def stage_a(buf, diagonal):
        s_out, mx_out = buf
        for cols, nk in (diag_groups if diagonal else full_groups):
            s_t = lax.dot_general(k_ref[0:nk, :], qs_sc[cols, :], nt, preferred_element_type=F32)
            if diagonal:
                key = lax.broadcasted_iota(jnp.int32, s_t.shape, 0)
                qry = lax.broadcasted_iota(jnp.int32, s_t.shape, 1) + (nk - dg)
                s_t = jnp.where(key // CHUNK <= qry // CHUNK, s_t, -jnp.inf)
            s_out[0:nk, cols] = s_t
            mx_out[:, cols] = jnp.max(s_t, axis=0, keepdims=True)

    def stage_b(buf, last):
        s_in, mx_in = buf
        vt_aug = jnp.concatenate([vt_ref[...], jnp.ones((16, vt_ref.shape[1]), BF16)], axis=0)
        for cols, nk in (diag_groups if last else full_groups):
            m_prev = m_sc[:, cols]
            m_new = jnp.maximum(m_prev, mx_in[:, cols])
            alpha = jnp.exp2(m_prev - m_new)
            p = jnp.exp2(s_in[0:nk, cols] - m_new).astype(BF16)
            pv = jnp.dot(vt_aug[:, 0:nk], p, preferred_element_type=F32)
            l_sc[:, cols] = alpha * l_sc[:, cols] + pv[V_HEAD_DIM:V_HEAD_DIM + 1, :]
            acc_sc[:, cols] = alpha * acc_sc[:, cols] + pv[0:V_HEAD_DIM, :]
            m_sc[:, cols] = m_new
        if last:
            lq1, lk1, lq2, lk2 = lam_ref[0:1, :], lam_ref[1:2, :], lam_ref[2:3, :], lam_ref[3:4, :]
            lam = (jnp.exp(jnp.sum(lq1 * lk1, axis=-1, keepdims=True))
                   - jnp.exp(jnp.sum(lq2 * lk2, axis=-1, keepdims=True)) + lam_init)
            o = acc_sc[...] / l_sc[...]
            o = o[:, 0:tq] - lam * o[:, tq:2 * tq]
            ms = jnp.mean(o * o, axis=0, keepdims=True)
            o = o * lax.rsqrt(ms + EPS) * (gain_ref[...] * (1.0 - lam_init))
            out_ref[...] = o.T.astype(out_ref.dtype)

    @pl.when(has_b & (kb == 0))
    def _():
        s_t = sm_sc[...]
        m0 = jnp.max(s_t, axis=0, keepdims=True)
        p = jnp.exp2(s_t - m0)
        m_sc[...] = m0
        l_sc[...] = jnp.sum(p, axis=0, keepdims=True)
        acc_sc[...] = jnp.dot(vmt_ref[...], p.astype(BF16), preferred_element_type=F32)

    @pl.when(has_a & (ka == 0))
    def _():
        q = q_ref[...]
        lane = lax.broadcasted_iota(jnp.int32, q.shape, 1)
        zero = jnp.zeros_like(q)
        qs_sc[0:tq, :] = jnp.where(lane < DIFF_HEAD_DIM, q, zero)
        qs_sc[tq:2 * tq, :] = jnp.where(lane >= DIFF_HEAD_DIM, q, zero)
        s_t = lax.dot_general(km_ref[...], qs_sc[...], nt, preferred_element_type=F32)
        key = lax.broadcasted_iota(jnp.int32, s_t.shape, 0)
        sm_sc[...] = jnp.where(key < N_META, s_t, -jnp.inf)

    buf0, buf1 = (s0_sc, mx0_sc), (s1_sc, mx1_sc)
    a_diag = ka == qa
    b_last = kb == qb
    both = has_a & has_b
    for parity, (buf_a, buf_b) in enumerate(((buf0, buf1), (buf1, buf0))):
        par = even if parity == 0 else jnp.logical_not(even)

        @pl.when(par & both & jnp.logical_not(a_diag) & jnp.logical_not(b_last))
        def _(buf_a=buf_a, buf_b=buf_b):
            stage_a(buf_a, False)
            stage_b(buf_b, False)

        @pl.when(par & both & a_diag & jnp.logical_not(b_last))
        def _(buf_a=buf_a, buf_b=buf_b):
            stage_a(buf_a, True)
            stage_b(buf_b, False)

        @pl.when(par & both & b_last)
        def _(buf_a=buf_a, buf_b=buf_b):
            stage_a(buf_a, False)
            stage_b(buf_b, True)

        @pl.when(par & has_a & jnp.logical_not(has_b))
        def _(buf_a=buf_a):
            stage_a(buf_a, True)

        @pl.when(par & has_b & jnp.logical_not(has_a))
        def _(buf_b=buf_b):
            stage_b(buf_b, True)


def _attention(qk, vt, kmeta, vmeta_t, lam_vecs, gain_col, *, bsz, seq, tq, dg, lam_init):
    nq = seq // tq
    h = N_DIFF_HEADS
    qi_list, ki_list = [], []
    for a in range(nq):
        for b in range(a + 1):
            qi_list.append(a)
            ki_list.append(b)
    n_steps = len(qi_list)
    ia = np.minimum(np.arange(n_steps + 1), n_steps - 1)
    ib = np.maximum(np.arange(n_steps + 1) - 1, 0)
    qi_arr, ki_arr = np.array(qi_list, np.int32), np.array(ki_list, np.int32)
    tabs = [jnp.asarray(t) for t in (qi_arr[ia], ki_arr[ia], qi_arr[ib], ki_arr[ib])]
    grid_spec = pltpu.PrefetchScalarGridSpec(
        num_scalar_prefetch=4,
        grid=(bsz, h, n_steps + 1),
        in_specs=[
            pl.BlockSpec((tq, V_HEAD_DIM), lambda b, hh, g, qa, ka, qb, kb: (b * nq + qa[g], hh)),
            pl.BlockSpec((tq, V_HEAD_DIM), lambda b, hh, g, qa, ka, qb, kb: (b * nq + ka[g], h + hh)),
            pl.BlockSpec((V_HEAD_DIM, tq), lambda b, hh, g, qa, ka, qb, kb: (hh, b * nq + kb[g])),
            pl.BlockSpec((LANES, V_HEAD_DIM), lambda b, hh, g, qa, ka, qb, kb: (0, hh)),
            pl.BlockSpec((V_HEAD_DIM, LANES), lambda b, hh, g, qa, ka, qb, kb: (hh, 0)),
            pl.BlockSpec((4, DIFF_HEAD_DIM), lambda b, hh, g, qa, ka, qb, kb: (0, 0)),
            pl.BlockSpec((V_HEAD_DIM, 1), lambda b, hh, g, qa, ka, qb, kb: (0, 0)),
        ],
        out_specs=pl.BlockSpec((tq, V_HEAD_DIM), lambda b, hh, g, qa, ka, qb, kb: (b * nq + qb[g], hh)),
        scratch_shapes=[
            pltpu.VMEM((2 * tq, V_HEAD_DIM), BF16),
            pltpu.VMEM((LANES, 2 * tq), F32),
            pltpu.VMEM((tq, 2 * tq), F32),
            pltpu.VMEM((tq, 2 * tq), F32),
            pltpu.VMEM((1, 2 * tq), F32),
            pltpu.VMEM((1, 2 * tq), F32),
            pltpu.VMEM((1, 2 * tq), F32),
            pltpu.VMEM((1, 2 * tq), F32),
            pltpu.VMEM((V_HEAD_DIM, 2 * tq), F32),
        ],
    )
    return pl.pallas_call(
        functools.partial(_attn_kernel, tq=tq, dg=dg, n_steps=n_steps, lam_init=lam_init),
        grid_spec=grid_spec,
        out_shape=jax.ShapeDtypeStruct((bsz * seq, h * V_HEAD_DIM), BF16),
        compiler_params=_cparams(("parallel", "parallel", "arbitrary")),
        name="attn",
    )(*tabs, qk, qk, vt, kmeta, vmeta_t, lam_vecs, gain_col)


def _ssd_kernel(*refs, t_blk, bsz, n_cast):
    dtt_refs = refs[:bsz]
    cast_in = refs[bsz:bsz + n_cast]
    (zx_ref, dt_ref, xbcm_ref, dtm_ref, dttm_ref,
     convw_ref, convb_ref, dtb_ref, dtbt_ref, alog_ref, alogt_ref, dskip_ref, nw_ref,
     expand_ref, tril_ref, triu2_ref, shift_ref, out_ref) = refs[bsz + n_cast:bsz + n_cast + 18]
    cast_out = refs[bsz + n_cast + 18:bsz + 2 * n_cast + 18]
    xs_sc, xm_sc, state_sc = refs[bsz + 2 * n_cast + 18:]
    for src, dst in zip(cast_in, cast_out):
        dst[...] = src[...].astype(BF16)
    n_heads, p_dim, n_state, n_groups = N_SSD_HEADS, SSD_HEAD_DIM, D_STATE, N_SSD_GROUPS
    width = n_heads * p_dim
    gw = width // n_groups
    n_pairs = n_heads // 2
    nt = (((1,), (1,)), ((), ()))
    expand = expand_ref[...]
    a_row = -jnp.exp(alog_ref[...])
    a_col = -jnp.exp(alogt_ref[...])
    d_full = dskip_ref[...]
    lane = lax.broadcasted_iota(jnp.int32, (CHUNK, LANES), 1)
    row = lax.broadcasted_iota(jnp.int32, (CHUNK, LANES), 0)
    left = lane < p_dim
    causal2 = (lane & (CHUNK - 1)) <= row

    def chunk(win_f32, st_ref, dt_raw, dtt_raw, pad_rows, z, out_r0, b=0):
        win = win_f32.astype(BF16)
        conv = convb_ref[...] + win[HALO:, :].astype(F32) * convw_ref[CONV_WIDTH - 1:CONV_WIDTH, :]
        for k in range(CONV_WIDTH - 1):
            conv = conv + jnp.dot(shift_ref[k], win, preferred_element_type=F32) * convw_ref[k:k + 1, :]
        xc = _silu(conv)
        x_s = xc[:, :width]
        dt = jax.nn.softplus(dt_raw + dtb_ref[...])
        dtt = jax.nn.softplus(dtt_raw + dtbt_ref[...])
        if pad_rows:
            dt = jnp.where(row >= pad_rows, dt, 0.0)
            dtt = jnp.where(lax.broadcasted_iota(jnp.int32, dtt.shape, 1) >= pad_rows, dtt, 0.0)
        a_cs = _dot_01_by_f32(tril_ref[...], dt * a_row)
        a_cs_t2 = _dot_f32_by_01(dtt * a_col, triu2_ref[...])
        a_full = _dot_f32_by_01(a_cs, expand)
        dt_full = _dot_f32_by_01(dt, expand)
        a_last = a_full[CHUNK - 1:CHUNK, :]
        xdt = x_s * dt_full
        w_state = (xdt * jnp.exp(a_last - a_full)).astype(BF16)
        xdt_b = xdt.astype(BF16)
        zero_b = jnp.zeros((CHUNK, LANES), BF16)
        y_parts = []
        for g in range(n_groups):
            b_g = xc[:, width + g * n_state: width + (g + 1) * n_state]
            c_g = xc[:, width + (n_groups + g) * n_state: width + (n_groups + g + 1) * n_state]
            b_gb = b_g.astype(BF16)
            c_gb = c_g.astype(BF16)
            st = st_ref[g]
            if out_r0 is not None:
                cb2 = lax.dot_general(c_gb, jnp.concatenate([b_gb, b_gb], axis=0), nt,
                                      preferred_element_type=F32)
                y_off = jnp.dot(c_gb, st.astype(BF16), preferred_element_type=F32)
                y_g = y_off * jnp.exp(a_full[:, g * gw:(g + 1) * gw])
                diag = []
                for pp in range(n_pairs // n_groups):
                    pr = g * (n_pairs // n_groups) + pp
                    colb = a_full[:, pr * LANES:(pr + 1) * LANES]
                    rowb = jnp.where(left, a_cs_t2[2 * pr:2 * pr + 1, :], a_cs_t2[2 * pr + 1:2 * pr + 2, :])
                    dec = jnp.exp(jnp.where(causal2, colb - rowb, -jnp.inf))
                    m_pair = (cb2 * dec).astype(BF16)
                    xp = xdt_b[:, pr * LANES:(pr + 1) * LANES]
                    rhs = jnp.concatenate([jnp.where(left, xp, zero_b), jnp.where(left, zero_b, xp)], axis=0)
                    diag.append(jnp.dot(m_pair, rhs, preferred_element_type=F32))
                y_parts.append(y_g + jnp.concatenate(diag, axis=1))
            new_st = st * jnp.exp(a_last[:, g * gw:(g + 1) * gw]) + jnp.dot(
                b_g.T.astype(BF16), w_state[:, g * gw:(g + 1) * gw], preferred_element_type=F32)
            st_ref[g] = new_st
        if out_r0 is not None:
            y = jnp.concatenate(y_parts, axis=1) + x_s * d_full
            gated = y * _silu(z)
            out_ref[b, out_r0:out_r0 + CHUNK, :] = _rms(gated, nw_ref[...]).astype(out_ref.dtype)

    @pl.when(pl.program_id(0) == 0)
    def _():
        state_sc[...] = jnp.zeros_like(state_sc)
        xm_sc[0:HALO, :] = jnp.zeros((HALO, xm_sc.shape[1]), F32)
        xm_sc[HALO:HALO + CHUNK, :] = xbcm_ref[...].astype(F32)
        chunk(xm_sc[...], state_sc.at[0], dtm_ref[...], dttm_ref[...], CHUNK - N_META, None, None)
        for b in range(bsz):
            if b:
                state_sc[b] = state_sc[0]
            xs_sc[b, 0:HALO, :] = xm_sc[CHUNK:CHUNK + HALO, :]

    for b in range(bsz):
        xs_sc[b, HALO:HALO + t_blk, :] = zx_ref[b, :, width:].astype(F32)
    for c in range(t_blk // CHUNK):
        rows = slice(c * CHUNK, (c + 1) * CHUNK)
        for b in range(bsz):
            chunk(xs_sc[b, c * CHUNK:(c + 1) * CHUNK + HALO, :], state_sc.at[b], dt_ref[b, rows, :],
                  dtt_refs[b][:, rows], 0, zx_ref[b, rows, 0:width].astype(F32), c * CHUNK, b)
    for b in range(bsz):
        xs_sc[b, 0:HALO, :] = xs_sc[b, t_blk:t_blk + HALO, :]


def _ssd(zx, dt, dtt, xbc_m, dt_m, dtt_m, conv_w, conv_b, dt_bias, a_log, d_skip, norm_w, cast_weights,
         *, bsz, seq, t_blk):
    nb = seq // t_blk
    slab = lambda w: pl.BlockSpec((w.shape[0] // nb, w.shape[1]), lambda i: (i, 0))
    width = N_SSD_HEADS * SSD_HEAD_DIM
    cdim = zx.shape[1] - width
    nh = N_SSD_HEADS

    def pad_row(v):
        return jnp.pad(v.reshape(1, nh), ((0, 0), (0, LANES - nh)))

    hh = np.arange(LANES)[:, None]
    ll = np.arange(width)[None, :]
    one_hot = lambda m: jnp.asarray(m.astype(np.float32), dtype=BF16)
    expand = one_hot(ll // SSD_HEAD_DIM == hh)
    ii = np.arange(CHUNK)
    tril = one_hot(ii[None, :] <= ii[:, None])
    triu2 = one_hot(np.tile(ii[:, None] <= ii[None, :], (1, 2)))
    jj = np.arange(HALO + CHUNK)
    shift = one_hot(np.stack([jj[None, :] == ii[:, None] + HALO - (CONV_WIDTH - 1) + k
                              for k in range(CONV_WIDTH - 1)]))
    full = lambda shape: pl.BlockSpec(shape, lambda i: tuple(0 for _ in shape))
    out, *cast = pl.pallas_call(
        functools.partial(_ssd_kernel, t_blk=t_blk, bsz=bsz, n_cast=len(cast_weights)),
        grid=(nb,),
        in_specs=[pl.BlockSpec((nh, t_blk), functools.partial(lambda i, b: (0, b * nb + i), b=b))
                  for b in range(bsz)] + [slab(w) for w in cast_weights] + [
            pl.BlockSpec((bsz, t_blk, width + cdim), lambda i: (0, i, 0)),
            pl.BlockSpec((bsz, t_blk, LANES), lambda i: (0, i, 0)),
            full((CHUNK, cdim)), full((CHUNK, LANES)), full((nh, CHUNK)),
            full((CONV_WIDTH, cdim)), full((1, cdim)), full((1, LANES)), full((nh, 1)),
            full((1, LANES)), full((nh, 1)), full((1, width)), full((1, width)),
            full((LANES, width)), full((CHUNK, CHUNK)), full((CHUNK, 2 * CHUNK)), full(shift.shape),
        ],
        out_specs=[pl.BlockSpec((bsz, t_blk, width), lambda i: (0, i, 0))] + [slab(w) for w in cast_weights],
        out_shape=[jax.ShapeDtypeStruct((bsz, seq, width), BF16)]
        + [jax.ShapeDtypeStruct(w.shape, BF16) for w in cast_weights],
        scratch_shapes=[
            pltpu.VMEM((bsz, t_blk + HALO, cdim), F32),
            pltpu.VMEM((CHUNK + HALO, cdim), F32),
            pltpu.VMEM((bsz, N_SSD_GROUPS, D_STATE, width // N_SSD_GROUPS), F32),
        ],
        compiler_params=_cparams(("arbitrary",)),
        name="ssd",
    )(*([dtt] * bsz), *cast_weights, zx.reshape(bsz, seq, width + cdim), dt.reshape(bsz, seq, LANES),
      xbc_m, dt_m, dtt_m,
      conv_w, conv_b.reshape(1, cdim), pad_row(dt_bias), dt_bias.reshape(nh, 1), pad_row(a_log),
      a_log.reshape(nh, 1), jnp.repeat(d_skip, SSD_HEAD_DIM).reshape(1, width),
      norm_w.reshape(1, width), expand, tril, triu2, shift)
    return out.reshape(bsz * seq, width), cast


def _rope_tables(n_pos):
    inv = jnp.power(ROPE_THETA, -jnp.arange(0, ROT_DIM, 2, dtype=F32) / ROT_DIM)
    ang = jnp.arange(n_pos, dtype=F32)[:, None] * inv[None, :]
    cos, sin = jnp.cos(ang), jnp.sin(ang)
    half = ROT_DIM // 2
    r = np.arange(LANES) % DIFF_HEAD_DIM
    idx = jnp.asarray(r % half)
    lo = jnp.asarray(r < half)[None, :]
    hi = jnp.asarray((r >= half) & (r < ROT_DIM))[None, :]
    cos_t = jnp.where(lo | hi, cos[:, idx], 1.0)
    sin_lo = jnp.where(lo, -sin[:, idx], 0.0)
    sin_hi = jnp.where(hi, sin[:, idx], 0.0)
    return jnp.stack([cos_t, sin_lo, sin_hi])


def kernel(x, meta_tokens, ffn1_norm, ffn1_w_gate, ffn1_w_up, ffn1_w_down, mix_norm, w_in, q_norm, k_norm,
           lambda_q1, lambda_k1, lambda_q2, lambda_k2, attn_out_norm, conv_w, conv_b, dt_bias, a_log, d_skip,
           ssd_norm, w_out, ffn2_norm, ffn2_w_gate, ffn2_w_up, ffn2_w_down):
    bsz, seq, d = x.shape
    assert ffn1_norm.shape[0] == 1, "single-layer block"
    aw = N_DIFF_HEADS * V_HEAD_DIM
    sw = N_SSD_HEADS * SSD_HEAD_DIM
    cdim = sw + 2 * N_SSD_GROUPS * D_STATE
    lam_init = 0.8 - 0.6 * math.exp(-0.3 * 0)
    row = lambda v: v.reshape(1, -1)

    tn = 512
    w_in0 = w_in[0]
    w_dt = w_in0[:, 3 * aw + sw + cdim:].astype(BF16)
    w_dt_pad = jnp.pad(w_dt, ((0, 0), (0, LANES - N_SSD_HEADS)))
    w_dt_t = w_dt.T

    n_sub = aw // DIFF_HEAD_DIM
    q_gain = jnp.tile(q_norm[0], n_sub) * (DIFF_HEAD_DIM ** -0.5 * math.log2(math.e))
    k_gain = jnp.tile(k_norm[0], n_sub)
    qk_gain = jnp.concatenate([q_gain, k_gain]).reshape(2 * aw // tn, 1, tn)
    rope = _rope_tables(N_META + seq)
    gi = np.arange(2 * LANES) // DIFF_HEAD_DIM
    bd = jnp.asarray((gi[:, None] == gi[None, :]).astype(np.float32) / DIFF_HEAD_DIM, dtype=BF16)

    hn_m, wg1, wu1, wd1 = _ffn_meta(meta_tokens.astype(F32), row(ffn1_norm[0]), ffn1_w_gate[0],
                                    ffn1_w_up[0], ffn1_w_down[0], row(mix_norm[0]), tf=FFN_TF)
    w_in_b = w_in0[:, :3 * aw + sw + cdim].astype(BF16)
    x2 = x.reshape(bsz * seq, d)
    h1, hn = _ffn(x2, row(ffn1_norm[0]), wg1, wu1, wd1, row(mix_norm[0]), tm=FFN_TM, tf=FFN_TF)

    qk = _proj_qk(hn, w_in_b, qk_gain, rope[:, N_META:], bd, tm=1024, tn=tn, n=2 * aw)
    qk_m = _proj_qk(hn_m, w_in_b, qk_gain, rope[:, :N_META], bd, tm=N_META, tn=tn, n=2 * aw)
    v_t = _proj(hn, w_in_b[:, 2 * aw:3 * aw].T, tm=1024, tn=tn, col0=0, n=aw, name="proj_vt", transpose=True)
    vzx_m = _proj(hn_m, w_in_b, tm=N_META, tn=tn, col0=2 * aw, n=aw + sw + cdim, name="proj_vzx_meta")
    zx = _proj(hn, w_in_b, tm=1024, tn=tn, col0=3 * aw, n=sw + cdim, name="proj_zx")
    v_t_m, xbc_m = vzx_m[:, :aw].T, vzx_m[:, aw + sw:]
    dt, dtt = _proj_dt(hn, w_dt_pad, w_dt_t, tm=1024)
    dt_m, dtt_m = _proj_dt(hn_m, w_dt_pad, w_dt_t, tm=N_META)

    kmeta = jnp.pad(qk_m[:, aw:], ((0, LANES - N_META), (0, 0)))
    vmeta_t = jnp.pad(v_t_m, ((0, 0), (0, LANES - N_META)))
    lam_vecs = jnp.stack([lambda_q1[0], lambda_k1[0], lambda_q2[0], lambda_k2[0]]).astype(F32)
    attn = _attention(qk, v_t, kmeta, vmeta_t, lam_vecs, attn_out_norm[0].reshape(V_HEAD_DIM, 1),
                      bsz=bsz, seq=seq, tq=1024, dg=256, lam_init=lam_init)

    lead = CHUNK - N_META
    ssd, (wg2, wu2, wd2, w_o) = _ssd(
        zx, dt, dtt,
        jnp.pad(xbc_m, ((lead, 0), (0, 0))), jnp.pad(dt_m, ((lead, 0), (0, 0))), jnp.pad(dtt_m, ((0, 0), (lead, 0))),
        conv_w[0], conv_b[0], dt_bias[0], a_log[0], d_skip[0], ssd_norm[0],
        (ffn2_w_gate[0], ffn2_w_up[0], ffn2_w_down[0], w_out[0]),
        bsz=bsz, seq=seq, t_blk=256)

    h2 = _outproj(h1, attn, ssd, w_o, tm=512, tn=d)
    (out,) = _ffn(h2, row(ffn2_norm[0]), wg2, wu2, wd2, tm=FFN_TM, tf=FFN_TF)
    return out.reshape(bsz, seq, d)
```

```python
import functools
import math

import jax
import jax.numpy as jnp
import numpy as np
from jax import lax
from jax.experimental import pallas as pl
from jax.experimental.pallas import tpu as pltpu

F32 = jnp.float32
BF16 = jnp.bfloat16

EPS = 1e-6
CHUNK = 64
N_META = 16
N_DIFF_HEADS = 8
DIFF_HEAD_DIM = 64
V_HEAD_DIM = 128
ROT_DIM = 16
ROPE_THETA = 500000.0
SSD_HEAD_DIM = 64
N_SSD_HEADS = 16
N_SSD_GROUPS = 2
D_STATE = 128
CONV_WIDTH = 4
HALO = 64
LANES = 128
VMEM_LIMIT = 56 * 1024 * 1024
FFN_TM = 512
FFN_TF = 512


def _cparams(sem):
    return pltpu.CompilerParams(dimension_semantics=sem, vmem_limit_bytes=VMEM_LIMIT)


def _silu(v):
    half = 0.5 * v
    return half + half * jnp.tanh(half)


def _split_bf16(a, pieces):
    out = []
    for _ in range(pieces - 1):
        p = a.astype(BF16)
        out.append(p)
        a = a - p.astype(F32)
    out.append(a.astype(BF16))
    return out


def _dot_f32_by_01(a, m01, pieces=3):
    return sum(jnp.dot(p, m01, preferred_element_type=F32) for p in _split_bf16(a, pieces))


def _rms(v, w):
    ms = jnp.mean(v * v, axis=-1, keepdims=True)
    return v * lax.rsqrt(ms + EPS) * w


def _swiglu_step(xn, wg_ref, wu_ref, wd_ref):
    g = jnp.dot(xn, wg_ref[...], preferred_element_type=F32)
    u = jnp.dot(xn, wu_ref[...], preferred_element_type=F32)
    a = (_silu(g) * u).astype(BF16)
    return jnp.dot(a, wd_ref[...], preferred_element_type=F32)


def _qk_norm_rope(acc, gain, rope_ref, bd):
    cos, sin_lo, sin_hi = rope_ref[0], rope_ref[1], rope_ref[2]
    half = ROT_DIM // 2
    wide = bd.shape[0]
    out = []
    for c in range(acc.shape[1] // wide):
        y = acc[:, c * wide:(c + 1) * wide]
        ms = sum(jnp.dot(p, bd, preferred_element_type=F32) for p in _split_bf16(y * y, 2))
        yn = y * lax.rsqrt(ms + EPS) * gain[:, c * wide:(c + 1) * wide]
        for s in range(wide // LANES):
            v = yn[:, s * LANES:(s + 1) * LANES]
            out.append(v * cos + pltpu.roll(v, half, 1) * sin_hi + pltpu.roll(v, LANES - half, 1) * sin_lo)
    return jnp.concatenate(out, axis=1)


def _ffn_kernel(x_ref, nw_ref, wg_ref, wu_ref, wd_ref, *rest, with_norm):
    if with_norm:
        pnw_ref, out_ref, hn_ref, xn_sc = rest
    else:
        out_ref, xn_sc = rest
    j = pl.program_id(1)

    @pl.when(j == 0)
    def _():
        xn_sc[...] = _rms(x_ref[...], nw_ref[...]).astype(BF16)
        out_ref[...] = jnp.zeros_like(out_ref)

    out_ref[...] += _swiglu_step(xn_sc[...], wg_ref, wu_ref, wd_ref)

    @pl.when(j == pl.num_programs(1) - 1)
    def _():
        h = x_ref[...] + 0.5 * out_ref[...]
        out_ref[...] = h
        if with_norm:
            hn_ref[...] = _rms(h, pnw_ref[...]).astype(BF16)


def _ffn(x2, nw, wg, wu, wd, post_nw=None, *, tm, tf):
    m, d = x2.shape
    f = wg.shape[1]
    with_norm = post_nw is not None
    row_spec = pl.BlockSpec((tm, d), lambda i, j: (i, 0))
    vec_spec = pl.BlockSpec((1, d), lambda i, j: (0, 0))
    in_specs = [row_spec, vec_spec,
                pl.BlockSpec((d, tf), lambda i, j: (0, j)),
                pl.BlockSpec((d, tf), lambda i, j: (0, j)),
                pl.BlockSpec((tf, d), lambda i, j: (j, 0))]
    args = [x2, nw, wg, wu, wd]
    out_specs, out_shape = [row_spec], [jax.ShapeDtypeStruct((m, d), F32)]
    if with_norm:
        in_specs.append(vec_spec)
        args.append(post_nw)
        out_specs.append(row_spec)
        out_shape.append(jax.ShapeDtypeStruct((m, d), BF16))
    return pl.pallas_call(
        functools.partial(_ffn_kernel, with_norm=with_norm),
        grid=(m // tm, f // tf),
        in_specs=in_specs,
        out_specs=out_specs,
        out_shape=out_shape,
        scratch_shapes=[pltpu.VMEM((tm, d), BF16)],
        compiler_params=_cparams(("parallel", "arbitrary")),
        name="ffn",
    )(*args)


def _ffn_meta_kernel(x_ref, nw_ref, wg_ref, wu_ref, wd_ref, pnw_ref,
                     hn_ref, wgb_ref, wub_ref, wdb_ref, xn_sc, acc_sc):
    j = pl.program_id(0)

    @pl.when(j == 0)
    def _():
        xn_sc[...] = _rms(x_ref[...], nw_ref[...]).astype(BF16)
        acc_sc[...] = jnp.zeros_like(acc_sc)

    wgb_ref[...] = wg_ref[...].astype(BF16)
    wub_ref[...] = wu_ref[...].astype(BF16)
    wdb_ref[...] = wd_ref[...].astype(BF16)
    acc_sc[...] += _swiglu_step(xn_sc[...], wgb_ref, wub_ref, wdb_ref)

    @pl.when(j == pl.num_programs(0) - 1)
    def _():
        hn_ref[...] = _rms(x_ref[...] + 0.5 * acc_sc[...], pnw_ref[...]).astype(BF16)


def _ffn_meta(xm, nw, wg, wu, wd, post_nw, *, tf):
    m, d = xm.shape
    n_ff = wg.shape[1] // tf
    vec_spec = pl.BlockSpec((1, d), lambda j: (0, 0))
    row_spec = pl.BlockSpec((m, d), lambda j: (0, 0))
    col_tile = pl.BlockSpec((d, tf), lambda j: (0, j))
    row_tile = pl.BlockSpec((tf, d), lambda j: (j, 0))
    return pl.pallas_call(
        _ffn_meta_kernel,
        grid=(n_ff,),
        in_specs=[row_spec, vec_spec, col_tile, col_tile, row_tile, vec_spec],
        out_specs=[row_spec, col_tile, col_tile, row_tile],
        out_shape=[jax.ShapeDtypeStruct((m, d), BF16), jax.ShapeDtypeStruct(wg.shape, BF16),
                   jax.ShapeDtypeStruct(wu.shape, BF16), jax.ShapeDtypeStruct(wd.shape, BF16)],
        scratch_shapes=[pltpu.VMEM((m, d), BF16), pltpu.VMEM((m, d), F32)],
        compiler_params=_cparams(("arbitrary",)),
        name="ffn_meta_cast",
    )(xm, nw, wg, wu, wd, post_nw)


def _proj_kernel(x_ref, w_ref, out_ref, *, transpose):
    if transpose:
        acc = lax.dot_general(w_ref[...], x_ref[...], (((1,), (1,)), ((), ())), preferred_element_type=F32)
    else:
        acc = jnp.dot(x_ref[...], w_ref[...], preferred_element_type=F32)
    out_ref[...] = acc.astype(out_ref.dtype)


def _proj(x2, w, *, tm, tn, col0, n, name, transpose=False):
    m, d = x2.shape
    c0 = col0 // tn
    if transpose:
        w_spec = pl.BlockSpec((tn, d), lambda i, j: (c0 + j, 0))
        out_spec, out_shape = pl.BlockSpec((tn, tm), lambda i, j: (j, i)), (n, m)
    else:
        w_spec = pl.BlockSpec((d, tn), lambda i, j: (0, c0 + j))
        out_spec, out_shape = pl.BlockSpec((tm, tn), lambda i, j: (i, j)), (m, n)
    return pl.pallas_call(
        functools.partial(_proj_kernel, transpose=transpose),
        grid=(m // tm, n // tn),
        in_specs=[pl.BlockSpec((tm, d), lambda i, j: (i, 0)), w_spec],
        out_specs=out_spec,
        out_shape=jax.ShapeDtypeStruct(out_shape, BF16),
        compiler_params=_cparams(("parallel", "arbitrary")),
        name=name,
    )(x2, w)


def _proj_qk_kernel(x_ref, w_ref, gain_ref, rope_ref, bd_ref, out_ref):
    acc = jnp.dot(x_ref[...], w_ref[...], preferred_element_type=F32)
    out_ref[...] = _qk_norm_rope(acc, gain_ref[0], rope_ref, bd_ref[...]).astype(out_ref.dtype)


def _proj_qk(x2, w, gain, rope, bd, *, tm, tn, n):
    m, d = x2.shape
    n_tab = rope.shape[1] // tm
    return pl.pallas_call(
        _proj_qk_kernel,
        grid=(m // tm, n // tn),
        in_specs=[
            pl.BlockSpec((tm, d), lambda i, j: (i, 0)),
            pl.BlockSpec((d, tn), lambda i, j: (0, j)),
            pl.BlockSpec((1, 1, tn), lambda i, j: (j, 0, 0)),
            pl.BlockSpec((3, tm, LANES), lambda i, j: (0, i % n_tab, 0)),
            pl.BlockSpec(bd.shape, lambda i, j: (0, 0)),
        ],
        out_specs=pl.BlockSpec((tm, tn), lambda i, j: (i, j)),
        out_shape=jax.ShapeDtypeStruct((m, n), BF16),
        compiler_params=_cparams(("parallel", "arbitrary")),
        name="proj_qk",
    )(x2, w, gain, rope, bd)


def _proj_dt_kernel(x_ref, w_ref, wt_ref, out_ref, outt_ref):
    x = x_ref[...]
    out_ref[...] = jnp.dot(x, w_ref[...], preferred_element_type=F32)
    outt_ref[...] = lax.dot_general(wt_ref[...], x, (((1,), (1,)), ((), ())), preferred_element_type=F32)


def _proj_dt(x2, w_pad, w_t, *, tm):
    m, d = x2.shape
    nh = w_t.shape[0]
    return pl.pallas_call(
        _proj_dt_kernel,
        grid=(m // tm,),
        in_specs=[
            pl.BlockSpec((tm, d), lambda i: (i, 0)),
            pl.BlockSpec((d, LANES), lambda i: (0, 0)),
            pl.BlockSpec((nh, d), lambda i: (0, 0)),
        ],
        out_specs=[pl.BlockSpec((tm, LANES), lambda i: (i, 0)), pl.BlockSpec((nh, tm), lambda i: (0, i))],
        out_shape=[jax.ShapeDtypeStruct((m, LANES), F32), jax.ShapeDtypeStruct((nh, m), F32)],
        compiler_params=_cparams(("parallel",)),
        name="proj_dt",
    )(x2, w_pad, w_t)


def _outproj_kernel(h_ref, a_ref, s_ref, wa_ref, ws_ref, out_ref):
    out_ref[...] = (h_ref[...]
                    + jnp.dot(a_ref[...], wa_ref[...], preferred_element_type=F32)
                    + jnp.dot(s_ref[...], ws_ref[...], preferred_element_type=F32))


def _outproj(h1, attn, ssd, w_o, *, tm, tn):
    m, d = h1.shape
    k = attn.shape[1]
    return pl.pallas_call(
        _outproj_kernel,
        grid=(m // tm, d // tn),
        in_specs=[
            pl.BlockSpec((tm, tn), lambda i, j: (i, j)),
            pl.BlockSpec((tm, k), lambda i, j: (i, 0)),
            pl.BlockSpec((tm, k), lambda i, j: (i, 0)),
            pl.BlockSpec((k, tn), lambda i, j: (0, j)),
            pl.BlockSpec((k, tn), lambda i, j: (1, j)),
        ],
        out_specs=pl.BlockSpec((tm, tn), lambda i, j: (i, j)),
        out_shape=jax.ShapeDtypeStruct((m, d), F32),
        compiler_params=_cparams(("parallel", "arbitrary")),
        name="outproj",
    )(h1, attn, ssd, w_o, w_o)


def _attn_kernel(qa_tab, ka_tab, qb_tab, kb_tab, q_ref, k_ref, vt_ref, km_ref, vmt_ref, lam_ref, gain_ref,
                 out_ref, qs_sc, sm_sc, s0_sc, s1_sc, mx0_sc, mx1_sc, m_sc, l_sc, acc_sc,
                 *, tq, dg, n_steps, lam_init):
    g = pl.program_id(2)
    qa = qa_tab[g]
    ka = ka_tab[g]
    qb = qb_tab[g]
    kb = kb_tab[g]
    has_a = g < n_steps
    has_b = g >= 1
    even = (g % 2) == 0
    nt = (((1,), (1,)), ((), ()))

    full_groups = [(slice(0, 2 * tq), tq)]
    diag_groups = [(slice(sub * tq + j * dg, sub * tq + (j + 1) * dg), (j + 1) * dg)
                   for sub in range(2) for j in range(tq // dg)]

    def stage_a(buf, diagonal):
        s_out, mx_out = buf
        for cols, nk in (diag_groups if diagonal else full_groups):
            s_t = lax.dot_general(k_ref[0:nk, :], qs_sc[cols, :], nt, preferred_element_type=F32)
            if diagonal:
                key = lax.broadcasted_iota(jnp.int32, s_t.shape, 0)
                qry = lax.broadcasted_iota(jnp.int32, s_t.shape, 1) + (nk - dg)
                s_t = jnp.where(key // CHUNK <= qry // CHUNK, s_t, -jnp.inf)
            s_out[0:nk, cols] = s_t
            mx_out[:, cols] = jnp.max(s_t, axis=0, keepdims=True)

    def stage_b(buf, last):
        s_in, mx_in = buf
        vt_aug = jnp.concatenate([vt_ref[...], jnp.ones((16, vt_ref.shape[1]), BF16)], axis=0)
        for cols, nk in (diag_groups if last else full_groups):
            m_prev = m_sc[:, cols]
            m_new = jnp.maximum(m_prev, mx_in[:, cols])
            alpha = jnp.exp2(m_prev - m_new)
            p = jnp.exp2(s_in[0:nk, cols] - m_new).astype(BF16)
            pv = jnp.dot(vt_aug[:, 0:nk], p, preferred_element_type=F32)
            l_sc[:, cols] = alpha * l_sc[:, cols] + pv[V_HEAD_DIM:V_HEAD_DIM + 1, :]
            acc_sc[:, cols] = alpha * acc_sc[:, cols] + pv[0:V_HEAD_DIM, :]
            m_sc[:, cols] = m_new
        if last:
            lq1, lk1, lq2, lk2 = lam_ref[0:1, :], lam_ref[1:2, :], lam_ref[2:3, :], lam_ref[3:4, :]
            lam = (jnp.exp(jnp.sum(lq1 * lk1, axis=-1, keepdims=True))
                   - jnp.exp(jnp.sum(lq2 * lk2, axis=-1, keepdims=True)) + lam_init)
            o = acc_sc[...] / l_sc[...]
            o = o[:, 0:tq] - lam * o[:, tq:2 * tq]
            ms = jnp.mean(o * o, axis=0, keepdims=True)
            o = o * lax.rsqrt(ms + EPS) * (gain_ref[...] * (1.0 - lam_init))
            out_ref[...] = o.T.astype(out_ref.dtype)

    @pl.when(has_b & (kb == 0))
    def _():
        s_t = sm_sc[...]
        m0 = jnp.max(s_t, axis=0, keepdims=True)
        p = jnp.exp2(s_t - m0)
        m_sc[...] = m0
        l_sc[...] = jnp.sum(p, axis=0, keepdims=True)
        acc_sc[...] = jnp.dot(vmt_ref[...], p.astype(BF16), preferred_element_type=F32)

    @pl.when(has_a & (ka == 0))
    def _():
        q = q_ref[...]
        lane = lax.broadcasted_iota(jnp.int32, q.shape, 1)
        zero = jnp.zeros_like(q)
        qs_sc[0:tq, :] = jnp.where(lane < DIFF_HEAD_DIM, q, zero)
        qs_sc[tq:2 * tq, :] = jnp.where(lane >= DIFF_HEAD_DIM, q, zero)
        s_t = lax.dot_general(km_ref[...], qs_sc[...], nt, preferred_element_type=F32)
        key = lax.broadcasted_iota(jnp.int32, s_t.shape, 0)
        sm_sc[...] = jnp.where(key < N_META, s_t, -jnp.inf)

    buf0, buf1 = (s0_sc, mx0_sc), (s1_sc, mx1_sc)
    a_diag = ka == qa
    b_last = kb == qb
    both = has_a & has_b
    for parity, (buf_a, buf_b) in enumerate(((buf0, buf1), (buf1, buf0))):
        par = even if parity == 0 else jnp.logical_not(even)

        @pl.when(par & both & jnp.logical_not(a_diag) & jnp.logical_not(b_last))
        def _(buf_a=buf_a, buf_b=buf_b):
            stage_a(buf_a, False)
            stage_b(buf_b, False)

        @pl.when(par & both & a_diag & jnp.logical_not(b_last))
        def _(buf_a=buf_a, buf_b=buf_b):
            stage_a(buf_a, True)
            stage_b(buf_b, False)

        @pl.when(par & both & b_last)
        def _(buf_a=buf_a, buf_b=buf_b):
            stage_a(buf_a, False)
            stage_b(buf_b, True)

        @pl.when(par & has_a & jnp.logical_not(has_b))
        def _(buf_a=buf_a):
            stage_a(buf_a, True)

        @pl.when(par & has_b & jnp.logical_not(has_a))
        def _(buf_b=buf_b):
            stage_b(buf_b, True)


def _attention(qk, vt, kmeta, vmeta_t, lam_vecs, gain_col, *, bsz, seq, tq, dg, lam_init):
    nq = seq // tq
    h = N_DIFF_HEADS
    qi_list, ki_list = [], []
    for a in range(nq):
        for b in range(a + 1):
            qi_list.append(a)
            ki_list.append(b)
    n_steps = len(qi_list)
    ia = np.minimum(np.arange(n_steps + 1), n_steps - 1)
    ib = np.maximum(np.arange(n_steps + 1) - 1, 0)
    qi_arr, ki_arr = np.array(qi_list, np.int32), np.array(ki_list, np.int32)
    tabs = [jnp.asarray(t) for t in (qi_arr[ia], ki_arr[ia], qi_arr[ib], ki_arr[ib])]
    grid_spec = pltpu.PrefetchScalarGridSpec(
        num_scalar_prefetch=4,
        grid=(bsz, h, n_steps + 1),
        in_specs=[
            pl.BlockSpec((tq, V_HEAD_DIM), lambda b, hh, g, qa, ka, qb, kb: (b * nq + qa[g], hh)),
            pl.BlockSpec((tq, V_HEAD_DIM), lambda b, hh, g, qa, ka, qb, kb: (b * nq + ka[g], h + hh)),
            pl.BlockSpec((V_HEAD_DIM, tq), lambda b, hh, g, qa, ka, qb, kb: (hh, b * nq + kb[g])),
            pl.BlockSpec((LANES, V_HEAD_DIM), lambda b, hh, g, qa, ka, qb, kb: (0, hh)),
            pl.BlockSpec((V_HEAD_DIM, LANES), lambda b, hh, g, qa, ka, qb, kb: (hh, 0)),
            pl.BlockSpec((4, DIFF_HEAD_DIM), lambda b, hh, g, qa, ka, qb, kb: (0, 0)),
            pl.BlockSpec((V_HEAD_DIM, 1), lambda b, hh, g, qa, ka, qb, kb: (0, 0)),
        ],
        out_specs=pl.BlockSpec((tq, V_HEAD_DIM), lambda b, hh, g, qa, ka, qb, kb: (b * nq + qb[g], hh)),
        scratch_shapes=[
            pltpu.VMEM((2 * tq, V_HEAD_DIM), BF16),
            pltpu.VMEM((LANES, 2 * tq), F32),
            pltpu.VMEM((tq, 2 * tq), F32),
            pltpu.VMEM((tq, 2 * tq), F32),
            pltpu.VMEM((1, 2 * tq), F32),
            pltpu.VMEM((1, 2 * tq), F32),
            pltpu.VMEM((1, 2 * tq), F32),
            pltpu.VMEM((1, 2 * tq), F32),
            pltpu.VMEM((V_HEAD_DIM, 2 * tq), F32),
        ],
    )
    return pl.pallas_call(
        functools.partial(_attn_kernel, tq=tq, dg=dg, n_steps=n_steps, lam_init=lam_init),
        grid_spec=grid_spec,
        out_shape=jax.ShapeDtypeStruct((bsz * seq, h * V_HEAD_DIM), BF16),
        compiler_params=_cparams(("parallel", "parallel", "arbitrary")),
        name="attn",
    )(*tabs, qk, qk, vt, kmeta, vmeta_t, lam_vecs, gain_col)


def _ssd_kernel(*refs, t_blk, bsz, n_cast):
    dtt_refs = refs[:bsz]
    cast_in = refs[bsz:bsz + n_cast]
    (zx_ref, dt_ref, xbcm_ref, dtm_ref, dttm_ref,
     convw_ref, convb_ref, dtb_ref, dtbt_ref, alog_ref, alogt_ref, dskip_ref, nw_ref,
     expand_ref, tril_ref, triu2_ref, shift_ref, out_ref) = refs[bsz + n_cast:bsz + n_cast + 18]
    cast_out = refs[bsz + n_cast + 18:bsz + 2 * n_cast + 18]
    xs_sc, xm_sc, state_sc = refs[bsz + 2 * n_cast + 18:]
    for src, dst in zip(cast_in, cast_out):
        dst[...] = src[...].astype(BF16)
    n_heads, p_dim, n_state, n_groups = N_SSD_HEADS, SSD_HEAD_DIM, D_STATE, N_SSD_GROUPS
    width = n_heads * p_dim
    gw = width // n_groups
    n_pairs = n_heads // 2
    nt = (((1,), (1,)), ((), ()))
    expand = expand_ref[...]
    a_row = -jnp.exp(alog_ref[...])
    a_col = -jnp.exp(alogt_ref[...])
    d_full = dskip_ref[...]
    lane = lax.broadcasted_iota(jnp.int32, (CHUNK, LANES), 1)
    row = lax.broadcasted_iota(jnp.int32, (CHUNK, LANES), 0)
    left = lane < p_dim
    causal2 = (lane & (CHUNK - 1)) <= row

    def chunk(win_f32, st_ref, dt_raw, dtt_raw, pad_rows, z, out_r0, b=0):
        win = win_f32.astype(BF16)
        taps = jnp.dot(shift_ref[...], win, preferred_element_type=F32)
        conv = convb_ref[...] + win_f32[HALO:, :] * convw_ref[CONV_WIDTH - 1:CONV_WIDTH, :]
        for k in range(CONV_WIDTH - 1):
            conv = conv + taps[k * CHUNK:(k + 1) * CHUNK, :] * convw_ref[k:k + 1, :]
        xc = _silu(conv)
        x_s = xc[:, :width]
        dt = jax.nn.softplus(dt_raw + dtb_ref[...])
        dtt = jax.nn.softplus(dtt_raw + dtbt_ref[...])
        if pad_rows:
            dt = jnp.where(row >= pad_rows, dt, 0.0)
            dtt = jnp.where(lax.broadcasted_iota(jnp.int32, dtt.shape, 1) >= pad_rows, dtt, 0.0)
        a_pieces = jnp.dot(tril_ref[...], jnp.concatenate(_split_bf16(dt * a_row, 3), axis=1),
                           preferred_element_type=F32)
        a_cs = a_pieces[:, 0:LANES] + a_pieces[:, LANES:2 * LANES] + a_pieces[:, 2 * LANES:3 * LANES]
        a_cs_t2 = _dot_f32_by_01(dtt * a_col, triu2_ref[...])
        spread = jnp.dot(jnp.concatenate(_split_bf16(a_cs, 3) + _split_bf16(dt, 3), axis=0), expand,
                         preferred_element_type=F32)
        a_full = spread[0:CHUNK] + spread[CHUNK:2 * CHUNK] + spread[2 * CHUNK:3 * CHUNK]
        dt_full = spread[3 * CHUNK:4 * CHUNK] + spread[4 * CHUNK:5 * CHUNK] + spread[5 * CHUNK:6 * CHUNK]
        a_last = a_full[CHUNK - 1:CHUNK, :]
        xdt = x_s * dt_full
        w_state = (xdt * jnp.exp(a_last - a_full)).astype(BF16)
        xdt_b = xdt.astype(BF16)
        zero_b = jnp.zeros((CHUNK, LANES), BF16)
        y_parts = []
        for g in range(n_groups):
            b_g = xc[:, width + g * n_state: width + (g + 1) * n_state]
            c_g = xc[:, width + (n_groups + g) * n_state: width + (n_groups + g + 1) * n_state]
            b_gb = b_g.astype(BF16)
            c_gb = c_g.astype(BF16)
            st = st_ref[g]
            if out_r0 is not None:
                cb2 = lax.dot_general(c_gb, jnp.concatenate([b_gb, b_gb], axis=0), nt,
                                      preferred_element_type=F32)
                y_off = jnp.dot(c_gb, st.astype(BF16), preferred_element_type=F32)
                y_g = y_off * jnp.exp(a_full[:, g * gw:(g + 1) * gw])
                diag = []
                for pp in range(n_pairs // n_groups):
                    pr = g * (n_pairs // n_groups) + pp
                    colb = a_full[:, pr * LANES:(pr + 1) * LANES]
                    rowb = jnp.where(left, a_cs_t2[2 * pr:2 * pr + 1, :], a_cs_t2[2 * pr + 1:2 * pr + 2, :])
                    dec = jnp.exp(jnp.where(causal2, colb - rowb, -jnp.inf))
                    m_pair = (cb2 * dec).astype(BF16)
                    xp = xdt_b[:, pr * LANES:(pr + 1) * LANES]
                    rhs = jnp.concatenate([jnp.where(left, xp, zero_b), jnp.where(left, zero_b, xp)], axis=0)
                    diag.append(jnp.dot(m_pair, rhs, preferred_element_type=F32))
                y_parts.append(y_g + jnp.concatenate(diag, axis=1))
            new_st = st * jnp.exp(a_last[:, g * gw:(g + 1) * gw]) + jnp.dot(
                b_g.T.astype(BF16), w_state[:, g * gw:(g + 1) * gw], preferred_element_type=F32)
            st_ref[g] = new_st
        if out_r0 is not None:
            y = jnp.concatenate(y_parts, axis=1) + x_s * d_full
            gated = y * _silu(z)
            out_ref[b, out_r0:out_r0 + CHUNK, :] = _rms(gated, nw_ref[...]).astype(out_ref.dtype)

    @pl.when(pl.program_id(0) == 0)
    def _():
        state_sc[...] = jnp.zeros_like(state_sc)
        xm_sc[0:HALO, :] = jnp.zeros((HALO, xm_sc.shape[1]), F32)
        xm_sc[HALO:HALO + CHUNK, :] = xbcm_ref[...].astype(F32)
        chunk(xm_sc[...], state_sc.at[0], dtm_ref[...], dttm_ref[...], CHUNK - N_META, None, None)
        for b in range(bsz):
            if b:
                state_sc[b] = state_sc[0]
            xs_sc[b, 0:HALO, :] = xm_sc[CHUNK:CHUNK + HALO, :]

    for b in range(bsz):
        xs_sc[b, HALO:HALO + t_blk, :] = zx_ref[b, :, width:].astype(F32)
    for c in range(t_blk // CHUNK):
        rows = slice(c * CHUNK, (c + 1) * CHUNK)
        for b in range(bsz):
            chunk(xs_sc[b, c * CHUNK:(c + 1) * CHUNK + HALO, :], state_sc.at[b], dt_ref[b, rows, :],
                  dtt_refs[b][:, rows], 0, zx_ref[b, rows, 0:width].astype(F32), c * CHUNK, b)
    for b in range(bsz):
        xs_sc[b, 0:HALO, :] = xs_sc[b, t_blk:t_blk + HALO, :]


def _ssd(zx, dt, dtt, xbc_m, dt_m, dtt_m, conv_w, conv_b, dt_bias, a_log, d_skip, norm_w, cast_weights,
         *, bsz, seq, t_blk):
    nb = seq // t_blk
    slab = lambda w: pl.BlockSpec((w.shape[0] // nb, w.shape[1]), lambda i: (i, 0))
    width = N_SSD_HEADS * SSD_HEAD_DIM
    cdim = zx.shape[1] - width
    nh = N_SSD_HEADS

    def pad_row(v):
        return jnp.pad(v.reshape(1, nh), ((0, 0), (0, LANES - nh)))

    hh = np.arange(LANES)[:, None]
    ll = np.arange(width)[None, :]
    one_hot = lambda m: jnp.asarray(m.astype(np.float32), dtype=BF16)
    expand = one_hot(ll // SSD_HEAD_DIM == hh)
    ii = np.arange(CHUNK)
    tril = one_hot(ii[None, :] <= ii[:, None])
    triu2 = one_hot(np.tile(ii[:, None] <= ii[None, :], (1, 2)))
    jj = np.arange(HALO + CHUNK)
    shift = one_hot(np.concatenate([jj[None, :] == ii[:, None] + HALO - (CONV_WIDTH - 1) + k
                                    for k in range(CONV_WIDTH - 1)]))
    full = lambda shape: pl.BlockSpec(shape, lambda i: tuple(0 for _ in shape))
    out, *cast = pl.pallas_call(
        functools.partial(_ssd_kernel, t_blk=t_blk, bsz=bsz, n_cast=len(cast_weights)),
        grid=(nb,),
        in_specs=[pl.BlockSpec((nh, t_blk), functools.partial(lambda i, b: (0, b * nb + i), b=b))
                  for b in range(bsz)] + [slab(w) for w in cast_weights] + [
            pl.BlockSpec((bsz, t_blk, width + cdim), lambda i: (0, i, 0)),
            pl.BlockSpec((bsz, t_blk, LANES), lambda i: (0, i, 0)),
            full((CHUNK, cdim)), full((CHUNK, LANES)), full((nh, CHUNK)),
            full((CONV_WIDTH, cdim)), full((1, cdim)), full((1, LANES)), full((nh, 1)),
            full((1, LANES)), full((nh, 1)), full((1, width)), full((1, width)),
            full((LANES, width)), full((CHUNK, CHUNK)), full((CHUNK, 2 * CHUNK)), full(shift.shape),
        ],
        out_specs=[pl.BlockSpec((bsz, t_blk, width), lambda i: (0, i, 0))] + [slab(w) for w in cast_weights],
        out_shape=[jax.ShapeDtypeStruct((bsz, seq, width), BF16)]
        + [jax.ShapeDtypeStruct(w.shape, BF16) for w in cast_weights],
        scratch_shapes=[
            pltpu.VMEM((bsz, t_blk + HALO, cdim), F32),
            pltpu.VMEM((CHUNK + HALO, cdim), F32),
            pltpu.VMEM((bsz, N_SSD_GROUPS, D_STATE, width // N_SSD_GROUPS), F32),
        ],
        compiler_params=_cparams(("arbitrary",)),
        name="ssd",
    )(*([dtt] * bsz), *cast_weights, zx.reshape(bsz, seq, width + cdim), dt.reshape(bsz, seq, LANES),
      xbc_m, dt_m, dtt_m,
      conv_w, conv_b.reshape(1, cdim), pad_row(dt_bias), dt_bias.reshape(nh, 1), pad_row(a_log),
      a_log.reshape(nh, 1), jnp.repeat(d_skip, SSD_HEAD_DIM).reshape(1, width),
      norm_w.reshape(1, width), expand, tril, triu2, shift)
    return out.reshape(bsz * seq, width), cast


def _rope_tables(n_pos):
    inv = jnp.power(ROPE_THETA, -jnp.arange(0, ROT_DIM, 2, dtype=F32) / ROT_DIM)
    ang = jnp.arange(n_pos, dtype=F32)[:, None] * inv[None, :]
    cos, sin = jnp.cos(ang), jnp.sin(ang)
    half = ROT_DIM // 2
    r = np.arange(LANES) % DIFF_HEAD_DIM
    idx = jnp.asarray(r % half)
    lo = jnp.asarray(r < half)[None, :]
    hi = jnp.asarray((r >= half) & (r < ROT_DIM))[None, :]
    cos_t = jnp.where(lo | hi, cos[:, idx], 1.0)
    sin_lo = jnp.where(lo, -sin[:, idx], 0.0)
    sin_hi = jnp.where(hi, sin[:, idx], 0.0)
    return jnp.stack([cos_t, sin_lo, sin_hi])


def kernel(x, meta_tokens, ffn1_norm, ffn1_w_gate, ffn1_w_up, ffn1_w_down, mix_norm, w_in, q_norm, k_norm,
           lambda_q1, lambda_k1, lambda_q2, lambda_k2, attn_out_norm, conv_w, conv_b, dt_bias, a_log, d_skip,
           ssd_norm, w_out, ffn2_norm, ffn2_w_gate, ffn2_w_up, ffn2_w_down):
    bsz, seq, d = x.shape
    assert ffn1_norm.shape[0] == 1, "single-layer block"
    aw = N_DIFF_HEADS * V_HEAD_DIM
    sw = N_SSD_HEADS * SSD_HEAD_DIM
    cdim = sw + 2 * N_SSD_GROUPS * D_STATE
    lam_init = 0.8 - 0.6 * math.exp(-0.3 * 0)
    row = lambda v: v.reshape(1, -1)

    tn = 512
    w_in0 = w_in[0]
    w_dt = w_in0[:, 3 * aw + sw + cdim:].astype(BF16)
    w_dt_pad = jnp.pad(w_dt, ((0, 0), (0, LANES - N_SSD_HEADS)))
    w_dt_t = w_dt.T

    n_sub = aw // DIFF_HEAD_DIM
    q_gain = jnp.tile(q_norm[0], n_sub) * (DIFF_HEAD_DIM ** -0.5 * math.log2(math.e))
    k_gain = jnp.tile(k_norm[0], n_sub)
    qk_gain = jnp.concatenate([q_gain, k_gain]).reshape(2 * aw // tn, 1, tn)
    rope = _rope_tables(N_META + seq)
    gi = np.arange(2 * LANES) // DIFF_HEAD_DIM
    bd = jnp.asarray((gi[:, None] == gi[None, :]).astype(np.float32) / DIFF_HEAD_DIM, dtype=BF16)

    hn_m, wg1, wu1, wd1 = _ffn_meta(meta_tokens.astype(F32), row(ffn1_norm[0]), ffn1_w_gate[0],
                                    ffn1_w_up[0], ffn1_w_down[0], row(mix_norm[0]), tf=FFN_TF)
    w_in_b = w_in0[:, :3 * aw + sw + cdim].astype(BF16)
    x2 = x.reshape(bsz * seq, d)
    h1, hn = _ffn(x2, row(ffn1_norm[0]), wg1, wu1, wd1, row(mix_norm[0]), tm=FFN_TM, tf=FFN_TF)

    qk = _proj_qk(hn, w_in_b, qk_gain, rope[:, N_META:], bd, tm=1024, tn=tn, n=2 * aw)
    qk_m = _proj_qk(hn_m, w_in_b, qk_gain, rope[:, :N_META], bd, tm=N_META, tn=tn, n=2 * aw)
    w_v_t = w_in0[:, 2 * aw:3 * aw].T.astype(BF16)
    v_t = _proj(hn, w_v_t, tm=1024, tn=tn, col0=0, n=aw, name="proj_vt", transpose=True)
    vzx_m = _proj(hn_m, w_in_b, tm=N_META, tn=tn, col0=2 * aw, n=aw + sw + cdim, name="proj_vzx_meta")
    zx = _proj(hn, w_in_b, tm=1024, tn=tn, col0=3 * aw, n=sw + cdim, name="proj_zx")
    v_t_m, xbc_m = vzx_m[:, :aw].T, vzx_m[:, aw + sw:]
    dt, dtt = _proj_dt(hn, w_dt_pad, w_dt_t, tm=1024)
    dt_m, dtt_m = _proj_dt(hn_m, w_dt_pad, w_dt_t, tm=N_META)

    kmeta = jnp.pad(qk_m[:, aw:], ((0, LANES - N_META), (0, 0)))
    vmeta_t = jnp.pad(v_t_m, ((0, 0), (0, LANES - N_META)))
    lam_vecs = jnp.stack([lambda_q1[0], lambda_k1[0], lambda_q2[0], lambda_k2[0]]).astype(F32)
    attn = _attention(qk, v_t, kmeta, vmeta_t, lam_vecs, attn_out_norm[0].reshape(V_HEAD_DIM, 1),
                      bsz=bsz, seq=seq, tq=1024, dg=256, lam_init=lam_init)

    lead = CHUNK - N_META
    ssd, (wg2, wu2, wd2, w_o) = _ssd(
        zx, dt, dtt,
        jnp.pad(xbc_m, ((lead, 0), (0, 0))), jnp.pad(dt_m, ((lead, 0), (0, 0))), jnp.pad(dtt_m, ((0, 0), (lead, 0))),
        conv_w[0], conv_b[0], dt_bias[0], a_log[0], d_skip[0], ssd_norm[0],
        (ffn2_w_gate[0], ffn2_w_up[0], ffn2_w_down[0], w_out[0]),
        bsz=bsz, seq=seq, t_blk=256)

    h2 = _outproj(h1, attn, ssd, w_o, tm=512, tn=d)
    (out,) = _ffn(h2, row(ffn2_norm[0]), wg2, wu2, wd2, tm=FFN_TM, tf=FFN_TF)
    return out.reshape(bsz, seq, d)
```

```python
import functools
import math

import jax
import jax.numpy as jnp
import numpy as np
from jax import lax
from jax.experimental import pallas as pl
from jax.experimental.pallas import tpu as pltpu

F32 = jnp.float32
BF16 = jnp.bfloat16

EPS = 1e-6
CHUNK = 64
N_META = 16
N_DIFF_HEADS = 8
DIFF_HEAD_DIM = 64
V_HEAD_DIM = 128
ROT_DIM = 16
ROPE_THETA = 500000.0
SSD_HEAD_DIM = 64
N_SSD_HEADS = 16
N_SSD_GROUPS = 2
D_STATE = 128
CONV_WIDTH = 4
HALO = 64
LANES = 128
VMEM_LIMIT = 56 * 1024 * 1024
FFN_TM = 512
FFN_TF = 512


def _cparams(sem):
    return pltpu.CompilerParams(dimension_semantics=sem, vmem_limit_bytes=VMEM_LIMIT)


def _silu(v):
    half = 0.5 * v
    return half + half * jnp.tanh(half)


def _split_bf16(a, pieces):
    out = []
    for _ in range(pieces - 1):
        p = a.astype(BF16)
        out.append(p)
        a = a - p.astype(F32)
    out.append(a.astype(BF16))
    return out


def _dot_f32_by_01(a, m01, pieces=3):
    return sum(jnp.dot(p, m01, preferred_element_type=F32) for p in _split_bf16(a, pieces))


def _rms(v, w):
    ms = jnp.mean(v * v, axis=-1, keepdims=True)
    return v * lax.rsqrt(ms + EPS) * w


def _swiglu_step(xn, wg_ref, wu_ref, wd_ref):
    g = jnp.dot(xn, wg_ref[...], preferred_element_type=F32)
    u = jnp.dot(xn, wu_ref[...], preferred_element_type=F32)
    a = (_silu(g) * u).astype(BF16)
    return jnp.dot(a, wd_ref[...], preferred_element_type=F32)


def _qk_norm_rope(acc, gain, rope_ref, bd):
    cos, sin_lo, sin_hi = rope_ref[0], rope_ref[1], rope_ref[2]
    half = ROT_DIM // 2
    wide = bd.shape[0]
    out = []
    for c in range(acc.shape[1] // wide):
        y = acc[:, c * wide:(c + 1) * wide]
        ms = sum(jnp.dot(p, bd, preferred_element_type=F32) for p in _split_bf16(y * y, 2))
        yn = y * lax.rsqrt(ms + EPS) * gain[:, c * wide:(c + 1) * wide]
        for s in range(wide // LANES):
            v = yn[:, s * LANES:(s + 1) * LANES]
            out.append(v * cos + pltpu.roll(v, half, 1) * sin_hi + pltpu.roll(v, LANES - half, 1) * sin_lo)
    return jnp.concatenate(out, axis=1)


def _ffn_kernel(x_ref, nw_ref, wg_ref, wu_ref, wd_ref, *rest, with_norm):
    if with_norm:
        pnw_ref, out_ref, hn_ref, xn_sc = rest
    else:
        out_ref, xn_sc = rest
    j = pl.program_id(1)

    @pl.when(j == 0)
    def _():
        xn_sc[...] = _rms(x_ref[...], nw_ref[...]).astype(BF16)
        out_ref[...] = jnp.zeros_like(out_ref)

    out_ref[...] += _swiglu_step(xn_sc[...], wg_ref, wu_ref, wd_ref)

    @pl.when(j == pl.num_programs(1) - 1)
    def _():
        h = x_ref[...] + 0.5 * out_ref[...]
        out_ref[...] = h
        if with_norm:
            hn_ref[...] = _rms(h, pnw_ref[...]).astype(BF16)


def _ffn(x2, nw, wg, wu, wd, post_nw=None, *, tm, tf):
    m, d = x2.shape
    f = wg.shape[1]
    with_norm = post_nw is not None
    row_spec = pl.BlockSpec((tm, d), lambda i, j: (i, 0))
    vec_spec = pl.BlockSpec((1, d), lambda i, j: (0, 0))
    in_specs = [row_spec, vec_spec,
                pl.BlockSpec((d, tf), lambda i, j: (0, j)),
                pl.BlockSpec((d, tf), lambda i, j: (0, j)),
                pl.BlockSpec((tf, d), lambda i, j: (j, 0))]
    args = [x2, nw, wg, wu, wd]
    out_specs, out_shape = [row_spec], [jax.ShapeDtypeStruct((m, d), F32)]
    if with_norm:
        in_specs.append(vec_spec)
        args.append(post_nw)
        out_specs.append(row_spec)
        out_shape.append(jax.ShapeDtypeStruct((m, d), BF16))
    return pl.pallas_call(
        functools.partial(_ffn_kernel, with_norm=with_norm),
        grid=(m // tm, f // tf),
        in_specs=in_specs,
        out_specs=out_specs,
        out_shape=out_shape,
        scratch_shapes=[pltpu.VMEM((tm, d), BF16)],
        compiler_params=_cparams(("parallel", "arbitrary")),
        name="ffn",
    )(*args)


def _ffn_meta_kernel(x_ref, nw_ref, wg_ref, wu_ref, wd_ref, pnw_ref,
                     hn_ref, wgb_ref, wub_ref, wdb_ref, xn_sc, acc_sc):
    j = pl.program_id(0)

    @pl.when(j == 0)
    def _():
        xn_sc[...] = _rms(x_ref[...], nw_ref[...]).astype(BF16)
        acc_sc[...] = jnp.zeros_like(acc_sc)

    wgb_ref[...] = wg_ref[...].astype(BF16)
    wub_ref[...] = wu_ref[...].astype(BF16)
    wdb_ref[...] = wd_ref[...].astype(BF16)
    acc_sc[...] += _swiglu_step(xn_sc[...], wgb_ref, wub_ref, wdb_ref)

    @pl.when(j == pl.num_programs(0) - 1)
    def _():
        hn_ref[...] = _rms(x_ref[...] + 0.5 * acc_sc[...], pnw_ref[...]).astype(BF16)


def _ffn_meta(xm, nw, wg, wu, wd, post_nw, *, tf):
    m, d = xm.shape
    n_ff = wg.shape[1] // tf
    vec_spec = pl.BlockSpec((1, d), lambda j: (0, 0))
    row_spec = pl.BlockSpec((m, d), lambda j: (0, 0))
    col_tile = pl.BlockSpec((d, tf), lambda j: (0, j))
    row_tile = pl.BlockSpec((tf, d), lambda j: (j, 0))
    return pl.pallas_call(
        _ffn_meta_kernel,
        grid=(n_ff,),
        in_specs=[row_spec, vec_spec, col_tile, col_tile, row_tile, vec_spec],
        out_specs=[row_spec, col_tile, col_tile, row_tile],
        out_shape=[jax.ShapeDtypeStruct((m, d), BF16), jax.ShapeDtypeStruct(wg.shape, BF16),
                   jax.ShapeDtypeStruct(wu.shape, BF16), jax.ShapeDtypeStruct(wd.shape, BF16)],
        scratch_shapes=[pltpu.VMEM((m, d), BF16), pltpu.VMEM((m, d), F32)],
        compiler_params=_cparams(("arbitrary",)),
        name="ffn_meta_cast",
    )(xm, nw, wg, wu, wd, post_nw)


_NT = (((1,), (1,)), ((), ()))


def _proj_kernel(x_ref, w_ref, out_ref, *, transpose):
    if transpose:
        acc = lax.dot_general(w_ref[...], x_ref[...], _NT, preferred_element_type=F32)
    else:
        acc = lax.dot_general(x_ref[...], w_ref[...], _NT, preferred_element_type=F32)
    out_ref[...] = acc.astype(out_ref.dtype)


def _proj(x2, w_t, *, tm, tn, row0, n, name, transpose=False):
    m, d = x2.shape
    r0 = row0 // tn
    if transpose:
        out_spec, out_shape = pl.BlockSpec((tn, tm), lambda i, j: (j, i)), (n, m)
    else:
        out_spec, out_shape = pl.BlockSpec((tm, tn), lambda i, j: (i, j)), (m, n)
    return pl.pallas_call(
        functools.partial(_proj_kernel, transpose=transpose),
        grid=(m // tm, n // tn),
        in_specs=[pl.BlockSpec((tm, d), lambda i, j: (i, 0)), pl.BlockSpec((tn, d), lambda i, j: (r0 + j, 0))],
        out_specs=out_spec,
        out_shape=jax.ShapeDtypeStruct(out_shape, BF16),
        compiler_params=_cparams(("parallel", "arbitrary")),
        name=name,
    )(x2, w_t)


def _proj_qk_kernel(x_ref, w_ref, gain_ref, rope_ref, bd_ref, out_ref):
    acc = lax.dot_general(x_ref[...], w_ref[...], _NT, preferred_element_type=F32)
    out_ref[...] = _qk_norm_rope(acc, gain_ref[0], rope_ref, bd_ref[...]).astype(out_ref.dtype)


def _proj_qk(x2, w_t, gain, rope, bd, *, tm, tn, n):
    m, d = x2.shape
    n_tab = rope.shape[1] // tm
    return pl.pallas_call(
        _proj_qk_kernel,
        grid=(m // tm, n // tn),
        in_specs=[
            pl.BlockSpec((tm, d), lambda i, j: (i, 0)),
            pl.BlockSpec((tn, d), lambda i, j: (j, 0)),
            pl.BlockSpec((1, 1, tn), lambda i, j: (j, 0, 0)),
            pl.BlockSpec((3, tm, LANES), lambda i, j: (0, i % n_tab, 0)),
            pl.BlockSpec(bd.shape, lambda i, j: (0, 0)),
        ],
        out_specs=pl.BlockSpec((tm, tn), lambda i, j: (i, j)),
        out_shape=jax.ShapeDtypeStruct((m, n), BF16),
        compiler_params=_cparams(("parallel", "arbitrary")),
        name="proj_qk",
    )(x2, w_t, gain, rope, bd)


def _proj_dt_kernel(x_ref, w_ref, wt_ref, out_ref, outt_ref):
    x = x_ref[...]
    out_ref[...] = jnp.dot(x, w_ref[...], preferred_element_type=F32)
    outt_ref[...] = lax.dot_general(wt_ref[...], x, (((1,), (1,)), ((), ())), preferred_element_type=F32)


def _proj_dt(x2, w_pad, w_t, *, tm):
    m, d = x2.shape
    nh = w_t.shape[0]
    return pl.pallas_call(
        _proj_dt_kernel,
        grid=(m // tm,),
        in_specs=[
            pl.BlockSpec((tm, d), lambda i: (i, 0)),
            pl.BlockSpec((d, LANES), lambda i: (0, 0)),
            pl.BlockSpec((nh, d), lambda i: (0, 0)),
        ],
        out_specs=[pl.BlockSpec((tm, LANES), lambda i: (i, 0)), pl.BlockSpec((nh, tm), lambda i: (0, i))],
        out_shape=[jax.ShapeDtypeStruct((m, LANES), F32), jax.ShapeDtypeStruct((nh, m), F32)],
        compiler_params=_cparams(("parallel",)),
        name="proj_dt",
    )(x2, w_pad, w_t)


def _outproj_kernel(h_ref, a_ref, s_ref, wa_ref, ws_ref, out_ref):
    out_ref[...] = (h_ref[...]
                    + jnp.dot(a_ref[...], wa_ref[...], preferred_element_type=F32)
                    + jnp.dot(s_ref[...], ws_ref[...], preferred_element_type=F32))


def _outproj(h1, attn, ssd, w_o, *, tm, tn):
    m, d = h1.shape
    k = attn.shape[1]
    return pl.pallas_call(
        _outproj_kernel,
        grid=(m // tm, d // tn),
        in_specs=[
            pl.BlockSpec((tm, tn), lambda i, j: (i, j)),
            pl.BlockSpec((tm, k), lambda i, j: (i, 0)),
            pl.BlockSpec((tm, k), lambda i, j: (i, 0)),
            pl.BlockSpec((k, tn), lambda i, j: (0, j)),
            pl.BlockSpec((k, tn), lambda i, j: (1, j)),
        ],
        out_specs=pl.BlockSpec((tm, tn), lambda i, j: (i, j)),
        out_shape=jax.ShapeDtypeStruct((m, d), F32),
        compiler_params=_cparams(("parallel", "arbitrary")),
        name="outproj",
    )(h1, attn, ssd, w_o, w_o)


def _attn_kernel(qa_tab, ka_tab, qb_tab, kb_tab, q_ref, k_ref, vt_ref, km_ref, vmt_ref, lam_ref, gain_ref,
                 out_ref, qs_sc, sm_sc, s0_sc, s1_sc, mx0_sc, mx1_sc, m_sc, l_sc, acc_sc,
                 *, tq, dg, n_steps, lam_init):
    g = pl.program_id(2)
    qa = qa_tab[g]
    ka = ka_tab[g]
    qb = qb_tab[g]
    kb = kb_tab[g]
    has_a = g < n_steps
    has_b = g >= 1
    even = (g % 2) == 0
    nt = (((1,), (1,)), ((), ()))

    full_groups = [(slice(0, 2 * tq), tq)]
    diag_groups = [(slice(sub * tq + j * dg, sub * tq + (j + 1) * dg), (j + 1) * dg)
                   for sub in range(2) for j in range(tq // dg)]

    def stage_a(buf, diagonal):
        s_out, mx_out = buf
        for cols, nk in (diag_groups if diagonal else full_groups):
            s_t = lax.dot_general(k_ref[0:nk, :], qs_sc[cols, :], nt, preferred_element_type=F32)
            if diagonal:
                key = lax.broadcasted_iota(jnp.int32, s_t.shape, 0)
                qry = lax.broadcasted_iota(jnp.int32, s_t.shape, 1) + (nk - dg)
                s_t = jnp.where(key // CHUNK <= qry // CHUNK, s_t, -jnp.inf)
            s_out[0:nk, cols] = s_t
            mx_out[:, cols] = jnp.max(s_t, axis=0, keepdims=True)

    def stage_b(buf, last):
        s_in, mx_in = buf
        vt_aug = jnp.concatenate([vt_ref[...], jnp.ones((16, vt_ref.shape[1]), BF16)], axis=0)
        for cols, nk in (diag_groups if last else full_groups):
            m_prev = m_sc[:, cols]
            m_new = jnp.maximum(m_prev, mx_in[:, cols])
            alpha = jnp.exp2(m_prev - m_new)
            p = jnp.exp2(s_in[0:nk, cols] - m_new).astype(BF16)
            pv = jnp.dot(vt_aug[:, 0:nk], p, preferred_element_type=F32)
            l_sc[:, cols] = alpha * l_sc[:, cols] + pv[V_HEAD_DIM:V_HEAD_DIM + 1, :]
            acc_sc[:, cols] = alpha * acc_sc[:, cols] + pv[0:V_HEAD_DIM, :]
            m_sc[:, cols] = m_new
        if last:
            lq1, lk1, lq2, lk2 = lam_ref[0:1, :], lam_ref[1:2, :], lam_ref[2:3, :], lam_ref[3:4, :]
            lam = (jnp.exp(jnp.sum(lq1 * lk1, axis=-1, keepdims=True))
                   - jnp.exp(jnp.sum(lq2 * lk2, axis=-1, keepdims=True)) + lam_init)
            o = acc_sc[...] / l_sc[...]
            o = o[:, 0:tq] - lam * o[:, tq:2 * tq]
            ms = jnp.mean(o * o, axis=0, keepdims=True)
            o = o * lax.rsqrt(ms + EPS) * (gain_ref[...] * (1.0 - lam_init))
            out_ref[...] = o.T.astype(out_ref.dtype)

    @pl.when(has_b & (kb == 0))
    def _():
        s_t = sm_sc[...]
        m0 = jnp.max(s_t, axis=0, keepdims=True)
        p = jnp.exp2(s_t - m0)
        m_sc[...] = m0
        l_sc[...] = jnp.sum(p, axis=0, keepdims=True)
        acc_sc[...] = jnp.dot(vmt_ref[...], p.astype(BF16), preferred_element_type=F32)

    @pl.when(has_a & (ka == 0))
    def _():
        q = q_ref[...]
        lane = lax.broadcasted_iota(jnp.int32, q.shape, 1)
        zero = jnp.zeros_like(q)
        qs_sc[0:tq, :] = jnp.where(lane < DIFF_HEAD_DIM, q, zero)
        qs_sc[tq:2 * tq, :] = jnp.where(lane >= DIFF_HEAD_DIM, q, zero)
        s_t = lax.dot_general(km_ref[...], qs_sc[...], nt, preferred_element_type=F32)
        key = lax.broadcasted_iota(jnp.int32, s_t.shape, 0)
        sm_sc[...] = jnp.where(key < N_META, s_t, -jnp.inf)

    buf0, buf1 = (s0_sc, mx0_sc), (s1_sc, mx1_sc)
    a_diag = ka == qa
    b_last = kb == qb
    both = has_a & has_b
    for parity, (buf_a, buf_b) in enumerate(((buf0, buf1), (buf1, buf0))):
        par = even if parity == 0 else jnp.logical_not(even)

        @pl.when(par & both & jnp.logical_not(a_diag) & jnp.logical_not(b_last))
        def _(buf_a=buf_a, buf_b=buf_b):
            stage_a(buf_a, False)
            stage_b(buf_b, False)

        @pl.when(par & both & a_diag & jnp.logical_not(b_last))
        def _(buf_a=buf_a, buf_b=buf_b):
            stage_a(buf_a, True)
            stage_b(buf_b, False)

        @pl.when(par & both & b_last)
        def _(buf_a=buf_a, buf_b=buf_b):
            stage_a(buf_a, False)
            stage_b(buf_b, True)

        @pl.when(par & has_a & jnp.logical_not(has_b))
        def _(buf_a=buf_a):
            stage_a(buf_a, True)

        @pl.when(par & has_b & jnp.logical_not(has_a))
        def _(buf_b=buf_b):
            stage_b(buf_b, True)


def _attention(qk, vt, kmeta, vmeta_t, lam_vecs, gain_col, *, bsz, seq, tq, dg, lam_init):
    nq = seq // tq
    h = N_DIFF_HEADS
    qi_list, ki_list = [], []
    for a in range(nq):
        for b in range(a + 1):
            qi_list.append(a)
            ki_list.append(b)
    n_steps = len(qi_list)
    ia = np.minimum(np.arange(n_steps + 1), n_steps - 1)
    ib = np.maximum(np.arange(n_steps + 1) - 1, 0)
    qi_arr, ki_arr = np.array(qi_list, np.int32), np.array(ki_list, np.int32)
    tabs = [jnp.asarray(t) for t in (qi_arr[ia], ki_arr[ia], qi_arr[ib], ki_arr[ib])]
    grid_spec = pltpu.PrefetchScalarGridSpec(
        num_scalar_prefetch=4,
        grid=(bsz, h, n_steps + 1),
        in_specs=[
            pl.BlockSpec((tq, V_HEAD_DIM), lambda b, hh, g, qa, ka, qb, kb: (b * nq + qa[g], hh)),
            pl.BlockSpec((tq, V_HEAD_DIM), lambda b, hh, g, qa, ka, qb, kb: (b * nq + ka[g], h + hh)),
            pl.BlockSpec((V_HEAD_DIM, tq), lambda b, hh, g, qa, ka, qb, kb: (hh, b * nq + kb[g])),
            pl.BlockSpec((LANES, V_HEAD_DIM), lambda b, hh, g, qa, ka, qb, kb: (0, hh)),
            pl.BlockSpec((V_HEAD_DIM, LANES), lambda b, hh, g, qa, ka, qb, kb: (hh, 0)),
            pl.BlockSpec((4, DIFF_HEAD_DIM), lambda b, hh, g, qa, ka, qb, kb: (0, 0)),
            pl.BlockSpec((V_HEAD_DIM, 1), lambda b, hh, g, qa, ka, qb, kb: (0, 0)),
        ],
        out_specs=pl.BlockSpec((tq, V_HEAD_DIM), lambda b, hh, g, qa, ka, qb, kb: (b * nq + qb[g], hh)),
        scratch_shapes=[
            pltpu.VMEM((2 * tq, V_HEAD_DIM), BF16),
            pltpu.VMEM((LANES, 2 * tq), F32),
            pltpu.VMEM((tq, 2 * tq), F32),
            pltpu.VMEM((tq, 2 * tq), F32),
            pltpu.VMEM((1, 2 * tq), F32),
            pltpu.VMEM((1, 2 * tq), F32),
            pltpu.VMEM((1, 2 * tq), F32),
            pltpu.VMEM((1, 2 * tq), F32),
            pltpu.VMEM((V_HEAD_DIM, 2 * tq), F32),
        ],
    )
    return pl.pallas_call(
        functools.partial(_attn_kernel, tq=tq, dg=dg, n_steps=n_steps, lam_init=lam_init),
        grid_spec=grid_spec,
        out_shape=jax.ShapeDtypeStruct((bsz * seq, h * V_HEAD_DIM), BF16),
        compiler_params=_cparams(("parallel", "parallel", "arbitrary")),
        name="attn",
    )(*tabs, qk, qk, vt, kmeta, vmeta_t, lam_vecs, gain_col)


def _ssd_kernel(*refs, t_blk, bsz, n_cast):
    dtt_refs = refs[:bsz]
    cast_in = refs[bsz:bsz + n_cast]
    (zx_ref, dt_ref, xbcm_ref, dtm_ref, dttm_ref,
     convw_ref, convb_ref, dtb_ref, dtbt_ref, alog_ref, alogt_ref, dskip_ref, nw_ref,
     expand_ref, tril_ref, triu2_ref, shift_ref, out_ref) = refs[bsz + n_cast:bsz + n_cast + 18]
    cast_out = refs[bsz + n_cast + 18:bsz + 2 * n_cast + 18]
    xs_sc, xm_sc, state_sc = refs[bsz + 2 * n_cast + 18:]
    for src, dst in zip(cast_in, cast_out):
        dst[...] = src[...].astype(BF16)
    n_heads, p_dim, n_state, n_groups = N_SSD_HEADS, SSD_HEAD_DIM, D_STATE, N_SSD_GROUPS
    width = n_heads * p_dim
    gw = width // n_groups
    n_pairs = n_heads // 2
    nt = (((1,), (1,)), ((), ()))
    expand = expand_ref[...]
    a_row = -jnp.exp(alog_ref[...])
    a_col = -jnp.exp(alogt_ref[...])
    d_full = dskip_ref[...]
    lane = lax.broadcasted_iota(jnp.int32, (CHUNK, LANES), 1)
    row = lax.broadcasted_iota(jnp.int32, (CHUNK, LANES), 0)
    left = lane < p_dim
    causal2 = (lane & (CHUNK - 1)) <= row

    def chunk(win_f32, st_ref, dt_raw, dtt_raw, pad_rows, z, out_r0, b=0):
        win = win_f32.astype(BF16)
        taps = jnp.dot(shift_ref[...], win, preferred_element_type=F32)
        conv = convb_ref[...] + win_f32[HALO:, :] * convw_ref[CONV_WIDTH - 1:CONV_WIDTH, :]
        for k in range(CONV_WIDTH - 1):
            conv = conv + taps[k * CHUNK:(k + 1) * CHUNK, :] * convw_ref[k:k + 1, :]
        xc = _silu(conv)
        x_s = xc[:, :width]
        dt = jax.nn.softplus(dt_raw + dtb_ref[...])
        dtt = jax.nn.softplus(dtt_raw + dtbt_ref[...])
        if pad_rows:
            dt = jnp.where(row >= pad_rows, dt, 0.0)
            dtt = jnp.where(lax.broadcasted_iota(jnp.int32, dtt.shape, 1) >= pad_rows, dtt, 0.0)
        a_pieces = jnp.dot(tril_ref[...], jnp.concatenate(_split_bf16(dt * a_row, 3), axis=1),
                           preferred_element_type=F32)
        a_cs = a_pieces[:, 0:LANES] + a_pieces[:, LANES:2 * LANES] + a_pieces[:, 2 * LANES:3 * LANES]
        a_cs_t2 = _dot_f32_by_01(dtt * a_col, triu2_ref[...])
        spread = jnp.dot(jnp.concatenate(_split_bf16(a_cs, 3) + _split_bf16(dt, 3), axis=0), expand,
                         preferred_element_type=F32)
        a_full = spread[0:CHUNK] + spread[CHUNK:2 * CHUNK] + spread[2 * CHUNK:3 * CHUNK]
        dt_full = spread[3 * CHUNK:4 * CHUNK] + spread[4 * CHUNK:5 * CHUNK] + spread[5 * CHUNK:6 * CHUNK]
        a_last = a_full[CHUNK - 1:CHUNK, :]
        xdt = x_s * dt_full
        w_state = (xdt * jnp.exp(a_last - a_full)).astype(BF16)
        xdt_b = xdt.astype(BF16)
        zero_b = jnp.zeros((CHUNK, LANES), BF16)
        y_parts = []
        for g in range(n_groups):
            b_g = xc[:, width + g * n_state: width + (g + 1) * n_state]
            c_g = xc[:, width + (n_groups + g) * n_state: width + (n_groups + g + 1) * n_state]
            b_gb = b_g.astype(BF16)
            c_gb = c_g.astype(BF16)
            st = st_ref[g]
            if out_r0 is not None:
                cb2 = lax.dot_general(c_gb, jnp.concatenate([b_gb, b_gb], axis=0), nt,
                                      preferred_element_type=F32)
                y_off = jnp.dot(c_gb, st.astype(BF16), preferred_element_type=F32)
                y_g = y_off * jnp.exp(a_full[:, g * gw:(g + 1) * gw])
                diag = []
                for pp in range(n_pairs // n_groups):
                    pr = g * (n_pairs // n_groups) + pp
                    colb = a_full[:, pr * LANES:(pr + 1) * LANES]
                    rowb = jnp.where(left, a_cs_t2[2 * pr:2 * pr + 1, :], a_cs_t2[2 * pr + 1:2 * pr + 2, :])
                    dec = jnp.exp(jnp.where(causal2, colb - rowb, -jnp.inf))
                    m_pair = (cb2 * dec).astype(BF16)
                    xp = xdt_b[:, pr * LANES:(pr + 1) * LANES]
                    rhs = jnp.concatenate([jnp.where(left, xp, zero_b), jnp.where(left, zero_b, xp)], axis=0)
                    diag.append(jnp.dot(m_pair, rhs, preferred_element_type=F32))
                y_parts.append(y_g + jnp.concatenate(diag, axis=1))
            new_st = st * jnp.exp(a_last[:, g * gw:(g + 1) * gw]) + jnp.dot(
                b_g.T.astype(BF16), w_state[:, g * gw:(g + 1) * gw], preferred_element_type=F32)
            st_ref[g] = new_st
        if out_r0 is not None:
            y = jnp.concatenate(y_parts, axis=1) + x_s * d_full
            gated = y * _silu(z)
            out_ref[b, out_r0:out_r0 + CHUNK, :] = _rms(gated, nw_ref[...]).astype(out_ref.dtype)

    @pl.when(pl.program_id(0) == 0)
    def _():
        state_sc[...] = jnp.zeros_like(state_sc)
        xm_sc[0:HALO, :] = jnp.zeros((HALO, xm_sc.shape[1]), F32)
        xm_sc[HALO:HALO + CHUNK, :] = xbcm_ref[...].astype(F32)
        chunk(xm_sc[...], state_sc.at[0], dtm_ref[...], dttm_ref[...], CHUNK - N_META, None, None)
        for b in range(bsz):
            if b:
                state_sc[b] = state_sc[0]
            xs_sc[b, 0:HALO, :] = xm_sc[CHUNK:CHUNK + HALO, :]

    for b in range(bsz):
        xs_sc[b, HALO:HALO + t_blk, :] = zx_ref[b, :, width:].astype(F32)
    for c in range(t_blk // CHUNK):
        rows = slice(c * CHUNK, (c + 1) * CHUNK)
        for b in range(bsz):
            chunk(xs_sc[b, c * CHUNK:(c + 1) * CHUNK + HALO, :], state_sc.at[b], dt_ref[b, rows, :],
                  dtt_refs[b][:, rows], 0, zx_ref[b, rows, 0:width].astype(F32), c * CHUNK, b)
    for b in range(bsz):
        xs_sc[b, 0:HALO, :] = xs_sc[b, t_blk:t_blk + HALO, :]


def _ssd(zx, dt, dtt, xbc_m, dt_m, dtt_m, conv_w, conv_b, dt_bias, a_log, d_skip, norm_w, cast_weights,
         *, bsz, seq, t_blk):
    nb = seq // t_blk
    slab = lambda w: pl.BlockSpec((w.shape[0] // nb, w.shape[1]), lambda i: (i, 0))
    width = N_SSD_HEADS * SSD_HEAD_DIM
    cdim = zx.shape[1] - width
    nh = N_SSD_HEADS

    def pad_row(v):
        return jnp.pad(v.reshape(1, nh), ((0, 0), (0, LANES - nh)))

    hh = np.arange(LANES)[:, None]
    ll = np.arange(width)[None, :]
    one_hot = lambda m: jnp.asarray(m.astype(np.float32), dtype=BF16)
    expand = one_hot(ll // SSD_HEAD_DIM == hh)
    ii = np.arange(CHUNK)
    tril = one_hot(ii[None, :] <= ii[:, None])
    triu2 = one_hot(np.tile(ii[:, None] <= ii[None, :], (1, 2)))
    jj = np.arange(HALO + CHUNK)
    shift = one_hot(np.concatenate([jj[None, :] == ii[:, None] + HALO - (CONV_WIDTH - 1) + k
                                    for k in range(CONV_WIDTH - 1)]))
    full = lambda shape: pl.BlockSpec(shape, lambda i: tuple(0 for _ in shape))
    out, *cast = pl.pallas_call(
        functools.partial(_ssd_kernel, t_blk=t_blk, bsz=bsz, n_cast=len(cast_weights)),
        grid=(nb,),
        in_specs=[pl.BlockSpec((nh, t_blk), functools.partial(lambda i, b: (0, b * nb + i), b=b))
                  for b in range(bsz)] + [slab(w) for w in cast_weights] + [
            pl.BlockSpec((bsz, t_blk, width + cdim), lambda i: (0, i, 0)),
            pl.BlockSpec((bsz, t_blk, LANES), lambda i: (0, i, 0)),
            full((CHUNK, cdim)), full((CHUNK, LANES)), full((nh, CHUNK)),
            full((CONV_WIDTH, cdim)), full((1, cdim)), full((1, LANES)), full((nh, 1)),
            full((1, LANES)), full((nh, 1)), full((1, width)), full((1, width)),
            full((LANES, width)), full((CHUNK, CHUNK)), full((CHUNK, 2 * CHUNK)), full(shift.shape),
        ],
        out_specs=[pl.BlockSpec((bsz, t_blk, width), lambda i: (0, i, 0))] + [slab(w) for w in cast_weights],
        out_shape=[jax.ShapeDtypeStruct((bsz, seq, width), BF16)]
        + [jax.ShapeDtypeStruct(w.shape, BF16) for w in cast_weights],
        scratch_shapes=[
            pltpu.VMEM((bsz, t_blk + HALO, cdim), F32),
            pltpu.VMEM((CHUNK + HALO, cdim), F32),
            pltpu.VMEM((bsz, N_SSD_GROUPS, D_STATE, width // N_SSD_GROUPS), F32),
        ],
        compiler_params=_cparams(("arbitrary",)),
        name="ssd",
    )(*([dtt] * bsz), *cast_weights, zx.reshape(bsz, seq, width + cdim), dt.reshape(bsz, seq, LANES),
      xbc_m, dt_m, dtt_m,
      conv_w, conv_b.reshape(1, cdim), pad_row(dt_bias), dt_bias.reshape(nh, 1), pad_row(a_log),
      a_log.reshape(nh, 1), jnp.repeat(d_skip, SSD_HEAD_DIM).reshape(1, width),
      norm_w.reshape(1, width), expand, tril, triu2, shift)
    return out.reshape(bsz * seq, width), cast


def _rope_tables(n_pos):
    inv = jnp.power(ROPE_THETA, -jnp.arange(0, ROT_DIM, 2, dtype=F32) / ROT_DIM)
    ang = jnp.arange(n_pos, dtype=F32)[:, None] * inv[None, :]
    cos, sin = jnp.cos(ang), jnp.sin(ang)
    half = ROT_DIM // 2
    r = np.arange(LANES) % DIFF_HEAD_DIM
    idx = jnp.asarray(r % half)
    lo = jnp.asarray(r < half)[None, :]
    hi = jnp.asarray((r >= half) & (r < ROT_DIM))[None, :]
    cos_t = jnp.where(lo | hi, cos[:, idx], 1.0)
    sin_lo = jnp.where(lo, -sin[:, idx], 0.0)
    sin_hi = jnp.where(hi, sin[:, idx], 0.0)
    return jnp.stack([cos_t, sin_lo, sin_hi])


def kernel(x, meta_tokens, ffn1_norm, ffn1_w_gate, ffn1_w_up, ffn1_w_down, mix_norm, w_in, q_norm, k_norm,
           lambda_q1, lambda_k1, lambda_q2, lambda_k2, attn_out_norm, conv_w, conv_b, dt_bias, a_log, d_skip,
           ssd_norm, w_out, ffn2_norm, ffn2_w_gate, ffn2_w_up, ffn2_w_down):
    bsz, seq, d = x.shape
    assert ffn1_norm.shape[0] == 1, "single-layer block"
    aw = N_DIFF_HEADS * V_HEAD_DIM
    sw = N_SSD_HEADS * SSD_HEAD_DIM
    cdim = sw + 2 * N_SSD_GROUPS * D_STATE
    lam_init = 0.8 - 0.6 * math.exp(-0.3 * 0)
    row = lambda v: v.reshape(1, -1)

    tn = 512
    w_in0 = w_in[0]
    w_dt = w_in0[:, 3 * aw + sw + cdim:].astype(BF16)
    w_dt_pad = jnp.pad(w_dt, ((0, 0), (0, LANES - N_SSD_HEADS)))
    w_dt_t = w_dt.T

    n_sub = aw // DIFF_HEAD_DIM
    q_gain = jnp.tile(q_norm[0], n_sub) * (DIFF_HEAD_DIM ** -0.5 * math.log2(math.e))
    k_gain = jnp.tile(k_norm[0], n_sub)
    qk_gain = jnp.concatenate([q_gain, k_gain]).reshape(2 * aw // tn, 1, tn)
    rope = _rope_tables(N_META + seq)
    gi = np.arange(2 * LANES) // DIFF_HEAD_DIM
    bd = jnp.asarray((gi[:, None] == gi[None, :]).astype(np.float32) / DIFF_HEAD_DIM, dtype=BF16)

    hn_m, wg1, wu1, wd1 = _ffn_meta(meta_tokens.astype(F32), row(ffn1_norm[0]), ffn1_w_gate[0],
                                    ffn1_w_up[0], ffn1_w_down[0], row(mix_norm[0]), tf=FFN_TF)
    w_in_t = w_in0.T.astype(BF16)
    x2 = x.reshape(bsz * seq, d)
    h1, hn = _ffn(x2, row(ffn1_norm[0]), wg1, wu1, wd1, row(mix_norm[0]), tm=FFN_TM, tf=FFN_TF)

    qk = _proj_qk(hn, w_in_t, qk_gain, rope[:, N_META:], bd, tm=1024, tn=tn, n=2 * aw)
    qk_m = _proj_qk(hn_m, w_in_t, qk_gain, rope[:, :N_META], bd, tm=N_META, tn=tn, n=2 * aw)
    v_t = _proj(hn, w_in_t, tm=1024, tn=tn, row0=2 * aw, n=aw, name="proj_vt", transpose=True)
    vzx_m = _proj(hn_m, w_in_t, tm=N_META, tn=tn, row0=2 * aw, n=aw + sw + cdim, name="proj_vzx_meta")
    zx = _proj(hn, w_in_t, tm=1024, tn=tn, row0=3 * aw, n=sw + cdim, name="proj_zx")
    v_t_m, xbc_m = vzx_m[:, :aw].T, vzx_m[:, aw + sw:]
    dt, dtt = _proj_dt(hn, w_dt_pad, w_dt_t, tm=1024)
    dt_m, dtt_m = _proj_dt(hn_m, w_dt_pad, w_dt_t, tm=N_META)

    kmeta = jnp.pad(qk_m[:, aw:], ((0, LANES - N_META), (0, 0)))
    vmeta_t = jnp.pad(v_t_m, ((0, 0), (0, LANES - N_META)))
    lam_vecs = jnp.stack([lambda_q1[0], lambda_k1[0], lambda_q2[0], lambda_k2[0]]).astype(F32)
    attn = _attention(qk, v_t, kmeta, vmeta_t, lam_vecs, attn_out_norm[0].reshape(V_HEAD_DIM, 1),
                      bsz=bsz, seq=seq, tq=1024, dg=256, lam_init=lam_init)

    lead = CHUNK - N_META
    ssd, (wg2, wu2, wd2, w_o) = _ssd(
        zx, dt, dtt,
        jnp.pad(xbc_m, ((lead, 0), (0, 0))), jnp.pad(dt_m, ((lead, 0), (0, 0))), jnp.pad(dtt_m, ((0, 0), (lead, 0))),
        conv_w[0], conv_b[0], dt_bias[0], a_log[0], d_skip[0], ssd_norm[0],
        (ffn2_w_gate[0], ffn2_w_up[0], ffn2_w_down[0], w_out[0]),
        bsz=bsz, seq=seq, t_blk=256)

    h2 = _outproj(h1, attn, ssd, w_o, tm=512, tn=d)
    (out,) = _ffn(h2, row(ffn2_norm[0]), wg2, wu2, wd2, tm=FFN_TM, tf=FFN_TF)
    return out.reshape(bsz, seq, d)
```

```python
import functools
import math

import jax
import jax.numpy as jnp
import numpy as np
from jax import lax
from jax.experimental import pallas as pl
from jax.experimental.pallas import tpu as pltpu

F32 = jnp.float32
BF16 = jnp.bfloat16

EPS = 1e-6
CHUNK = 64
N_META = 16
N_DIFF_HEADS = 8
DIFF_HEAD_DIM = 64
V_HEAD_DIM = 128
ROT_DIM = 16
ROPE_THETA = 500000.0
SSD_HEAD_DIM = 64
N_SSD_HEADS = 16
N_SSD_GROUPS = 2
D_STATE = 128
CONV_WIDTH = 4
HALO = 64
LANES = 128
VMEM_LIMIT = 56 * 1024 * 1024
FFN_TM = 512
FFN_TF = 512


def _cparams(sem):
    return pltpu.CompilerParams(dimension_semantics=sem, vmem_limit_bytes=VMEM_LIMIT)


def _silu(v):
    half = 0.5 * v
    return half + half * jnp.tanh(half)


def _split_bf16(a, pieces):
    out = []
    for _ in range(pieces - 1):
        p = a.astype(BF16)
        out.append(p)
        a = a - p.astype(F32)
    out.append(a.astype(BF16))
    return out


def _dot_f32_by_01(a, m01, pieces=3):
    return sum(jnp.dot(p, m01, preferred_element_type=F32) for p in _split_bf16(a, pieces))


def _rms(v, w):
    ms = jnp.mean(v * v, axis=-1, keepdims=True)
    return v * lax.rsqrt(ms + EPS) * w


def _swiglu_step(xn, wg_ref, wu_ref, wd_ref):
    g = jnp.dot(xn, wg_ref[...], preferred_element_type=F32)
    u = jnp.dot(xn, wu_ref[...], preferred_element_type=F32)
    a = (_silu(g) * u).astype(BF16)
    return jnp.dot(a, wd_ref[...], preferred_element_type=F32)


def _qk_norm_rope(y, gain, rope_ref, bd):
    cos, sin_lo, sin_hi = rope_ref[0], rope_ref[1], rope_ref[2]
    half = ROT_DIM // 2
    ms = sum(jnp.dot(p, bd, preferred_element_type=F32) for p in _split_bf16(y * y, 2))
    yn = y * lax.rsqrt(ms + EPS) * gain
    out = []
    for s in range(y.shape[1] // LANES):
        v = yn[:, s * LANES:(s + 1) * LANES]
        out.append(v * cos + pltpu.roll(v, half, 1) * sin_hi + pltpu.roll(v, LANES - half, 1) * sin_lo)
    return jnp.concatenate(out, axis=1)


def _ffn_kernel(x_ref, nw_ref, wg_ref, wu_ref, wd_ref, *rest, with_norm):
    if with_norm:
        pnw_ref, out_ref, hn_ref, xn_sc = rest
    else:
        out_ref, xn_sc = rest
    j = pl.program_id(1)

    @pl.when(j == 0)
    def _():
        xn_sc[...] = _rms(x_ref[...], nw_ref[...]).astype(BF16)
        out_ref[...] = jnp.zeros_like(out_ref)

    out_ref[...] += _swiglu_step(xn_sc[...], wg_ref, wu_ref, wd_ref)

    @pl.when(j == pl.num_programs(1) - 1)
    def _():
        h = x_ref[...] + 0.5 * out_ref[...]
        out_ref[...] = h
        if with_norm:
            hn_ref[...] = _rms(h, pnw_ref[...]).astype(BF16)


def _ffn(x2, nw, wg, wu, wd, post_nw=None, *, tm, tf):
    m, d = x2.shape
    f = wg.shape[1]
    with_norm = post_nw is not None
    row_spec = pl.BlockSpec((tm, d), lambda i, j: (i, 0))
    vec_spec = pl.BlockSpec((1, d), lambda i, j: (0, 0))
    in_specs = [row_spec, vec_spec,
                pl.BlockSpec((d, tf), lambda i, j: (0, j)),
                pl.BlockSpec((d, tf), lambda i, j: (0, j)),
                pl.BlockSpec((tf, d), lambda i, j: (j, 0))]
    args = [x2, nw, wg, wu, wd]
    out_specs, out_shape = [row_spec], [jax.ShapeDtypeStruct((m, d), F32)]
    if with_norm:
        in_specs.append(vec_spec)
        args.append(post_nw)
        out_specs.append(row_spec)
        out_shape.append(jax.ShapeDtypeStruct((m, d), BF16))
    return pl.pallas_call(
        functools.partial(_ffn_kernel, with_norm=with_norm),
        grid=(m // tm, f // tf),
        in_specs=in_specs,
        out_specs=out_specs,
        out_shape=out_shape,
        scratch_shapes=[pltpu.VMEM((tm, d), BF16)],
        compiler_params=_cparams(("parallel", "arbitrary")),
        name="ffn",
    )(*args)


def _ffn_meta_kernel(x_ref, nw_ref, wg_ref, wu_ref, wd_ref, pnw_ref,
                     hn_ref, wgb_ref, wub_ref, wdb_ref, xn_sc, acc_sc):
    j = pl.program_id(0)

    @pl.when(j == 0)
    def _():
        xn_sc[...] = _rms(x_ref[...], nw_ref[...]).astype(BF16)
        acc_sc[...] = jnp.zeros_like(acc_sc)

    wgb_ref[...] = wg_ref[...].astype(BF16)
    wub_ref[...] = wu_ref[...].astype(BF16)
    wdb_ref[...] = wd_ref[...].astype(BF16)
    acc_sc[...] += _swiglu_step(xn_sc[...], wgb_ref, wub_ref, wdb_ref)

    @pl.when(j == pl.num_programs(0) - 1)
    def _():
        hn_ref[...] = _rms(x_ref[...] + 0.5 * acc_sc[...], pnw_ref[...]).astype(BF16)


def _ffn_meta(xm, nw, wg, wu, wd, post_nw, *, tf):
    m, d = xm.shape
    n_ff = wg.shape[1] // tf
    vec_spec = pl.BlockSpec((1, d), lambda j: (0, 0))
    row_spec = pl.BlockSpec((m, d), lambda j: (0, 0))
    col_tile = pl.BlockSpec((d, tf), lambda j: (0, j))
    row_tile = pl.BlockSpec((tf, d), lambda j: (j, 0))
    return pl.pallas_call(
        _ffn_meta_kernel,
        grid=(n_ff,),
        in_specs=[row_spec, vec_spec, col_tile, col_tile, row_tile, vec_spec],
        out_specs=[row_spec, col_tile, col_tile, row_tile],
        out_shape=[jax.ShapeDtypeStruct((m, d), BF16), jax.ShapeDtypeStruct(wg.shape, BF16),
                   jax.ShapeDtypeStruct(wu.shape, BF16), jax.ShapeDtypeStruct(wd.shape, BF16)],
        scratch_shapes=[pltpu.VMEM((m, d), BF16), pltpu.VMEM((m, d), F32)],
        compiler_params=_cparams(("arbitrary",)),
        name="ffn_meta_cast",
    )(xm, nw, wg, wu, wd, post_nw)


_NT = (((1,), (1,)), ((), ()))


def _proj_kernel(x_ref, w_ref, out_ref, *, transpose):
    if transpose:
        acc = lax.dot_general(w_ref[...], x_ref[...], _NT, preferred_element_type=F32)
    else:
        acc = lax.dot_general(x_ref[...], w_ref[...], _NT, preferred_element_type=F32)
    out_ref[...] = acc.astype(out_ref.dtype)


def _proj(x2, w_t, *, tm, tn, row0, n, name, transpose=False):
    m, d = x2.shape
    r0 = row0 // tn
    if transpose:
        out_spec, out_shape = pl.BlockSpec((tn, tm), lambda i, j: (j, i)), (n, m)
    else:
        out_spec, out_shape = pl.BlockSpec((tm, tn), lambda i, j: (i, j)), (m, n)
    return pl.pallas_call(
        functools.partial(_proj_kernel, transpose=transpose),
        grid=(m // tm, n // tn),
        in_specs=[pl.BlockSpec((tm, d), lambda i, j: (i, 0)), pl.BlockSpec((tn, d), lambda i, j: (r0 + j, 0))],
        out_specs=out_spec,
        out_shape=jax.ShapeDtypeStruct(out_shape, BF16),
        compiler_params=_cparams(("parallel", "arbitrary")),
        name=name,
    )(x2, w_t)


def _proj_qk_kernel(x_ref, w_ref, gain_ref, rope_ref, bd_ref, out_ref, acc0_sc, acc1_sc, *, n_tiles):
    j = pl.program_id(1)
    bufs = (acc0_sc, acc1_sc)
    bd = bd_ref[...]
    wide = bd.shape[0]

    def finish(buf):
        for c in range(out_ref.shape[1] // wide):
            cols = slice(c * wide, (c + 1) * wide)
            out_ref[:, cols] = _qk_norm_rope(buf[:, cols], gain_ref[0][:, cols], rope_ref, bd).astype(out_ref.dtype)

    for t in range(n_tiles + 1):
        @pl.when(j == t)
        def _(t=t):
            if t > 0:
                finish(bufs[(t - 1) % 2])
            if t < n_tiles:
                bufs[t % 2][...] = lax.dot_general(x_ref[...], w_ref[...], _NT, preferred_element_type=F32)


def _proj_qk(x2, w_t, gain, rope, bd, *, tm, tn, n):
    m, d = x2.shape
    n_tab = rope.shape[1] // tm
    n_tiles = n // tn
    mm = lambda j: jnp.minimum(j, n_tiles - 1)
    ep = lambda j: jnp.maximum(j - 1, 0)
    return pl.pallas_call(
        functools.partial(_proj_qk_kernel, n_tiles=n_tiles),
        grid=(m // tm, n_tiles + 1),
        in_specs=[
            pl.BlockSpec((tm, d), lambda i, j: (i, 0)),
            pl.BlockSpec((tn, d), lambda i, j: (mm(j), 0)),
            pl.BlockSpec((1, 1, tn), lambda i, j: (ep(j), 0, 0)),
            pl.BlockSpec((3, tm, LANES), lambda i, j: (0, i % n_tab, 0)),
            pl.BlockSpec(bd.shape, lambda i, j: (0, 0)),
        ],
        out_specs=pl.BlockSpec((tm, tn), lambda i, j: (i, ep(j))),
        out_shape=jax.ShapeDtypeStruct((m, n), BF16),
        scratch_shapes=[pltpu.VMEM((tm, tn), F32), pltpu.VMEM((tm, tn), F32)],
        compiler_params=_cparams(("parallel", "arbitrary")),
        name="proj_qk",
    )(x2, w_t, gain, rope, bd)


def _proj_dt_kernel(x_ref, w_ref, wt_ref, out_ref, outt_ref):
    x = x_ref[...]
    out_ref[...] = jnp.dot(x, w_ref[...], preferred_element_type=F32)
    outt_ref[...] = lax.dot_general(wt_ref[...], x, (((1,), (1,)), ((), ())), preferred_element_type=F32)


def _proj_dt(x2, w_pad, w_t, *, tm):
    m, d = x2.shape
    nh = w_t.shape[0]
    return pl.pallas_call(
        _proj_dt_kernel,
        grid=(m // tm,),
        in_specs=[
            pl.BlockSpec((tm, d), lambda i: (i, 0)),
            pl.BlockSpec((d, LANES), lambda i: (0, 0)),
            pl.BlockSpec((nh, d), lambda i: (0, 0)),
        ],
        out_specs=[pl.BlockSpec((tm, LANES), lambda i: (i, 0)), pl.BlockSpec((nh, tm), lambda i: (0, i))],
        out_shape=[jax.ShapeDtypeStruct((m, LANES), F32), jax.ShapeDtypeStruct((nh, m), F32)],
        compiler_params=_cparams(("parallel",)),
        name="proj_dt",
    )(x2, w_pad, w_t)


def _outproj_kernel(h_ref, a_ref, s_ref, wa_ref, ws_ref, out_ref):
    out_ref[...] = (h_ref[...]
                    + jnp.dot(a_ref[...], wa_ref[...], preferred_element_type=F32)
                    + jnp.dot(s_ref[...], ws_ref[...], preferred_element_type=F32))


def _outproj(h1, attn, ssd, w_o, *, tm, tn):
    m, d = h1.shape
    k = attn.shape[1]
    return pl.pallas_call(
        _outproj_kernel,
        grid=(m // tm, d // tn),
        in_specs=[
            pl.BlockSpec((tm, tn), lambda i, j: (i, j)),
            pl.BlockSpec((tm, k), lambda i, j: (i, 0)),
            pl.BlockSpec((tm, k), lambda i, j: (i, 0)),
            pl.BlockSpec((k, tn), lambda i, j: (0, j)),
            pl.BlockSpec((k, tn), lambda i, j: (1, j)),
        ],
        out_specs=pl.BlockSpec((tm, tn), lambda i, j: (i, j)),
        out_shape=jax.ShapeDtypeStruct((m, d), F32),
        compiler_params=_cparams(("parallel", "arbitrary")),
        name="outproj",
    )(h1, attn, ssd, w_o, w_o)


def _attn_kernel(qa_tab, ka_tab, qb_tab, kb_tab, q_ref, k_ref, vt_ref, km_ref, vmt_ref, lam_ref, gain_ref,
                 out_ref, qs_sc, sm_sc, s0_sc, s1_sc, mx0_sc, mx1_sc, m_sc, l_sc, acc_sc,
                 *, tq, dg, n_steps, lam_init):
    g = pl.program_id(2)
    qa = qa_tab[g]
    ka = ka_tab[g]
    qb = qb_tab[g]
    kb = kb_tab[g]
    has_a = g < n_steps
    has_b = g >= 1
    even = (g % 2) == 0
    nt = (((1,), (1,)), ((), ()))

    full_groups = [(slice(0, 2 * tq), tq)]
    diag_groups = [(slice(sub * tq + j * dg, sub * tq + (j + 1) * dg), (j + 1) * dg)
                   for sub in range(2) for j in range(tq // dg)]

    def stage_a(buf, diagonal):
        s_out, mx_out = buf
        for cols, nk in (diag_groups if diagonal else full_groups):
            s_t = lax.dot_general(k_ref[0:nk, :], qs_sc[cols, :], nt, preferred_element_type=F32)
            if diagonal:
                key = lax.broadcasted_iota(jnp.int32, s_t.shape, 0)
                qry = lax.broadcasted_iota(jnp.int32, s_t.shape, 1) + (nk - dg)
                s_t = jnp.where(key // CHUNK <= qry // CHUNK, s_t, -jnp.inf)
            s_out[0:nk, cols] = s_t
            mx_out[:, cols] = jnp.max(s_t, axis=0, keepdims=True)

    def stage_b(buf, last):
        s_in, mx_in = buf
        vt_aug = jnp.concatenate([vt_ref[...], jnp.ones((16, vt_ref.shape[1]), BF16)], axis=0)
        for cols, nk in (diag_groups if last else full_groups):
            m_prev = m_sc[:, cols]
            m_new = jnp.maximum(m_prev, mx_in[:, cols])
            alpha = jnp.exp2(m_prev - m_new)
            p = jnp.exp2(s_in[0:nk, cols] - m_new).astype(BF16)
            pv = jnp.dot(vt_aug[:, 0:nk], p, preferred_element_type=F32)
            l_sc[:, cols] = alpha * l_sc[:, cols] + pv[V_HEAD_DIM:V_HEAD_DIM + 1, :]
            acc_sc[:, cols] = alpha * acc_sc[:, cols] + pv[0:V_HEAD_DIM, :]
            m_sc[:, cols] = m_new
        if last:
            lq1, lk1, lq2, lk2 = lam_ref[0:1, :], lam_ref[1:2, :], lam_ref[2:3, :], lam_ref[3:4, :]
            lam = (jnp.exp(jnp.sum(lq1 * lk1, axis=-1, keepdims=True))
                   - jnp.exp(jnp.sum(lq2 * lk2, axis=-1, keepdims=True)) + lam_init)
            o = acc_sc[...] / l_sc[...]
            o = o[:, 0:tq] - lam * o[:, tq:2 * tq]
            ms = jnp.mean(o * o, axis=0, keepdims=True)
            o = o * lax.rsqrt(ms + EPS) * (gain_ref[...] * (1.0 - lam_init))
            out_ref[...] = o.T.astype(out_ref.dtype)

    @pl.when(has_b & (kb == 0))
    def _():
        s_t = sm_sc[...]
        m0 = jnp.max(s_t, axis=0, keepdims=True)
        p = jnp.exp2(s_t - m0)
        m_sc[...] = m0
        l_sc[...] = jnp.sum(p, axis=0, keepdims=True)
        acc_sc[...] = jnp.dot(vmt_ref[...], p.astype(BF16), preferred_element_type=F32)

    @pl.when(has_a & (ka == 0))
    def _():
        q = q_ref[...]
        lane = lax.broadcasted_iota(jnp.int32, q.shape, 1)
        zero = jnp.zeros_like(q)
        qs_sc[0:tq, :] = jnp.where(lane < DIFF_HEAD_DIM, q, zero)
        qs_sc[tq:2 * tq, :] = jnp.where(lane >= DIFF_HEAD_DIM, q, zero)
        s_t = lax.dot_general(km_ref[...], qs_sc[...], nt, preferred_element_type=F32)
        key = lax.broadcasted_iota(jnp.int32, s_t.shape, 0)
        sm_sc[...] = jnp.where(key < N_META, s_t, -jnp.inf)

    buf0, buf1 = (s0_sc, mx0_sc), (s1_sc, mx1_sc)
    a_diag = ka == qa
    b_last = kb == qb
    both = has_a & has_b
    for parity, (buf_a, buf_b) in enumerate(((buf0, buf1), (buf1, buf0))):
        par = even if parity == 0 else jnp.logical_not(even)

        @pl.when(par & both & jnp.logical_not(a_diag) & jnp.logical_not(b_last))
        def _(buf_a=buf_a, buf_b=buf_b):
            stage_a(buf_a, False)
            stage_b(buf_b, False)

        @pl.when(par & both & a_diag & jnp.logical_not(b_last))
        def _(buf_a=buf_a, buf_b=buf_b):
            stage_a(buf_a, True)
            stage_b(buf_b, False)

        @pl.when(par & both & b_last)
        def _(buf_a=buf_a, buf_b=buf_b):
            stage_a(buf_a, False)
            stage_b(buf_b, True)

        @pl.when(par & has_a & jnp.logical_not(has_b))
        def _(buf_a=buf_a):
            stage_a(buf_a, True)

        @pl.when(par & has_b & jnp.logical_not(has_a))
        def _(buf_b=buf_b):
            stage_b(buf_b, True)


def _attention(qk, vt, kmeta, vmeta_t, lam_vecs, gain_col, *, bsz, seq, tq, dg, lam_init):
    nq = seq // tq
    h = N_DIFF_HEADS
    qi_list, ki_list = [], []
    for a in range(nq):
        for b in range(a + 1):
            qi_list.append(a)
            ki_list.append(b)
    n_steps = len(qi_list)
    ia = np.minimum(np.arange(n_steps + 1), n_steps - 1)
    ib = np.maximum(np.arange(n_steps + 1) - 1, 0)
    qi_arr, ki_arr = np.array(qi_list, np.int32), np.array(ki_list, np.int32)
    tabs = [jnp.asarray(t) for t in (qi_arr[ia], ki_arr[ia], qi_arr[ib], ki_arr[ib])]
    grid_spec = pltpu.PrefetchScalarGridSpec(
        num_scalar_prefetch=4,
        grid=(bsz, h, n_steps + 1),
        in_specs=[
            pl.BlockSpec((tq, V_HEAD_DIM), lambda b, hh, g, qa, ka, qb, kb: (b * nq + qa[g], hh)),
            pl.BlockSpec((tq, V_HEAD_DIM), lambda b, hh, g, qa, ka, qb, kb: (b * nq + ka[g], h + hh)),
            pl.BlockSpec((V_HEAD_DIM, tq), lambda b, hh, g, qa, ka, qb, kb: (hh, b * nq + kb[g])),
            pl.BlockSpec((LANES, V_HEAD_DIM), lambda b, hh, g, qa, ka, qb, kb: (0, hh)),
            pl.BlockSpec((V_HEAD_DIM, LANES), lambda b, hh, g, qa, ka, qb, kb: (hh, 0)),
            pl.BlockSpec((4, DIFF_HEAD_DIM), lambda b, hh, g, qa, ka, qb, kb: (0, 0)),
            pl.BlockSpec((V_HEAD_DIM, 1), lambda b, hh, g, qa, ka, qb, kb: (0, 0)),
        ],
        out_specs=pl.BlockSpec((tq, V_HEAD_DIM), lambda b, hh, g, qa, ka, qb, kb: (b * nq + qb[g], hh)),
        scratch_shapes=[
            pltpu.VMEM((2 * tq, V_HEAD_DIM), BF16),
            pltpu.VMEM((LANES, 2 * tq), F32),
            pltpu.VMEM((tq, 2 * tq), F32),
            pltpu.VMEM((tq, 2 * tq), F32),
            pltpu.VMEM((1, 2 * tq), F32),
            pltpu.VMEM((1, 2 * tq), F32),
            pltpu.VMEM((1, 2 * tq), F32),
            pltpu.VMEM((1, 2 * tq), F32),
            pltpu.VMEM((V_HEAD_DIM, 2 * tq), F32),
        ],
    )
    return pl.pallas_call(
        functools.partial(_attn_kernel, tq=tq, dg=dg, n_steps=n_steps, lam_init=lam_init),
        grid_spec=grid_spec,
        out_shape=jax.ShapeDtypeStruct((bsz * seq, h * V_HEAD_DIM), BF16),
        compiler_params=_cparams(("parallel", "parallel", "arbitrary")),
        name="attn",
    )(*tabs, qk, qk, vt, kmeta, vmeta_t, lam_vecs, gain_col)


def _ssd_kernel(*refs, t_blk, bsz, n_cast):
    dtt_refs = refs[:bsz]
    cast_in = refs[bsz:bsz + n_cast]
    (zx_ref, dt_ref, xbcm_ref, dtm_ref, dttm_ref,
     convw_ref, convb_ref, dtb_ref, dtbt_ref, alog_ref, alogt_ref, dskip_ref, nw_ref,
     expand_ref, tril_ref, triu2_ref, shift_ref, out_ref) = refs[bsz + n_cast:bsz + n_cast + 18]
    cast_out = refs[bsz + n_cast + 18:bsz + 2 * n_cast + 18]
    xs_sc, xm_sc, state_sc = refs[bsz + 2 * n_cast + 18:]
    for src, dst in zip(cast_in, cast_out):
        dst[...] = src[...].astype(BF16)
    n_heads, p_dim, n_state, n_groups = N_SSD_HEADS, SSD_HEAD_DIM, D_STATE, N_SSD_GROUPS
    width = n_heads * p_dim
    gw = width // n_groups
    n_pairs = n_heads // 2
    nt = (((1,), (1,)), ((), ()))
    expand = expand_ref[...]
    a_row = -jnp.exp(alog_ref[...])
    a_col = -jnp.exp(alogt_ref[...])
    d_full = dskip_ref[...]
    lane = lax.broadcasted_iota(jnp.int32, (CHUNK, LANES), 1)
    row = lax.broadcasted_iota(jnp.int32, (CHUNK, LANES), 0)
    left = lane < p_dim
    causal2 = (lane & (CHUNK - 1)) <= row

    def chunk(win_f32, st_ref, dt_raw, dtt_raw, pad_rows, z, out_r0, b=0):
        win = win_f32.astype(BF16)
        taps = jnp.dot(shift_ref[...], win, preferred_element_type=F32)
        conv = convb_ref[...] + win_f32[HALO:, :] * convw_ref[CONV_WIDTH - 1:CONV_WIDTH, :]
        for k in range(CONV_WIDTH - 1):
            conv = conv + taps[k * CHUNK:(k + 1) * CHUNK, :] * convw_ref[k:k + 1, :]
        xc = _silu(conv)
        x_s = xc[:, :width]
        dt = jax.nn.softplus(dt_raw + dtb_ref[...])
        dtt = jax.nn.softplus(dtt_raw + dtbt_ref[...])
        if pad_rows:
            dt = jnp.where(row >= pad_rows, dt, 0.0)
            dtt = jnp.where(lax.broadcasted_iota(jnp.int32, dtt.shape, 1) >= pad_rows, dtt, 0.0)
        a_pieces = jnp.dot(tril_ref[...], jnp.concatenate(_split_bf16(dt * a_row, 3), axis=1),
                           preferred_element_type=F32)
        a_cs = a_pieces[:, 0:LANES] + a_pieces[:, LANES:2 * LANES] + a_pieces[:, 2 * LANES:3 * LANES]
        a_cs_t2 = _dot_f32_by_01(dtt * a_col, triu2_ref[...])
        spread = jnp.dot(jnp.concatenate(_split_bf16(a_cs, 3) + _split_bf16(dt, 3), axis=0), expand,
                         preferred_element_type=F32)
        a_full = spread[0:CHUNK] + spread[CHUNK:2 * CHUNK] + spread[2 * CHUNK:3 * CHUNK]
        dt_full = spread[3 * CHUNK:4 * CHUNK] + spread[4 * CHUNK:5 * CHUNK] + spread[5 * CHUNK:6 * CHUNK]
        a_last = a_full[CHUNK - 1:CHUNK, :]
        xdt = x_s * dt_full
        w_state = (xdt * jnp.exp(a_last - a_full)).astype(BF16)
        xdt_b = xdt.astype(BF16)
        zero_b = jnp.zeros((CHUNK, LANES), BF16)
        y_parts = []
        for g in range(n_groups):
            b_g = xc[:, width + g * n_state: width + (g + 1) * n_state]
            c_g = xc[:, width + (n_groups + g) * n_state: width + (n_groups + g + 1) * n_state]
            b_gb = b_g.astype(BF16)
            c_gb = c_g.astype(BF16)
            st = st_ref[g]
            if out_r0 is not None:
                cb2 = lax.dot_general(c_gb, jnp.concatenate([b_gb, b_gb], axis=0), nt,
                                      preferred_element_type=F32)
                y_off = jnp.dot(c_gb, st.astype(BF16), preferred_element_type=F32)
                y_g = y_off * jnp.exp(a_full[:, g * gw:(g + 1) * gw])
                diag = []
                for pp in range(n_pairs // n_groups):
                    pr = g * (n_pairs // n_groups) + pp
                    colb = a_full[:, pr * LANES:(pr + 1) * LANES]
                    rowb = jnp.where(left, a_cs_t2[2 * pr:2 * pr + 1, :], a_cs_t2[2 * pr + 1:2 * pr + 2, :])
                    dec = jnp.exp(jnp.where(causal2, colb - rowb, -jnp.inf))
                    m_pair = (cb2 * dec).astype(BF16)
                    xp = xdt_b[:, pr * LANES:(pr + 1) * LANES]
                    rhs = jnp.concatenate([jnp.where(left, xp, zero_b), jnp.where(left, zero_b, xp)], axis=0)
                    diag.append(jnp.dot(m_pair, rhs, preferred_element_type=F32))
                y_parts.append(y_g + jnp.concatenate(diag, axis=1))
            new_st = st * jnp.exp(a_last[:, g * gw:(g + 1) * gw]) + jnp.dot(
                b_g.T.astype(BF16), w_state[:, g * gw:(g + 1) * gw], preferred_element_type=F32)
            st_ref[g] = new_st
        if out_r0 is not None:
            y = jnp.concatenate(y_parts, axis=1) + x_s * d_full
            gated = y * _silu(z)
            out_ref[b, out_r0:out_r0 + CHUNK, :] = _rms(gated, nw_ref[...]).astype(out_ref.dtype)

    @pl.when(pl.program_id(0) == 0)
    def _():
        state_sc[...] = jnp.zeros_like(state_sc)
        xm_sc[0:HALO, :] = jnp.zeros((HALO, xm_sc.shape[1]), F32)
        xm_sc[HALO:HALO + CHUNK, :] = xbcm_ref[...].astype(F32)
        chunk(xm_sc[...], state_sc.at[0], dtm_ref[...], dttm_ref[...], CHUNK - N_META, None, None)
        for b in range(bsz):
            if b:
                state_sc[b] = state_sc[0]
            xs_sc[b, 0:HALO, :] = xm_sc[CHUNK:CHUNK + HALO, :]

    for b in range(bsz):
        xs_sc[b, HALO:HALO + t_blk, :] = zx_ref[b, :, width:].astype(F32)
    for c in range(t_blk // CHUNK):
        rows = slice(c * CHUNK, (c + 1) * CHUNK)
        for b in range(bsz):
            chunk(xs_sc[b, c * CHUNK:(c + 1) * CHUNK + HALO, :], state_sc.at[b], dt_ref[b, rows, :],
                  dtt_refs[b][:, rows], 0, zx_ref[b, rows, 0:width].astype(F32), c * CHUNK, b)
    for b in range(bsz):
        xs_sc[b, 0:HALO, :] = xs_sc[b, t_blk:t_blk + HALO, :]


def _ssd(zx, dt, dtt, xbc_m, dt_m, dtt_m, conv_w, conv_b, dt_bias, a_log, d_skip, norm_w, cast_weights,
         *, bsz, seq, t_blk):
    nb = seq // t_blk
    slab = lambda w: pl.BlockSpec((w.shape[0] // nb, w.shape[1]), lambda i: (i, 0))
    width = N_SSD_HEADS * SSD_HEAD_DIM
    cdim = zx.shape[1] - width
    nh = N_SSD_HEADS

    def pad_row(v):
        return jnp.pad(v.reshape(1, nh), ((0, 0), (0, LANES - nh)))

    hh = np.arange(LANES)[:, None]
    ll = np.arange(width)[None, :]
    one_hot = lambda m: jnp.asarray(m.astype(np.float32), dtype=BF16)
    expand = one_hot(ll // SSD_HEAD_DIM == hh)
    ii = np.arange(CHUNK)
    tril = one_hot(ii[None, :] <= ii[:, None])
    triu2 = one_hot(np.tile(ii[:, None] <= ii[None, :], (1, 2)))
    jj = np.arange(HALO + CHUNK)
    shift = one_hot(np.concatenate([jj[None, :] == ii[:, None] + HALO - (CONV_WIDTH - 1) + k
                                    for k in range(CONV_WIDTH - 1)]))
    full = lambda shape: pl.BlockSpec(shape, lambda i: tuple(0 for _ in shape))
    out, *cast = pl.pallas_call(
        functools.partial(_ssd_kernel, t_blk=t_blk, bsz=bsz, n_cast=len(cast_weights)),
        grid=(nb,),
        in_specs=[pl.BlockSpec((nh, t_blk), functools.partial(lambda i, b: (0, b * nb + i), b=b))
                  for b in range(bsz)] + [slab(w) for w in cast_weights] + [
            pl.BlockSpec((bsz, t_blk, width + cdim), lambda i: (0, i, 0)),
            pl.BlockSpec((bsz, t_blk, LANES), lambda i: (0, i, 0)),
            full((CHUNK, cdim)), full((CHUNK, LANES)), full((nh, CHUNK)),
            full((CONV_WIDTH, cdim)), full((1, cdim)), full((1, LANES)), full((nh, 1)),
            full((1, LANES)), full((nh, 1)), full((1, width)), full((1, width)),
            full((LANES, width)), full((CHUNK, CHUNK)), full((CHUNK, 2 * CHUNK)), full(shift.shape),
        ],
        out_specs=[pl.BlockSpec((bsz, t_blk, width), lambda i: (0, i, 0))] + [slab(w) for w in cast_weights],
        out_shape=[jax.ShapeDtypeStruct((bsz, seq, width), BF16)]
        + [jax.ShapeDtypeStruct(w.shape, BF16) for w in cast_weights],
        scratch_shapes=[
            pltpu.VMEM((bsz, t_blk + HALO, cdim), F32),
            pltpu.VMEM((CHUNK + HALO, cdim), F32),
            pltpu.VMEM((bsz, N_SSD_GROUPS, D_STATE, width // N_SSD_GROUPS), F32),
        ],
        compiler_params=_cparams(("arbitrary",)),
        name="ssd",
    )(*([dtt] * bsz), *cast_weights, zx.reshape(bsz, seq, width + cdim), dt.reshape(bsz, seq, LANES),
      xbc_m, dt_m, dtt_m,
      conv_w, conv_b.reshape(1, cdim), pad_row(dt_bias), dt_bias.reshape(nh, 1), pad_row(a_log),
      a_log.reshape(nh, 1), jnp.repeat(d_skip, SSD_HEAD_DIM).reshape(1, width),
      norm_w.reshape(1, width), expand, tril, triu2, shift)
    return out.reshape(bsz * seq, width), cast


def _rope_tables(n_pos):
    inv = jnp.power(ROPE_THETA, -jnp.arange(0, ROT_DIM, 2, dtype=F32) / ROT_DIM)
    ang = jnp.arange(n_pos, dtype=F32)[:, None] * inv[None, :]
    cos, sin = jnp.cos(ang), jnp.sin(ang)
    half = ROT_DIM // 2
    r = np.arange(LANES) % DIFF_HEAD_DIM
    idx = jnp.asarray(r % half)
    lo = jnp.asarray(r < half)[None, :]
    hi = jnp.asarray((r >= half) & (r < ROT_DIM))[None, :]
    cos_t = jnp.where(lo | hi, cos[:, idx], 1.0)
    sin_lo = jnp.where(lo, -sin[:, idx], 0.0)
    sin_hi = jnp.where(hi, sin[:, idx], 0.0)
    return jnp.stack([cos_t, sin_lo, sin_hi])


def kernel(x, meta_tokens, ffn1_norm, ffn1_w_gate, ffn1_w_up, ffn1_w_down, mix_norm, w_in, q_norm, k_norm,
           lambda_q1, lambda_k1, lambda_q2, lambda_k2, attn_out_norm, conv_w, conv_b, dt_bias, a_log, d_skip,
           ssd_norm, w_out, ffn2_norm, ffn2_w_gate, ffn2_w_up, ffn2_w_down):
    bsz, seq, d = x.shape
    assert ffn1_norm.shape[0] == 1, "single-layer block"
    aw = N_DIFF_HEADS * V_HEAD_DIM
    sw = N_SSD_HEADS * SSD_HEAD_DIM
    cdim = sw + 2 * N_SSD_GROUPS * D_STATE
    lam_init = 0.8 - 0.6 * math.exp(-0.3 * 0)
    row = lambda v: v.reshape(1, -1)

    tn = 512
    w_in0 = w_in[0]
    w_dt = w_in0[:, 3 * aw + sw + cdim:].astype(BF16)
    w_dt_pad = jnp.pad(w_dt, ((0, 0), (0, LANES - N_SSD_HEADS)))
    w_dt_t = w_dt.T

    n_sub = aw // DIFF_HEAD_DIM
    q_gain = jnp.tile(q_norm[0], n_sub) * (DIFF_HEAD_DIM ** -0.5 * math.log2(math.e))
    k_gain = jnp.tile(k_norm[0], n_sub)
    qk_gain = jnp.concatenate([q_gain, k_gain]).reshape(2 * aw // tn, 1, tn)
    rope = _rope_tables(N_META + seq)
    gi = np.arange(2 * LANES) // DIFF_HEAD_DIM
    bd = jnp.asarray((gi[:, None] == gi[None, :]).astype(np.float32) / DIFF_HEAD_DIM, dtype=BF16)

    hn_m, wg1, wu1, wd1 = _ffn_meta(meta_tokens.astype(F32), row(ffn1_norm[0]), ffn1_w_gate[0],
                                    ffn1_w_up[0], ffn1_w_down[0], row(mix_norm[0]), tf=FFN_TF)
    w_in_t = w_in0.T.astype(BF16)
    x2 = x.reshape(bsz * seq, d)
    h1, hn = _ffn(x2, row(ffn1_norm[0]), wg1, wu1, wd1, row(mix_norm[0]), tm=FFN_TM, tf=FFN_TF)

    qk = _proj_qk(hn, w_in_t, qk_gain, rope[:, N_META:], bd, tm=1024, tn=tn, n=2 * aw)
    qk_m = _proj_qk(hn_m, w_in_t, qk_gain, rope[:, :N_META], bd, tm=N_META, tn=tn, n=2 * aw)
    v_t = _proj(hn, w_in_t, tm=1024, tn=tn, row0=2 * aw, n=aw, name="proj_vt", transpose=True)
    vzx_m = _proj(hn_m, w_in_t, tm=N_META, tn=tn, row0=2 * aw, n=aw + sw + cdim, name="proj_vzx_meta")
    zx = _proj(hn, w_in_t, tm=1024, tn=tn, row0=3 * aw, n=sw + cdim, name="proj_zx")
    v_t_m, xbc_m = vzx_m[:, :aw].T, vzx_m[:, aw + sw:]
    dt, dtt = _proj_dt(hn, w_dt_pad, w_dt_t, tm=1024)
    dt_m, dtt_m = _proj_dt(hn_m, w_dt_pad, w_dt_t, tm=N_META)

    kmeta = jnp.pad(qk_m[:, aw:], ((0, LANES - N_META), (0, 0)))
    vmeta_t = jnp.pad(v_t_m, ((0, 0), (0, LANES - N_META)))
    lam_vecs = jnp.stack([lambda_q1[0], lambda_k1[0], lambda_q2[0], lambda_k2[0]]).astype(F32)
    attn = _attention(qk, v_t, kmeta, vmeta_t, lam_vecs, attn_out_norm[0].reshape(V_HEAD_DIM, 1),
                      bsz=bsz, seq=seq, tq=1024, dg=256, lam_init=lam_init)

    lead = CHUNK - N_META
    ssd, (wg2, wu2, wd2, w_o) = _ssd(
        zx, dt, dtt,
        jnp.pad(xbc_m, ((lead, 0), (0, 0))), jnp.pad(dt_m, ((lead, 0), (0, 0))), jnp.pad(dtt_m, ((0, 0), (lead, 0))),
        conv_w[0], conv_b[0], dt_bias[0], a_log[0], d_skip[0], ssd_norm[0],
        (ffn2_w_gate[0], ffn2_w_up[0], ffn2_w_down[0], w_out[0]),
        bsz=bsz, seq=seq, t_blk=256)

    h2 = _outproj(h1, attn, ssd, w_o, tm=512, tn=d)
    (out,) = _ffn(h2, row(ffn2_norm[0]), wg2, wu2, wd2, tm=FFN_TM, tf=FFN_TF)
    return out.reshape(bsz, seq, d)
```

```python
import functools
import math

import jax
import jax.numpy as jnp
import numpy as np
from jax import lax
from jax.experimental import pallas as pl
from jax.experimental.pallas import tpu as pltpu

F32 = jnp.float32
BF16 = jnp.bfloat16

EPS = 1e-6
CHUNK = 64
N_META = 16
N_DIFF_HEADS = 8
DIFF_HEAD_DIM = 64
V_HEAD_DIM = 128
ROT_DIM = 16
ROPE_THETA = 500000.0
SSD_HEAD_DIM = 64
N_SSD_HEADS = 16
N_SSD_GROUPS = 2
D_STATE = 128
CONV_WIDTH = 4
HALO = 64
LANES = 128
VMEM_LIMIT = 56 * 1024 * 1024
FFN_TM = 512
FFN_TF = 512


def _cparams(sem):
    return pltpu.CompilerParams(dimension_semantics=sem, vmem_limit_bytes=VMEM_LIMIT)


def _silu(v):
    half = 0.5 * v
    return half + half * jnp.tanh(half)


def _split_bf16(a, pieces):
    out = []
    for _ in range(pieces - 1):
        p = a.astype(BF16)
        out.append(p)
        a = a - p.astype(F32)
    out.append(a.astype(BF16))
    return out


def _dot_f32_by_01(a, m01, pieces=3):
    return sum(jnp.dot(p, m01, preferred_element_type=F32) for p in _split_bf16(a, pieces))


def _rms(v, w):
    ms = jnp.mean(v * v, axis=-1, keepdims=True)
    return v * lax.rsqrt(ms + EPS) * w


def _swiglu_step(xn, wg_ref, wu_ref, wd_ref):
    g = jnp.dot(xn, wg_ref[...], preferred_element_type=F32)
    u = jnp.dot(xn, wu_ref[...], preferred_element_type=F32)
    a = (_silu(g) * u).astype(BF16)
    return jnp.dot(a, wd_ref[...], preferred_element_type=F32)


def _subhead_norm(y, gain, avg):
    ms = sum(jnp.dot(p, avg, preferred_element_type=F32) for p in _split_bf16(y * y, 2))
    return y * lax.rsqrt(ms + EPS) * gain


def _partial_rotary(yn, rope_ref, perm):
    cos, sin = rope_ref[0], rope_ref[1]
    partner = sum(jnp.dot(p, perm, preferred_element_type=F32) for p in _split_bf16(yn, 2))
    out = []
    for s in range(yn.shape[1] // LANES):
        cols = slice(s * LANES, (s + 1) * LANES)
        out.append(yn[:, cols] * cos + partner[:, cols] * sin)
    return jnp.concatenate(out, axis=1)


def _ffn_kernel(x_ref, nw_ref, wg_ref, wu_ref, wd_ref, *rest, with_norm):
    if with_norm:
        pnw_ref, out_ref, hn_ref, xn_sc = rest
    else:
        out_ref, xn_sc = rest
    j = pl.program_id(1)

    @pl.when(j == 0)
    def _():
        xn_sc[...] = _rms(x_ref[...], nw_ref[...]).astype(BF16)
        out_ref[...] = jnp.zeros_like(out_ref)

    out_ref[...] += _swiglu_step(xn_sc[...], wg_ref, wu_ref, wd_ref)

    @pl.when(j == pl.num_programs(1) - 1)
    def _():
        h = x_ref[...] + 0.5 * out_ref[...]
        out_ref[...] = h
        if with_norm:
            hn_ref[...] = _rms(h, pnw_ref[...]).astype(BF16)


def _ffn(x2, nw, wg, wu, wd, post_nw=None, *, tm, tf):
    m, d = x2.shape
    f = wg.shape[1]
    with_norm = post_nw is not None
    row_spec = pl.BlockSpec((tm, d), lambda i, j: (i, 0))
    vec_spec = pl.BlockSpec((1, d), lambda i, j: (0, 0))
    in_specs = [row_spec, vec_spec,
                pl.BlockSpec((d, tf), lambda i, j: (0, j)),
                pl.BlockSpec((d, tf), lambda i, j: (0, j)),
                pl.BlockSpec((tf, d), lambda i, j: (j, 0))]
    args = [x2, nw, wg, wu, wd]
    out_specs, out_shape = [row_spec], [jax.ShapeDtypeStruct((m, d), F32)]
    if with_norm:
        in_specs.append(vec_spec)
        args.append(post_nw)
        out_specs.append(row_spec)
        out_shape.append(jax.ShapeDtypeStruct((m, d), BF16))
    return pl.pallas_call(
        functools.partial(_ffn_kernel, with_norm=with_norm),
        grid=(m // tm, f // tf),
        in_specs=in_specs,
        out_specs=out_specs,
        out_shape=out_shape,
        scratch_shapes=[pltpu.VMEM((tm, d), BF16)],
        compiler_params=_cparams(("parallel", "arbitrary")),
        name="ffn",
    )(*args)


def _ffn_meta_kernel(x_ref, nw_ref, wg_ref, wu_ref, wd_ref, pnw_ref,
                     hn_ref, wgb_ref, wub_ref, wdb_ref, xn_sc, acc_sc):
    j = pl.program_id(0)

    @pl.when(j == 0)
    def _():
        xn_sc[...] = _rms(x_ref[...], nw_ref[...]).astype(BF16)
        acc_sc[...] = jnp.zeros_like(acc_sc)

    wgb_ref[...] = wg_ref[...].astype(BF16)
    wub_ref[...] = wu_ref[...].astype(BF16)
    wdb_ref[...] = wd_ref[...].astype(BF16)
    acc_sc[...] += _swiglu_step(xn_sc[...], wgb_ref, wub_ref, wdb_ref)

    @pl.when(j == pl.num_programs(0) - 1)
    def _():
        hn_ref[...] = _rms(x_ref[...] + 0.5 * acc_sc[...], pnw_ref[...]).astype(BF16)


def _ffn_meta(xm, nw, wg, wu, wd, post_nw, *, tf):
    m, d = xm.shape
    n_ff = wg.shape[1] // tf
    vec_spec = pl.BlockSpec((1, d), lambda j: (0, 0))
    row_spec = pl.BlockSpec((m, d), lambda j: (0, 0))
    col_tile = pl.BlockSpec((d, tf), lambda j: (0, j))
    row_tile = pl.BlockSpec((tf, d), lambda j: (j, 0))
    return pl.pallas_call(
        _ffn_meta_kernel,
        grid=(n_ff,),
        in_specs=[row_spec, vec_spec, col_tile, col_tile, row_tile, vec_spec],
        out_specs=[row_spec, col_tile, col_tile, row_tile],
        out_shape=[jax.ShapeDtypeStruct((m, d), BF16), jax.ShapeDtypeStruct(wg.shape, BF16),
                   jax.ShapeDtypeStruct(wu.shape, BF16), jax.ShapeDtypeStruct(wd.shape, BF16)],
        scratch_shapes=[pltpu.VMEM((m, d), BF16), pltpu.VMEM((m, d), F32)],
        compiler_params=_cparams(("arbitrary",)),
        name="ffn_meta_cast",
    )(xm, nw, wg, wu, wd, post_nw)


_NT = (((1,), (1,)), ((), ()))


def _proj_kernel(x_ref, w_ref, out_ref, *, transpose):
    if transpose:
        acc = lax.dot_general(w_ref[...], x_ref[...], _NT, preferred_element_type=F32)
    else:
        acc = lax.dot_general(x_ref[...], w_ref[...], _NT, preferred_element_type=F32)
    out_ref[...] = acc.astype(out_ref.dtype)


def _proj(x2, w_t, *, tm, tn, row0, n, name, transpose=False):
    m, d = x2.shape
    r0 = row0 // tn
    if transpose:
        out_spec, out_shape = pl.BlockSpec((tn, tm), lambda i, j: (j, i)), (n, m)
    else:
        out_spec, out_shape = pl.BlockSpec((tm, tn), lambda i, j: (i, j)), (m, n)
    return pl.pallas_call(
        functools.partial(_proj_kernel, transpose=transpose),
        grid=(m // tm, n // tn),
        in_specs=[pl.BlockSpec((tm, d), lambda i, j: (i, 0)), pl.BlockSpec((tn, d), lambda i, j: (r0 + j, 0))],
        out_specs=out_spec,
        out_shape=jax.ShapeDtypeStruct(out_shape, BF16),
        compiler_params=_cparams(("parallel", "arbitrary")),
        name=name,
    )(x2, w_t)


def _proj_qk_kernel(x_ref, w_ref, gain_ref, rope_ref, bd_ref, out_ref, acc0_sc, acc1_sc, *, n_tiles):
    j = pl.program_id(1)
    bufs = (acc0_sc, acc1_sc)
    wide = bd_ref.shape[1]
    chunks = [slice(c * wide, (c + 1) * wide) for c in range(out_ref.shape[1] // wide)]

    for t in range(n_tiles + 1):
        @pl.when(j == t)
        def _(t=t):
            if t > 0:
                prev = bufs[(t - 1) % 2]
                normed = [_subhead_norm(prev[:, cols], gain_ref[0][:, cols], bd_ref[0]) for cols in chunks]
            if t < n_tiles:
                bufs[t % 2][...] = lax.dot_general(x_ref[...], w_ref[...], _NT, preferred_element_type=F32)
            if t > 0:
                for cols, yn in zip(chunks, normed):
                    out_ref[:, cols] = _partial_rotary(yn, rope_ref, bd_ref[1]).astype(out_ref.dtype)


def _proj_qk(x2, w_t, gain, rope, bd, *, tm, tn, n):
    m, d = x2.shape
    n_tab = rope.shape[1] // tm
    n_tiles = n // tn
    mm = lambda j: jnp.minimum(j, n_tiles - 1)
    ep = lambda j: jnp.maximum(j - 1, 0)
    return pl.pallas_call(
        functools.partial(_proj_qk_kernel, n_tiles=n_tiles),
        grid=(m // tm, n_tiles + 1),
        in_specs=[
            pl.BlockSpec((tm, d), lambda i, j: (i, 0)),
            pl.BlockSpec((tn, d), lambda i, j: (mm(j), 0)),
            pl.BlockSpec((1, 1, tn), lambda i, j: (ep(j), 0, 0)),
            pl.BlockSpec((2, tm, LANES), lambda i, j: (0, i % n_tab, 0)),
            pl.BlockSpec(bd.shape, lambda i, j: (0, 0, 0)),
        ],
        out_specs=pl.BlockSpec((tm, tn), lambda i, j: (i, ep(j))),
        out_shape=jax.ShapeDtypeStruct((m, n), BF16),
        scratch_shapes=[pltpu.VMEM((tm, tn), F32), pltpu.VMEM((tm, tn), F32)],
        compiler_params=_cparams(("parallel", "arbitrary")),
        name="proj_qk",
    )(x2, w_t, gain, rope, bd)


def _proj_dt_kernel(x_ref, w_ref, wt_ref, out_ref, outt_ref):
    x = x_ref[...]
    out_ref[...] = jnp.dot(x, w_ref[...], preferred_element_type=F32)
    outt_ref[...] = lax.dot_general(wt_ref[...], x, (((1,), (1,)), ((), ())), preferred_element_type=F32)


def _proj_dt(x2, w_pad, w_t, *, tm):
    m, d = x2.shape
    nh = w_t.shape[0]
    return pl.pallas_call(
        _proj_dt_kernel,
        grid=(m // tm,),
        in_specs=[
            pl.BlockSpec((tm, d), lambda i: (i, 0)),
            pl.BlockSpec((d, LANES), lambda i: (0, 0)),
            pl.BlockSpec((nh, d), lambda i: (0, 0)),
        ],
        out_specs=[pl.BlockSpec((tm, LANES), lambda i: (i, 0)), pl.BlockSpec((nh, tm), lambda i: (0, i))],
        out_shape=[jax.ShapeDtypeStruct((m, LANES), F32), jax.ShapeDtypeStruct((nh, m), F32)],
        compiler_params=_cparams(("parallel",)),
        name="proj_dt",
    )(x2, w_pad, w_t)


def _outproj_kernel(h_ref, a_ref, s_ref, wa_ref, ws_ref, out_ref):
    out_ref[...] = (h_ref[...]
                    + jnp.dot(a_ref[...], wa_ref[...], preferred_element_type=F32)
                    + jnp.dot(s_ref[...], ws_ref[...], preferred_element_type=F32))


def _outproj(h1, attn, ssd, w_o, *, tm, tn):
    m, d = h1.shape
    k = attn.shape[1]
    return pl.pallas_call(
        _outproj_kernel,
        grid=(m // tm, d // tn),
        in_specs=[
            pl.BlockSpec((tm, tn), lambda i, j: (i, j)),
            pl.BlockSpec((tm, k), lambda i, j: (i, 0)),
            pl.BlockSpec((tm, k), lambda i, j: (i, 0)),
            pl.BlockSpec((k, tn), lambda i, j: (0, j)),
            pl.BlockSpec((k, tn), lambda i, j: (1, j)),
        ],
        out_specs=pl.BlockSpec((tm, tn), lambda i, j: (i, j)),
        out_shape=jax.ShapeDtypeStruct((m, d), F32),
        compiler_params=_cparams(("parallel", "arbitrary")),
        name="outproj",
    )(h1, attn, ssd, w_o, w_o)


def _attn_kernel(qa_tab, ka_tab, qb_tab, kb_tab, q_ref, k_ref, vt_ref, km_ref, vmt_ref, lam_ref, gain_ref,
                 out_ref, qs_sc, sm_sc, s0_sc, s1_sc, mx0_sc, mx1_sc, m_sc, l_sc, acc_sc,
                 *, tq, dg, n_steps, lam_init):
    g = pl.program_id(2)
    qa = qa_tab[g]
    ka = ka_tab[g]
    qb = qb_tab[g]
    kb = kb_tab[g]
    has_a = g < n_steps
    has_b = g >= 1
    even = (g % 2) == 0
    nt = (((1,), (1,)), ((), ()))

    full_groups = [(slice(0, 2 * tq), tq)]
    diag_groups = [(slice(sub * tq + j * dg, sub * tq + (j + 1) * dg), (j + 1) * dg)
                   for sub in range(2) for j in range(tq // dg)]

    def stage_a(buf, diagonal):
        s_out, mx_out = buf
        for cols, nk in (diag_groups if diagonal else full_groups):
            s_t = lax.dot_general(k_ref[0:nk, :], qs_sc[cols, :], nt, preferred_element_type=F32)
            if diagonal:
                key = lax.broadcasted_iota(jnp.int32, s_t.shape, 0)
                qry = lax.broadcasted_iota(jnp.int32, s_t.shape, 1) + (nk - dg)
                s_t = jnp.where(key // CHUNK <= qry // CHUNK, s_t, -jnp.inf)
            s_out[0:nk, cols] = s_t
            mx_out[:, cols] = jnp.max(s_t, axis=0, keepdims=True)

    def stage_b(buf, last):
        s_in, mx_in = buf
        vt_aug = jnp.concatenate([vt_ref[...], jnp.ones((16, vt_ref.shape[1]), BF16)], axis=0)
        for cols, nk in (diag_groups if last else full_groups):
            m_prev = m_sc[:, cols]
            m_new = jnp.maximum(m_prev, mx_in[:, cols])
            alpha = jnp.exp2(m_prev - m_new)
            p = jnp.exp2(s_in[0:nk, cols] - m_new).astype(BF16)
            pv = jnp.dot(vt_aug[:, 0:nk], p, preferred_element_type=F32)
            l_sc[:, cols] = alpha * l_sc[:, cols] + pv[V_HEAD_DIM:V_HEAD_DIM + 1, :]
            acc_sc[:, cols] = alpha * acc_sc[:, cols] + pv[0:V_HEAD_DIM, :]
            m_sc[:, cols] = m_new
        if last:
            lq1, lk1, lq2, lk2 = lam_ref[0:1, :], lam_ref[1:2, :], lam_ref[2:3, :], lam_ref[3:4, :]
            lam = (jnp.exp(jnp.sum(lq1 * lk1, axis=-1, keepdims=True))
                   - jnp.exp(jnp.sum(lq2 * lk2, axis=-1, keepdims=True)) + lam_init)
            o = acc_sc[...] / l_sc[...]
            o = o[:, 0:tq] - lam * o[:, tq:2 * tq]
            ms = jnp.mean(o * o, axis=0, keepdims=True)
            o = o * lax.rsqrt(ms + EPS) * (gain_ref[...] * (1.0 - lam_init))
            out_ref[...] = o.T.astype(out_ref.dtype)

    @pl.when(has_b & (kb == 0))
    def _():
        s_t = sm_sc[...]
        m0 = jnp.max(s_t, axis=0, keepdims=True)
        p = jnp.exp2(s_t - m0)
        m_sc[...] = m0
        l_sc[...] = jnp.sum(p, axis=0, keepdims=True)
        acc_sc[...] = jnp.dot(vmt_ref[...], p.astype(BF16), preferred_element_type=F32)

    @pl.when(has_a & (ka == 0))
    def _():
        q = q_ref[...]
        lane = lax.broadcasted_iota(jnp.int32, q.shape, 1)
        zero = jnp.zeros_like(q)
        qs_sc[0:tq, :] = jnp.where(lane < DIFF_HEAD_DIM, q, zero)
        qs_sc[tq:2 * tq, :] = jnp.where(lane >= DIFF_HEAD_DIM, q, zero)
        s_t = lax.dot_general(km_ref[...], qs_sc[...], nt, preferred_element_type=F32)
        key = lax.broadcasted_iota(jnp.int32, s_t.shape, 0)
        sm_sc[...] = jnp.where(key < N_META, s_t, -jnp.inf)

    buf0, buf1 = (s0_sc, mx0_sc), (s1_sc, mx1_sc)
    a_diag = ka == qa
    b_last = kb == qb
    both = has_a & has_b
    for parity, (buf_a, buf_b) in enumerate(((buf0, buf1), (buf1, buf0))):
        par = even if parity == 0 else jnp.logical_not(even)

        @pl.when(par & both & jnp.logical_not(a_diag) & jnp.logical_not(b_last))
        def _(buf_a=buf_a, buf_b=buf_b):
            stage_a(buf_a, False)
            stage_b(buf_b, False)

        @pl.when(par & both & a_diag & jnp.logical_not(b_last))
        def _(buf_a=buf_a, buf_b=buf_b):
            stage_a(buf_a, True)
            stage_b(buf_b, False)

        @pl.when(par & both & b_last)
        def _(buf_a=buf_a, buf_b=buf_b):
            stage_a(buf_a, False)
            stage_b(buf_b, True)

        @pl.when(par & has_a & jnp.logical_not(has_b))
        def _(buf_a=buf_a):
            stage_a(buf_a, True)

        @pl.when(par & has_b & jnp.logical_not(has_a))
        def _(buf_b=buf_b):
            stage_b(buf_b, True)


def _attention(qk, vt, kmeta, vmeta_t, lam_vecs, gain_col, *, bsz, seq, tq, dg, lam_init):
    nq = seq // tq
    h = N_DIFF_HEADS
    qi_list, ki_list = [], []
    for a in range(nq):
        for b in range(a + 1):
            qi_list.append(a)
            ki_list.append(b)
    n_steps = len(qi_list)
    ia = np.minimum(np.arange(n_steps + 1), n_steps - 1)
    ib = np.maximum(np.arange(n_steps + 1) - 1, 0)
    qi_arr, ki_arr = np.array(qi_list, np.int32), np.array(ki_list, np.int32)
    tabs = [jnp.asarray(t) for t in (qi_arr[ia], ki_arr[ia], qi_arr[ib], ki_arr[ib])]
    grid_spec = pltpu.PrefetchScalarGridSpec(
        num_scalar_prefetch=4,
        grid=(bsz, h, n_steps + 1),
        in_specs=[
            pl.BlockSpec((tq, V_HEAD_DIM), lambda b, hh, g, qa, ka, qb, kb: (b * nq + qa[g], hh)),
            pl.BlockSpec((tq, V_HEAD_DIM), lambda b, hh, g, qa, ka, qb, kb: (b * nq + ka[g], h + hh)),
            pl.BlockSpec((V_HEAD_DIM, tq), lambda b, hh, g, qa, ka, qb, kb: (hh, b * nq + kb[g])),
            pl.BlockSpec((LANES, V_HEAD_DIM), lambda b, hh, g, qa, ka, qb, kb: (0, hh)),
            pl.BlockSpec((V_HEAD_DIM, LANES), lambda b, hh, g, qa, ka, qb, kb: (hh, 0)),
            pl.BlockSpec((4, DIFF_HEAD_DIM), lambda b, hh, g, qa, ka, qb, kb: (0, 0)),
            pl.BlockSpec((V_HEAD_DIM, 1), lambda b, hh, g, qa, ka, qb, kb: (0, 0)),
        ],
        out_specs=pl.BlockSpec((tq, V_HEAD_DIM), lambda b, hh, g, qa, ka, qb, kb: (b * nq + qb[g], hh)),
        scratch_shapes=[
            pltpu.VMEM((2 * tq, V_HEAD_DIM), BF16),
            pltpu.VMEM((LANES, 2 * tq), F32),
            pltpu.VMEM((tq, 2 * tq), F32),
            pltpu.VMEM((tq, 2 * tq), F32),
            pltpu.VMEM((1, 2 * tq), F32),
            pltpu.VMEM((1, 2 * tq), F32),
            pltpu.VMEM((1, 2 * tq), F32),
            pltpu.VMEM((1, 2 * tq), F32),
            pltpu.VMEM((V_HEAD_DIM, 2 * tq), F32),
        ],
    )
    return pl.pallas_call(
        functools.partial(_attn_kernel, tq=tq, dg=dg, n_steps=n_steps, lam_init=lam_init),
        grid_spec=grid_spec,
        out_shape=jax.ShapeDtypeStruct((bsz * seq, h * V_HEAD_DIM), BF16),
        compiler_params=_cparams(("parallel", "parallel", "arbitrary")),
        name="attn",
    )(*tabs, qk, qk, vt, kmeta, vmeta_t, lam_vecs, gain_col)


def _ssd_kernel(*refs, t_blk, bsz, n_cast):
    dtt_refs = refs[:bsz]
    cast_in = refs[bsz:bsz + n_cast]
    (zx_ref, dt_ref, xbcm_ref, dtm_ref, dttm_ref,
     convw_ref, convb_ref, dtb_ref, dtbt_ref, alog_ref, alogt_ref, dskip_ref, nw_ref,
     expand_ref, tril_ref, triu2_ref, shift_ref, out_ref) = refs[bsz + n_cast:bsz + n_cast + 18]
    cast_out = refs[bsz + n_cast + 18:bsz + 2 * n_cast + 18]
    xs_sc, xm_sc, state_sc = refs[bsz + 2 * n_cast + 18:]
    for src, dst in zip(cast_in, cast_out):
        dst[...] = src[...].astype(BF16)
    n_heads, p_dim, n_state, n_groups = N_SSD_HEADS, SSD_HEAD_DIM, D_STATE, N_SSD_GROUPS
    width = n_heads * p_dim
    gw = width // n_groups
    n_pairs = n_heads // 2
    nt = (((1,), (1,)), ((), ()))
    expand = expand_ref[...]
    a_row = -jnp.exp(alog_ref[...])
    a_col = -jnp.exp(alogt_ref[...])
    d_full = dskip_ref[...]
    lane = lax.broadcasted_iota(jnp.int32, (CHUNK, LANES), 1)
    row = lax.broadcasted_iota(jnp.int32, (CHUNK, LANES), 0)
    left = lane < p_dim
    causal2 = (lane & (CHUNK - 1)) <= row

    def chunk(win_f32, st_ref, dt_raw, dtt_raw, pad_rows, z, out_r0, b=0):
        win = win_f32.astype(BF16)
        taps = jnp.dot(shift_ref[...], win, preferred_element_type=F32)
        conv = convb_ref[...] + win_f32[HALO:, :] * convw_ref[CONV_WIDTH - 1:CONV_WIDTH, :]
        for k in range(CONV_WIDTH - 1):
            conv = conv + taps[k * CHUNK:(k + 1) * CHUNK, :] * convw_ref[k:k + 1, :]
        xc = _silu(conv)
        x_s = xc[:, :width]
        dt = jax.nn.softplus(dt_raw + dtb_ref[...])
        dtt = jax.nn.softplus(dtt_raw + dtbt_ref[...])
        if pad_rows:
            dt = jnp.where(row >= pad_rows, dt, 0.0)
            dtt = jnp.where(lax.broadcasted_iota(jnp.int32, dtt.shape, 1) >= pad_rows, dtt, 0.0)
        a_pieces = jnp.dot(tril_ref[...], jnp.concatenate(_split_bf16(dt * a_row, 3), axis=1),
                           preferred_element_type=F32)
        a_cs = a_pieces[:, 0:LANES] + a_pieces[:, LANES:2 * LANES] + a_pieces[:, 2 * LANES:3 * LANES]
        a_cs_t2 = _dot_f32_by_01(dtt * a_col, triu2_ref[...])
        spread = jnp.dot(jnp.concatenate(_split_bf16(a_cs, 3) + _split_bf16(dt, 3), axis=0), expand,
                         preferred_element_type=F32)
        a_full = spread[0:CHUNK] + spread[CHUNK:2 * CHUNK] + spread[2 * CHUNK:3 * CHUNK]
        dt_full = spread[3 * CHUNK:4 * CHUNK] + spread[4 * CHUNK:5 * CHUNK] + spread[5 * CHUNK:6 * CHUNK]
        a_last = a_full[CHUNK - 1:CHUNK, :]
        xdt = x_s * dt_full
        w_state = (xdt * jnp.exp(a_last - a_full)).astype(BF16)
        xdt_b = xdt.astype(BF16)
        zero_b = jnp.zeros((CHUNK, LANES), BF16)
        y_parts = []
        for g in range(n_groups):
            b_g = xc[:, width + g * n_state: width + (g + 1) * n_state]
            c_g = xc[:, width + (n_groups + g) * n_state: width + (n_groups + g + 1) * n_state]
            b_gb = b_g.astype(BF16)
            c_gb = c_g.astype(BF16)
            st = st_ref[g]
            if out_r0 is not None:
                cb2 = lax.dot_general(c_gb, jnp.concatenate([b_gb, b_gb], axis=0), nt,
                                      preferred_element_type=F32)
                y_off = jnp.dot(c_gb, st.astype(BF16), preferred_element_type=F32)
                y_g = y_off * jnp.exp(a_full[:, g * gw:(g + 1) * gw])
                diag = []
                for pp in range(n_pairs // n_groups):
                    pr = g * (n_pairs // n_groups) + pp
                    colb = a_full[:, pr * LANES:(pr + 1) * LANES]
                    rowb = jnp.where(left, a_cs_t2[2 * pr:2 * pr + 1, :], a_cs_t2[2 * pr + 1:2 * pr + 2, :])
                    dec = jnp.exp(jnp.where(causal2, colb - rowb, -jnp.inf))
                    m_pair = (cb2 * dec).astype(BF16)
                    xp = xdt_b[:, pr * LANES:(pr + 1) * LANES]
                    rhs = jnp.concatenate([jnp.where(left, xp, zero_b), jnp.where(left, zero_b, xp)], axis=0)
                    diag.append(jnp.dot(m_pair, rhs, preferred_element_type=F32))
                y_parts.append(y_g + jnp.concatenate(diag, axis=1))
            new_st = st * jnp.exp(a_last[:, g * gw:(g + 1) * gw]) + jnp.dot(
                b_g.T.astype(BF16), w_state[:, g * gw:(g + 1) * gw], preferred_element_type=F32)
            st_ref[g] = new_st
        if out_r0 is not None:
            y = jnp.concatenate(y_parts, axis=1) + x_s * d_full
            gated = y * _silu(z)
            out_ref[b, out_r0:out_r0 + CHUNK, :] = _rms(gated, nw_ref[...]).astype(out_ref.dtype)

    @pl.when(pl.program_id(0) == 0)
    def _():
        state_sc[...] = jnp.zeros_like(state_sc)
        xm_sc[0:HALO, :] = jnp.zeros((HALO, xm_sc.shape[1]), F32)
        xm_sc[HALO:HALO + CHUNK, :] = xbcm_ref[...].astype(F32)
        chunk(xm_sc[...], state_sc.at[0], dtm_ref[...], dttm_ref[...], CHUNK - N_META, None, None)
        for b in range(bsz):
            if b:
                state_sc[b] = state_sc[0]
            xs_sc[b, 0:HALO, :] = xm_sc[CHUNK:CHUNK + HALO, :]

    for b in range(bsz):
        xs_sc[b, HALO:HALO + t_blk, :] = zx_ref[b, :, width:].astype(F32)
    for c in range(t_blk // CHUNK):
        rows = slice(c * CHUNK, (c + 1) * CHUNK)
        for b in range(bsz):
            chunk(xs_sc[b, c * CHUNK:(c + 1) * CHUNK + HALO, :], state_sc.at[b], dt_ref[b, rows, :],
                  dtt_refs[b][:, rows], 0, zx_ref[b, rows, 0:width].astype(F32), c * CHUNK, b)
    for b in range(bsz):
        xs_sc[b, 0:HALO, :] = xs_sc[b, t_blk:t_blk + HALO, :]


def _ssd(zx, dt, dtt, xbc_m, dt_m, dtt_m, conv_w, conv_b, dt_bias, a_log, d_skip, norm_w, cast_weights,
         *, bsz, seq, t_blk):
    nb = seq // t_blk
    slab = lambda w: pl.BlockSpec((w.shape[0] // nb, w.shape[1]), lambda i: (i, 0))
    width = N_SSD_HEADS * SSD_HEAD_DIM
    cdim = zx.shape[1] - width
    nh = N_SSD_HEADS

    def pad_row(v):
        return jnp.pad(v.reshape(1, nh), ((0, 0), (0, LANES - nh)))

    hh = np.arange(LANES)[:, None]
    ll = np.arange(width)[None, :]
    one_hot = lambda m: jnp.asarray(m.astype(np.float32), dtype=BF16)
    expand = one_hot(ll // SSD_HEAD_DIM == hh)
    ii = np.arange(CHUNK)
    tril = one_hot(ii[None, :] <= ii[:, None])
    triu2 = one_hot(np.tile(ii[:, None] <= ii[None, :], (1, 2)))
    jj = np.arange(HALO + CHUNK)
    shift = one_hot(np.concatenate([jj[None, :] == ii[:, None] + HALO - (CONV_WIDTH - 1) + k
                                    for k in range(CONV_WIDTH - 1)]))
    full = lambda shape: pl.BlockSpec(shape, lambda i: tuple(0 for _ in shape))
    out, *cast = pl.pallas_call(
        functools.partial(_ssd_kernel, t_blk=t_blk, bsz=bsz, n_cast=len(cast_weights)),
        grid=(nb,),
        in_specs=[pl.BlockSpec((nh, t_blk), functools.partial(lambda i, b: (0, b * nb + i), b=b))
                  for b in range(bsz)] + [slab(w) for w in cast_weights] + [
            pl.BlockSpec((bsz, t_blk, width + cdim), lambda i: (0, i, 0)),
            pl.BlockSpec((bsz, t_blk, LANES), lambda i: (0, i, 0)),
            full((CHUNK, cdim)), full((CHUNK, LANES)), full((nh, CHUNK)),
            full((CONV_WIDTH, cdim)), full((1, cdim)), full((1, LANES)), full((nh, 1)),
            full((1, LANES)), full((nh, 1)), full((1, width)), full((1, width)),
            full((LANES, width)), full((CHUNK, CHUNK)), full((CHUNK, 2 * CHUNK)), full(shift.shape),
        ],
        out_specs=[pl.BlockSpec((bsz, t_blk, width), lambda i: (0, i, 0))] + [slab(w) for w in cast_weights],
        out_shape=[jax.ShapeDtypeStruct((bsz, seq, width), BF16)]
        + [jax.ShapeDtypeStruct(w.shape, BF16) for w in cast_weights],
        scratch_shapes=[
            pltpu.VMEM((bsz, t_blk + HALO, cdim), F32),
            pltpu.VMEM((CHUNK + HALO, cdim), F32),
            pltpu.VMEM((bsz, N_SSD_GROUPS, D_STATE, width // N_SSD_GROUPS), F32),
        ],
        compiler_params=_cparams(("arbitrary",)),
        name="ssd",
    )(*([dtt] * bsz), *cast_weights, zx.reshape(bsz, seq, width + cdim), dt.reshape(bsz, seq, LANES),
      xbc_m, dt_m, dtt_m,
      conv_w, conv_b.reshape(1, cdim), pad_row(dt_bias), dt_bias.reshape(nh, 1), pad_row(a_log),
      a_log.reshape(nh, 1), jnp.repeat(d_skip, SSD_HEAD_DIM).reshape(1, width),
      norm_w.reshape(1, width), expand, tril, triu2, shift)
    return out.reshape(bsz * seq, width), cast


def _rope_tables(n_pos):
    inv = jnp.power(ROPE_THETA, -jnp.arange(0, ROT_DIM, 2, dtype=F32) / ROT_DIM)
    ang = jnp.arange(n_pos, dtype=F32)[:, None] * inv[None, :]
    cos, sin = jnp.cos(ang), jnp.sin(ang)
    r = np.arange(LANES) % DIFF_HEAD_DIM
    idx = jnp.asarray(r % (ROT_DIM // 2))
    rotary = jnp.asarray(r < ROT_DIM)[None, :]
    return jnp.stack([jnp.where(rotary, cos[:, idx], 1.0), jnp.where(rotary, sin[:, idx], 0.0)])


def kernel(x, meta_tokens, ffn1_norm, ffn1_w_gate, ffn1_w_up, ffn1_w_down, mix_norm, w_in, q_norm, k_norm,
           lambda_q1, lambda_k1, lambda_q2, lambda_k2, attn_out_norm, conv_w, conv_b, dt_bias, a_log, d_skip,
           ssd_norm, w_out, ffn2_norm, ffn2_w_gate, ffn2_w_up, ffn2_w_down):
    bsz, seq, d = x.shape
    assert ffn1_norm.shape[0] == 1, "single-layer block"
    aw = N_DIFF_HEADS * V_HEAD_DIM
    sw = N_SSD_HEADS * SSD_HEAD_DIM
    cdim = sw + 2 * N_SSD_GROUPS * D_STATE
    lam_init = 0.8 - 0.6 * math.exp(-0.3 * 0)
    row = lambda v: v.reshape(1, -1)

    tn = 512
    w_in0 = w_in[0]
    w_dt = w_in0[:, 3 * aw + sw + cdim:].astype(BF16)
    w_dt_pad = jnp.pad(w_dt, ((0, 0), (0, LANES - N_SSD_HEADS)))
    w_dt_t = w_dt.T

    n_sub = aw // DIFF_HEAD_DIM
    q_gain = jnp.tile(q_norm[0], n_sub) * (DIFF_HEAD_DIM ** -0.5 * math.log2(math.e))
    k_gain = jnp.tile(k_norm[0], n_sub)
    qk_gain = jnp.concatenate([q_gain, k_gain]).reshape(2 * aw // tn, 1, tn)
    rope = _rope_tables(N_META + seq)
    li = np.arange(2 * LANES)
    gi, ri, half = li // DIFF_HEAD_DIM, li % DIFF_HEAD_DIM, ROT_DIM // 2
    avg = (gi[:, None] == gi[None, :]).astype(np.float32) / DIFF_HEAD_DIM
    perm = (np.where((ri < half)[None, :] & (li[:, None] == li[None, :] + half), -1.0, 0.0)
            + np.where(((ri >= half) & (ri < ROT_DIM))[None, :] & (li[:, None] == li[None, :] - half), 1.0, 0.0))
    bd = jnp.asarray(np.stack([avg, perm]), dtype=BF16)

    hn_m, wg1, wu1, wd1 = _ffn_meta(meta_tokens.astype(F32), row(ffn1_norm[0]), ffn1_w_gate[0],
                                    ffn1_w_up[0], ffn1_w_down[0], row(mix_norm[0]), tf=FFN_TF)
    w_in_t = w_in0.T.astype(BF16)
    x2 = x.reshape(bsz * seq, d)
    h1, hn = _ffn(x2, row(ffn1_norm[0]), wg1, wu1, wd1, row(mix_norm[0]), tm=FFN_TM, tf=FFN_TF)

    qk = _proj_qk(hn, w_in_t, qk_gain, rope[:, N_META:], bd, tm=1024, tn=tn, n=2 * aw)
    qk_m = _proj_qk(hn_m, w_in_t, qk_gain, rope[:, :N_META], bd, tm=N_META, tn=tn, n=2 * aw)
    v_t = _proj(hn, w_in_t, tm=1024, tn=tn, row0=2 * aw, n=aw, name="proj_vt", transpose=True)
    vzx_m = _proj(hn_m, w_in_t, tm=N_META, tn=tn, row0=2 * aw, n=aw + sw + cdim, name="proj_vzx_meta")
    zx = _proj(hn, w_in_t, tm=1024, tn=tn, row0=3 * aw, n=sw + cdim, name="proj_zx")
    v_t_m, xbc_m = vzx_m[:, :aw].T, vzx_m[:, aw + sw:]
    dt, dtt = _proj_dt(hn, w_dt_pad, w_dt_t, tm=1024)
    dt_m, dtt_m = _proj_dt(hn_m, w_dt_pad, w_dt_t, tm=N_META)

    kmeta = jnp.pad(qk_m[:, aw:], ((0, LANES - N_META), (0, 0)))
    vmeta_t = jnp.pad(v_t_m, ((0, 0), (0, LANES - N_META)))
    lam_vecs = jnp.stack([lambda_q1[0], lambda_k1[0], lambda_q2[0], lambda_k2[0]]).astype(F32)
    attn = _attention(qk, v_t, kmeta, vmeta_t, lam_vecs, attn_out_norm[0].reshape(V_HEAD_DIM, 1),
                      bsz=bsz, seq=seq, tq=1024, dg=256, lam_init=lam_init)

    lead = CHUNK - N_META
    ssd, (wg2, wu2, wd2, w_o) = _ssd(
        zx, dt, dtt,
        jnp.pad(xbc_m, ((lead, 0), (0, 0))), jnp.pad(dt_m, ((lead, 0), (0, 0))), jnp.pad(dtt_m, ((0, 0), (lead, 0))),
        conv_w[0], conv_b[0], dt_bias[0], a_log[0], d_skip[0], ssd_norm[0],
        (ffn2_w_gate[0], ffn2_w_up[0], ffn2_w_down[0], w_out[0]),
        bsz=bsz, seq=seq, t_blk=256)

    h2 = _outproj(h1, attn, ssd, w_o, tm=512, tn=d)
    (out,) = _ffn(h2, row(ffn2_norm[0]), wg2, wu2, wd2, tm=FFN_TM, tf=FFN_TF)
    return out.reshape(bsz, seq, d)
```

```python
import functools
import math

import jax
import jax.numpy as jnp
import numpy as np
from jax import lax
from jax.experimental import pallas as pl
from jax.experimental.pallas import tpu as pltpu

F32 = jnp.float32
BF16 = jnp.bfloat16

EPS = 1e-6
CHUNK = 64
N_META = 16
N_DIFF_HEADS = 8
DIFF_HEAD_DIM = 64
V_HEAD_DIM = 128
ROT_DIM = 16
ROPE_THETA = 500000.0
SSD_HEAD_DIM = 64
N_SSD_HEADS = 16
N_SSD_GROUPS = 2
D_STATE = 128
CONV_WIDTH = 4
HALO = 64
LANES = 128
VMEM_LIMIT = 56 * 1024 * 1024
FFN_TM = 512
FFN_TF = 512


def _cparams(sem):
    return pltpu.CompilerParams(dimension_semantics=sem, vmem_limit_bytes=VMEM_LIMIT)


def _silu(v):
    half = 0.5 * v
    return half + half * jnp.tanh(half)


def _split_bf16(a, pieces):
    out = []
    for _ in range(pieces - 1):
        p = a.astype(BF16)
        out.append(p)
        a = a - p.astype(F32)
    out.append(a.astype(BF16))
    return out


def _dot_f32_by_01(a, m01, pieces=3):
    return sum(jnp.dot(p, m01, preferred_element_type=F32) for p in _split_bf16(a, pieces))


def _rms(v, w):
    ms = jnp.mean(v * v, axis=-1, keepdims=True)
    return v * lax.rsqrt(ms + EPS) * w


def _swiglu_step(xn, wg_ref, wu_ref, wd_ref):
    g = jnp.dot(xn, wg_ref[...], preferred_element_type=F32)
    u = jnp.dot(xn, wu_ref[...], preferred_element_type=F32)
    a = (_silu(g) * u).astype(BF16)
    return jnp.dot(a, wd_ref[...], preferred_element_type=F32)


def _qk_norm_rope(y, gain, rope_ref, bd):
    cos, sin_lo, sin_hi = rope_ref[0], rope_ref[1], rope_ref[2]
    half = ROT_DIM // 2
    ms = sum(jnp.dot(p, bd, preferred_element_type=F32) for p in _split_bf16(y * y, 2))
    yn = y * lax.rsqrt(ms + EPS) * gain
    out = []
    for s in range(y.shape[1] // LANES):
        v = yn[:, s * LANES:(s + 1) * LANES]
        out.append(v * cos + pltpu.roll(v, half, 1) * sin_hi + pltpu.roll(v, LANES - half, 1) * sin_lo)
    return jnp.concatenate(out, axis=1)


def _ffn_kernel(x_ref, nw_ref, wg_ref, wu_ref, wd_ref, *rest, with_norm):
    if with_norm:
        pnw_ref, out_ref, hn_ref, xn_sc = rest
    else:
        out_ref, xn_sc = rest
    j = pl.program_id(1)

    @pl.when(j == 0)
    def _():
        xn_sc[...] = _rms(x_ref[...], nw_ref[...]).astype(BF16)
        out_ref[...] = jnp.zeros_like(out_ref)

    out_ref[...] += _swiglu_step(xn_sc[...], wg_ref, wu_ref, wd_ref)

    @pl.when(j == pl.num_programs(1) - 1)
    def _():
        h = x_ref[...] + 0.5 * out_ref[...]
        out_ref[...] = h
        if with_norm:
            hn_ref[...] = _rms(h, pnw_ref[...]).astype(BF16)


def _ffn(x2, nw, wg, wu, wd, post_nw=None, *, tm, tf):
    m, d = x2.shape
    f = wg.shape[1]
    with_norm = post_nw is not None
    row_spec = pl.BlockSpec((tm, d), lambda i, j: (i, 0))
    vec_spec = pl.BlockSpec((1, d), lambda i, j: (0, 0))
    in_specs = [row_spec, vec_spec,
                pl.BlockSpec((d, tf), lambda i, j: (0, j)),
                pl.BlockSpec((d, tf), lambda i, j: (0, j)),
                pl.BlockSpec((tf, d), lambda i, j: (j, 0))]
    args = [x2, nw, wg, wu, wd]
    out_specs, out_shape = [row_spec], [jax.ShapeDtypeStruct((m, d), F32)]
    if with_norm:
        in_specs.append(vec_spec)
        args.append(post_nw)
        out_specs.append(row_spec)
        out_shape.append(jax.ShapeDtypeStruct((m, d), BF16))
    return pl.pallas_call(
        functools.partial(_ffn_kernel, with_norm=with_norm),
        grid=(m // tm, f // tf),
        in_specs=in_specs,
        out_specs=out_specs,
        out_shape=out_shape,
        scratch_shapes=[pltpu.VMEM((tm, d), BF16)],
        compiler_params=_cparams(("parallel", "arbitrary")),
        name="ffn",
    )(*args)


def _ffn_meta_kernel(x_ref, nw_ref, wg_ref, wu_ref, wd_ref, pnw_ref,
                     hn_ref, wgb_ref, wub_ref, wdb_ref, xn_sc, acc_sc):
    j = pl.program_id(0)

    @pl.when(j == 0)
    def _():
        xn_sc[...] = _rms(x_ref[...], nw_ref[...]).astype(BF16)
        acc_sc[...] = jnp.zeros_like(acc_sc)

    wgb_ref[...] = wg_ref[...].astype(BF16)
    wub_ref[...] = wu_ref[...].astype(BF16)
    wdb_ref[...] = wd_ref[...].astype(BF16)
    acc_sc[...] += _swiglu_step(xn_sc[...], wgb_ref, wub_ref, wdb_ref)

    @pl.when(j == pl.num_programs(0) - 1)
    def _():
        hn_ref[...] = _rms(x_ref[...] + 0.5 * acc_sc[...], pnw_ref[...]).astype(BF16)


def _ffn_meta(xm, nw, wg, wu, wd, post_nw, *, tf):
    m, d = xm.shape
    n_ff = wg.shape[1] // tf
    vec_spec = pl.BlockSpec((1, d), lambda j: (0, 0))
    row_spec = pl.BlockSpec((m, d), lambda j: (0, 0))
    col_tile = pl.BlockSpec((d, tf), lambda j: (0, j))
    row_tile = pl.BlockSpec((tf, d), lambda j: (j, 0))
    return pl.pallas_call(
        _ffn_meta_kernel,
        grid=(n_ff,),
        in_specs=[row_spec, vec_spec, col_tile, col_tile, row_tile, vec_spec],
        out_specs=[row_spec, col_tile, col_tile, row_tile],
        out_shape=[jax.ShapeDtypeStruct((m, d), BF16), jax.ShapeDtypeStruct(wg.shape, BF16),
                   jax.ShapeDtypeStruct(wu.shape, BF16), jax.ShapeDtypeStruct(wd.shape, BF16)],
        scratch_shapes=[pltpu.VMEM((m, d), BF16), pltpu.VMEM((m, d), F32)],
        compiler_params=_cparams(("arbitrary",)),
        name="ffn_meta_cast",
    )(xm, nw, wg, wu, wd, post_nw)


_NT = (((1,), (1,)), ((), ()))


def _proj_kernel(x_ref, w_ref, out_ref, *, transpose):
    if transpose:
        acc = lax.dot_general(w_ref[...], x_ref[...], _NT, preferred_element_type=F32)
    else:
        acc = lax.dot_general(x_ref[...], w_ref[...], _NT, preferred_element_type=F32)
    out_ref[...] = acc.astype(out_ref.dtype)


def _proj(x2, w_t, *, tm, tn, row0, n, name, transpose=False):
    m, d = x2.shape
    r0 = row0 // tn
    if transpose:
        out_spec, out_shape = pl.BlockSpec((tn, tm), lambda i, j: (j, i)), (n, m)
    else:
        out_spec, out_shape = pl.BlockSpec((tm, tn), lambda i, j: (i, j)), (m, n)
    return pl.pallas_call(
        functools.partial(_proj_kernel, transpose=transpose),
        grid=(m // tm, n // tn),
        in_specs=[pl.BlockSpec((tm, d), lambda i, j: (i, 0)), pl.BlockSpec((tn, d), lambda i, j: (r0 + j, 0))],
        out_specs=out_spec,
        out_shape=jax.ShapeDtypeStruct(out_shape, BF16),
        compiler_params=_cparams(("parallel", "arbitrary")),
        name=name,
    )(x2, w_t)


def _proj_qk_kernel(x_ref, w_ref, gain_ref, rope_ref, bd_ref, out_ref, acc0_sc, acc1_sc, *, n_tiles):
    j = pl.program_id(1)
    bufs = (acc0_sc, acc1_sc)
    bd = bd_ref[...]
    wide = bd.shape[0]

    def finish(buf):
        for c in range(out_ref.shape[1] // wide):
            cols = slice(c * wide, (c + 1) * wide)
            out_ref[:, cols] = _qk_norm_rope(buf[:, cols], gain_ref[0][:, cols], rope_ref, bd).astype(out_ref.dtype)

    for t in range(n_tiles + 1):
        @pl.when(j == t)
        def _(t=t):
            if t > 0:
                finish(bufs[(t - 1) % 2])
            if t < n_tiles:
                bufs[t % 2][...] = lax.dot_general(x_ref[...], w_ref[...], _NT, preferred_element_type=F32)


def _proj_qk(x2, w_t, gain, rope, bd, *, tm, tn, n):
    m, d = x2.shape
    n_tab = rope.shape[1] // tm
    n_tiles = n // tn
    mm = lambda j: jnp.minimum(j, n_tiles - 1)
    ep = lambda j: jnp.maximum(j - 1, 0)
    return pl.pallas_call(
        functools.partial(_proj_qk_kernel, n_tiles=n_tiles),
        grid=(m // tm, n_tiles + 1),
        in_specs=[
            pl.BlockSpec((tm, d), lambda i, j: (i, 0)),
            pl.BlockSpec((tn, d), lambda i, j: (mm(j), 0)),
            pl.BlockSpec((1, 1, tn), lambda i, j: (ep(j), 0, 0)),
            pl.BlockSpec((3, tm, LANES), lambda i, j: (0, i % n_tab, 0)),
            pl.BlockSpec(bd.shape, lambda i, j: (0, 0)),
        ],
        out_specs=pl.BlockSpec((tm, tn), lambda i, j: (i, ep(j))),
        out_shape=jax.ShapeDtypeStruct((m, n), BF16),
        scratch_shapes=[pltpu.VMEM((tm, tn), F32), pltpu.VMEM((tm, tn), F32)],
        compiler_params=_cparams(("parallel", "arbitrary")),
        name="proj_qk",
    )(x2, w_t, gain, rope, bd)


def _proj_dt_kernel(x_ref, w_ref, wt_ref, out_ref, outt_ref):
    x = x_ref[...]
    out_ref[...] = jnp.dot(x, w_ref[...], preferred_element_type=F32)
    outt_ref[...] = lax.dot_general(wt_ref[...], x, (((1,), (1,)), ((), ())), preferred_element_type=F32)


def _proj_dt(x2, w_pad, w_t, *, tm):
    m, d = x2.shape
    nh = w_t.shape[0]
    return pl.pallas_call(
        _proj_dt_kernel,
        grid=(m // tm,),
        in_specs=[
            pl.BlockSpec((tm, d), lambda i: (i, 0)),
            pl.BlockSpec((d, LANES), lambda i: (0, 0)),
            pl.BlockSpec((nh, d), lambda i: (0, 0)),
        ],
        out_specs=[pl.BlockSpec((tm, LANES), lambda i: (i, 0)), pl.BlockSpec((nh, tm), lambda i: (0, i))],
        out_shape=[jax.ShapeDtypeStruct((m, LANES), F32), jax.ShapeDtypeStruct((nh, m), F32)],
        compiler_params=_cparams(("parallel",)),
        name="proj_dt",
    )(x2, w_pad, w_t)


def _outproj_kernel(h_ref, a_ref, s_ref, wa_ref, ws_ref, out_ref):
    out_ref[...] = (h_ref[...]
                    + jnp.dot(a_ref[...], wa_ref[...], preferred_element_type=F32)
                    + jnp.dot(s_ref[...], ws_ref[...], preferred_element_type=F32))


def _outproj(h1, attn, ssd, w_o, *, tm, tn):
    m, d = h1.shape
    k = attn.shape[1]
    return pl.pallas_call(
        _outproj_kernel,
        grid=(m // tm, d // tn),
        in_specs=[
            pl.BlockSpec((tm, tn), lambda i, j: (i, j)),
            pl.BlockSpec((tm, k), lambda i, j: (i, 0)),
            pl.BlockSpec((tm, k), lambda i, j: (i, 0)),
            pl.BlockSpec((k, tn), lambda i, j: (0, j)),
            pl.BlockSpec((k, tn), lambda i, j: (1, j)),
        ],
        out_specs=pl.BlockSpec((tm, tn), lambda i, j: (i, j)),
        out_shape=jax.ShapeDtypeStruct((m, d), F32),
        compiler_params=_cparams(("parallel", "arbitrary")),
        name="outproj",
    )(h1, attn, ssd, w_o, w_o)


def _attn_kernel(qa_tab, ka_tab, qb_tab, kb_tab, q_ref, k_ref, vt_ref, km_ref, vmt_ref, lam_ref, gain_ref,
                 out_ref, qs_sc, sm_sc, s0_sc, s1_sc, mx0_sc, mx1_sc, m_sc, l_sc, acc_sc,
                 *, tq, dg, n_steps, lam_init):
    g = pl.program_id(2)
    qa = qa_tab[g]
    ka = ka_tab[g]
    qb = qb_tab[g]
    kb = kb_tab[g]
    has_a = g < n_steps
    has_b = g >= 1
    even = (g % 2) == 0
    nt = (((1,), (1,)), ((), ()))

    full_groups = [(slice(0, 2 * tq), tq)]
    diag_groups = [(slice(sub * tq + j * dg, sub * tq + (j + 1) * dg), (j + 1) * dg)
                   for sub in range(2) for j in range(tq // dg)]

    def stage_a(buf, diagonal):
        s_out, mx_out = buf
        for cols, nk in (diag_groups if diagonal else full_groups):
            s_t = lax.dot_general(k_ref[0:nk, :], qs_sc[cols, :], nt, preferred_element_type=F32)
            if diagonal:
                key = lax.broadcasted_iota(jnp.int32, s_t.shape, 0)
                qry = lax.broadcasted_iota(jnp.int32, s_t.shape, 1) + (nk - dg)
                s_t = jnp.where(key // CHUNK <= qry // CHUNK, s_t, -jnp.inf)
            s_out[0:nk, cols] = s_t
            mx_out[:, cols] = jnp.max(s_t, axis=0, keepdims=True)

    def stage_b(buf, last):
        s_in, mx_in = buf
        vt_aug = jnp.concatenate([vt_ref[...], jnp.ones((16, vt_ref.shape[1]), BF16)], axis=0)
        for cols, nk in (diag_groups if last else full_groups):
            m_prev = m_sc[:, cols]
            m_new = jnp.maximum(m_prev, mx_in[:, cols])
            alpha = jnp.exp2(m_prev - m_new)
            p = jnp.exp2(s_in[0:nk, cols] - m_new).astype(BF16)
            pv = jnp.dot(vt_aug[:, 0:nk], p, preferred_element_type=F32)
            l_sc[:, cols] = alpha * l_sc[:, cols] + pv[V_HEAD_DIM:V_HEAD_DIM + 1, :]
            acc_sc[:, cols] = alpha * acc_sc[:, cols] + pv[0:V_HEAD_DIM, :]
            m_sc[:, cols] = m_new
        if last:
            lq1, lk1, lq2, lk2 = lam_ref[0:1, :], lam_ref[1:2, :], lam_ref[2:3, :], lam_ref[3:4, :]
            lam = (jnp.exp(jnp.sum(lq1 * lk1, axis=-1, keepdims=True))
                   - jnp.exp(jnp.sum(lq2 * lk2, axis=-1, keepdims=True)) + lam_init)
            o = acc_sc[...] / l_sc[...]
            o = o[:, 0:tq] - lam * o[:, tq:2 * tq]
            ms = jnp.mean(o * o, axis=0, keepdims=True)
            o = o * lax.rsqrt(ms + EPS) * (gain_ref[...] * (1.0 - lam_init))
            out_ref[...] = o.T.astype(out_ref.dtype)

    @pl.when(has_b & (kb == 0))
    def _():
        s_t = sm_sc[...]
        m0 = jnp.max(s_t, axis=0, keepdims=True)
        p = jnp.exp2(s_t - m0)
        m_sc[...] = m0
        l_sc[...] = jnp.sum(p, axis=0, keepdims=True)
        acc_sc[...] = jnp.dot(vmt_ref[...], p.astype(BF16), preferred_element_type=F32)

    @pl.when(has_a & (ka == 0))
    def _():
        q = q_ref[...]
        lane = lax.broadcasted_iota(jnp.int32, q.shape, 1)
        zero = jnp.zeros_like(q)
        qs_sc[0:tq, :] = jnp.where(lane < DIFF_HEAD_DIM, q, zero)
        qs_sc[tq:2 * tq, :] = jnp.where(lane >= DIFF_HEAD_DIM, q, zero)
        s_t = lax.dot_general(km_ref[...], qs_sc[...], nt, preferred_element_type=F32)
        key = lax.broadcasted_iota(jnp.int32, s_t.shape, 0)
        sm_sc[...] = jnp.where(key < N_META, s_t, -jnp.inf)

    buf0, buf1 = (s0_sc, mx0_sc), (s1_sc, mx1_sc)
    a_diag = ka == qa
    b_last = kb == qb
    both = has_a & has_b
    for parity, (buf_a, buf_b) in enumerate(((buf0, buf1), (buf1, buf0))):
        par = even if parity == 0 else jnp.logical_not(even)

        @pl.when(par & both & jnp.logical_not(a_diag) & jnp.logical_not(b_last))
        def _(buf_a=buf_a, buf_b=buf_b):
            stage_a(buf_a, False)
            stage_b(buf_b, False)

        @pl.when(par & both & a_diag & jnp.logical_not(b_last))
        def _(buf_a=buf_a, buf_b=buf_b):
            stage_a(buf_a, True)
            stage_b(buf_b, False)

        @pl.when(par & both & b_last)
        def _(buf_a=buf_a, buf_b=buf_b):
            stage_a(buf_a, False)
            stage_b(buf_b, True)

        @pl.when(par & has_a & jnp.logical_not(has_b))
        def _(buf_a=buf_a):
            stage_a(buf_a, True)

        @pl.when(par & has_b & jnp.logical_not(has_a))
        def _(buf_b=buf_b):
            stage_b(buf_b, True)


def _attention(qk, vt, kmeta, vmeta_t, lam_vecs, gain_col, *, bsz, seq, tq, dg, lam_init):
    nq = seq // tq
    h = N_DIFF_HEADS
    qi_list, ki_list = [], []
    for a in range(nq):
        for b in range(a + 1):
            qi_list.append(a)
            ki_list.append(b)
    n_steps = len(qi_list)
    ia = np.minimum(np.arange(n_steps + 1), n_steps - 1)
    ib = np.maximum(np.arange(n_steps + 1) - 1, 0)
    qi_arr, ki_arr = np.array(qi_list, np.int32), np.array(ki_list, np.int32)
    tabs = [jnp.asarray(t) for t in (qi_arr[ia], ki_arr[ia], qi_arr[ib], ki_arr[ib])]
    grid_spec = pltpu.PrefetchScalarGridSpec(
        num_scalar_prefetch=4,
        grid=(bsz, h, n_steps + 1),
        in_specs=[
            pl.BlockSpec((tq, V_HEAD_DIM), lambda b, hh, g, qa, ka, qb, kb: (b * nq + qa[g], hh)),
            pl.BlockSpec((tq, V_HEAD_DIM), lambda b, hh, g, qa, ka, qb, kb: (b * nq + ka[g], h + hh)),
            pl.BlockSpec((V_HEAD_DIM, tq), lambda b, hh, g, qa, ka, qb, kb: (hh, b * nq + kb[g])),
            pl.BlockSpec((LANES, V_HEAD_DIM), lambda b, hh, g, qa, ka, qb, kb: (0, hh)),
            pl.BlockSpec((V_HEAD_DIM, LANES), lambda b, hh, g, qa, ka, qb, kb: (hh, 0)),
            pl.BlockSpec((4, DIFF_HEAD_DIM), lambda b, hh, g, qa, ka, qb, kb: (0, 0)),
            pl.BlockSpec((V_HEAD_DIM, 1), lambda b, hh, g, qa, ka, qb, kb: (0, 0)),
        ],
        out_specs=pl.BlockSpec((tq, V_HEAD_DIM), lambda b, hh, g, qa, ka, qb, kb: (b * nq + qb[g], hh)),
        scratch_shapes=[
            pltpu.VMEM((2 * tq, V_HEAD_DIM), BF16),
            pltpu.VMEM((LANES, 2 * tq), F32),
            pltpu.VMEM((tq, 2 * tq), F32),
            pltpu.VMEM((tq, 2 * tq), F32),
            pltpu.VMEM((1, 2 * tq), F32),
            pltpu.VMEM((1, 2 * tq), F32),
            pltpu.VMEM((1, 2 * tq), F32),
            pltpu.VMEM((1, 2 * tq), F32),
            pltpu.VMEM((V_HEAD_DIM, 2 * tq), F32),
        ],
    )
    return pl.pallas_call(
        functools.partial(_attn_kernel, tq=tq, dg=dg, n_steps=n_steps, lam_init=lam_init),
        grid_spec=grid_spec,
        out_shape=jax.ShapeDtypeStruct((bsz * seq, h * V_HEAD_DIM), BF16),
        compiler_params=_cparams(("parallel", "parallel", "arbitrary")),
        name="attn",
    )(*tabs, qk, qk, vt, kmeta, vmeta_t, lam_vecs, gain_col)


def _ssd_kernel(*refs, t_blk, bsz, n_cast):
    dtt_refs = refs[:bsz]
    cast_in = refs[bsz:bsz + n_cast]
    (zx_ref, dt_ref, xbcm_ref, dtm_ref, dttm_ref,
     convw_ref, convb_ref, dtb_ref, dtbt_ref, alog_ref, alogt_ref, dskip_ref, nw_ref,
     expand_ref, tril_ref, triu2_ref, shift_ref, out_ref) = refs[bsz + n_cast:bsz + n_cast + 18]
    cast_out = refs[bsz + n_cast + 18:bsz + 2 * n_cast + 18]
    xs_sc, xm_sc, state_sc = refs[bsz + 2 * n_cast + 18:]
    for src, dst in zip(cast_in, cast_out):
        dst[...] = src[...].astype(BF16)
    n_heads, p_dim, n_state, n_groups = N_SSD_HEADS, SSD_HEAD_DIM, D_STATE, N_SSD_GROUPS
    width = n_heads * p_dim
    gw = width // n_groups
    n_pairs = n_heads // 2
    nt = (((1,), (1,)), ((), ()))
    expand = expand_ref[...]
    a_row = -jnp.exp(alog_ref[...])
    a_col = -jnp.exp(alogt_ref[...])
    d_full = dskip_ref[...]
    lane = lax.broadcasted_iota(jnp.int32, (CHUNK, LANES), 1)
    row = lax.broadcasted_iota(jnp.int32, (CHUNK, LANES), 0)
    left = lane < p_dim
    causal2 = (lane & (CHUNK - 1)) <= row

    def chunk(win_f32, st_ref, dt_raw, dtt_raw, pad_rows, z, out_r0, b=0):
        win = win_f32.astype(BF16)
        taps = jnp.dot(shift_ref[...], win, preferred_element_type=F32)
        conv = convb_ref[...] + win_f32[HALO:, :] * convw_ref[CONV_WIDTH - 1:CONV_WIDTH, :]
        for k in range(CONV_WIDTH - 1):
            conv = conv + taps[k * CHUNK:(k + 1) * CHUNK, :] * convw_ref[k:k + 1, :]
        xc = _silu(conv)
        x_s = xc[:, :width]
        dt = jax.nn.softplus(dt_raw + dtb_ref[...])
        dtt = jax.nn.softplus(dtt_raw + dtbt_ref[...])
        if pad_rows:
            dt = jnp.where(row >= pad_rows, dt, 0.0)
            dtt = jnp.where(lax.broadcasted_iota(jnp.int32, dtt.shape, 1) >= pad_rows, dtt, 0.0)
        a_pieces = jnp.dot(tril_ref[...], jnp.concatenate(_split_bf16(dt * a_row, 3), axis=1),
                           preferred_element_type=F32)
        a_cs = a_pieces[:, 0:LANES] + a_pieces[:, LANES:2 * LANES] + a_pieces[:, 2 * LANES:3 * LANES]
        a_cs_t2 = _dot_f32_by_01(dtt * a_col, triu2_ref[...])
        spread = jnp.dot(jnp.concatenate(_split_bf16(a_cs, 3) + _split_bf16(dt, 3), axis=0), expand,
                         preferred_element_type=F32)
        a_full = spread[0:CHUNK] + spread[CHUNK:2 * CHUNK] + spread[2 * CHUNK:3 * CHUNK]
        dt_full = spread[3 * CHUNK:4 * CHUNK] + spread[4 * CHUNK:5 * CHUNK] + spread[5 * CHUNK:6 * CHUNK]
        a_last = a_full[CHUNK - 1:CHUNK, :]
        xdt = x_s * dt_full
        w_state = (xdt * jnp.exp(a_last - a_full)).astype(BF16)
        xdt_b = xdt.astype(BF16)
        zero_b = jnp.zeros((CHUNK, LANES), BF16)
        y_parts = []
        for g in range(n_groups):
            b_g = xc[:, width + g * n_state: width + (g + 1) * n_state]
            c_g = xc[:, width + (n_groups + g) * n_state: width + (n_groups + g + 1) * n_state]
            b_gb = b_g.astype(BF16)
            c_gb = c_g.astype(BF16)
            st = st_ref[g]
            if out_r0 is not None:
                cb2 = lax.dot_general(c_gb, jnp.concatenate([b_gb, b_gb], axis=0), nt,
                                      preferred_element_type=F32)
                y_off = jnp.dot(c_gb, st.astype(BF16), preferred_element_type=F32)
                y_g = y_off * jnp.exp(a_full[:, g * gw:(g + 1) * gw])
                diag = []
                for pp in range(n_pairs // n_groups):
                    pr = g * (n_pairs // n_groups) + pp
                    colb = a_full[:, pr * LANES:(pr + 1) * LANES]
                    rowb = jnp.where(left, a_cs_t2[2 * pr:2 * pr + 1, :], a_cs_t2[2 * pr + 1:2 * pr + 2, :])
                    dec = jnp.exp(jnp.where(causal2, colb - rowb, -jnp.inf))
                    m_pair = (cb2 * dec).astype(BF16)
                    xp = xdt_b[:, pr * LANES:(pr + 1) * LANES]
                    rhs = jnp.concatenate([jnp.where(left, xp, zero_b), jnp.where(left, zero_b, xp)], axis=0)
                    diag.append(jnp.dot(m_pair, rhs, preferred_element_type=F32))
                y_parts.append(y_g + jnp.concatenate(diag, axis=1))
            new_st = st * jnp.exp(a_last[:, g * gw:(g + 1) * gw]) + jnp.dot(
                b_g.T.astype(BF16), w_state[:, g * gw:(g + 1) * gw], preferred_element_type=F32)
            st_ref[g] = new_st
        if out_r0 is not None:
            y = jnp.concatenate(y_parts, axis=1) + x_s * d_full
            gated = y * _silu(z)
            out_ref[b, out_r0:out_r0 + CHUNK, :] = _rms(gated, nw_ref[...]).astype(out_ref.dtype)

    @pl.when(pl.program_id(0) == 0)
    def _():
        state_sc[...] = jnp.zeros_like(state_sc)
        xm_sc[0:HALO, :] = jnp.zeros((HALO, xm_sc.shape[1]), F32)
        xm_sc[HALO:HALO + CHUNK, :] = xbcm_ref[...].astype(F32)
        chunk(xm_sc[...], state_sc.at[0], dtm_ref[...], dttm_ref[...], CHUNK - N_META, None, None)
        for b in range(bsz):
            if b:
                state_sc[b] = state_sc[0]
            xs_sc[b, 0:HALO, :] = xm_sc[CHUNK:CHUNK + HALO, :]

    for b in range(bsz):
        xs_sc[b, HALO:HALO + t_blk, :] = zx_ref[b, :, width:].astype(F32)
    for c in range(t_blk // CHUNK):
        rows = slice(c * CHUNK, (c + 1) * CHUNK)
        for b in range(bsz):
            chunk(xs_sc[b, c * CHUNK:(c + 1) * CHUNK + HALO, :], state_sc.at[b], dt_ref[b, rows, :],
                  dtt_refs[b][:, rows], 0, zx_ref[b, rows, 0:width].astype(F32), c * CHUNK, b)
    for b in range(bsz):
        xs_sc[b, 0:HALO, :] = xs_sc[b, t_blk:t_blk + HALO, :]


def _ssd(zx, dt, dtt, xbc_m, dt_m, dtt_m, conv_w, conv_b, dt_bias, a_log, d_skip, norm_w, cast_weights,
         *, bsz, seq, t_blk):
    nb = seq // t_blk
    slab = lambda w: pl.BlockSpec((w.shape[0] // nb, w.shape[1]), lambda i: (i, 0))
    width = N_SSD_HEADS * SSD_HEAD_DIM
    cdim = zx.shape[1] - width
    nh = N_SSD_HEADS

    def pad_row(v):
        return jnp.pad(v.reshape(1, nh), ((0, 0), (0, LANES - nh)))

    hh = np.arange(LANES)[:, None]
    ll = np.arange(width)[None, :]
    one_hot = lambda m: jnp.asarray(m.astype(np.float32), dtype=BF16)
    expand = one_hot(ll // SSD_HEAD_DIM == hh)
    ii = np.arange(CHUNK)
    tril = one_hot(ii[None, :] <= ii[:, None])
    triu2 = one_hot(np.tile(ii[:, None] <= ii[None, :], (1, 2)))
    jj = np.arange(HALO + CHUNK)
    shift = one_hot(np.concatenate([jj[None, :] == ii[:, None] + HALO - (CONV_WIDTH - 1) + k
                                    for k in range(CONV_WIDTH - 1)]))
    full = lambda shape: pl.BlockSpec(shape, lambda i: tuple(0 for _ in shape))
    out, *cast = pl.pallas_call(
        functools.partial(_ssd_kernel, t_blk=t_blk, bsz=bsz, n_cast=len(cast_weights)),
        grid=(nb,),
        in_specs=[pl.BlockSpec((nh, t_blk), functools.partial(lambda i, b: (0, b * nb + i), b=b))
                  for b in range(bsz)] + [slab(w) for w in cast_weights] + [
            pl.BlockSpec((bsz, t_blk, width + cdim), lambda i: (0, i, 0)),
            pl.BlockSpec((bsz, t_blk, LANES), lambda i: (0, i, 0)),
            full((CHUNK, cdim)), full((CHUNK, LANES)), full((nh, CHUNK)),
            full((CONV_WIDTH, cdim)), full((1, cdim)), full((1, LANES)), full((nh, 1)),
            full((1, LANES)), full((nh, 1)), full((1, width)), full((1, width)),
            full((LANES, width)), full((CHUNK, CHUNK)), full((CHUNK, 2 * CHUNK)), full(shift.shape),
        ],
        out_specs=[pl.BlockSpec((bsz, t_blk, width), lambda i: (0, i, 0))] + [slab(w) for w in cast_weights],
        out_shape=[jax.ShapeDtypeStruct((bsz, seq, width), BF16)]
        + [jax.ShapeDtypeStruct(w.shape, BF16) for w in cast_weights],
        scratch_shapes=[
            pltpu.VMEM((bsz, t_blk + HALO, cdim), F32),
            pltpu.VMEM((CHUNK + HALO, cdim), F32),
            pltpu.VMEM((bsz, N_SSD_GROUPS, D_STATE, width // N_SSD_GROUPS), F32),
        ],
        compiler_params=_cparams(("arbitrary",)),
        name="ssd",
    )(*([dtt] * bsz), *cast_weights, zx.reshape(bsz, seq, width + cdim), dt.reshape(bsz, seq, LANES),
      xbc_m, dt_m, dtt_m,
      conv_w, conv_b.reshape(1, cdim), pad_row(dt_bias), dt_bias.reshape(nh, 1), pad_row(a_log),
      a_log.reshape(nh, 1), jnp.repeat(d_skip, SSD_HEAD_DIM).reshape(1, width),
      norm_w.reshape(1, width), expand, tril, triu2, shift)
    return out.reshape(bsz * seq, width), cast


def _rope_tables(n_pos):
    inv = jnp.power(ROPE_THETA, -jnp.arange(0, ROT_DIM, 2, dtype=F32) / ROT_DIM)
    ang = jnp.arange(n_pos, dtype=F32)[:, None] * inv[None, :]
    cos, sin = jnp.cos(ang), jnp.sin(ang)
    half = ROT_DIM // 2
    r = np.arange(LANES) % DIFF_HEAD_DIM
    idx = jnp.asarray(r % half)
    lo = jnp.asarray(r < half)[None, :]
    hi = jnp.asarray((r >= half) & (r < ROT_DIM))[None, :]
    cos_t = jnp.where(lo | hi, cos[:, idx], 1.0)
    sin_lo = jnp.where(lo, -sin[:, idx], 0.0)
    sin_hi = jnp.where(hi, sin[:, idx], 0.0)
    return jnp.stack([cos_t, sin_lo, sin_hi])


def kernel(x, meta_tokens, ffn1_norm, ffn1_w_gate, ffn1_w_up, ffn1_w_down, mix_norm, w_in, q_norm, k_norm,
           lambda_q1, lambda_k1, lambda_q2, lambda_k2, attn_out_norm, conv_w, conv_b, dt_bias, a_log, d_skip,
           ssd_norm, w_out, ffn2_norm, ffn2_w_gate, ffn2_w_up, ffn2_w_down):
    bsz, seq, d = x.shape
    assert ffn1_norm.shape[0] == 1, "single-layer block"
    aw = N_DIFF_HEADS * V_HEAD_DIM
    sw = N_SSD_HEADS * SSD_HEAD_DIM
    cdim = sw + 2 * N_SSD_GROUPS * D_STATE
    lam_init = 0.8 - 0.6 * math.exp(-0.3 * 0)
    row = lambda v: v.reshape(1, -1)

    tn = 512
    w_in0 = w_in[0]
    w_dt = w_in0[:, 3 * aw + sw + cdim:].astype(BF16)
    w_dt_pad = jnp.pad(w_dt, ((0, 0), (0, LANES - N_SSD_HEADS)))
    w_dt_t = w_dt.T

    n_sub = aw // DIFF_HEAD_DIM
    q_gain = jnp.tile(q_norm[0], n_sub) * (DIFF_HEAD_DIM ** -0.5 * math.log2(math.e))
    k_gain = jnp.tile(k_norm[0], n_sub)
    qk_gain = jnp.concatenate([q_gain, k_gain]).reshape(2 * aw // tn, 1, tn)
    rope = _rope_tables(N_META + seq)
    gi = np.arange(2 * LANES) // DIFF_HEAD_DIM
    bd = jnp.asarray((gi[:, None] == gi[None, :]).astype(np.float32) / DIFF_HEAD_DIM, dtype=BF16)

    hn_m, wg1, wu1, wd1 = _ffn_meta(meta_tokens.astype(F32), row(ffn1_norm[0]), ffn1_w_gate[0],
                                    ffn1_w_up[0], ffn1_w_down[0], row(mix_norm[0]), tf=FFN_TF)
    w_in_t = w_in0.T.astype(BF16)
    x2 = x.reshape(bsz * seq, d)
    h1, hn = _ffn(x2, row(ffn1_norm[0]), wg1, wu1, wd1, row(mix_norm[0]), tm=FFN_TM, tf=FFN_TF)

    qk = _proj_qk(hn, w_in_t, qk_gain, rope[:, N_META:], bd, tm=1024, tn=tn, n=2 * aw)
    qk_m = _proj_qk(hn_m, w_in_t, qk_gain, rope[:, :N_META], bd, tm=N_META, tn=tn, n=2 * aw)
    v_t = _proj(hn, w_in_t, tm=2048, tn=tn, row0=2 * aw, n=aw, name="proj_vt", transpose=True)
    vzx_m = _proj(hn_m, w_in_t, tm=N_META, tn=tn, row0=2 * aw, n=aw + sw + cdim, name="proj_vzx_meta")
    zx = _proj(hn, w_in_t, tm=2048, tn=tn, row0=3 * aw, n=sw + cdim, name="proj_zx")
    v_t_m, xbc_m = vzx_m[:, :aw].T, vzx_m[:, aw + sw:]
    dt, dtt = _proj_dt(hn, w_dt_pad, w_dt_t, tm=1024)
    dt_m, dtt_m = _proj_dt(hn_m, w_dt_pad, w_dt_t, tm=N_META)

    kmeta = jnp.pad(qk_m[:, aw:], ((0, LANES - N_META), (0, 0)))
    vmeta_t = jnp.pad(v_t_m, ((0, 0), (0, LANES - N_META)))
    lam_vecs = jnp.stack([lambda_q1[0], lambda_k1[0], lambda_q2[0], lambda_k2[0]]).astype(F32)
    attn = _attention(qk, v_t, kmeta, vmeta_t, lam_vecs, attn_out_norm[0].reshape(V_HEAD_DIM, 1),
                      bsz=bsz, seq=seq, tq=1024, dg=256, lam_init=lam_init)

    lead = CHUNK - N_META
    ssd, (wg2, wu2, wd2, w_o) = _ssd(
        zx, dt, dtt,
        jnp.pad(xbc_m, ((lead, 0), (0, 0))), jnp.pad(dt_m, ((lead, 0), (0, 0))), jnp.pad(dtt_m, ((0, 0), (lead, 0))),
        conv_w[0], conv_b[0], dt_bias[0], a_log[0], d_skip[0], ssd_norm[0],
        (ffn2_w_gate[0], ffn2_w_up[0], ffn2_w_down[0], w_out[0]),
        bsz=bsz, seq=seq, t_blk=256)

    h2 = _outproj(h1, attn, ssd, w_o, tm=512, tn=d)
    (out,) = _ffn(h2, row(ffn2_norm[0]), wg2, wu2, wd2, tm=FFN_TM, tf=FFN_TF)
    return out.reshape(bsz, seq, d)
```

```python
import functools
import math

import jax
import jax.numpy as jnp
import numpy as np
from jax import lax
from jax.experimental import pallas as pl
from jax.experimental.pallas import tpu as pltpu

F32 = jnp.float32
BF16 = jnp.bfloat16

EPS = 1e-6
CHUNK = 64
N_META = 16
N_DIFF_HEADS = 8
DIFF_HEAD_DIM = 64
V_HEAD_DIM = 128
ROT_DIM = 16
ROPE_THETA = 500000.0
SSD_HEAD_DIM = 64
N_SSD_HEADS = 16
N_SSD_GROUPS = 2
D_STATE = 128
CONV_WIDTH = 4
HALO = 64
LANES = 128
VMEM_LIMIT = 56 * 1024 * 1024
FFN_TM = 512
FFN_TF = 512


def _cparams(sem):
    return pltpu.CompilerParams(dimension_semantics=sem, vmem_limit_bytes=VMEM_LIMIT)


def _silu(v):
    half = 0.5 * v
    return half + half * jnp.tanh(half)


def _split_bf16(a, pieces):
    out = []
    for _ in range(pieces - 1):
        p = a.astype(BF16)
        out.append(p)
        a = a - p.astype(F32)
    out.append(a.astype(BF16))
    return out


def _dot_f32_by_01(a, m01, pieces=3):
    return sum(jnp.dot(p, m01, preferred_element_type=F32) for p in _split_bf16(a, pieces))


def _rms(v, w):
    ms = jnp.mean(v * v, axis=-1, keepdims=True)
    return v * lax.rsqrt(ms + EPS) * w


def _swiglu_step(xn, wg_ref, wu_ref, wd_ref):
    g = jnp.dot(xn, wg_ref[...], preferred_element_type=F32)
    u = jnp.dot(xn, wu_ref[...], preferred_element_type=F32)
    a = (_silu(g) * u).astype(BF16)
    return jnp.dot(a, wd_ref[...], preferred_element_type=F32)


def _qk_norm_rope(y, gain, rope_ref, bd):
    cos, sin_lo, sin_hi = rope_ref[0], rope_ref[1], rope_ref[2]
    half = ROT_DIM // 2
    ms = sum(jnp.dot(p, bd, preferred_element_type=F32) for p in _split_bf16(y * y, 2))
    yn = y * lax.rsqrt(ms + EPS) * gain
    out = []
    for s in range(y.shape[1] // LANES):
        v = yn[:, s * LANES:(s + 1) * LANES]
        out.append(v * cos + pltpu.roll(v, half, 1) * sin_hi + pltpu.roll(v, LANES - half, 1) * sin_lo)
    return jnp.concatenate(out, axis=1)


def _ffn_kernel(x_ref, nw_ref, wg_ref, wu_ref, wd_ref, *rest, with_norm):
    if with_norm:
        pnw_ref, out_ref, hn_ref, xn_sc = rest
    else:
        out_ref, xn_sc = rest
    j = pl.program_id(1)

    @pl.when(j == 0)
    def _():
        xn_sc[...] = _rms(x_ref[...], nw_ref[...]).astype(BF16)
        out_ref[...] = jnp.zeros_like(out_ref)

    out_ref[...] += _swiglu_step(xn_sc[...], wg_ref, wu_ref, wd_ref)

    @pl.when(j == pl.num_programs(1) - 1)
    def _():
        h = x_ref[...] + 0.5 * out_ref[...]
        out_ref[...] = h
        if with_norm:
            hn_ref[...] = _rms(h, pnw_ref[...]).astype(BF16)


def _ffn(x2, nw, wg, wu, wd, post_nw=None, *, tm, tf):
    m, d = x2.shape
    f = wg.shape[1]
    with_norm = post_nw is not None
    row_spec = pl.BlockSpec((tm, d), lambda i, j: (i, 0))
    vec_spec = pl.BlockSpec((1, d), lambda i, j: (0, 0))
    in_specs = [row_spec, vec_spec,
                pl.BlockSpec((d, tf), lambda i, j: (0, j)),
                pl.BlockSpec((d, tf), lambda i, j: (0, j)),
                pl.BlockSpec((tf, d), lambda i, j: (j, 0))]
    args = [x2, nw, wg, wu, wd]
    out_specs, out_shape = [row_spec], [jax.ShapeDtypeStruct((m, d), F32)]
    if with_norm:
        in_specs.append(vec_spec)
        args.append(post_nw)
        out_specs.append(row_spec)
        out_shape.append(jax.ShapeDtypeStruct((m, d), BF16))
    return pl.pallas_call(
        functools.partial(_ffn_kernel, with_norm=with_norm),
        grid=(m // tm, f // tf),
        in_specs=in_specs,
        out_specs=out_specs,
        out_shape=out_shape,
        scratch_shapes=[pltpu.VMEM((tm, d), BF16)],
        compiler_params=_cparams(("parallel", "arbitrary")),
        name="ffn",
    )(*args)


def _ffn_meta_kernel(x_ref, nw_ref, wg_ref, wu_ref, wd_ref, pnw_ref,
                     hn_ref, wgb_ref, wub_ref, wdb_ref, xn_sc, acc_sc):
    j = pl.program_id(0)

    @pl.when(j == 0)
    def _():
        xn_sc[...] = _rms(x_ref[...], nw_ref[...]).astype(BF16)
        acc_sc[...] = jnp.zeros_like(acc_sc)

    wgb_ref[...] = wg_ref[...].astype(BF16)
    wub_ref[...] = wu_ref[...].astype(BF16)
    wdb_ref[...] = wd_ref[...].astype(BF16)
    acc_sc[...] += _swiglu_step(xn_sc[...], wgb_ref, wub_ref, wdb_ref)

    @pl.when(j == pl.num_programs(0) - 1)
    def _():
        hn_ref[...] = _rms(x_ref[...] + 0.5 * acc_sc[...], pnw_ref[...]).astype(BF16)


def _ffn_meta(xm, nw, wg, wu, wd, post_nw, *, tf):
    m, d = xm.shape
    n_ff = wg.shape[1] // tf
    vec_spec = pl.BlockSpec((1, d), lambda j: (0, 0))
    row_spec = pl.BlockSpec((m, d), lambda j: (0, 0))
    col_tile = pl.BlockSpec((d, tf), lambda j: (0, j))
    row_tile = pl.BlockSpec((tf, d), lambda j: (j, 0))
    return pl.pallas_call(
        _ffn_meta_kernel,
        grid=(n_ff,),
        in_specs=[row_spec, vec_spec, col_tile, col_tile, row_tile, vec_spec],
        out_specs=[row_spec, col_tile, col_tile, row_tile],
        out_shape=[jax.ShapeDtypeStruct((m, d), BF16), jax.ShapeDtypeStruct(wg.shape, BF16),
                   jax.ShapeDtypeStruct(wu.shape, BF16), jax.ShapeDtypeStruct(wd.shape, BF16)],
        scratch_shapes=[pltpu.VMEM((m, d), BF16), pltpu.VMEM((m, d), F32)],
        compiler_params=_cparams(("arbitrary",)),
        name="ffn_meta_cast",
    )(xm, nw, wg, wu, wd, post_nw)


_NT = (((1,), (1,)), ((), ()))


def _proj_kernel(x_ref, w_ref, out_ref, *, transpose):
    if transpose:
        acc = lax.dot_general(w_ref[...], x_ref[...], _NT, preferred_element_type=F32)
    else:
        acc = lax.dot_general(x_ref[...], w_ref[...], _NT, preferred_element_type=F32)
    out_ref[...] = acc.astype(out_ref.dtype)


def _proj(x2, w_t, *, tm, tn, row0, n, name, transpose=False):
    m, d = x2.shape
    r0 = row0 // tn
    if transpose:
        out_spec, out_shape = pl.BlockSpec((tn, tm), lambda i, j: (j, i)), (n, m)
    else:
        out_spec, out_shape = pl.BlockSpec((tm, tn), lambda i, j: (i, j)), (m, n)
    return pl.pallas_call(
        functools.partial(_proj_kernel, transpose=transpose),
        grid=(m // tm, n // tn),
        in_specs=[pl.BlockSpec((tm, d), lambda i, j: (i, 0)), pl.BlockSpec((tn, d), lambda i, j: (r0 + j, 0))],
        out_specs=out_spec,
        out_shape=jax.ShapeDtypeStruct(out_shape, BF16),
        compiler_params=_cparams(("parallel", "arbitrary")),
        name=name,
    )(x2, w_t)


def _proj_qk_kernel(x_ref, w_ref, gain_ref, rope_ref, bd_ref, out_ref, acc0_sc, acc1_sc, *, n_tiles):
    j = pl.program_id(1)
    bufs = (acc0_sc, acc1_sc)
    bd = bd_ref[...]
    wide = bd.shape[0]

    def finish(buf):
        for c in range(out_ref.shape[1] // wide):
            cols = slice(c * wide, (c + 1) * wide)
            out_ref[:, cols] = _qk_norm_rope(buf[:, cols], gain_ref[0][:, cols], rope_ref, bd).astype(out_ref.dtype)

    for t in range(n_tiles + 1):
        @pl.when(j == t)
        def _(t=t):
            if t > 0:
                finish(bufs[(t - 1) % 2])
            if t < n_tiles:
                bufs[t % 2][...] = lax.dot_general(x_ref[...], w_ref[...], _NT, preferred_element_type=F32)


def _proj_qk(x2, w_t, gain, rope, bd, *, tm, tn, n):
    m, d = x2.shape
    n_tab = rope.shape[1] // tm
    n_tiles = n // tn
    mm = lambda j: jnp.minimum(j, n_tiles - 1)
    ep = lambda j: jnp.maximum(j - 1, 0)
    return pl.pallas_call(
        functools.partial(_proj_qk_kernel, n_tiles=n_tiles),
        grid=(m // tm, n_tiles + 1),
        in_specs=[
            pl.BlockSpec((tm, d), lambda i, j: (i, 0)),
            pl.BlockSpec((tn, d), lambda i, j: (mm(j), 0)),
            pl.BlockSpec((1, 1, tn), lambda i, j: (ep(j), 0, 0)),
            pl.BlockSpec((3, tm, LANES), lambda i, j: (0, i % n_tab, 0)),
            pl.BlockSpec(bd.shape, lambda i, j: (0, 0)),
        ],
        out_specs=pl.BlockSpec((tm, tn), lambda i, j: (i, ep(j))),
        out_shape=jax.ShapeDtypeStruct((m, n), BF16),
        scratch_shapes=[pltpu.VMEM((tm, tn), F32), pltpu.VMEM((tm, tn), F32)],
        compiler_params=_cparams(("parallel", "arbitrary")),
        name="proj_qk",
    )(x2, w_t, gain, rope, bd)


def _proj_dt_kernel(x_ref, w_ref, wt_ref, out_ref, outt_ref):
    x = x_ref[...]
    out_ref[...] = jnp.dot(x, w_ref[...], preferred_element_type=F32)
    outt_ref[...] = lax.dot_general(wt_ref[...], x, (((1,), (1,)), ((), ())), preferred_element_type=F32)


def _proj_dt(x2, w_pad, w_t, *, tm):
    m, d = x2.shape
    nh = w_t.shape[0]
    return pl.pallas_call(
        _proj_dt_kernel,
        grid=(m // tm,),
        in_specs=[
            pl.BlockSpec((tm, d), lambda i: (i, 0)),
            pl.BlockSpec((d, LANES), lambda i: (0, 0)),
            pl.BlockSpec((nh, d), lambda i: (0, 0)),
        ],
        out_specs=[pl.BlockSpec((tm, LANES), lambda i: (i, 0)), pl.BlockSpec((nh, tm), lambda i: (0, i))],
        out_shape=[jax.ShapeDtypeStruct((m, LANES), F32), jax.ShapeDtypeStruct((nh, m), F32)],
        compiler_params=_cparams(("parallel",)),
        name="proj_dt",
    )(x2, w_pad, w_t)


def _outproj_kernel(h_ref, a_ref, s_ref, wa_ref, ws_ref, out_ref):
    out_ref[...] = (h_ref[...]
                    + jnp.dot(a_ref[...], wa_ref[...], preferred_element_type=F32)
                    + jnp.dot(s_ref[...], ws_ref[...], preferred_element_type=F32))


def _outproj(h1, attn, ssd, w_o, *, tm, tn):
    m, d = h1.shape
    k = attn.shape[1]
    return pl.pallas_call(
        _outproj_kernel,
        grid=(m // tm, d // tn),
        in_specs=[
            pl.BlockSpec((tm, tn), lambda i, j: (i, j)),
            pl.BlockSpec((tm, k), lambda i, j: (i, 0)),
            pl.BlockSpec((tm, k), lambda i, j: (i, 0)),
            pl.BlockSpec((k, tn), lambda i, j: (0, j)),
            pl.BlockSpec((k, tn), lambda i, j: (1, j)),
        ],
        out_specs=pl.BlockSpec((tm, tn), lambda i, j: (i, j)),
        out_shape=jax.ShapeDtypeStruct((m, d), F32),
        compiler_params=_cparams(("parallel", "arbitrary")),
        name="outproj",
    )(h1, attn, ssd, w_o, w_o)


def _attn_kernel(qa_tab, ka_tab, qb_tab, kb_tab, q_ref, k_ref, vt_ref, km_ref, vmt_ref, lam_ref, gain_ref,
                 out_ref, qs_sc, sm_sc, s0_sc, s1_sc, mx0_sc, mx1_sc, m_sc, l_sc, acc_sc,
                 *, tq, dg, n_steps, lam_init):
    g = pl.program_id(2)
    qa = qa_tab[g]
    ka = ka_tab[g]
    qb = qb_tab[g]
    kb = kb_tab[g]
    has_a = g < n_steps
    has_b = g >= 1
    even = (g % 2) == 0
    nt = (((1,), (1,)), ((), ()))

    full_groups = [(slice(0, 2 * tq), tq)]
    diag_groups = [(slice(sub * tq + j * dg, sub * tq + (j + 1) * dg), (j + 1) * dg)
                   for sub in range(2) for j in range(tq // dg)]

    def stage_a(buf, diagonal):
        s_out, mx_out = buf
        for cols, nk in (diag_groups if diagonal else full_groups):
            s_t = lax.dot_general(k_ref[0:nk, :], qs_sc[cols, :], nt, preferred_element_type=F32)
            if diagonal:
                key = lax.broadcasted_iota(jnp.int32, s_t.shape, 0)
                qry = lax.broadcasted_iota(jnp.int32, s_t.shape, 1) + (nk - dg)
                s_t = jnp.where(key // CHUNK <= qry // CHUNK, s_t, -jnp.inf)
            s_out[0:nk, cols] = s_t
            mx_out[:, cols] = jnp.max(s_t, axis=0, keepdims=True)

    def stage_b(buf, last):
        s_in, mx_in = buf
        vt_aug = jnp.concatenate([vt_ref[...], jnp.ones((16, vt_ref.shape[1]), BF16)], axis=0)
        for cols, nk in (diag_groups if last else full_groups):
            m_prev = m_sc[:, cols]
            m_new = jnp.maximum(m_prev, mx_in[:, cols])
            alpha = jnp.exp2(m_prev - m_new)
            p = jnp.exp2(s_in[0:nk, cols] - m_new).astype(BF16)
            pv = jnp.dot(vt_aug[:, 0:nk], p, preferred_element_type=F32)
            l_sc[:, cols] = alpha * l_sc[:, cols] + pv[V_HEAD_DIM:V_HEAD_DIM + 1, :]
            acc_sc[:, cols] = alpha * acc_sc[:, cols] + pv[0:V_HEAD_DIM, :]
            m_sc[:, cols] = m_new
        if last:
            lq1, lk1, lq2, lk2 = lam_ref[0:1, :], lam_ref[1:2, :], lam_ref[2:3, :], lam_ref[3:4, :]
            lam = (jnp.exp(jnp.sum(lq1 * lk1, axis=-1, keepdims=True))
                   - jnp.exp(jnp.sum(lq2 * lk2, axis=-1, keepdims=True)) + lam_init)
            o = acc_sc[...] / l_sc[...]
            o = o[:, 0:tq] - lam * o[:, tq:2 * tq]
            ms = jnp.mean(o * o, axis=0, keepdims=True)
            o = o * lax.rsqrt(ms + EPS) * (gain_ref[...] * (1.0 - lam_init))
            out_ref[...] = o.T.astype(out_ref.dtype)

    @pl.when(has_b & (kb == 0))
    def _():
        s_t = sm_sc[...]
        m0 = jnp.max(s_t, axis=0, keepdims=True)
        p = jnp.exp2(s_t - m0)
        m_sc[...] = m0
        l_sc[...] = jnp.sum(p, axis=0, keepdims=True)
        acc_sc[...] = jnp.dot(vmt_ref[...], p.astype(BF16), preferred_element_type=F32)

    @pl.when(has_a & (ka == 0))
    def _():
        q = q_ref[...]
        lane = lax.broadcasted_iota(jnp.int32, q.shape, 1)
        zero = jnp.zeros_like(q)
        qs_sc[0:tq, :] = jnp.where(lane < DIFF_HEAD_DIM, q, zero)
        qs_sc[tq:2 * tq, :] = jnp.where(lane >= DIFF_HEAD_DIM, q, zero)
        s_t = lax.dot_general(km_ref[...], qs_sc[...], nt, preferred_element_type=F32)
        key = lax.broadcasted_iota(jnp.int32, s_t.shape, 0)
        sm_sc[...] = jnp.where(key < N_META, s_t, -jnp.inf)

    buf0, buf1 = (s0_sc, mx0_sc), (s1_sc, mx1_sc)
    a_diag = ka == qa
    b_last = kb == qb
    both = has_a & has_b
    for parity, (buf_a, buf_b) in enumerate(((buf0, buf1), (buf1, buf0))):
        par = even if parity == 0 else jnp.logical_not(even)

        @pl.when(par & both & jnp.logical_not(a_diag) & jnp.logical_not(b_last))
        def _(buf_a=buf_a, buf_b=buf_b):
            stage_a(buf_a, False)
            stage_b(buf_b, False)

        @pl.when(par & both & a_diag & jnp.logical_not(b_last))
        def _(buf_a=buf_a, buf_b=buf_b):
            stage_a(buf_a, True)
            stage_b(buf_b, False)

        @pl.when(par & both & b_last)
        def _(buf_a=buf_a, buf_b=buf_b):
            stage_a(buf_a, False)
            stage_b(buf_b, True)

        @pl.when(par & has_a & jnp.logical_not(has_b))
        def _(buf_a=buf_a):
            stage_a(buf_a, True)

        @pl.when(par & has_b & jnp.logical_not(has_a))
        def _(buf_b=buf_b):
            stage_b(buf_b, True)


def _attention(qk, vt, kmeta, vmeta_t, lam_vecs, gain_col, *, bsz, seq, tq, dg, lam_init):
    nq = seq // tq
    h = N_DIFF_HEADS
    qi_list, ki_list = [], []
    for a in range(nq):
        for b in range(a + 1):
            qi_list.append(a)
            ki_list.append(b)
    n_steps = len(qi_list)
    ia = np.minimum(np.arange(n_steps + 1), n_steps - 1)
    ib = np.maximum(np.arange(n_steps + 1) - 1, 0)
    qi_arr, ki_arr = np.array(qi_list, np.int32), np.array(ki_list, np.int32)
    tabs = [jnp.asarray(t) for t in (qi_arr[ia], ki_arr[ia], qi_arr[ib], ki_arr[ib])]
    grid_spec = pltpu.PrefetchScalarGridSpec(
        num_scalar_prefetch=4,
        grid=(bsz, h, n_steps + 1),
        in_specs=[
            pl.BlockSpec((tq, V_HEAD_DIM), lambda b, hh, g, qa, ka, qb, kb: (b * nq + qa[g], hh)),
            pl.BlockSpec((tq, V_HEAD_DIM), lambda b, hh, g, qa, ka, qb, kb: (b * nq + ka[g], h + hh)),
            pl.BlockSpec((V_HEAD_DIM, tq), lambda b, hh, g, qa, ka, qb, kb: (hh, b * nq + kb[g])),
            pl.BlockSpec((LANES, V_HEAD_DIM), lambda b, hh, g, qa, ka, qb, kb: (0, hh)),
            pl.BlockSpec((V_HEAD_DIM, LANES), lambda b, hh, g, qa, ka, qb, kb: (hh, 0)),
            pl.BlockSpec((4, DIFF_HEAD_DIM), lambda b, hh, g, qa, ka, qb, kb: (0, 0)),
            pl.BlockSpec((V_HEAD_DIM, 1), lambda b, hh, g, qa, ka, qb, kb: (0, 0)),
        ],
        out_specs=pl.BlockSpec((tq, V_HEAD_DIM), lambda b, hh, g, qa, ka, qb, kb: (b * nq + qb[g], hh)),
        scratch_shapes=[
            pltpu.VMEM((2 * tq, V_HEAD_DIM), BF16),
            pltpu.VMEM((LANES, 2 * tq), F32),
            pltpu.VMEM((tq, 2 * tq), F32),
            pltpu.VMEM((tq, 2 * tq), F32),
            pltpu.VMEM((1, 2 * tq), F32),
            pltpu.VMEM((1, 2 * tq), F32),
            pltpu.VMEM((1, 2 * tq), F32),
            pltpu.VMEM((1, 2 * tq), F32),
            pltpu.VMEM((V_HEAD_DIM, 2 * tq), F32),
        ],
    )
    return pl.pallas_call(
        functools.partial(_attn_kernel, tq=tq, dg=dg, n_steps=n_steps, lam_init=lam_init),
        grid_spec=grid_spec,
        out_shape=jax.ShapeDtypeStruct((bsz * seq, h * V_HEAD_DIM), BF16),
        compiler_params=_cparams(("parallel", "parallel", "arbitrary")),
        name="attn",
    )(*tabs, qk, qk, vt, kmeta, vmeta_t, lam_vecs, gain_col)


def _ssd_kernel(*refs, t_blk, bsz, n_cast):
    dtt_refs = refs[:bsz]
    cast_in = refs[bsz:bsz + n_cast]
    (zx_ref, dt_ref, xbcm_ref, dtm_ref, dttm_ref,
     convw_ref, convb_ref, dtb_ref, dtbt_ref, alog_ref, alogt_ref, dskip_ref, nw_ref,
     expand_ref, tril_ref, triu2_ref, shift_ref, out_ref) = refs[bsz + n_cast:bsz + n_cast + 18]
    cast_out = refs[bsz + n_cast + 18:bsz + 2 * n_cast + 18]
    xs_sc, xm_sc, state_sc = refs[bsz + 2 * n_cast + 18:]
    for src, dst in zip(cast_in, cast_out):
        dst[...] = src[...].astype(BF16)
    n_heads, p_dim, n_state, n_groups = N_SSD_HEADS, SSD_HEAD_DIM, D_STATE, N_SSD_GROUPS
    width = n_heads * p_dim
    gw = width // n_groups
    n_pairs = n_heads // 2
    nt = (((1,), (1,)), ((), ()))
    expand = expand_ref[...]
    a_row = -jnp.exp(alog_ref[...])
    a_col = -jnp.exp(alogt_ref[...])
    d_full = dskip_ref[...]
    lane = lax.broadcasted_iota(jnp.int32, (CHUNK, LANES), 1)
    row = lax.broadcasted_iota(jnp.int32, (CHUNK, LANES), 0)
    left = lane < p_dim
    causal2 = (lane & (CHUNK - 1)) <= row

    def chunk(win_f32, st_ref, dt_raw, dtt_raw, pad_rows, z, out_r0, b=0):
        win = win_f32.astype(BF16)
        taps = jnp.dot(shift_ref[...], win, preferred_element_type=F32)
        conv = convb_ref[...] + win_f32[HALO:, :] * convw_ref[CONV_WIDTH - 1:CONV_WIDTH, :]
        for k in range(CONV_WIDTH - 1):
            conv = conv + taps[k * CHUNK:(k + 1) * CHUNK, :] * convw_ref[k:k + 1, :]
        xc = _silu(conv)
        x_s = xc[:, :width]
        dt = jax.nn.softplus(dt_raw + dtb_ref[...])
        dtt = jax.nn.softplus(dtt_raw + dtbt_ref[...])
        if pad_rows:
            dt = jnp.where(row >= pad_rows, dt, 0.0)
            dtt = jnp.where(lax.broadcasted_iota(jnp.int32, dtt.shape, 1) >= pad_rows, dtt, 0.0)
        a_pieces = jnp.dot(tril_ref[...], jnp.concatenate(_split_bf16(dt * a_row, 3), axis=1),
                           preferred_element_type=F32)
        a_cs = a_pieces[:, 0:LANES] + a_pieces[:, LANES:2 * LANES] + a_pieces[:, 2 * LANES:3 * LANES]
        a_cs_t2 = _dot_f32_by_01(dtt * a_col, triu2_ref[...])
        spread = jnp.dot(jnp.concatenate(_split_bf16(a_cs, 3) + _split_bf16(dt, 3), axis=0), expand,
                         preferred_element_type=F32)
        a_full = spread[0:CHUNK] + spread[CHUNK:2 * CHUNK] + spread[2 * CHUNK:3 * CHUNK]
        dt_full = spread[3 * CHUNK:4 * CHUNK] + spread[4 * CHUNK:5 * CHUNK] + spread[5 * CHUNK:6 * CHUNK]
        a_last = a_full[CHUNK - 1:CHUNK, :]
        xdt = x_s * dt_full
        w_state = (xdt * jnp.exp(a_last - a_full)).astype(BF16)
        xdt_b = xdt.astype(BF16)
        zero_b = jnp.zeros((CHUNK, LANES), BF16)
        y_parts = []
        for g in range(n_groups):
            b_g = xc[:, width + g * n_state: width + (g + 1) * n_state]
            c_g = xc[:, width + (n_groups + g) * n_state: width + (n_groups + g + 1) * n_state]
            b_gb = b_g.astype(BF16)
            c_gb = c_g.astype(BF16)
            st = st_ref[g]
            if out_r0 is not None:
                cb2 = lax.dot_general(c_gb, jnp.concatenate([b_gb, b_gb], axis=0), nt,
                                      preferred_element_type=F32)
                y_off = jnp.dot(c_gb, st.astype(BF16), preferred_element_type=F32)
                y_g = y_off * jnp.exp(a_full[:, g * gw:(g + 1) * gw])
                diag = []
                for pp in range(n_pairs // n_groups):
                    pr = g * (n_pairs // n_groups) + pp
                    colb = a_full[:, pr * LANES:(pr + 1) * LANES]
                    rowb = jnp.where(left, a_cs_t2[2 * pr:2 * pr + 1, :], a_cs_t2[2 * pr + 1:2 * pr + 2, :])
                    dec = jnp.exp(jnp.where(causal2, colb - rowb, -jnp.inf))
                    m_pair = (cb2 * dec).astype(BF16)
                    xp = xdt_b[:, pr * LANES:(pr + 1) * LANES]
                    rhs = jnp.concatenate([jnp.where(left, xp, zero_b), jnp.where(left, zero_b, xp)], axis=0)
                    diag.append(jnp.dot(m_pair, rhs, preferred_element_type=F32))
                y_parts.append(y_g + jnp.concatenate(diag, axis=1))
            new_st = st * jnp.exp(a_last[:, g * gw:(g + 1) * gw]) + jnp.dot(
                b_g.T.astype(BF16), w_state[:, g * gw:(g + 1) * gw], preferred_element_type=F32)
            st_ref[g] = new_st
        if out_r0 is not None:
            y = jnp.concatenate(y_parts, axis=1) + x_s * d_full
            gated = y * _silu(z)
            out_ref[b, out_r0:out_r0 + CHUNK, :] = _rms(gated, nw_ref[...]).astype(out_ref.dtype)

    @pl.when(pl.program_id(0) == 0)
    def _():
        state_sc[...] = jnp.zeros_like(state_sc)
        xm_sc[0:HALO, :] = jnp.zeros((HALO, xm_sc.shape[1]), F32)
        xm_sc[HALO:HALO + CHUNK, :] = xbcm_ref[...].astype(F32)
        chunk(xm_sc[...], state_sc.at[0], dtm_ref[...], dttm_ref[...], CHUNK - N_META, None, None)
        for b in range(bsz):
            if b:
                state_sc[b] = state_sc[0]
            xs_sc[b, 0:HALO, :] = xm_sc[CHUNK:CHUNK + HALO, :]

    for b in range(bsz):
        xs_sc[b, HALO:HALO + t_blk, :] = zx_ref[b, :, width:].astype(F32)
    for c in range(t_blk // CHUNK):
        rows = slice(c * CHUNK, (c + 1) * CHUNK)
        for b in range(bsz):
            chunk(xs_sc[b, c * CHUNK:(c + 1) * CHUNK + HALO, :], state_sc.at[b], dt_ref[b, rows, :],
                  dtt_refs[b][:, rows], 0, zx_ref[b, rows, 0:width].astype(F32), c * CHUNK, b)
    for b in range(bsz):
        xs_sc[b, 0:HALO, :] = xs_sc[b, t_blk:t_blk + HALO, :]


def _ssd(zx, dt, dtt, xbc_m, dt_m, dtt_m, conv_w, conv_b, dt_bias, a_log, d_skip, norm_w, cast_weights,
         *, bsz, seq, t_blk):
    nb = seq // t_blk
    slab = lambda w: pl.BlockSpec((w.shape[0] // nb, w.shape[1]), lambda i: (i, 0))
    width = N_SSD_HEADS * SSD_HEAD_DIM
    cdim = zx.shape[1] - width
    nh = N_SSD_HEADS

    def pad_row(v):
        return jnp.pad(v.reshape(1, nh), ((0, 0), (0, LANES - nh)))

    hh = np.arange(LANES)[:, None]
    ll = np.arange(width)[None, :]
    one_hot = lambda m: jnp.asarray(m.astype(np.float32), dtype=BF16)
    expand = one_hot(ll // SSD_HEAD_DIM == hh)
    ii = np.arange(CHUNK)
    tril = one_hot(ii[None, :] <= ii[:, None])
    triu2 = one_hot(np.tile(ii[:, None] <= ii[None, :], (1, 2)))
    jj = np.arange(HALO + CHUNK)
    shift = one_hot(np.concatenate([jj[None, :] == ii[:, None] + HALO - (CONV_WIDTH - 1) + k
                                    for k in range(CONV_WIDTH - 1)]))
    full = lambda shape: pl.BlockSpec(shape, lambda i: tuple(0 for _ in shape))
    out, *cast = pl.pallas_call(
        functools.partial(_ssd_kernel, t_blk=t_blk, bsz=bsz, n_cast=len(cast_weights)),
        grid=(nb,),
        in_specs=[pl.BlockSpec((nh, t_blk), functools.partial(lambda i, b: (0, b * nb + i), b=b))
                  for b in range(bsz)] + [slab(w) for w in cast_weights] + [
            pl.BlockSpec((bsz, t_blk, width + cdim), lambda i: (0, i, 0)),
            pl.BlockSpec((bsz, t_blk, LANES), lambda i: (0, i, 0)),
            full((CHUNK, cdim)), full((CHUNK, LANES)), full((nh, CHUNK)),
            full((CONV_WIDTH, cdim)), full((1, cdim)), full((1, LANES)), full((nh, 1)),
            full((1, LANES)), full((nh, 1)), full((1, width)), full((1, width)),
            full((LANES, width)), full((CHUNK, CHUNK)), full((CHUNK, 2 * CHUNK)), full(shift.shape),
        ],
        out_specs=[pl.BlockSpec((bsz, t_blk, width), lambda i: (0, i, 0))] + [slab(w) for w in cast_weights],
        out_shape=[jax.ShapeDtypeStruct((bsz, seq, width), BF16)]
        + [jax.ShapeDtypeStruct(w.shape, BF16) for w in cast_weights],
        scratch_shapes=[
            pltpu.VMEM((bsz, t_blk + HALO, cdim), F32),
            pltpu.VMEM((CHUNK + HALO, cdim), F32),
            pltpu.VMEM((bsz, N_SSD_GROUPS, D_STATE, width // N_SSD_GROUPS), F32),
        ],
        compiler_params=_cparams(("arbitrary",)),
        name="ssd",
    )(*([dtt] * bsz), *cast_weights, zx.reshape(bsz, seq, width + cdim), dt.reshape(bsz, seq, LANES),
      xbc_m, dt_m, dtt_m,
      conv_w, conv_b.reshape(1, cdim), pad_row(dt_bias), dt_bias.reshape(nh, 1), pad_row(a_log),
      a_log.reshape(nh, 1), jnp.repeat(d_skip, SSD_HEAD_DIM).reshape(1, width),
      norm_w.reshape(1, width), expand, tril, triu2, shift)
    return out.reshape(bsz * seq, width), cast


def _rope_tables(n_pos):
    inv = jnp.power(ROPE_THETA, -jnp.arange(0, ROT_DIM, 2, dtype=F32) / ROT_DIM)
    ang = jnp.arange(n_pos, dtype=F32)[:, None] * inv[None, :]
    cos, sin = jnp.cos(ang), jnp.sin(ang)
    half = ROT_DIM // 2
    r = np.arange(LANES) % DIFF_HEAD_DIM
    idx = jnp.asarray(r % half)
    lo = jnp.asarray(r < half)[None, :]
    hi = jnp.asarray((r >= half) & (r < ROT_DIM))[None, :]
    cos_t = jnp.where(lo | hi, cos[:, idx], 1.0)
    sin_lo = jnp.where(lo, -sin[:, idx], 0.0)
    sin_hi = jnp.where(hi, sin[:, idx], 0.0)
    return jnp.stack([cos_t, sin_lo, sin_hi])


def kernel(x, meta_tokens, ffn1_norm, ffn1_w_gate, ffn1_w_up, ffn1_w_down, mix_norm, w_in, q_norm, k_norm,
           lambda_q1, lambda_k1, lambda_q2, lambda_k2, attn_out_norm, conv_w, conv_b, dt_bias, a_log, d_skip,
           ssd_norm, w_out, ffn2_norm, ffn2_w_gate, ffn2_w_up, ffn2_w_down):
    bsz, seq, d = x.shape
    assert ffn1_norm.shape[0] == 1, "single-layer block"
    aw = N_DIFF_HEADS * V_HEAD_DIM
    sw = N_SSD_HEADS * SSD_HEAD_DIM
    cdim = sw + 2 * N_SSD_GROUPS * D_STATE
    lam_init = 0.8 - 0.6 * math.exp(-0.3 * 0)
    row = lambda v: v.reshape(1, -1)

    tn = 512
    w_in0 = w_in[0]
    w_dt = w_in0[:, 3 * aw + sw + cdim:].astype(BF16)
    w_dt_pad = jnp.pad(w_dt, ((0, 0), (0, LANES - N_SSD_HEADS)))
    w_dt_t = w_dt.T

    n_sub = aw // DIFF_HEAD_DIM
    q_gain = jnp.tile(q_norm[0], n_sub) * (DIFF_HEAD_DIM ** -0.5 * math.log2(math.e))
    k_gain = jnp.tile(k_norm[0], n_sub)
    qk_gain = jnp.concatenate([q_gain, k_gain]).reshape(2 * aw // tn, 1, tn)
    rope = _rope_tables(N_META + seq)
    gi = np.arange(2 * LANES) // DIFF_HEAD_DIM
    bd = jnp.asarray((gi[:, None] == gi[None, :]).astype(np.float32) / DIFF_HEAD_DIM, dtype=BF16)

    hn_m, wg1, wu1, wd1 = _ffn_meta(meta_tokens.astype(F32), row(ffn1_norm[0]), ffn1_w_gate[0],
                                    ffn1_w_up[0], ffn1_w_down[0], row(mix_norm[0]), tf=FFN_TF)
    w_in_t = w_in0.T.astype(BF16)
    x2 = x.reshape(bsz * seq, d)
    h1, hn = _ffn(x2, row(ffn1_norm[0]), wg1, wu1, wd1, row(mix_norm[0]), tm=FFN_TM, tf=FFN_TF)

    qk = _proj_qk(hn, w_in_t, qk_gain, rope[:, N_META:], bd, tm=2048, tn=tn, n=2 * aw)
    qk_m = _proj_qk(hn_m, w_in_t, qk_gain, rope[:, :N_META], bd, tm=N_META, tn=tn, n=2 * aw)
    v_t = _proj(hn, w_in_t, tm=4096, tn=tn, row0=2 * aw, n=aw, name="proj_vt", transpose=True)
    vzx_m = _proj(hn_m, w_in_t, tm=N_META, tn=tn, row0=2 * aw, n=aw + sw + cdim, name="proj_vzx_meta")
    zx = _proj(hn, w_in_t, tm=4096, tn=tn, row0=3 * aw, n=sw + cdim, name="proj_zx")
    v_t_m, xbc_m = vzx_m[:, :aw].T, vzx_m[:, aw + sw:]
    dt, dtt = _proj_dt(hn, w_dt_pad, w_dt_t, tm=2048)
    dt_m, dtt_m = _proj_dt(hn_m, w_dt_pad, w_dt_t, tm=N_META)

    kmeta = jnp.pad(qk_m[:, aw:], ((0, LANES - N_META), (0, 0)))
    vmeta_t = jnp.pad(v_t_m, ((0, 0), (0, LANES - N_META)))
    lam_vecs = jnp.stack([lambda_q1[0], lambda_k1[0], lambda_q2[0], lambda_k2[0]]).astype(F32)
    attn = _attention(qk, v_t, kmeta, vmeta_t, lam_vecs, attn_out_norm[0].reshape(V_HEAD_DIM, 1),
                      bsz=bsz, seq=seq, tq=1024, dg=256, lam_init=lam_init)

    lead = CHUNK - N_META
    ssd, (wg2, wu2, wd2, w_o) = _ssd(
        zx, dt, dtt,
        jnp.pad(xbc_m, ((lead, 0), (0, 0))), jnp.pad(dt_m, ((lead, 0), (0, 0))), jnp.pad(dtt_m, ((0, 0), (lead, 0))),
        conv_w[0], conv_b[0], dt_bias[0], a_log[0], d_skip[0], ssd_norm[0],
        (ffn2_w_gate[0], ffn2_w_up[0], ffn2_w_down[0], w_out[0]),
        bsz=bsz, seq=seq, t_blk=256)

    h2 = _outproj(h1, attn, ssd, w_o, tm=512, tn=d)
    (out,) = _ffn(h2, row(ffn2_norm[0]), wg2, wu2, wd2, tm=FFN_TM, tf=FFN_TF)
    return out.reshape(bsz, seq, d)
```

```python
import functools
import math

import jax
import jax.numpy as jnp
import numpy as np
from jax import lax
from jax.experimental import pallas as pl
from jax.experimental.pallas import tpu as pltpu

F32 = jnp.float32
BF16 = jnp.bfloat16

EPS = 1e-6
CHUNK = 64
N_META = 16
N_DIFF_HEADS = 8
DIFF_HEAD_DIM = 64
V_HEAD_DIM = 128
ROT_DIM = 16
ROPE_THETA = 500000.0
SSD_HEAD_DIM = 64
N_SSD_HEADS = 16
N_SSD_GROUPS = 2
D_STATE = 128
CONV_WIDTH = 4
HALO = 64
LANES = 128
VMEM_LIMIT = 56 * 1024 * 1024
FFN_TM = 512
FFN_TF = 512


def _cparams(sem):
    return pltpu.CompilerParams(dimension_semantics=sem, vmem_limit_bytes=VMEM_LIMIT)


def _silu(v):
    half = 0.5 * v
    return half + half * jnp.tanh(half)


def _split_bf16(a, pieces):
    out = []
    for _ in range(pieces - 1):
        p = a.astype(BF16)
        out.append(p)
        a = a - p.astype(F32)
    out.append(a.astype(BF16))
    return out


def _dot_f32_by_01(a, m01, pieces=3):
    return sum(jnp.dot(p, m01, preferred_element_type=F32) for p in _split_bf16(a, pieces))


def _rms(v, w):
    ms = jnp.mean(v * v, axis=-1, keepdims=True)
    return v * lax.rsqrt(ms + EPS) * w


def _swiglu_step(xn, wg_ref, wu_ref, wd_ref):
    g = jnp.dot(xn, wg_ref[...], preferred_element_type=F32)
    u = jnp.dot(xn, wu_ref[...], preferred_element_type=F32)
    a = (_silu(g) * u).astype(BF16)
    return jnp.dot(a, wd_ref[...], preferred_element_type=F32)


def _qk_norm_rope(y, gain, rope_ref, bd):
    cos, sin_lo, sin_hi = rope_ref[0], rope_ref[1], rope_ref[2]
    half = ROT_DIM // 2
    ms = sum(jnp.dot(p, bd, preferred_element_type=F32) for p in _split_bf16(y * y, 2))
    yn = y * lax.rsqrt(ms + EPS) * gain
    out = []
    for s in range(y.shape[1] // LANES):
        v = yn[:, s * LANES:(s + 1) * LANES]
        out.append(v * cos + pltpu.roll(v, half, 1) * sin_hi + pltpu.roll(v, LANES - half, 1) * sin_lo)
    return jnp.concatenate(out, axis=1)


def _ffn_kernel(x_ref, nw_ref, wg_ref, wu_ref, wd_ref, *rest, with_norm):
    if with_norm:
        pnw_ref, out_ref, hn_ref, xn_sc = rest
    else:
        out_ref, xn_sc = rest
    j = pl.program_id(1)

    @pl.when(j == 0)
    def _():
        xn_sc[...] = _rms(x_ref[...], nw_ref[...]).astype(BF16)
        out_ref[...] = jnp.zeros_like(out_ref)

    out_ref[...] += _swiglu_step(xn_sc[...], wg_ref, wu_ref, wd_ref)

    @pl.when(j == pl.num_programs(1) - 1)
    def _():
        h = x_ref[...] + 0.5 * out_ref[...]
        out_ref[...] = h
        if with_norm:
            hn_ref[...] = _rms(h, pnw_ref[...]).astype(BF16)


def _ffn(x2, nw, wg, wu, wd, post_nw=None, *, tm, tf):
    m, d = x2.shape
    f = wg.shape[1]
    with_norm = post_nw is not None
    row_spec = pl.BlockSpec((tm, d), lambda i, j: (i, 0))
    vec_spec = pl.BlockSpec((1, d), lambda i, j: (0, 0))
    in_specs = [row_spec, vec_spec,
                pl.BlockSpec((d, tf), lambda i, j: (0, j)),
                pl.BlockSpec((d, tf), lambda i, j: (0, j)),
                pl.BlockSpec((tf, d), lambda i, j: (j, 0))]
    args = [x2, nw, wg, wu, wd]
    out_specs, out_shape = [row_spec], [jax.ShapeDtypeStruct((m, d), F32)]
    if with_norm:
        in_specs.append(vec_spec)
        args.append(post_nw)
        out_specs.append(row_spec)
        out_shape.append(jax.ShapeDtypeStruct((m, d), BF16))
    return pl.pallas_call(
        functools.partial(_ffn_kernel, with_norm=with_norm),
        grid=(m // tm, f // tf),
        in_specs=in_specs,
        out_specs=out_specs,
        out_shape=out_shape,
        scratch_shapes=[pltpu.VMEM((tm, d), BF16)],
        compiler_params=_cparams(("parallel", "arbitrary")),
        name="ffn",
    )(*args)


def _ffn_meta_kernel(x_ref, nw_ref, wg_ref, wu_ref, wd_ref, pnw_ref,
                     hn_ref, wgb_ref, wub_ref, wdb_ref, xn_sc, acc_sc):
    j = pl.program_id(0)

    @pl.when(j == 0)
    def _():
        xn_sc[...] = _rms(x_ref[...], nw_ref[...]).astype(BF16)
        acc_sc[...] = jnp.zeros_like(acc_sc)

    wgb_ref[...] = wg_ref[...].astype(BF16)
    wub_ref[...] = wu_ref[...].astype(BF16)
    wdb_ref[...] = wd_ref[...].astype(BF16)
    acc_sc[...] += _swiglu_step(xn_sc[...], wgb_ref, wub_ref, wdb_ref)

    @pl.when(j == pl.num_programs(0) - 1)
    def _():
        hn_ref[...] = _rms(x_ref[...] + 0.5 * acc_sc[...], pnw_ref[...]).astype(BF16)


def _ffn_meta(xm, nw, wg, wu, wd, post_nw, *, tf):
    m, d = xm.shape
    n_ff = wg.shape[1] // tf
    vec_spec = pl.BlockSpec((1, d), lambda j: (0, 0))
    row_spec = pl.BlockSpec((m, d), lambda j: (0, 0))
    col_tile = pl.BlockSpec((d, tf), lambda j: (0, j))
    row_tile = pl.BlockSpec((tf, d), lambda j: (j, 0))
    return pl.pallas_call(
        _ffn_meta_kernel,
        grid=(n_ff,),
        in_specs=[row_spec, vec_spec, col_tile, col_tile, row_tile, vec_spec],
        out_specs=[row_spec, col_tile, col_tile, row_tile],
        out_shape=[jax.ShapeDtypeStruct((m, d), BF16), jax.ShapeDtypeStruct(wg.shape, BF16),
                   jax.ShapeDtypeStruct(wu.shape, BF16), jax.ShapeDtypeStruct(wd.shape, BF16)],
        scratch_shapes=[pltpu.VMEM((m, d), BF16), pltpu.VMEM((m, d), F32)],
        compiler_params=_cparams(("arbitrary",)),
        name="ffn_meta_cast",
    )(xm, nw, wg, wu, wd, post_nw)


_NT = (((1,), (1,)), ((), ()))


def _proj_kernel(x_ref, w_ref, out_ref, *, transpose):
    if transpose:
        acc = lax.dot_general(w_ref[...], x_ref[...], _NT, preferred_element_type=F32)
    else:
        acc = lax.dot_general(x_ref[...], w_ref[...], _NT, preferred_element_type=F32)
    out_ref[...] = acc.astype(out_ref.dtype)


def _proj(x2, w_t, *, tm, tn, row0, n, name, transpose=False):
    m, d = x2.shape
    r0 = row0 // tn
    if transpose:
        out_spec, out_shape = pl.BlockSpec((tn, tm), lambda i, j: (j, i)), (n, m)
    else:
        out_spec, out_shape = pl.BlockSpec((tm, tn), lambda i, j: (i, j)), (m, n)
    return pl.pallas_call(
        functools.partial(_proj_kernel, transpose=transpose),
        grid=(m // tm, n // tn),
        in_specs=[pl.BlockSpec((tm, d), lambda i, j: (i, 0)), pl.BlockSpec((tn, d), lambda i, j: (r0 + j, 0))],
        out_specs=out_spec,
        out_shape=jax.ShapeDtypeStruct(out_shape, BF16),
        compiler_params=_cparams(("parallel", "arbitrary")),
        name=name,
    )(x2, w_t)


def _proj_qk_kernel(x_ref, w_ref, gain_ref, rope_ref, bd_ref, out_ref, acc0_sc, acc1_sc, *, n_tiles):
    j = pl.program_id(1)
    bufs = (acc0_sc, acc1_sc)
    bd = bd_ref[...]
    wide = bd.shape[0]

    def finish(buf):
        for c in range(out_ref.shape[1] // wide):
            cols = slice(c * wide, (c + 1) * wide)
            out_ref[:, cols] = _qk_norm_rope(buf[:, cols], gain_ref[0][:, cols], rope_ref, bd).astype(out_ref.dtype)

    for t in range(n_tiles + 1):
        @pl.when(j == t)
        def _(t=t):
            if t > 0:
                finish(bufs[(t - 1) % 2])
            if t < n_tiles:
                bufs[t % 2][...] = lax.dot_general(x_ref[...], w_ref[...], _NT, preferred_element_type=F32)


def _proj_qk(x2, w_t, gain, rope, bd, *, tm, tn, n):
    m, d = x2.shape
    n_tab = rope.shape[1] // tm
    n_tiles = n // tn
    mm = lambda j: jnp.minimum(j, n_tiles - 1)
    ep = lambda j: jnp.maximum(j - 1, 0)
    return pl.pallas_call(
        functools.partial(_proj_qk_kernel, n_tiles=n_tiles),
        grid=(m // tm, n_tiles + 1),
        in_specs=[
            pl.BlockSpec((tm, d), lambda i, j: (i, 0)),
            pl.BlockSpec((tn, d), lambda i, j: (mm(j), 0)),
            pl.BlockSpec((1, 1, tn), lambda i, j: (ep(j), 0, 0)),
            pl.BlockSpec((3, tm, LANES), lambda i, j: (0, i % n_tab, 0)),
            pl.BlockSpec(bd.shape, lambda i, j: (0, 0)),
        ],
        out_specs=pl.BlockSpec((tm, tn), lambda i, j: (i, ep(j))),
        out_shape=jax.ShapeDtypeStruct((m, n), BF16),
        scratch_shapes=[pltpu.VMEM((tm, tn), F32), pltpu.VMEM((tm, tn), F32)],
        compiler_params=_cparams(("parallel", "arbitrary")),
        name="proj_qk",
    )(x2, w_t, gain, rope, bd)


def _proj_dt_kernel(x_ref, w_ref, wt_ref, out_ref, outt_ref):
    x = x_ref[...]
    out_ref[...] = jnp.dot(x, w_ref[...], preferred_element_type=F32)
    outt_ref[...] = lax.dot_general(wt_ref[...], x, (((1,), (1,)), ((), ())), preferred_element_type=F32)


def _proj_dt(x2, w_pad, w_t, *, tm):
    m, d = x2.shape
    nh = w_t.shape[0]
    return pl.pallas_call(
        _proj_dt_kernel,
        grid=(m // tm,),
        in_specs=[
            pl.BlockSpec((tm, d), lambda i: (i, 0)),
            pl.BlockSpec((d, LANES), lambda i: (0, 0)),
            pl.BlockSpec((nh, d), lambda i: (0, 0)),
        ],
        out_specs=[pl.BlockSpec((tm, LANES), lambda i: (i, 0)), pl.BlockSpec((nh, tm), lambda i: (0, i))],
        out_shape=[jax.ShapeDtypeStruct((m, LANES), F32), jax.ShapeDtypeStruct((nh, m), F32)],
        compiler_params=_cparams(("parallel",)),
        name="proj_dt",
    )(x2, w_pad, w_t)


def _outproj_kernel(h_ref, a_ref, s_ref, wa_ref, ws_ref, out_ref):
    out_ref[...] = (h_ref[...]
                    + jnp.dot(a_ref[...], wa_ref[...], preferred_element_type=F32)
                    + jnp.dot(s_ref[...], ws_ref[...], preferred_element_type=F32))


def _outproj(h1, attn, ssd, w_o, *, tm, tn):
    m, d = h1.shape
    k = attn.shape[1]
    return pl.pallas_call(
        _outproj_kernel,
        grid=(m // tm, d // tn),
        in_specs=[
            pl.BlockSpec((tm, tn), lambda i, j: (i, j)),
            pl.BlockSpec((tm, k), lambda i, j: (i, 0)),
            pl.BlockSpec((tm, k), lambda i, j: (i, 0)),
            pl.BlockSpec((k, tn), lambda i, j: (0, j)),
            pl.BlockSpec((k, tn), lambda i, j: (1, j)),
        ],
        out_specs=pl.BlockSpec((tm, tn), lambda i, j: (i, j)),
        out_shape=jax.ShapeDtypeStruct((m, d), F32),
        compiler_params=_cparams(("parallel", "arbitrary")),
        name="outproj",
    )(h1, attn, ssd, w_o, w_o)


def _attn_kernel(qa_tab, ka_tab, qb_tab, kb_tab, q_ref, k_ref, vt_ref, km_ref, vmt_ref, lam_ref, gain_ref,
                 out_ref, qs_sc, sm_sc, s0_sc, s1_sc, mx0_sc, mx1_sc, m_sc, l_sc, acc_sc,
                 *, tq, dg, n_steps, lam_init):
    g = pl.program_id(2)
    qa = qa_tab[g]
    ka = ka_tab[g]
    qb = qb_tab[g]
    kb = kb_tab[g]
    has_a = g < n_steps
    has_b = g >= 1
    even = (g % 2) == 0
    nt = (((1,), (1,)), ((), ()))

    full_groups = [(slice(0, 2 * tq), tq)]
    diag_groups = [(slice(sub * tq + j * dg, sub * tq + (j + 1) * dg), (j + 1) * dg)
                   for sub in range(2) for j in range(tq // dg)]

    def stage_a(buf, diagonal):
        s_out, mx_out = buf
        for cols, nk in (diag_groups if diagonal else full_groups):
            s_t = lax.dot_general(k_ref[0:nk, :], qs_sc[cols, :], nt, preferred_element_type=F32)
            if diagonal:
                key = lax.broadcasted_iota(jnp.int32, s_t.shape, 0)
                qry = lax.broadcasted_iota(jnp.int32, s_t.shape, 1) + (nk - dg)
                s_t = jnp.where(key // CHUNK <= qry // CHUNK, s_t, -jnp.inf)
            s_out[0:nk, cols] = s_t
            mx_out[:, cols] = jnp.max(s_t, axis=0, keepdims=True)

    def stage_b(buf, last):
        s_in, mx_in = buf
        vt_aug = jnp.concatenate([vt_ref[...], jnp.ones((16, vt_ref.shape[1]), BF16)], axis=0)
        for cols, nk in (diag_groups if last else full_groups):
            m_prev = m_sc[:, cols]
            m_new = jnp.maximum(m_prev, mx_in[:, cols])
            alpha = jnp.exp2(m_prev - m_new)
            p = jnp.exp2(s_in[0:nk, cols] - m_new).astype(BF16)
            pv = jnp.dot(vt_aug[:, 0:nk], p, preferred_element_type=F32)
            l_sc[:, cols] = alpha * l_sc[:, cols] + pv[V_HEAD_DIM:V_HEAD_DIM + 1, :]
            acc_sc[:, cols] = alpha * acc_sc[:, cols] + pv[0:V_HEAD_DIM, :]
            m_sc[:, cols] = m_new
        if last:
            lq1, lk1, lq2, lk2 = lam_ref[0:1, :], lam_ref[1:2, :], lam_ref[2:3, :], lam_ref[3:4, :]
            lam = (jnp.exp(jnp.sum(lq1 * lk1, axis=-1, keepdims=True))
                   - jnp.exp(jnp.sum(lq2 * lk2, axis=-1, keepdims=True)) + lam_init)
            o = acc_sc[...] / l_sc[...]
            o = o[:, 0:tq] - lam * o[:, tq:2 * tq]
            ms = jnp.mean(o * o, axis=0, keepdims=True)
            o = o * lax.rsqrt(ms + EPS) * (gain_ref[...] * (1.0 - lam_init))
            out_ref[...] = o.T.astype(out_ref.dtype)

    @pl.when(has_b & (kb == 0))
    def _():
        s_t = sm_sc[...]
        m0 = jnp.max(s_t, axis=0, keepdims=True)
        p = jnp.exp2(s_t - m0)
        m_sc[...] = m0
        l_sc[...] = jnp.sum(p, axis=0, keepdims=True)
        acc_sc[...] = jnp.dot(vmt_ref[...], p.astype(BF16), preferred_element_type=F32)

    @pl.when(has_a & (ka == 0))
    def _():
        q = q_ref[...]
        lane = lax.broadcasted_iota(jnp.int32, q.shape, 1)
        zero = jnp.zeros_like(q)
        qs_sc[0:tq, :] = jnp.where(lane < DIFF_HEAD_DIM, q, zero)
        qs_sc[tq:2 * tq, :] = jnp.where(lane >= DIFF_HEAD_DIM, q, zero)
        s_t = lax.dot_general(km_ref[...], qs_sc[...], nt, preferred_element_type=F32)
        key = lax.broadcasted_iota(jnp.int32, s_t.shape, 0)
        sm_sc[...] = jnp.where(key < N_META, s_t, -jnp.inf)

    buf0, buf1 = (s0_sc, mx0_sc), (s1_sc, mx1_sc)
    a_diag = ka == qa
    b_last = kb == qb
    both = has_a & has_b
    for parity, (buf_a, buf_b) in enumerate(((buf0, buf1), (buf1, buf0))):
        par = even if parity == 0 else jnp.logical_not(even)

        @pl.when(par & both & jnp.logical_not(a_diag) & jnp.logical_not(b_last))
        def _(buf_a=buf_a, buf_b=buf_b):
            stage_a(buf_a, False)
            stage_b(buf_b, False)

        @pl.when(par & both & a_diag & jnp.logical_not(b_last))
        def _(buf_a=buf_a, buf_b=buf_b):
            stage_a(buf_a, True)
            stage_b(buf_b, False)

        @pl.when(par & both & b_last)
        def _(buf_a=buf_a, buf_b=buf_b):
            stage_a(buf_a, False)
            stage_b(buf_b, True)

        @pl.when(par & has_a & jnp.logical_not(has_b))
        def _(buf_a=buf_a):
            stage_a(buf_a, True)

        @pl.when(par & has_b & jnp.logical_not(has_a))
        def _(buf_b=buf_b):
            stage_b(buf_b, True)


def _attention(qk, vt, kmeta, vmeta_t, lam_vecs, gain_col, *, bsz, seq, tq, dg, lam_init):
    nq = seq // tq
    h = N_DIFF_HEADS
    qi_list, ki_list = [], []
    for a in range(nq):
        for b in range(a + 1):
            qi_list.append(a)
            ki_list.append(b)
    n_steps = len(qi_list)
    ia = np.minimum(np.arange(n_steps + 1), n_steps - 1)
    ib = np.maximum(np.arange(n_steps + 1) - 1, 0)
    qi_arr, ki_arr = np.array(qi_list, np.int32), np.array(ki_list, np.int32)
    tabs = [jnp.asarray(t) for t in (qi_arr[ia], ki_arr[ia], qi_arr[ib], ki_arr[ib])]
    grid_spec = pltpu.PrefetchScalarGridSpec(
        num_scalar_prefetch=4,
        grid=(bsz, h, n_steps + 1),
        in_specs=[
            pl.BlockSpec((tq, V_HEAD_DIM), lambda b, hh, g, qa, ka, qb, kb: (b * nq + qa[g], hh)),
            pl.BlockSpec((tq, V_HEAD_DIM), lambda b, hh, g, qa, ka, qb, kb: (b * nq + ka[g], h + hh)),
            pl.BlockSpec((V_HEAD_DIM, tq), lambda b, hh, g, qa, ka, qb, kb: (hh, b * nq + kb[g])),
            pl.BlockSpec((LANES, V_HEAD_DIM), lambda b, hh, g, qa, ka, qb, kb: (0, hh)),
            pl.BlockSpec((V_HEAD_DIM, LANES), lambda b, hh, g, qa, ka, qb, kb: (hh, 0)),
            pl.BlockSpec((4, DIFF_HEAD_DIM), lambda b, hh, g, qa, ka, qb, kb: (0, 0)),
            pl.BlockSpec((V_HEAD_DIM, 1), lambda b, hh, g, qa, ka, qb, kb: (0, 0)),
        ],
        out_specs=pl.BlockSpec((tq, V_HEAD_DIM), lambda b, hh, g, qa, ka, qb, kb: (b * nq + qb[g], hh)),
        scratch_shapes=[
            pltpu.VMEM((2 * tq, V_HEAD_DIM), BF16),
            pltpu.VMEM((LANES, 2 * tq), F32),
            pltpu.VMEM((tq, 2 * tq), F32),
            pltpu.VMEM((tq, 2 * tq), F32),
            pltpu.VMEM((1, 2 * tq), F32),
            pltpu.VMEM((1, 2 * tq), F32),
            pltpu.VMEM((1, 2 * tq), F32),
            pltpu.VMEM((1, 2 * tq), F32),
            pltpu.VMEM((V_HEAD_DIM, 2 * tq), F32),
        ],
    )
    return pl.pallas_call(
        functools.partial(_attn_kernel, tq=tq, dg=dg, n_steps=n_steps, lam_init=lam_init),
        grid_spec=grid_spec,
        out_shape=jax.ShapeDtypeStruct((bsz * seq, h * V_HEAD_DIM), BF16),
        compiler_params=_cparams(("parallel", "parallel", "arbitrary")),
        name="attn",
    )(*tabs, qk, qk, vt, kmeta, vmeta_t, lam_vecs, gain_col)


def _ssd_kernel(*refs, t_blk, bsz, n_cast):
    dtt_refs = refs[:bsz]
    cast_in = refs[bsz:bsz + n_cast]
    (zx_ref, dt_ref, xbcm_ref, dtm_ref, dttm_ref,
     convw_ref, convb_ref, dtb_ref, dtbt_ref, alog_ref, alogt_ref, dskip_ref, nw_ref,
     expand_ref, tril_ref, triu2_ref, shift_ref, out_ref) = refs[bsz + n_cast:bsz + n_cast + 18]
    cast_out = refs[bsz + n_cast + 18:bsz + 2 * n_cast + 18]
    xs_sc, xm_sc, state_sc = refs[bsz + 2 * n_cast + 18:]
    for src, dst in zip(cast_in, cast_out):
        dst[...] = src[...].astype(BF16)
    n_heads, p_dim, n_state, n_groups = N_SSD_HEADS, SSD_HEAD_DIM, D_STATE, N_SSD_GROUPS
    width = n_heads * p_dim
    gw = width // n_groups
    n_pairs = n_heads // 2
    nt = (((1,), (1,)), ((), ()))
    expand = expand_ref[...]
    a_row = -jnp.exp(alog_ref[...])
    a_col = -jnp.exp(alogt_ref[...])
    d_full = dskip_ref[...]
    lane = lax.broadcasted_iota(jnp.int32, (CHUNK, LANES), 1)
    row = lax.broadcasted_iota(jnp.int32, (CHUNK, LANES), 0)
    left = lane < p_dim
    causal2 = (lane & (CHUNK - 1)) <= row

    def chunk(win_f32, st_ref, dt_raw, dtt_raw, pad_rows, z, out_r0, b=0):
        win = win_f32.astype(BF16)
        taps = jnp.dot(shift_ref[...], win, preferred_element_type=F32)
        conv = convb_ref[...] + win_f32[HALO:, :] * convw_ref[CONV_WIDTH - 1:CONV_WIDTH, :]
        for k in range(CONV_WIDTH - 1):
            conv = conv + taps[k * CHUNK:(k + 1) * CHUNK, :] * convw_ref[k:k + 1, :]
        xc = _silu(conv)
        x_s = xc[:, :width]
        dt = jax.nn.softplus(dt_raw + dtb_ref[...])
        dtt = jax.nn.softplus(dtt_raw + dtbt_ref[...])
        if pad_rows:
            dt = jnp.where(row >= pad_rows, dt, 0.0)
            dtt = jnp.where(lax.broadcasted_iota(jnp.int32, dtt.shape, 1) >= pad_rows, dtt, 0.0)
        a_pieces = jnp.dot(tril_ref[...], jnp.concatenate(_split_bf16(dt * a_row, 3), axis=1),
                           preferred_element_type=F32)
        a_cs = a_pieces[:, 0:LANES] + a_pieces[:, LANES:2 * LANES] + a_pieces[:, 2 * LANES:3 * LANES]
        a_cs_t2 = _dot_f32_by_01(dtt * a_col, triu2_ref[...])
        spread = jnp.dot(jnp.concatenate(_split_bf16(a_cs, 3) + _split_bf16(dt, 3), axis=0), expand,
                         preferred_element_type=F32)
        a_full = spread[0:CHUNK] + spread[CHUNK:2 * CHUNK] + spread[2 * CHUNK:3 * CHUNK]
        dt_full = spread[3 * CHUNK:4 * CHUNK] + spread[4 * CHUNK:5 * CHUNK] + spread[5 * CHUNK:6 * CHUNK]
        a_last = a_full[CHUNK - 1:CHUNK, :]
        xdt = x_s * dt_full
        w_state = (xdt * jnp.exp(a_last - a_full)).astype(BF16)
        xdt_b = xdt.astype(BF16)
        zero_b = jnp.zeros((CHUNK, LANES), BF16)
        y_parts = []
        for g in range(n_groups):
            b_g = xc[:, width + g * n_state: width + (g + 1) * n_state]
            c_g = xc[:, width + (n_groups + g) * n_state: width + (n_groups + g + 1) * n_state]
            b_gb = b_g.astype(BF16)
            c_gb = c_g.astype(BF16)
            st = st_ref[g]
            if out_r0 is not None:
                cb2 = lax.dot_general(c_gb, jnp.concatenate([b_gb, b_gb], axis=0), nt,
                                      preferred_element_type=F32)
                y_off = jnp.dot(c_gb, st.astype(BF16), preferred_element_type=F32)
                y_g = y_off * jnp.exp(a_full[:, g * gw:(g + 1) * gw])
                diag = []
                for pp in range(n_pairs // n_groups):
                    pr = g * (n_pairs // n_groups) + pp
                    colb = a_full[:, pr * LANES:(pr + 1) * LANES]
                    rowb = jnp.where(left, a_cs_t2[2 * pr:2 * pr + 1, :], a_cs_t2[2 * pr + 1:2 * pr + 2, :])
                    dec = jnp.exp(jnp.where(causal2, colb - rowb, -jnp.inf))
                    m_pair = (cb2 * dec).astype(BF16)
                    xp = xdt_b[:, pr * LANES:(pr + 1) * LANES]
                    rhs = jnp.concatenate([jnp.where(left, xp, zero_b), jnp.where(left, zero_b, xp)], axis=0)
                    diag.append(jnp.dot(m_pair, rhs, preferred_element_type=F32))
                y_parts.append(y_g + jnp.concatenate(diag, axis=1))
            new_st = st * jnp.exp(a_last[:, g * gw:(g + 1) * gw]) + jnp.dot(
                b_g.T.astype(BF16), w_state[:, g * gw:(g + 1) * gw], preferred_element_type=F32)
            st_ref[g] = new_st
        if out_r0 is not None:
            y = jnp.concatenate(y_parts, axis=1) + x_s * d_full
            gated = y * _silu(z)
            out_ref[b, out_r0:out_r0 + CHUNK, :] = _rms(gated, nw_ref[...]).astype(out_ref.dtype)

    @pl.when(pl.program_id(0) == 0)
    def _():
        state_sc[...] = jnp.zeros_like(state_sc)
        xm_sc[0:HALO, :] = jnp.zeros((HALO, xm_sc.shape[1]), F32)
        xm_sc[HALO:HALO + CHUNK, :] = xbcm_ref[...].astype(F32)
        chunk(xm_sc[...], state_sc.at[0], dtm_ref[...], dttm_ref[...], CHUNK - N_META, None, None)
        for b in range(bsz):
            if b:
                state_sc[b] = state_sc[0]
            xs_sc[b, 0:HALO, :] = xm_sc[CHUNK:CHUNK + HALO, :]

    for b in range(bsz):
        xs_sc[b, HALO:HALO + t_blk, :] = zx_ref[b, :, width:].astype(F32)
    for c in range(t_blk // CHUNK):
        rows = slice(c * CHUNK, (c + 1) * CHUNK)
        for b in range(bsz):
            chunk(xs_sc[b, c * CHUNK:(c + 1) * CHUNK + HALO, :], state_sc.at[b], dt_ref[b, rows, :],
                  dtt_refs[b][:, rows], 0, zx_ref[b, rows, 0:width].astype(F32), c * CHUNK, b)
    for b in range(bsz):
        xs_sc[b, 0:HALO, :] = xs_sc[b, t_blk:t_blk + HALO, :]


def _ssd(zx, dt, dtt, xbc_m, dt_m, dtt_m, conv_w, conv_b, dt_bias, a_log, d_skip, norm_w, cast_weights,
         *, bsz, seq, t_blk):
    nb = seq // t_blk
    slab = lambda w: pl.BlockSpec((w.shape[0] // nb, w.shape[1]), lambda i: (i, 0))
    width = N_SSD_HEADS * SSD_HEAD_DIM
    cdim = zx.shape[1] - width
    nh = N_SSD_HEADS

    def pad_row(v):
        return jnp.pad(v.reshape(1, nh), ((0, 0), (0, LANES - nh)))

    hh = np.arange(LANES)[:, None]
    ll = np.arange(width)[None, :]
    one_hot = lambda m: jnp.asarray(m.astype(np.float32), dtype=BF16)
    expand = one_hot(ll // SSD_HEAD_DIM == hh)
    ii = np.arange(CHUNK)
    tril = one_hot(ii[None, :] <= ii[:, None])
    triu2 = one_hot(np.tile(ii[:, None] <= ii[None, :], (1, 2)))
    jj = np.arange(HALO + CHUNK)
    shift = one_hot(np.concatenate([jj[None, :] == ii[:, None] + HALO - (CONV_WIDTH - 1) + k
                                    for k in range(CONV_WIDTH - 1)]))
    full = lambda shape: pl.BlockSpec(shape, lambda i: tuple(0 for _ in shape))
    out, *cast = pl.pallas_call(
        functools.partial(_ssd_kernel, t_blk=t_blk, bsz=bsz, n_cast=len(cast_weights)),
        grid=(nb,),
        in_specs=[pl.BlockSpec((nh, t_blk), functools.partial(lambda i, b: (0, b * nb + i), b=b))
                  for b in range(bsz)] + [slab(w) for w in cast_weights] + [
            pl.BlockSpec((bsz, t_blk, width + cdim), lambda i: (0, i, 0)),
            pl.BlockSpec((bsz, t_blk, LANES), lambda i: (0, i, 0)),
            full((CHUNK, cdim)), full((CHUNK, LANES)), full((nh, CHUNK)),
            full((CONV_WIDTH, cdim)), full((1, cdim)), full((1, LANES)), full((nh, 1)),
            full((1, LANES)), full((nh, 1)), full((1, width)), full((1, width)),
            full((LANES, width)), full((CHUNK, CHUNK)), full((CHUNK, 2 * CHUNK)), full(shift.shape),
        ],
        out_specs=[pl.BlockSpec((bsz, t_blk, width), lambda i: (0, i, 0))] + [slab(w) for w in cast_weights],
        out_shape=[jax.ShapeDtypeStruct((bsz, seq, width), BF16)]
        + [jax.ShapeDtypeStruct(w.shape, BF16) for w in cast_weights],
        scratch_shapes=[
            pltpu.VMEM((bsz, t_blk + HALO, cdim), F32),
            pltpu.VMEM((CHUNK + HALO, cdim), F32),
            pltpu.VMEM((bsz, N_SSD_GROUPS, D_STATE, width // N_SSD_GROUPS), F32),
        ],
        compiler_params=_cparams(("arbitrary",)),
        name="ssd",
    )(*([dtt] * bsz), *cast_weights, zx.reshape(bsz, seq, width + cdim), dt.reshape(bsz, seq, LANES),
      xbc_m, dt_m, dtt_m,
      conv_w, conv_b.reshape(1, cdim), pad_row(dt_bias), dt_bias.reshape(nh, 1), pad_row(a_log),
      a_log.reshape(nh, 1), jnp.repeat(d_skip, SSD_HEAD_DIM).reshape(1, width),
      norm_w.reshape(1, width), expand, tril, triu2, shift)
    return out.reshape(bsz * seq, width), cast


def _rope_tables(n_pos):
    inv = jnp.power(ROPE_THETA, -jnp.arange(0, ROT_DIM, 2, dtype=F32) / ROT_DIM)
    ang = jnp.arange(n_pos, dtype=F32)[:, None] * inv[None, :]
    cos, sin = jnp.cos(ang), jnp.sin(ang)
    half = ROT_DIM // 2
    r = np.arange(LANES) % DIFF_HEAD_DIM
    idx = jnp.asarray(r % half)
    lo = jnp.asarray(r < half)[None, :]
    hi = jnp.asarray((r >= half) & (r < ROT_DIM))[None, :]
    cos_t = jnp.where(lo | hi, cos[:, idx], 1.0)
    sin_lo = jnp.where(lo, -sin[:, idx], 0.0)
    sin_hi = jnp.where(hi, sin[:, idx], 0.0)
    return jnp.stack([cos_t, sin_lo, sin_hi])


def kernel(x, meta_tokens, ffn1_norm, ffn1_w_gate, ffn1_w_up, ffn1_w_down, mix_norm, w_in, q_norm, k_norm,
           lambda_q1, lambda_k1, lambda_q2, lambda_k2, attn_out_norm, conv_w, conv_b, dt_bias, a_log, d_skip,
           ssd_norm, w_out, ffn2_norm, ffn2_w_gate, ffn2_w_up, ffn2_w_down):
    bsz, seq, d = x.shape
    assert ffn1_norm.shape[0] == 1, "single-layer block"
    aw = N_DIFF_HEADS * V_HEAD_DIM
    sw = N_SSD_HEADS * SSD_HEAD_DIM
    cdim = sw + 2 * N_SSD_GROUPS * D_STATE
    lam_init = 0.8 - 0.6 * math.exp(-0.3 * 0)
    row = lambda v: v.reshape(1, -1)

    tn = 512
    w_in0 = w_in[0]
    w_dt = w_in0[:, 3 * aw + sw + cdim:].astype(BF16)
    w_dt_pad = jnp.pad(w_dt, ((0, 0), (0, LANES - N_SSD_HEADS)))
    w_dt_t = w_dt.T

    n_sub = aw // DIFF_HEAD_DIM
    q_gain = jnp.tile(q_norm[0], n_sub) * (DIFF_HEAD_DIM ** -0.5 * math.log2(math.e))
    k_gain = jnp.tile(k_norm[0], n_sub)
    qk_gain = jnp.concatenate([q_gain, k_gain]).reshape(2 * aw // tn, 1, tn)
    rope = _rope_tables(N_META + seq)
    gi = np.arange(2 * LANES) // DIFF_HEAD_DIM
    bd = jnp.asarray((gi[:, None] == gi[None, :]).astype(np.float32) / DIFF_HEAD_DIM, dtype=BF16)

    hn_m, wg1, wu1, wd1 = _ffn_meta(meta_tokens.astype(F32), row(ffn1_norm[0]), ffn1_w_gate[0],
                                    ffn1_w_up[0], ffn1_w_down[0], row(mix_norm[0]), tf=FFN_TF)
    w_in_t = w_in0.T.astype(BF16)
    x2 = x.reshape(bsz * seq, d)
    h1, hn = _ffn(x2, row(ffn1_norm[0]), wg1, wu1, wd1, row(mix_norm[0]), tm=FFN_TM, tf=FFN_TF)

    qk = _proj_qk(hn, w_in_t, qk_gain, rope[:, N_META:], bd, tm=512, tn=tn, n=2 * aw)
    qk_m = _proj_qk(hn_m, w_in_t, qk_gain, rope[:, :N_META], bd, tm=N_META, tn=tn, n=2 * aw)
    v_t = _proj(hn, w_in_t, tm=2048, tn=tn, row0=2 * aw, n=aw, name="proj_vt", transpose=True)
    vzx_m = _proj(hn_m, w_in_t, tm=N_META, tn=tn, row0=2 * aw, n=aw + sw + cdim, name="proj_vzx_meta")
    zx = _proj(hn, w_in_t, tm=2048, tn=tn, row0=3 * aw, n=sw + cdim, name="proj_zx")
    v_t_m, xbc_m = vzx_m[:, :aw].T, vzx_m[:, aw + sw:]
    dt, dtt = _proj_dt(hn, w_dt_pad, w_dt_t, tm=1024)
    dt_m, dtt_m = _proj_dt(hn_m, w_dt_pad, w_dt_t, tm=N_META)

    kmeta = jnp.pad(qk_m[:, aw:], ((0, LANES - N_META), (0, 0)))
    vmeta_t = jnp.pad(v_t_m, ((0, 0), (0, LANES - N_META)))
    lam_vecs = jnp.stack([lambda_q1[0], lambda_k1[0], lambda_q2[0], lambda_k2[0]]).astype(F32)
    attn = _attention(qk, v_t, kmeta, vmeta_t, lam_vecs, attn_out_norm[0].reshape(V_HEAD_DIM, 1),
                      bsz=bsz, seq=seq, tq=1024, dg=256, lam_init=lam_init)

    lead = CHUNK - N_META
    ssd, (wg2, wu2, wd2, w_o) = _ssd(
        zx, dt, dtt,
        jnp.pad(xbc_m, ((lead, 0), (0, 0))), jnp.pad(dt_m, ((lead, 0), (0, 0))), jnp.pad(dtt_m, ((0, 0), (lead, 0))),
        conv_w[0], conv_b[0], dt_bias[0], a_log[0], d_skip[0], ssd_norm[0],
        (ffn2_w_gate[0], ffn2_w_up[0], ffn2_w_down[0], w_out[0]),
        bsz=bsz, seq=seq, t_blk=256)

    h2 = _outproj(h1, attn, ssd, w_o, tm=512, tn=d)
    (out,) = _ffn(h2, row(ffn2_norm[0]), wg2, wu2, wd2, tm=FFN_TM, tf=FFN_TF)
    return out.reshape(bsz, seq, d)
```
